```python
import math
import jax, jax.numpy as jnp
from jax import lax
import numpy as np

D_MODEL = 1024
BATCH = 4
SEQ = 8192
DEPTH = 4

GRID_W = 64
CTX_LEN = 256
HEAD_DIM = 64
HALF = HEAD_DIM // 2
AXIS_DIM = HEAD_DIM // 2
ROPE_THETA = 10000.0
DA_HEADS = 4
GQ_HEADS = 8
GKV_HEADS = 2
HY_WIDTH = 512
HY_ORDER = 2
HY_EMB = 33
HY_BANDS = (HY_EMB - 1) // 2
HY_HIDDEN = 64
HY_MIN_DECAY = math.log(1e-2) / 1.5
HY_MAX_DECAY = math.log(1e-2) / 0.3
N_BRANCH = 3
D_FF = 2816
Q_BLOCK = 128
LN_EPS = 1e-6
DEEPNORM_ALPHA = (2 * DEPTH) ** 0.25
DEEPNORM_BETA = (8 * DEPTH) ** -0.25

DA_QK = DA_HEADS * 2 * HEAD_DIM
DA_V = DA_HEADS * 2 * HEAD_DIM
GQ_Q = GQ_HEADS * HEAD_DIM
GQ_KV = GKV_HEADS * HEAD_DIM
BRANCH_W = 512
SPLITS = (DA_QK, DA_V, GQ_KV, GQ_KV, DA_QK, GQ_Q, 3 * HY_WIDTH, N_BRANCH * D_MODEL)
IN_COLS = sum(SPLITS)
KV_COLS = sum(SPLITS[:4])

kernel_name = 'hybrid_diffattn_gqa_hyena_convffn_dit'

F32 = jnp.float32


def layer_norm(x, g=None, b=None):
    x32 = x.astype(F32)
    xc = x32 - jnp.mean(x32, axis=-1, keepdims=True)
    y = xc * lax.rsqrt(jnp.mean(xc * xc, axis=-1, keepdims=True) + LN_EPS)
    if g is not None:
        y = y * g + b
    return y.astype(x.dtype)


def rms_norm(x, g):
    x32 = x.astype(F32)
    y = x32 * lax.rsqrt(jnp.mean(x32 * x32, axis=-1, keepdims=True) + LN_EPS)
    return (y * g).astype(x.dtype)


def modulate(x, shift, scale):
    return layer_norm(x) * (1.0 + scale) + shift


def split_cols(p, sizes):
    offs = [int(v) for v in np.cumsum(sizes)[:-1]]
    return jnp.split(p, offs, axis=-1)


def flat_heads(t):
    return t.reshape(t.shape[0], t.shape[1], -1)


def axial_rope(n):
    rows = n // GRID_W
    r = jnp.repeat(jnp.arange(rows, dtype=F32), GRID_W)
    col = jnp.tile(jnp.arange(GRID_W, dtype=F32), rows)
    inv = ROPE_THETA ** (-jnp.arange(0, AXIS_DIM, 2, dtype=F32) / AXIS_DIM)
    ang = jnp.concatenate([r[:, None] * inv, col[:, None] * inv], axis=-1)
    return jnp.cos(ang), jnp.sin(ang)


def apply_rope(x, rope):
    cos, sin = rope
    x32 = x.astype(F32)
    x1, x2 = x32[..., :HALF], x32[..., HALF:]
    c_, s_ = cos[:, None, :], sin[:, None, :]
    return jnp.concatenate([x1 * c_ - x2 * s_, x1 * s_ + x2 * c_], axis=-1).astype(x.dtype)


def dw_conv3(x, w, b):
    xp = jnp.pad(x, ((0, 0), (1, 1), (0, 0)))
    return xp[:, :-2] * w[0] + xp[:, 1:-1] * w[1] + xp[:, 2:] * w[2] + b


def split_pair(t):
    b, n = t.shape[:2]
    t = t.reshape(b, n, DA_HEADS, 2, HEAD_DIM)
    return t[..., 0, :], t[..., 1, :]


def prep_kv(ka, va, kb, vb, gq_kn, rope):
    b, n = ka.shape[:2]
    k1, k2 = split_pair(ka)
    va = va.reshape(b, n, DA_HEADS, 2 * HEAD_DIM)
    kb = rms_norm(kb.reshape(b, n, GKV_HEADS, HEAD_DIM), gq_kn)
    vb = vb.reshape(b, n, GKV_HEADS, HEAD_DIM)
    if rope is not None:
        k1, k2, kb = apply_rope(k1, rope), apply_rope(k2, rope), apply_rope(kb, rope)
    return [k1, k2, va, kb, vb]


def prep_q(qa, qb, gq_qn, rope):
    b, n = qa.shape[:2]
    q1, q2 = split_pair(qa)
    qb = rms_norm(qb.reshape(b, n, GQ_HEADS, HEAD_DIM), gq_qn)
    if rope is not None:
        q1, q2, qb = apply_rope(q1, rope), apply_rope(q2, rope), apply_rope(qb, rope)
    return q1, q2, qb.reshape(b, n, GKV_HEADS, GQ_HEADS // GKV_HEADS, HEAD_DIM)


def diff_core(q1, q2, k1, k2, v, lam):
    sc = HEAD_DIM ** -0.5
    p1 = jax.nn.softmax(jnp.einsum('bqhd,bkhd->bhqk', q1, k1).astype(F32) * sc, axis=-1)
    p2 = jax.nn.softmax(jnp.einsum('bqhd,bkhd->bhqk', q2, k2).astype(F32) * sc, axis=-1)
    p = (p1 - lam * p2).astype(v.dtype)
    return jnp.einsum('bhqk,bkhd->bqhd', p, v)


def gqa_core(q, k, v):
    s = jnp.einsum('bqhgd,bkhd->bhgqk', q, k).astype(F32) * (HEAD_DIM ** -0.5)
    p = jax.nn.softmax(s, axis=-1).astype(v.dtype)
    return jnp.einsum('bhgqk,bkhd->bqhgd', p, v)


def sweep_queries(core, qs, kvs):
    b, n = qs[0].shape[:2]
    nb = n // Q_BLOCK
    blocks = tuple(jnp.swapaxes(q.reshape(b, nb, Q_BLOCK, *q.shape[2:]), 0, 1) for q in qs)
    out = lax.map(lambda qb: core(*qb, *kvs), blocks)
    return jnp.swapaxes(out, 0, 1).reshape(b, n, *out.shape[3:])


def diff_post(o, g, lam_init):
    return flat_heads(rms_norm(o, g) * (1.0 - lam_init))


def hyena_filters(n, w1, b1, w2, b2, w3, freq):
    t = jnp.linspace(0.0, 1.0, n, dtype=F32)[:, None]
    f = jnp.linspace(1e-4, HY_BANDS - 1, HY_BANDS, dtype=F32)
    ang = (2.0 * math.pi / n) * jnp.arange(n, dtype=F32)[:, None] * f[None, :]
    z = jnp.concatenate([t, jnp.cos(ang), -jnp.sin(ang)], axis=-1)
    hid = jnp.sin(freq * (z @ w1 + b1))
    hid = jnp.sin(freq * (hid @ w2 + b2))
    h = (hid @ w3).astype(F32).reshape(n, HY_ORDER, 2, HY_WIDTH)
    deltas = jnp.abs(jnp.linspace(HY_MIN_DECAY, HY_MAX_DECAY, HY_WIDTH, dtype=F32))
    h = h * jnp.exp(-t * deltas)[:, None, None, :]
    h = h / jnp.sum(jnp.abs(h), axis=(0, 2), keepdims=True)
    return jnp.moveaxis(h, 0, 2)


def long_conv(z, h_fwd, h_bwd, bias):
    n = z.shape[1]
    filt = jnp.concatenate([h_fwd, jnp.zeros_like(h_fwd[:1]), h_bwd[:0:-1]], axis=0)
    zf = jnp.fft.rfft(z.astype(F32), n=2 * n, axis=1)
    hf = jnp.fft.rfft(filt, n=2 * n, axis=0)
    y = jnp.fft.irfft(zf * hf[None], n=2 * n, axis=1)[:, :n]
    return (y + z.astype(F32) * bias).astype(z.dtype)


def hyena_mix(u, filt, conv_w, conv_b, bias):
    v, x1, x2 = jnp.split(dw_conv3(u, conv_w, conv_b), 3, axis=-1)
    z = x1 * long_conv(v, filt[0, 0], filt[0, 1], bias[0])
    return x2 * long_conv(z, filt[1, 0], filt[1, 1], bias[1])


def branch_merge(a, b_, c_, gates, w_pa, w_pb, w_pc, w_o):
    ga, gb, gc = jnp.split(jax.nn.sigmoid(gates), N_BRANCH, axis=-1)
    m = ga * (a @ w_pa) + gb * (b_ @ w_pb) + gc * (c_ @ w_pc)
    return m @ w_o


def conv_ffn(h, w_up, conv_w, conv_b, w_down):
    val, gate = jnp.split(dw_conv3(h @ w_up, conv_w, conv_b), 2, axis=-1)
    return (jax.nn.silu(gate) * val) @ w_down


def setup_inputs(seed: int = 0) -> dict:
    key = jax.random.key(seed)
    ks = iter(jax.random.split(key, 40))
    d, L = D_MODEL, DEPTH

    def nrm(shape, s):
        return jax.random.normal(next(ks), shape, F32) * s

    def gain(shape):
        return 1.0 + nrm(shape, 0.02)

    return {
        'x': nrm((BATCH, SEQ, d), 1.0),
        'c': nrm((BATCH, d), 1.0),
        'ctx': nrm((BATCH, CTX_LEN, d), 1.0),
        'c_ctx': nrm((d,), 1.0),
        'w_mod': nrm((L, d, 6 * d), d ** -0.5),
        'b_mod': nrm((L, 6 * d), 0.02),
        'w_in': nrm((L, d, IN_COLS), d ** -0.5),
        'da_lq1': nrm((L, HEAD_DIM), 0.1),
        'da_lk1': nrm((L, HEAD_DIM), 0.1),
        'da_lq2': nrm((L, HEAD_DIM), 0.1),
        'da_lk2': nrm((L, HEAD_DIM), 0.1),
        'da_subln': gain((L, 2 * HEAD_DIM)),
        'gq_qn': gain((L, HEAD_DIM)),
        'gq_kn': gain((L, HEAD_DIM)),
        'hy_conv_w': nrm((L, 3, 3 * HY_WIDTH), 3 ** -0.5),
        'hy_conv_b': nrm((L, 3 * HY_WIDTH), 0.02),
        'hy_w1': nrm((L, HY_EMB, HY_HIDDEN), HY_EMB ** -0.5),
        'hy_b1': nrm((L, HY_HIDDEN), 0.02),
        'hy_w2': nrm((L, HY_HIDDEN, HY_HIDDEN), HY_HIDDEN ** -0.5),
        'hy_b2': nrm((L, HY_HIDDEN), 0.02),
        'hy_w3': nrm((L, HY_HIDDEN, HY_ORDER * 2 * HY_WIDTH), HY_HIDDEN ** -0.5),
        'hy_freq': gain((L, HY_HIDDEN)),
        'hy_bias': nrm((L, HY_ORDER, HY_WIDTH), 0.1),
        'w_pa': nrm((L, BRANCH_W, d), BRANCH_W ** -0.5),
        'w_pb': nrm((L, BRANCH_W, d), BRANCH_W ** -0.5),
        'w_pc': nrm((L, BRANCH_W, d), BRANCH_W ** -0.5),
        'w_o': nrm((L, d, d), DEEPNORM_BETA * d ** -0.5),
        'ln1_g': gain((L, d)),
        'ln1_b': nrm((L, d), 0.02),
        'w_up': nrm((L, d, 2 * D_FF), d ** -0.5),
        'ffn_conv_w': nrm((L, 3, 2 * D_FF), 3 ** -0.5),
        'ffn_conv_b': nrm((L, 2 * D_FF), 0.02),
        'w_down': nrm((L, D_FF, d), DEEPNORM_BETA * D_FF ** -0.5),
        'ln2_g': gain((L, d)),
        'ln2_b': nrm((L, d), 0.02),
    }


def reference(x, c, ctx, c_ctx, w_mod, b_mod, w_in, da_lq1, da_lk1, da_lq2, da_lk2, da_subln,
              gq_qn, gq_kn, hy_conv_w, hy_conv_b, hy_w1, hy_b1, hy_w2, hy_b2, hy_w3, hy_freq, hy_bias,
              w_pa, w_pb, w_pc, w_o, ln1_g, ln1_b, w_up, ffn_conv_w, ffn_conv_b, w_down, ln2_g, ln2_b):
    n_lat = x.shape[1]
    n_ctx = ctx.shape[1]
    rope = axial_rope(n_lat)
    xc = ctx
    for l in range(DEPTH):
        last = l == DEPTH - 1
        lam_init = 0.8 - 0.6 * math.exp(-0.3 * l)
        lam = (jnp.exp(jnp.sum(da_lq1[l].astype(F32) * da_lk1[l].astype(F32)))
               - jnp.exp(jnp.sum(da_lq2[l].astype(F32) * da_lk2[l].astype(F32))) + lam_init)
        hy = (hy_w1[l], hy_b1[l], hy_w2[l], hy_b2[l], hy_w3[l], hy_freq[l])

        mod = (jax.nn.silu(c) @ w_mod[l] + b_mod[l])[:, None, :]
        sh1, sc1, g1, sh2, sc2, g2 = jnp.split(mod, 6, axis=-1)
        n_mc = 2 if last else 6
        mc = jnp.split(jax.nn.silu(c_ctx) @ w_mod[l, :, :n_mc * D_MODEL] + b_mod[l, :n_mc * D_MODEL], n_mc)

        hc = modulate(xc, mc[0], mc[1])
        pc = split_cols(hc @ w_in[l, :, :(KV_COLS if last else IN_COLS)], SPLITS[:4] if last else SPLITS)
        kv_c = prep_kv(pc[0], pc[1], pc[2], pc[3], gq_kn[l], None)

        hl = modulate(x, sh1, sc1)
        pl = split_cols(hl @ w_in[l], SPLITS)
        kv_l = prep_kv(pl[0], pl[1], pl[2], pl[3], gq_kn[l], rope)
        k1, k2, va, kb, vb = [jnp.concatenate([a, b], axis=1) for a, b in zip(kv_c, kv_l)]
        q1, q2, qb = prep_q(pl[4], pl[5], gq_qn[l], rope)
        a_l = diff_post(sweep_queries(diff_core, (q1, q2), (k1, k2, va, lam)), da_subln[l], lam_init)
        b_l = flat_heads(sweep_queries(gqa_core, (qb,), (kb, vb)))
        c_l = hyena_mix(pl[6], hyena_filters(n_lat, *hy), hy_conv_w[l], hy_conv_b[l], hy_bias[l])
        out_l = branch_merge(a_l, b_l, c_l, pl[7], w_pa[l], w_pb[l], w_pc[l], w_o[l])
        x = layer_norm(DEEPNORM_ALPHA * x + g1 * out_l, ln1_g[l], ln1_b[l])
        f_l = conv_ffn(modulate(x, sh2, sc2), w_up[l], ffn_conv_w[l], ffn_conv_b[l], w_down[l])
        x = layer_norm(DEEPNORM_ALPHA * x + g2 * f_l, ln2_g[l], ln2_b[l])

        if not last:
            q1c, q2c, qbc = prep_q(pc[4], pc[5], gq_qn[l], None)
            a_c = diff_post(diff_core(q1c, q2c, kv_c[0], kv_c[1], kv_c[2], lam), da_subln[l], lam_init)
            b_c = flat_heads(gqa_core(qbc, kv_c[3], kv_c[4]))
            c_c = hyena_mix(pc[6], hyena_filters(n_ctx, *hy), hy_conv_w[l], hy_conv_b[l], hy_bias[l])
            out_c = branch_merge(a_c, b_c, c_c, pc[7], w_pa[l], w_pb[l], w_pc[l], w_o[l])
            xc = layer_norm(DEEPNORM_ALPHA * xc + mc[2] * out_c, ln1_g[l], ln1_b[l])
            f_c = conv_ffn(modulate(xc, mc[3], mc[4]), w_up[l], ffn_conv_w[l], ffn_conv_b[l], w_down[l])
            xc = layer_norm(DEEPNORM_ALPHA * xc + mc[5] * f_c, ln2_g[l], ln2_b[l])
    return x
```

```python
import functools
import math

import numpy as np
import jax
import jax.numpy as jnp
from jax import lax
from jax.experimental import pallas as pl
from jax.experimental.pallas import tpu as pltpu

F32 = jnp.float32
BF16 = jnp.bfloat16

HEAD_DIM = 64
HALF = HEAD_DIM // 2
GRID_W = 64
ROPE_THETA = 10000.0
DA_HEADS = 4
GQ_HEADS = 8
GKV_HEADS = 2
HY_WIDTH = 512
HY_ORDER = 2
HY_EMB = 33
HY_BANDS = (HY_EMB - 1) // 2
HY_HIDDEN = 64
HY_MIN_DECAY = math.log(1e-2) / 1.5
HY_MAX_DECAY = math.log(1e-2) / 0.3
LN_EPS = 1e-6
LOG2E = 1.4426950408889634
NEG_BIG = -1e30

VMEM_LIMIT = 56 * 1024 * 1024


def _cp(n_axes):
    return pltpu.CompilerParams(dimension_semantics=("arbitrary",) * n_axes,
                                vmem_limit_bytes=VMEM_LIMIT)


def _tile(n, pref, mult):
    if n <= pref:
        return n
    t = (pref // mult) * mult
    while t >= mult:
        if n % t == 0:
            return t
        t -= mult
    return n


def _ln(x):
    mu = jnp.mean(x, axis=-1, keepdims=True)
    xc = x - mu
    return xc * lax.rsqrt(jnp.mean(xc * xc, axis=-1, keepdims=True) + LN_EPS)


def _ln_mod_kernel(x_ref, sh_ref, sc_ref, o_ref):
    o_ref[0] = (_ln(x_ref[0]) * (1.0 + sc_ref[0]) + sh_ref[0]).astype(o_ref.dtype)


def ln_mod(x, shift, scale):
    B, N, D = x.shape
    tm = _tile(N, 512, 8)
    return pl.pallas_call(
        _ln_mod_kernel,
        out_shape=jax.ShapeDtypeStruct((B, N, D), BF16),
        grid=(B, N // tm),
        in_specs=[pl.BlockSpec((1, tm, D), lambda b, i: (b, i, 0)),
                  pl.BlockSpec((1, 1, D), lambda b, i: (b, 0, 0)),
                  pl.BlockSpec((1, 1, D), lambda b, i: (b, 0, 0))],
        out_specs=pl.BlockSpec((1, tm, D), lambda b, i: (b, i, 0)),
        compiler_params=_cp(2), name="ln_mod",
    )(x, shift, scale)


def _resid_ln_tail(x, y, gate, g, b, alpha):
    return _ln(alpha * x + gate * y) * g + b


def _mm_kernel(a_ref, w_ref, o_ref, *, act):
    acc = jnp.dot(a_ref[...], w_ref[...], preferred_element_type=F32)
    if act == "sigmoid":
        acc = jax.nn.sigmoid(acc)
    o_ref[...] = acc.astype(o_ref.dtype)


def matmul(a, w, out_dtype, act=None, tm_pref=1024, tn_pref=512):
    M, K = a.shape
    Nn = w.shape[1]
    tm = _tile(M, tm_pref, 16)
    tn = _tile(Nn, tn_pref, 128)
    return pl.pallas_call(
        functools.partial(_mm_kernel, act=act),
        out_shape=jax.ShapeDtypeStruct((M, Nn), out_dtype),
        grid=(M // tm, Nn // tn),
        in_specs=[pl.BlockSpec((tm, K), lambda i, j: (i, 0)),
                  pl.BlockSpec((K, tn), lambda i, j: (0, j))],
        out_specs=pl.BlockSpec((tm, tn), lambda i, j: (i, j)),
        compiler_params=_cp(2), name="matmul",
    )(a, w)


def _proj_t_kernel(w_ref, h_ref, g_ref, cos_ref, sin_ref, o_ref, *, rms, rope):
    acc = lax.dot_general(w_ref[...], h_ref[0], (((1,), (1,)), ((), ())),
                          preferred_element_type=F32)
    n_heads = acc.shape[0] // HEAD_DIM
    if rope:
        c = cos_ref[...]
        s = sin_ref[...]
    for hd in range(n_heads):
        lo = hd * HEAD_DIM
        x = acc[lo:lo + HEAD_DIM]
        if rms:
            x = x * lax.rsqrt(jnp.mean(x * x, axis=0, keepdims=True) + LN_EPS)
        x = x * g_ref[lo:lo + HEAD_DIM]
        if rope:
            x1, x2 = x[:HALF], x[HALF:]
            o_ref[0, lo:lo + HALF] = (x1 * c - x2 * s).astype(o_ref.dtype)
            o_ref[0, lo + HALF:lo + HEAD_DIM] = (x1 * s + x2 * c).astype(o_ref.dtype)
        else:
            o_ref[0, lo:lo + HEAD_DIM] = x.astype(o_ref.dtype)


def proj_t(w_t, h, gain, cos_t, sin_t, *, rms, rope):
    B, N, K = h.shape
    Fdim = w_t.shape[0]
    tt = _tile(N, 512, 128)
    return pl.pallas_call(
        functools.partial(_proj_t_kernel, rms=rms, rope=rope),
        out_shape=jax.ShapeDtypeStruct((B, Fdim, N), BF16),
        grid=(B, N // tt),
        in_specs=[pl.BlockSpec((Fdim, K), lambda b, i: (0, 0)),
                  pl.BlockSpec((1, tt, K), lambda b, i: (b, i, 0)),
                  pl.BlockSpec((Fdim, 1), lambda b, i: (0, 0)),
                  pl.BlockSpec((HALF, tt), lambda b, i: (0, i)),
                  pl.BlockSpec((HALF, tt), lambda b, i: (0, i))],
        out_specs=pl.BlockSpec((1, Fdim, tt), lambda b, i: (b, 0, i)),
        compiler_params=_cp(2), name="proj_t",
    )(w_t, h, gain, cos_t, sin_t)


def _flash_kernel(lam_ref, q_ref, k_ref, v_ref, g_ref, o_ref, m_sc, l_sc, acc_sc, *, mode, nj, post_scale):
    qf = q_ref[0].astype(F32)
    tq = qf.shape[1]
    zero = jnp.zeros((HEAD_DIM, tq), F32)
    qa, qb = qf[:HEAD_DIM], qf[HEAD_DIM:]
    if mode == "da":
        qms = (jnp.concatenate([qa, zero], axis=0), jnp.concatenate([zero, qb], axis=0))
    else:
        first_group = (pl.program_id(1) // 2) == 0
        qms = tuple(jnp.where(first_group, jnp.concatenate([qh, zero], axis=0),
                              jnp.concatenate([zero, qh], axis=0)) for qh in (qa, qb))
    outs = []
    for mi in range(2):
        qm = qms[mi].astype(BF16)
        m_sc[...] = jnp.full(m_sc.shape, NEG_BIG, F32)
        l_sc[...] = jnp.zeros(l_sc.shape, F32)
        acc_sc[...] = jnp.zeros(acc_sc.shape, F32)

        def body(j, carry, qm=qm):
            s = jnp.dot(k_ref[0, j], qm, preferred_element_type=F32)
            m_old = m_sc[...]
            m_new = jnp.maximum(m_old, jnp.max(s, axis=0, keepdims=True))
            p = jnp.exp2(s - m_new)
            alpha = jnp.exp2(m_old - m_new)
            l_sc[...] = alpha * l_sc[...] + jnp.sum(p, axis=0, keepdims=True)
            acc_sc[...] = alpha * acc_sc[...] + jnp.dot(v_ref[0, j], p.astype(BF16),
                                                        preferred_element_type=F32)
            m_sc[...] = m_new
            return carry

        lax.fori_loop(0, nj, body, 0)
        outs.append(acc_sc[...] / l_sc[...])
    if mode == "da":
        o = outs[0] - lam_ref[0] * outs[1]
        o = o * lax.rsqrt(jnp.mean(o * o, axis=0, keepdims=True) + LN_EPS)
        o = o * (g_ref[...] * post_scale)
    else:
        o = jnp.concatenate(outs, axis=0)
    o_ref[0] = o.T.astype(o_ref.dtype)


def flash_attention(q_t, k_tok, v_t, lam, gain, *, mode, post_scale=1.0):
    B, _, Nq = q_t.shape
    Nk = k_tok.shape[1]
    tq = _tile(Nq, 512, 128)
    tk = _tile(Nk, 768, 128)
    nj = Nk // tk
    k4 = k_tok.reshape(B, nj, tk, k_tok.shape[2])
    v4 = jnp.swapaxes(v_t.reshape(B, v_t.shape[1], nj, tk), 1, 2)
    if mode == "da":
        dv = 2 * HEAD_DIM
        k_map = lambda b, h, i: (b, 0, 0, h)
        v_map = lambda b, h, i: (b, 0, h, 0)
    else:
        dv = HEAD_DIM
        k_map = lambda b, h, i: (b, 0, 0, 0)
        v_map = lambda b, h, i: (b, 0, h // 2, 0)
    return pl.pallas_call(
        functools.partial(_flash_kernel, mode=mode, nj=nj, post_scale=post_scale),
        out_shape=jax.ShapeDtypeStruct((B, Nq, 4 * 128), BF16),
        grid=(B, 4, Nq // tq),
        in_specs=[pl.BlockSpec(memory_space=pltpu.SMEM),
                  pl.BlockSpec((1, 128, tq), lambda b, h, i: (b, h, i)),
                  pl.BlockSpec((1, nj, tk, 128), k_map),
                  pl.BlockSpec((1, nj, dv, tk), v_map),
                  pl.BlockSpec((128, 1), lambda b, h, i: (0, 0))],
        out_specs=pl.BlockSpec((1, tq, 128), lambda b, h, i: (b, i, h)),
        scratch_shapes=[pltpu.VMEM((1, tq), F32), pltpu.VMEM((1, tq), F32), pltpu.VMEM((dv, tq), F32)],
        compiler_params=_cp(3), name="flash_" + mode,
    )(lam, q_t, k4, v4, gain)


def _shift_rows(x, prev_row, next_row):
    T = x.shape[0]
    row = lax.broadcasted_iota(jnp.int32, x.shape, 0)
    xp = jnp.where(row == 0, prev_row, pltpu.roll(x, 1, axis=0))
    xn = jnp.where(row == T - 1, next_row, pltpu.roll(x, T - 1, axis=0))
    return xp, xn


def _conv3_block(x_ref, p_ref, n_ref, w_ref, b_ref, halo):
    i = pl.program_id(1)
    last = pl.num_programs(1) - 1
    x = x_ref[0].astype(F32)
    prev_row = jnp.where(i > 0, p_ref[0].astype(F32)[halo - 1:halo], 0.0)
    next_row = jnp.where(i < last, n_ref[0].astype(F32)[0:1], 0.0)
    xp, xn = _shift_rows(x, prev_row, next_row)
    return xp * w_ref[0:1] + x * w_ref[1:2] + xn * w_ref[2:3] + b_ref[...]


def _halo_specs(tr, tc, halo, n_rows, col_of):
    per = tr // halo
    n_halo = n_rows // halo
    return [pl.BlockSpec((1, tr, tc), lambda b, i, j: (b, i, col_of(j))),
            pl.BlockSpec((1, halo, tc), lambda b, i, j: (b, jnp.maximum(i * per - 1, 0), col_of(j))),
            pl.BlockSpec((1, halo, tc), lambda b, i, j: (b, jnp.minimum((i + 1) * per, n_halo - 1), col_of(j)))]


def _dwconv_kernel(x_ref, p_ref, n_ref, w_ref, b_ref, o_ref):
    o_ref[0, 0] = _conv3_block(x_ref, p_ref, n_ref, w_ref, b_ref, 8)


def hyena_short_conv(u, w, b):
    B, L, C3 = u.shape
    C = C3 // 3
    tr = _tile(L, 512, 8)
    ident = lambda j: j
    return pl.pallas_call(
        _dwconv_kernel,
        out_shape=jax.ShapeDtypeStruct((3, B, L, C), F32),
        grid=(B, L // tr, 3),
        in_specs=_halo_specs(tr, C, 8, L, ident) + [
            pl.BlockSpec((3, C), lambda b, i, j: (0, j)),
            pl.BlockSpec((1, C), lambda b, i, j: (0, j))],
        out_specs=pl.BlockSpec((1, 1, tr, C), lambda b, i, j: (j, b, i, 0)),
        compiler_params=_cp(3), name="hyena_short_conv",
    )(u, u, u, w, b)


def _convact_kernel(xv_ref, pv_ref, nv_ref, xg_ref, pg_ref, ng_ref, wv_ref, bv_ref, wg_ref, bg_ref, o_ref):
    val = _conv3_block(xv_ref, pv_ref, nv_ref, wv_ref, bv_ref, 16)
    gate = _conv3_block(xg_ref, pg_ref, ng_ref, wg_ref, bg_ref, 16)
    o_ref[0] = (gate * jax.nn.sigmoid(gate) * val).astype(o_ref.dtype)


def ffn_conv_act(up, w, b):
    B, N, D2 = up.shape
    Dff = D2 // 2
    tc = _tile(Dff, 1408, 128)
    nc = Dff // tc
    tr = _tile(N, 256, 16)
    val_col = lambda j: j
    gate_col = lambda j: j + nc
    wspec = lambda col: [pl.BlockSpec((3, tc), lambda b_, i, j: (0, col(j))),
                         pl.BlockSpec((1, tc), lambda b_, i, j: (0, col(j)))]
    return pl.pallas_call(
        _convact_kernel,
        out_shape=jax.ShapeDtypeStruct((B, N, Dff), BF16),
        grid=(B, N // tr, nc),
        in_specs=_halo_specs(tr, tc, 16, N, val_col) + _halo_specs(tr, tc, 16, N, gate_col)
        + wspec(val_col) + wspec(gate_col),
        out_specs=pl.BlockSpec((1, tr, tc), lambda b_, i, j: (b_, i, j)),
        compiler_params=_cp(3), name="ffn_conv_act",
    )(up, up, up, up, up, up, w, b, w, b)


def _down_resid_kernel(a_ref, w_ref, x_ref, gate_ref, g_ref, b_ref, o_ref, *, alpha):
    y = jnp.dot(a_ref[0], w_ref[...], preferred_element_type=F32)
    o_ref[0] = _resid_ln_tail(x_ref[0], y, gate_ref[0], g_ref[...], b_ref[...], alpha)


def down_resid_ln(a, w, x, gate, g, b, alpha):
    B, N, K = a.shape
    D = w.shape[1]
    tm = _tile(N, 512, 16)
    row = lambda b_, i: (b_, i, 0)
    per_b = lambda b_, i: (b_, 0, 0)
    const2 = lambda b_, i: (0, 0)
    return pl.pallas_call(
        functools.partial(_down_resid_kernel, alpha=alpha),
        out_shape=jax.ShapeDtypeStruct((B, N, D), F32),
        grid=(B, N // tm),
        in_specs=[pl.BlockSpec((1, tm, K), row), pl.BlockSpec((K, D), const2),
                  pl.BlockSpec((1, tm, D), row), pl.BlockSpec((1, 1, D), per_b),
                  pl.BlockSpec((1, D), const2), pl.BlockSpec((1, D), const2)],
        out_specs=pl.BlockSpec((1, tm, D), row),
        compiler_params=_cp(2), name="down_resid_ln",
    )(a, w, x, gate, g, b)


def _merge_kernel(a_ref, b_ref, c_ref, gt_ref, wpa_ref, wpb_ref, wpc_ref, wo_ref, x_ref, gate_ref,
                  g_ref, bb_ref, sh_ref, sc_ref, ox_ref, oh_ref, *, alpha):
    D = wo_ref.shape[0]
    gt = gt_ref[0]
    m = gt[:, :D].astype(F32) * jnp.dot(a_ref[0], wpa_ref[...], preferred_element_type=F32)
    m = m + gt[:, D:2 * D].astype(F32) * jnp.dot(b_ref[0], wpb_ref[...], preferred_element_type=F32)
    m = m + gt[:, 2 * D:].astype(F32) * jnp.dot(c_ref[0], wpc_ref[...], preferred_element_type=F32)
    y = jnp.dot(m.astype(BF16), wo_ref[...], preferred_element_type=F32)
    xn = _resid_ln_tail(x_ref[0], y, gate_ref[0], g_ref[...], bb_ref[...], alpha)
    ox_ref[0] = xn
    oh_ref[0] = (_ln(xn) * (1.0 + sc_ref[0]) + sh_ref[0]).astype(oh_ref.dtype)


def merge_resid_ln(a, b, c, gates, wpa, wpb, wpc, wo, x, gate, g, bb, sh, sc, alpha):
    B, N, D = x.shape
    W = a.shape[2]
    tm = _tile(N, 512, 16)
    row = lambda b_, i: (b_, i, 0)
    per_b = lambda b_, i: (b_, 0, 0)
    const2 = lambda b_, i: (0, 0)
    return pl.pallas_call(
        functools.partial(_merge_kernel, alpha=alpha),
        out_shape=(jax.ShapeDtypeStruct((B, N, D), F32), jax.ShapeDtypeStruct((B, N, D), BF16)),
        grid=(B, N // tm),
        in_specs=[pl.BlockSpec((1, tm, W), row), pl.BlockSpec((1, tm, W), row), pl.BlockSpec((1, tm, W), row),
                  pl.BlockSpec((1, tm, 3 * D), row),
                  pl.BlockSpec((W, D), const2), pl.BlockSpec((W, D), const2), pl.BlockSpec((W, D), const2),
                  pl.BlockSpec((D, D), const2),
                  pl.BlockSpec((1, tm, D), row), pl.BlockSpec((1, 1, D), per_b),
                  pl.BlockSpec((1, D), const2), pl.BlockSpec((1, D), const2),
                  pl.BlockSpec((1, 1, D), per_b), pl.BlockSpec((1, 1, D), per_b)],
        out_specs=(pl.BlockSpec((1, tm, D), row), pl.BlockSpec((1, tm, D), row)),
        compiler_params=_cp(2), name="merge_resid_ln",
    )(a, b, c, gates, wpa, wpb, wpc, wo, x, gate, g, bb, sh, sc)


def _split_bf16(x):
    hi = x.astype(BF16)
    lo = (x - hi.astype(F32)).astype(BF16)
    return hi, lo


def _dot3(a, b):
    ah, al = _split_bf16(a)
    bh, bl = _split_bf16(b)
    d = functools.partial(jnp.dot, preferred_element_type=F32)
    return d(ah, bh) + (d(ah, bl) + d(al, bh))


def _filter_kernel(z_ref, w1_ref, b1_ref, w2_ref, b2_ref, w3_ref, fr_ref, dec_ref, h_ref, s_ref):
    fr = fr_ref[...]
    hid = jnp.sin(fr * (_dot3(z_ref[...], w1_ref[...]) + b1_ref[...]))
    hid = jnp.sin(fr * (_dot3(hid, w2_ref[...]) + b2_ref[...]))
    h = _dot3(hid, w3_ref[...])
    dec = dec_ref[...]
    h = h * jnp.concatenate([dec] * (h.shape[1] // dec.shape[1]), axis=1)
    h_ref[...] = h

    @pl.when(pl.program_id(0) == 0)
    def _():
        s_ref[...] = jnp.zeros(s_ref.shape, F32)

    s_ref[...] += jnp.sum(jnp.abs(h), axis=0, keepdims=True)


def hyena_filter_mlp(z, w1, b1, w2, b2, w3, freq, decay):
    n, E = z.shape
    Hh = w2.shape[0]
    O = w3.shape[1]
    tr = _tile(n, 512, 8)
    c2 = lambda i: (0, 0)
    return pl.pallas_call(
        _filter_kernel,
        out_shape=(jax.ShapeDtypeStruct((n, O), F32), jax.ShapeDtypeStruct((1, O), F32)),
        grid=(n // tr,),
        in_specs=[pl.BlockSpec((tr, E), lambda i: (i, 0)), pl.BlockSpec((E, Hh), c2), pl.BlockSpec((1, Hh), c2),
                  pl.BlockSpec((Hh, Hh), c2), pl.BlockSpec((1, Hh), c2), pl.BlockSpec((Hh, O), c2),
                  pl.BlockSpec((1, Hh), c2), pl.BlockSpec((tr, decay.shape[1]), lambda i: (i, 0))],
        out_specs=(pl.BlockSpec((tr, O), lambda i: (i, 0)), pl.BlockSpec((1, O), c2)),
        compiler_params=_cp(1), name="hyena_filter_mlp",
    )(z, w1, b1, w2, b2, w3, freq, decay)


def _mm_split(m_hi, m_lo, x, passes):
    d = functools.partial(jnp.dot, preferred_element_type=F32)
    if passes == 1:
        return d(m_hi, x.astype(BF16))
    xh, xl = _split_bf16(x)
    return d(m_hi, xh) + (d(m_lo, xh) + d(m_hi, xl))


def _lmul_kernel(mh_ref, ml_ref, x_ref, o_ref, *, passes):
    o_ref[0] = _mm_split(mh_ref[...], ml_ref[...], x_ref[0], passes).astype(o_ref.dtype)


def _lmul_gate_kernel(mh_ref, ml_ref, x_ref, z_ref, xg_ref, bias_ref, o_ref, *, passes):
    y = _mm_split(mh_ref[...], ml_ref[...], x_ref[0], passes)
    z = z_ref[0]
    o_ref[0] = (xg_ref[0] * (y + bias_ref[...] * z)).astype(o_ref.dtype)


def fft_lmul(m_hi, m_lo, x, passes, gate_args=None):
    P, R_in, cols = x.shape
    R_out = m_hi.shape[0]
    tn = _tile(cols, 2048, 128)
    mspec = pl.BlockSpec((R_out, R_in), lambda p, j: (0, 0))
    xspec = pl.BlockSpec((1, R_in, tn), lambda p, j: (p, 0, j))
    ospec = pl.BlockSpec((1, R_out, tn), lambda p, j: (p, 0, j))
    if gate_args is None:
        kern = functools.partial(_lmul_kernel, passes=passes)
        ins, specs = (m_hi, m_lo, x), [mspec, mspec, xspec]
    else:
        z, xg, bias = gate_args
        kern = functools.partial(_lmul_gate_kernel, passes=passes)
        ins = (m_hi, m_lo, x, z, xg, bias)
        specs = [mspec, mspec, xspec, ospec, ospec, pl.BlockSpec((1, tn), lambda p, j: (0, j))]
    return pl.pallas_call(
        kern, out_shape=jax.ShapeDtypeStruct((P, R_out, cols), F32),
        grid=(P, cols // tn), in_specs=specs, out_specs=ospec,
        compiler_params=_cp(2), name="fft_lmul",
    )(*ins)


def _fft_mid_kernel(gh_ref, gl_ref, gih_ref, gil_ref, h_ref, a_ref, o_ref, *, passes):
    nb = a_ref.shape[3]
    x = a_ref[0, :, 0].reshape(2 * nb, a_ref.shape[4])
    X = _mm_split(gh_ref[0], gl_ref[0], x, passes)
    xr, xi = X[:nb], X[nb:]
    hr, hi = h_ref[0, 0], h_ref[1, 0]
    Y = jnp.concatenate([xr * hr - xi * hi, xr * hi + xi * hr], axis=0)
    Bv = _mm_split(gih_ref[0], gil_ref[0], Y, passes)
    o_ref[0, :, 0] = Bv.reshape(2, nb, a_ref.shape[4])


def fft_mid(g_hi, g_lo, gi_hi, gi_lo, hspec, a5, passes):
    P, _, Na, Nb, C = a5.shape
    gspec = pl.BlockSpec((1, 2 * Nb, 2 * Nb), lambda k, p: (k, 0, 0))
    return pl.pallas_call(
        functools.partial(_fft_mid_kernel, passes=passes),
        out_shape=jax.ShapeDtypeStruct(a5.shape, F32),
        grid=(Na, P),
        in_specs=[gspec, gspec, gspec, gspec,
                  pl.BlockSpec((2, 1, Nb, C), lambda k, p: (0, k, 0, 0)),
                  pl.BlockSpec((1, 2, 1, Nb, C), lambda k, p: (p, 0, k, 0, 0))],
        out_specs=pl.BlockSpec((1, 2, 1, Nb, C), lambda k, p: (p, 0, k, 0, 0)),
        compiler_params=_cp(2), name="fft_mid",
    )(g_hi, g_lo, gi_hi, gi_lo, hspec, a5)


def _fft_spec_kernel(gh_ref, gl_ref, inv_ref, a_ref, o_ref, *, passes):
    nb = a_ref.shape[3]
    x = a_ref[0, :, 0].reshape(2 * nb, a_ref.shape[4])
    X = _mm_split(gh_ref[0], gl_ref[0], x, passes) * inv_ref[0]
    o_ref[0, :, 0] = X.reshape(2, nb, a_ref.shape[4])


def fft_filter_spectrum(g_hi, g_lo, inv_norm, a5, passes):
    P, _, Na, Nb, C = a5.shape
    gspec = pl.BlockSpec((1, 2 * Nb, 2 * Nb), lambda k, p: (k, 0, 0))
    blk = pl.BlockSpec((1, 2, 1, Nb, C), lambda k, p: (p, 0, k, 0, 0))
    return pl.pallas_call(
        functools.partial(_fft_spec_kernel, passes=passes),
        out_shape=jax.ShapeDtypeStruct(a5.shape, F32),
        grid=(Na, P),
        in_specs=[gspec, gspec, pl.BlockSpec((1, 1, C), lambda k, p: (p, 0, 0)), blk],
        out_specs=blk,
        compiler_params=_cp(2), name="fft_filter_spectrum",
    )(g_hi, g_lo, inv_norm, a5)


def _fft_factors(n_fft):
    na = 1 << (int(math.log2(n_fft)) // 2)
    return na, n_fft // na


def _fft_tables(L):
    n_fft = 2 * L
    na, nb = _fft_factors(n_fft)

    def cis(num, den):
        ang = (2.0 * math.pi / den) * (num % den).astype(F32)
        return jnp.cos(ang), -jnp.sin(ang)

    ia = jnp.arange(na, dtype=jnp.int32)
    ib = jnp.arange(nb, dtype=jnp.int32)
    far, fai = cis(ia[:, None] * ia[None, :], na)
    fh_r, fh_i = far[:, :na // 2], fai[:, :na // 2]
    m_fwd = jnp.block([[fh_r, -fh_i], [fh_i, fh_r]])
    m_flt = jnp.concatenate([far, fai], axis=0)
    m_inv = jnp.block([[fh_r.T, fh_i.T], [-fh_i.T, fh_r.T]]) / n_fft
    num = ib[None, :, None] * ib[None, None, :] * na + ia[:, None, None] * ib[None, None, :]
    gr, gi = cis(num, n_fft)
    g = jnp.concatenate([jnp.concatenate([gr, -gi], axis=2), jnp.concatenate([gi, gr], axis=2)], axis=1)
    grt, git = jnp.swapaxes(gr, 1, 2), jnp.swapaxes(gi, 1, 2)
    ginv = jnp.concatenate([jnp.concatenate([grt, git], axis=2), jnp.concatenate([-git, grt], axis=2)], axis=1)
    split = lambda m: _split_bf16(m.astype(F32))
    return dict(na=na, nb=nb, fwd=split(m_fwd), flt=split(m_flt), inv=split(m_inv), g=split(g), ginv=split(ginv))


def hyena_filter_spectra(L, w1, b1, w2, b2, w3, freq, tabs, passes):
    C = HY_WIDTH
    t = jnp.linspace(0.0, 1.0, L, dtype=F32)[:, None]
    f = jnp.linspace(1e-4, HY_BANDS - 1, HY_BANDS, dtype=F32)
    ang = (2.0 * math.pi / L) * jnp.arange(L, dtype=F32)[:, None] * f[None, :]
    z = jnp.concatenate([t, jnp.cos(ang), -jnp.sin(ang)], axis=-1)
    e_pad = 128
    z = jnp.pad(z, ((0, 0), (0, e_pad - HY_EMB)))
    w1p = jnp.pad(w1, ((0, e_pad - HY_EMB), (0, 0)))
    deltas = jnp.abs(jnp.linspace(HY_MIN_DECAY, HY_MAX_DECAY, C, dtype=F32))
    decay = jnp.exp(-t * deltas)
    h, s = hyena_filter_mlp(z, w1p, b1[None], w2, b2[None], w3, freq[None], decay)
    h = h.reshape(L, HY_ORDER, 2, C)
    s = s.reshape(HY_ORDER, 2, C)
    inv_norm = (1.0 / (s[:, 0] + s[:, 1]))[:, None, :]
    h_fwd = jnp.moveaxis(h[:, :, 0], 0, 1)
    h_bwd = jnp.moveaxis(h[:, :, 1], 0, 1)
    filt = jnp.concatenate([h_fwd, jnp.zeros_like(h_fwd[:, :1]), h_bwd[:, :0:-1]], axis=1)
    na, nb = tabs["na"], tabs["nb"]
    a = fft_lmul(*tabs["flt"], filt.reshape(HY_ORDER, na, nb * C), passes)
    return fft_filter_spectrum(*tabs["g"], inv_norm, a.reshape(HY_ORDER, 2, na, nb, C), passes)


def hyena_long_conv_gate(z, xg, bias, hspec, tabs, passes):
    B, L, C = z.shape
    na, nb = tabs["na"], tabs["nb"]
    P = B // 2
    a = fft_lmul(*tabs["fwd"], z.reshape(P, na, nb * C), passes)
    bv = fft_mid(*tabs["g"], *tabs["ginv"], hspec, a.reshape(P, 2, na, nb, C), passes)
    bias_t = jnp.tile(bias, nb)[None]
    y = fft_lmul(*tabs["inv"], bv.reshape(P, 2 * na, nb * C), passes,
                 gate_args=(z.reshape(P, na, nb * C), xg.reshape(P, na, nb * C), bias_t))
    return y.reshape(B, L, C)


def hyena_mix(u, spectra, conv_w, conv_b, bias, tabs, passes):
    vx = hyena_short_conv(u, conv_w, conv_b[None])
    zz = hyena_long_conv_gate(vx[0], vx[1], bias[0], spectra[0], tabs, passes)
    return hyena_long_conv_gate(zz, vx[2], bias[1], spectra[1], tabs, passes)


FFT_PASSES = 3


def _rope_tables_t(n):
    rows = n // GRID_W
    r = jnp.repeat(jnp.arange(rows, dtype=F32), GRID_W)
    col = jnp.tile(jnp.arange(GRID_W, dtype=F32), rows)
    axis_dim = HEAD_DIM // 2
    inv = ROPE_THETA ** (-jnp.arange(0, axis_dim, 2, dtype=F32) / axis_dim)
    ang = jnp.concatenate([r[:, None] * inv, col[:, None] * inv], axis=-1)
    return jnp.cos(ang).T, jnp.sin(ang).T


def _layer_weights(l, w_in, gq_qn, gq_kn):
    w = w_in[l]
    d_qk = DA_HEADS * 2 * HEAD_DIM
    gq_kv = GKV_HEADS * HEAD_DIM
    gq_q = GQ_HEADS * HEAD_DIM
    o = np.cumsum([0, d_qk, d_qk, gq_kv, gq_kv, d_qk, gq_q, 3 * HY_WIDTH])
    ka, va, kb, vb, qa, qb, hy = (w[:, o[i]:o[i + 1]] for i in range(7))
    gates = w[:, o[7]:]
    qscale = (HEAD_DIM ** -0.5) * LOG2E
    t = lambda parts: jnp.concatenate(parts, axis=1).T.astype(BF16)
    w_rope = t([ka, qa])
    g_rope = jnp.concatenate([jnp.ones((d_qk,), F32), jnp.full((d_qk,), qscale, F32)])[:, None]
    w_rms = t([kb, qb])
    g_rms = jnp.concatenate([jnp.tile(gq_kn[l], GKV_HEADS), jnp.tile(gq_qn[l], GQ_HEADS) * qscale])[:, None]
    w_v = t([va, vb])
    g_v = jnp.ones((d_qk + gq_kv, 1), F32)
    return dict(w_rope=w_rope, g_rope=g_rope, w_rms=w_rms, g_rms=g_rms, w_v=w_v, g_v=g_v,
                w_hy=hy.astype(BF16), w_gates=gates.astype(BF16), d_qk=d_qk, gq_kv=gq_kv)


def _project(h, lw, cos_t, sin_t, need_q):
    B, N, D = h.shape
    rope_o = proj_t(lw["w_rope"], h, lw["g_rope"], cos_t, sin_t, rms=False, rope=True)
    rms_o = proj_t(lw["w_rms"], h, lw["g_rms"], cos_t, sin_t, rms=True, rope=True)
    v_o = proj_t(lw["w_v"], h, lw["g_v"], cos_t, sin_t, rms=False, rope=False)
    d_qk, gq_kv = lw["d_qk"], lw["gq_kv"]
    out = dict(ka=rope_o[:, :d_qk], qa=rope_o[:, d_qk:], kb=rms_o[:, :gq_kv], qb=rms_o[:, gq_kv:],
               va=v_o[:, :d_qk], vb=v_o[:, d_qk:])
    if need_q:
        h2 = h.reshape(B * N, D)
        out["u"] = matmul(h2, lw["w_hy"], F32).reshape(B, N, -1)
        out["gates"] = matmul(h2, lw["w_gates"], BF16, act="sigmoid").reshape(B, N, -1)
    return out


def kernel(x, c, ctx, c_ctx, w_mod, b_mod, w_in, da_lq1, da_lk1, da_lq2, da_lk2, da_subln, gq_qn, gq_kn,
           hy_conv_w, hy_conv_b, hy_w1, hy_b1, hy_w2, hy_b2, hy_w3, hy_freq, hy_bias, w_pa, w_pb, w_pc, w_o,
           ln1_g, ln1_b, w_up, ffn_conv_w, ffn_conv_b, w_down, ln2_g, ln2_b):
    B, n_lat, D = x.shape
    n_ctx = ctx.shape[1]
    depth = w_in.shape[0]
    alpha = (2 * depth) ** 0.25
    cos_l, sin_l = _rope_tables_t(n_lat)
    cos_c, sin_c = jnp.ones((HALF, n_ctx), F32), jnp.zeros((HALF, n_ctx), F32)
    tabs_l = _fft_tables(n_lat)
    tabs_c = _fft_tables(n_ctx)
    xc = ctx

    cond = jnp.concatenate([c, c_ctx[None]], axis=0)
    cond = jnp.pad(jax.nn.silu(cond), ((0, 16 - (B + 1) % 16), (0, 0))).astype(BF16)

    for l in range(depth):
        last = l == depth - 1
        lam_init = 0.8 - 0.6 * math.exp(-0.3 * l)
        lam = (jnp.exp(jnp.sum(da_lq1[l] * da_lk1[l])) - jnp.exp(jnp.sum(da_lq2[l] * da_lk2[l])) + lam_init)
        lam = lam.reshape(1).astype(F32)
        mod = matmul(cond, w_mod[l].astype(BF16), F32, tn_pref=1024) + b_mod[l]
        sh1, sc1, g1, sh2, sc2, g2 = [m[:, None, :] for m in jnp.split(mod[:B], 6, axis=-1)]
        mc = [jnp.broadcast_to(m[None, None, :], (B, 1, D)) for m in jnp.split(mod[B], 6)]
        lw = _layer_weights(l, w_in, gq_qn, gq_kn)
        subln = da_subln[l][:, None]
        spectra_l = hyena_filter_spectra(n_lat, hy_w1[l], hy_b1[l], hy_w2[l], hy_b2[l], hy_w3[l], hy_freq[l],
                                         tabs_l, FFT_PASSES)
        wpa, wpb, wpc, wo = (w[l].astype(BF16) for w in (w_pa, w_pb, w_pc, w_o))
        wup, wdn = w_up[l].astype(BF16), w_down[l].astype(BF16)
        ln1 = (ln1_g[l][None], ln1_b[l][None])
        ln2 = (ln2_g[l][None], ln2_b[l][None])

        pc = _project(ln_mod(xc, mc[0], mc[1]), lw, cos_c, sin_c, need_q=not last)
        pl_ = _project(ln_mod(x, sh1, sc1), lw, cos_l, sin_l, need_q=True)

        cat = lambda name: jnp.concatenate([pc[name], pl_[name]], axis=2)
        ka_tok = jnp.swapaxes(cat("ka"), 1, 2)
        kb_tok = jnp.swapaxes(cat("kb"), 1, 2)
        a_l = flash_attention(pl_["qa"], ka_tok, cat("va"), lam, subln, mode="da", post_scale=1.0 - lam_init)
        b_l = flash_attention(pl_["qb"], kb_tok, cat("vb"), lam, subln, mode="gqa")
        c_l = hyena_mix(pl_["u"], spectra_l, hy_conv_w[l], hy_conv_b[l], hy_bias[l], tabs_l, FFT_PASSES)
        x, h2 = merge_resid_ln(a_l, b_l, c_l.astype(BF16), pl_["gates"], wpa, wpb, wpc, wo, x, g1, *ln1,
                               sh2, sc2, alpha)
        up = matmul(h2.reshape(B * n_lat, D), wup, BF16).reshape(B, n_lat, -1)
        act = ffn_conv_act(up, ffn_conv_w[l], ffn_conv_b[l][None])
        x = down_resid_ln(act, wdn, x, g2, *ln2, alpha)

        if not last:
            spectra_c = hyena_filter_spectra(n_ctx, hy_w1[l], hy_b1[l], hy_w2[l], hy_b2[l], hy_w3[l],
                                             hy_freq[l], tabs_c, FFT_PASSES)
            a_c = flash_attention(pc["qa"], jnp.swapaxes(pc["ka"], 1, 2), pc["va"], lam, subln, mode="da",
                                  post_scale=1.0 - lam_init)
            b_c = flash_attention(pc["qb"], jnp.swapaxes(pc["kb"], 1, 2), pc["vb"], lam, subln, mode="gqa")
            c_c = hyena_mix(pc["u"], spectra_c, hy_conv_w[l], hy_conv_b[l], hy_bias[l], tabs_c, FFT_PASSES)
            xc, hc2 = merge_resid_ln(a_c, b_c, c_c.astype(BF16), pc["gates"], wpa, wpb, wpc, wo, xc, mc[2],
                                     *ln1, mc[3], mc[4], alpha)
            upc = matmul(hc2.reshape(B * n_ctx, D), wup, BF16).reshape(B, n_ctx, -1)
            actc = ffn_conv_act(upc, ffn_conv_w[l], ffn_conv_b[l][None])
            xc = down_resid_ln(actc, wdn, xc, mc[5], *ln2, alpha)
    return x
```

```python
import functools
import math

import numpy as np
import jax
import jax.numpy as jnp
from jax import lax
from jax.experimental import pallas as pl
from jax.experimental.pallas import tpu as pltpu

F32 = jnp.float32
BF16 = jnp.bfloat16

HEAD_DIM = 64
HALF = HEAD_DIM // 2
GRID_W = 64
ROPE_THETA = 10000.0
DA_HEADS = 4
GQ_HEADS = 8
GKV_HEADS = 2
HY_WIDTH = 512
HY_ORDER = 2
HY_EMB = 33
HY_EMB_PAD = 128
HY_BANDS = (HY_EMB - 1) // 2
HY_HIDDEN = 64
HY_MIN_DECAY = math.log(1e-2) / 1.5
HY_MAX_DECAY = math.log(1e-2) / 0.3
LN_EPS = 1e-6
LOG2E = 1.4426950408889634
NEG_BIG = -1e30

VMEM_LIMIT = 56 * 1024 * 1024


def _cp(n_axes):
    return pltpu.CompilerParams(dimension_semantics=("arbitrary",) * n_axes,
                                vmem_limit_bytes=VMEM_LIMIT)


def _tile(n, pref, mult):
    if n <= pref:
        return n
    t = (pref // mult) * mult
    while t >= mult:
        if n % t == 0:
            return t
        t -= mult
    return n


def _ln(x):
    mu = jnp.mean(x, axis=-1, keepdims=True)
    xc = x - mu
    return xc * lax.rsqrt(jnp.mean(xc * xc, axis=-1, keepdims=True) + LN_EPS)


def _ln_mod_kernel(x_ref, sh_ref, sc_ref, o_ref):
    o_ref[0] = (_ln(x_ref[0]) * (1.0 + sc_ref[0]) + sh_ref[0]).astype(o_ref.dtype)


def ln_mod(x, shift, scale):
    B, N, D = x.shape
    tm = _tile(N, 512, 8)
    return pl.pallas_call(
        _ln_mod_kernel,
        out_shape=jax.ShapeDtypeStruct((B, N, D), BF16),
        grid=(B, N // tm),
        in_specs=[pl.BlockSpec((1, tm, D), lambda b, i: (b, i, 0)),
                  pl.BlockSpec((1, 1, D), lambda b, i: (b, 0, 0)),
                  pl.BlockSpec((1, 1, D), lambda b, i: (b, 0, 0))],
        out_specs=pl.BlockSpec((1, tm, D), lambda b, i: (b, i, 0)),
        compiler_params=_cp(2), name="ln_mod",
    )(x, shift, scale)


def _resid_ln_tail(x, y, gate, g, b, alpha):
    return _ln(alpha * x + gate * y) * g + b


def _mm_kernel(a_ref, w_ref, o_ref, *, act):
    acc = jnp.dot(a_ref[...], w_ref[...], preferred_element_type=F32)
    if act == "sigmoid":
        acc = jax.nn.sigmoid(acc)
    o_ref[...] = acc.astype(o_ref.dtype)


def matmul(a, w, out_dtype, act=None, tm_pref=1024, tn_pref=512):
    M, K = a.shape
    Nn = w.shape[1]
    tm = _tile(M, tm_pref, 16)
    tn = _tile(Nn, tn_pref, 128)
    return pl.pallas_call(
        functools.partial(_mm_kernel, act=act),
        out_shape=jax.ShapeDtypeStruct((M, Nn), out_dtype),
        grid=(M // tm, Nn // tn),
        in_specs=[pl.BlockSpec((tm, K), lambda i, j: (i, 0)),
                  pl.BlockSpec((K, tn), lambda i, j: (0, j))],
        out_specs=pl.BlockSpec((tm, tn), lambda i, j: (i, j)),
        compiler_params=_cp(2), name="matmul",
    )(a, w)


def _proj_t_kernel(w_ref, h_ref, g_ref, cos_ref, sin_ref, o_ref, *, rms, rope):
    acc = lax.dot_general(w_ref[...], h_ref[0], (((1,), (1,)), ((), ())),
                          preferred_element_type=F32)
    n_heads = acc.shape[0] // HEAD_DIM
    if rope:
        c = cos_ref[...]
        s = sin_ref[...]
    for hd in range(n_heads):
        lo = hd * HEAD_DIM
        x = acc[lo:lo + HEAD_DIM]
        if rms:
            x = x * lax.rsqrt(jnp.mean(x * x, axis=0, keepdims=True) + LN_EPS)
        x = x * g_ref[lo:lo + HEAD_DIM]
        if rope:
            x1, x2 = x[:HALF], x[HALF:]
            o_ref[0, lo:lo + HALF] = (x1 * c - x2 * s).astype(o_ref.dtype)
            o_ref[0, lo + HALF:lo + HEAD_DIM] = (x1 * s + x2 * c).astype(o_ref.dtype)
        else:
            o_ref[0, lo:lo + HEAD_DIM] = x.astype(o_ref.dtype)


def proj_t(w_t, h, gain, cos_t, sin_t, *, rms, rope):
    B, N, K = h.shape
    Fdim = w_t.shape[0]
    tt = _tile(N, 512, 128)
    return pl.pallas_call(
        functools.partial(_proj_t_kernel, rms=rms, rope=rope),
        out_shape=jax.ShapeDtypeStruct((B, Fdim, N), BF16),
        grid=(B, N // tt),
        in_specs=[pl.BlockSpec((Fdim, K), lambda b, i: (0, 0)),
                  pl.BlockSpec((1, tt, K), lambda b, i: (b, i, 0)),
                  pl.BlockSpec((Fdim, 1), lambda b, i: (0, 0)),
                  pl.BlockSpec((HALF, tt), lambda b, i: (0, i)),
                  pl.BlockSpec((HALF, tt), lambda b, i: (0, i))],
        out_specs=pl.BlockSpec((1, Fdim, tt), lambda b, i: (b, 0, i)),
        compiler_params=_cp(2), name="proj_t",
    )(w_t, h, gain, cos_t, sin_t)


V_PAD = 16


def _flash_kernel(lam_ref, q_ref, k_ref, v_ref, g_ref, o_ref, q_sc, s_buf, p_buf, al_buf, m_sc, acc_sc,
                  *, mode, nj, dv, post_scale):
    qf = q_ref[0].astype(F32)
    tq = qf.shape[1]
    zero = jnp.zeros((HEAD_DIM, tq), F32)
    qa, qb = qf[:HEAD_DIM], qf[HEAD_DIM:]
    if mode == "da":
        q_sc[0] = jnp.concatenate([qa, zero], axis=0).astype(BF16)
        q_sc[1] = jnp.concatenate([zero, qb], axis=0).astype(BF16)
    else:
        first_group = (pl.program_id(1) // 2) == 0
        for mi, qh in enumerate((qa, qb)):
            q_sc[mi] = jnp.where(first_group, jnp.concatenate([qh, zero], axis=0),
                                 jnp.concatenate([zero, qh], axis=0)).astype(BF16)
    m_sc[...] = jnp.full(m_sc.shape, NEG_BIG, F32)
    acc_sc[...] = jnp.zeros(acc_sc.shape, F32)

    def stage_scores(j, slot):
        kb = k_ref[0, j]
        for mi in range(2):
            s_buf[slot, mi] = jnp.dot(kb, q_sc[mi], preferred_element_type=F32)

    def stage_softmax(slot):
        for mi in range(2):
            s = s_buf[slot, mi]
            m_old = m_sc[mi]
            m_new = jnp.maximum(m_old, jnp.max(s, axis=0, keepdims=True))
            p_buf[slot, mi] = jnp.exp2(s - m_new).astype(BF16)
            al_buf[slot, mi] = jnp.exp2(m_old - m_new)
            m_sc[mi] = m_new

    def stage_values(j, slot):
        vb = v_ref[0, j]
        for mi in range(2):
            acc_sc[mi] = al_buf[slot, mi] * acc_sc[mi] + jnp.dot(vb, p_buf[slot, mi],
                                                                 preferred_element_type=F32)

    def tick(t, parity, scores=True, softmax=True, values=True):
        if scores:
            stage_scores(t, parity)
        if softmax:
            stage_softmax(1 - parity)
        if values:
            stage_values(t - 2, parity)

    steady = list(range(2, nj))
    for t in range(0, min(2, nj + 2)):
        tick(t, t % 2, scores=t < nj, softmax=0 <= t - 1 < nj, values=False)
    if len(steady) % 2 == 1:
        t = steady.pop(0)
        tick(t, t % 2)
    if steady:
        t0 = steady[0]

        def body(i, carry):
            t = t0 + 2 * i
            tick(t, t0 % 2)
            tick(t + 1, 1 - t0 % 2)
            return carry

        lax.fori_loop(0, len(steady) // 2, body, 0)
    for t in range(max(nj, 2), nj + 2):
        tick(t, t % 2, scores=False, softmax=t - 1 < nj, values=True)

    outs = []
    for mi in range(2):
        acc = acc_sc[mi]
        outs.append(acc[:dv] / acc[dv:dv + 1])
    if mode == "da":
        o = outs[0] - lam_ref[0] * outs[1]
        o = o * lax.rsqrt(jnp.mean(o * o, axis=0, keepdims=True) + LN_EPS)
        o = o * (g_ref[...] * post_scale)
    else:
        o = jnp.concatenate(outs, axis=0)
    o_ref[0] = o.T.astype(o_ref.dtype)


def flash_attention(q_t, k_tok, v_t, lam, gain, *, mode, post_scale=1.0, q_row_block=0, tk_pref=768):
    B, _, Nq = q_t.shape
    Nk = k_tok.shape[1]
    tq = _tile(Nq, 512, 128)
    tk = _tile(Nk, tk_pref, 128)
    nj = Nk // tk
    dv = 2 * HEAD_DIM if mode == "da" else HEAD_DIM
    dvp = dv + V_PAD
    heads = v_t.shape[1] // dv
    ones_rows = jnp.zeros((B, heads, V_PAD, Nk), BF16).at[:, :, 0].set(1.0)
    v_aug = jnp.concatenate([v_t.reshape(B, heads, dv, Nk), ones_rows], axis=2)
    v4 = jnp.swapaxes(v_aug.reshape(B, heads * dvp, nj, tk), 1, 2)
    k4 = k_tok.reshape(B, nj, tk, k_tok.shape[2])
    if mode == "da":
        k_map = lambda b, h, i: (b, 0, 0, h)
        v_map = lambda b, h, i: (b, 0, h, 0)
    else:
        k_map = lambda b, h, i: (b, 0, 0, 0)
        v_map = lambda b, h, i: (b, 0, h // 2, 0)
    return pl.pallas_call(
        functools.partial(_flash_kernel, mode=mode, nj=nj, dv=dv, post_scale=post_scale),
        out_shape=jax.ShapeDtypeStruct((B, Nq, 4 * 128), BF16),
        grid=(B, 4, Nq // tq),
        in_specs=[pl.BlockSpec(memory_space=pltpu.SMEM),
                  pl.BlockSpec((1, 128, tq), lambda b, h, i: (b, h + q_row_block, i)),
                  pl.BlockSpec((1, nj, tk, 128), k_map),
                  pl.BlockSpec((1, nj, dvp, tk), v_map),
                  pl.BlockSpec((128, 1), lambda b, h, i: (0, 0))],
        out_specs=pl.BlockSpec((1, tq, 128), lambda b, h, i: (b, i, h)),
        scratch_shapes=[pltpu.VMEM((2, 128, tq), BF16),
                        pltpu.VMEM((2, 2, tk, tq), F32),
                        pltpu.VMEM((2, 2, tk, tq), BF16),
                        pltpu.VMEM((2, 2, 1, tq), F32),
                        pltpu.VMEM((2, 1, tq), F32),
                        pltpu.VMEM((2, dvp, tq), F32)],
        compiler_params=_cp(3), name="flash_" + mode,
    )(lam, q_t, k4, v4, gain)


def _shift_rows(x, prev_row, next_row):
    T = x.shape[0]
    row = lax.broadcasted_iota(jnp.int32, x.shape, 0)
    xp = jnp.where(row == 0, prev_row, pltpu.roll(x, 1, axis=0))
    xn = jnp.where(row == T - 1, next_row, pltpu.roll(x, T - 1, axis=0))
    return xp, xn


def _conv3_block(x_ref, p_ref, n_ref, w_ref, b_ref, halo):
    i = pl.program_id(1)
    last = pl.num_programs(1) - 1
    x = x_ref[0].astype(F32)
    prev_row = jnp.where(i > 0, p_ref[0].astype(F32)[halo - 1:halo], 0.0)
    next_row = jnp.where(i < last, n_ref[0].astype(F32)[0:1], 0.0)
    xp, xn = _shift_rows(x, prev_row, next_row)
    return xp * w_ref[0:1] + x * w_ref[1:2] + xn * w_ref[2:3] + b_ref[...]


def _halo_specs(tr, tc, halo, n_rows, col_of):
    per = tr // halo
    n_halo = n_rows // halo
    return [pl.BlockSpec((1, tr, tc), lambda b, i, j: (b, i, col_of(j))),
            pl.BlockSpec((1, halo, tc), lambda b, i, j: (b, jnp.maximum(i * per - 1, 0), col_of(j))),
            pl.BlockSpec((1, halo, tc), lambda b, i, j: (b, jnp.minimum((i + 1) * per, n_halo - 1), col_of(j)))]


def _dwconv_kernel(x_ref, p_ref, n_ref, w_ref, b_ref, v_ref, x1_ref, x2_ref):
    y = _conv3_block(x_ref, p_ref, n_ref, w_ref, b_ref, 8)
    C = v_ref.shape[2]
    v_ref[0] = y[:, :C]
    x1_ref[0] = y[:, C:2 * C]
    x2_ref[0] = y[:, 2 * C:]


def hyena_short_conv(u, w, b):
    B, L, C3 = u.shape
    C = C3 // 3
    tr = _tile(L, 512, 8)
    zero = lambda j: 0
    out = jax.ShapeDtypeStruct((B, L, C), F32)
    ospec = pl.BlockSpec((1, tr, C), lambda b, i, j: (b, i, 0))
    return pl.pallas_call(
        _dwconv_kernel,
        out_shape=(out, out, out),
        grid=(B, L // tr, 1),
        in_specs=_halo_specs(tr, C3, 8, L, zero) + [
            pl.BlockSpec((3, C3), lambda b, i, j: (0, 0)),
            pl.BlockSpec((1, C3), lambda b, i, j: (0, 0))],
        out_specs=(ospec, ospec, ospec),
        compiler_params=_cp(3), name="hyena_short_conv",
    )(u, u, u, w, b)


def _convact_kernel(xv_ref, pv_ref, nv_ref, xg_ref, pg_ref, ng_ref, wv_ref, bv_ref, wg_ref, bg_ref, o_ref):
    val = _conv3_block(xv_ref, pv_ref, nv_ref, wv_ref, bv_ref, 16)
    gate = _conv3_block(xg_ref, pg_ref, ng_ref, wg_ref, bg_ref, 16)
    o_ref[0] = (gate * jax.nn.sigmoid(gate) * val).astype(o_ref.dtype)


def ffn_conv_act(up, w, b):
    B, N, D2 = up.shape
    Dff = D2 // 2
    tc = _tile(Dff, 1408, 128)
    nc = Dff // tc
    tr = _tile(N, 256, 16)
    val_col = lambda j: j
    gate_col = lambda j: j + nc
    wspec = lambda col: [pl.BlockSpec((3, tc), lambda b_, i, j: (0, col(j))),
                         pl.BlockSpec((1, tc), lambda b_, i, j: (0, col(j)))]
    return pl.pallas_call(
        _convact_kernel,
        out_shape=jax.ShapeDtypeStruct((B, N, Dff), BF16),
        grid=(B, N // tr, nc),
        in_specs=_halo_specs(tr, tc, 16, N, val_col) + _halo_specs(tr, tc, 16, N, gate_col)
        + wspec(val_col) + wspec(gate_col),
        out_specs=pl.BlockSpec((1, tr, tc), lambda b_, i, j: (b_, i, j)),
        compiler_params=_cp(3), name="ffn_conv_act",
    )(up, up, up, up, up, up, w, b, w, b)


def _down_resid_kernel(a_ref, w_ref, x_ref, gate_ref, g_ref, b_ref, o_ref, *, alpha):
    y = jnp.dot(a_ref[0], w_ref[...], preferred_element_type=F32)
    o_ref[0] = _resid_ln_tail(x_ref[0], y, gate_ref[0], g_ref[...], b_ref[...], alpha)


def down_resid_ln(a, w, x, gate, g, b, alpha):
    B, N, K = a.shape
    D = w.shape[1]
    tm = _tile(N, 512, 16)
    row = lambda b_, i: (b_, i, 0)
    per_b = lambda b_, i: (b_, 0, 0)
    const2 = lambda b_, i: (0, 0)
    return pl.pallas_call(
        functools.partial(_down_resid_kernel, alpha=alpha),
        out_shape=jax.ShapeDtypeStruct((B, N, D), F32),
        grid=(B, N // tm),
        in_specs=[pl.BlockSpec((1, tm, K), row), pl.BlockSpec((K, D), const2),
                  pl.BlockSpec((1, tm, D), row), pl.BlockSpec((1, 1, D), per_b),
                  pl.BlockSpec((1, D), const2), pl.BlockSpec((1, D), const2)],
        out_specs=pl.BlockSpec((1, tm, D), row),
        compiler_params=_cp(2), name="down_resid_ln",
    )(a, w, x, gate, g, b)


def _merge_kernel(a_ref, b_ref, c_ref, gt_ref, wpa_ref, wpb_ref, wpc_ref, wo_ref, x_ref, gate_ref,
                  g_ref, bb_ref, sh_ref, sc_ref, ox_ref, oh_ref, *, alpha):
    D = wo_ref.shape[0]
    gt = gt_ref[0]
    m = gt[:, :D].astype(F32) * jnp.dot(a_ref[0], wpa_ref[...], preferred_element_type=F32)
    m = m + gt[:, D:2 * D].astype(F32) * jnp.dot(b_ref[0], wpb_ref[...], preferred_element_type=F32)
    m = m + gt[:, 2 * D:].astype(F32) * jnp.dot(c_ref[0].astype(BF16), wpc_ref[...],
                                                preferred_element_type=F32)
    y = jnp.dot(m.astype(BF16), wo_ref[...], preferred_element_type=F32)
    xn = _resid_ln_tail(x_ref[0], y, gate_ref[0], g_ref[...], bb_ref[...], alpha)
    ox_ref[0] = xn
    oh_ref[0] = (_ln(xn) * (1.0 + sc_ref[0]) + sh_ref[0]).astype(oh_ref.dtype)


def merge_resid_ln(a, b, c, gates, wpa, wpb, wpc, wo, x, gate, g, bb, sh, sc, alpha):
    B, N, D = x.shape
    W = a.shape[2]
    tm = _tile(N, 512, 16)
    row = lambda b_, i: (b_, i, 0)
    per_b = lambda b_, i: (b_, 0, 0)
    const2 = lambda b_, i: (0, 0)
    return pl.pallas_call(
        functools.partial(_merge_kernel, alpha=alpha),
        out_shape=(jax.ShapeDtypeStruct((B, N, D), F32), jax.ShapeDtypeStruct((B, N, D), BF16)),
        grid=(B, N // tm),
        in_specs=[pl.BlockSpec((1, tm, W), row), pl.BlockSpec((1, tm, W), row), pl.BlockSpec((1, tm, W), row),
                  pl.BlockSpec((1, tm, 3 * D), row),
                  pl.BlockSpec((W, D), const2), pl.BlockSpec((W, D), const2), pl.BlockSpec((W, D), const2),
                  pl.BlockSpec((D, D), const2),
                  pl.BlockSpec((1, tm, D), row), pl.BlockSpec((1, 1, D), per_b),
                  pl.BlockSpec((1, D), const2), pl.BlockSpec((1, D), const2),
                  pl.BlockSpec((1, 1, D), per_b), pl.BlockSpec((1, 1, D), per_b)],
        out_specs=(pl.BlockSpec((1, tm, D), row), pl.BlockSpec((1, tm, D), row)),
        compiler_params=_cp(2), name="merge_resid_ln",
    )(a, b, c, gates, wpa, wpb, wpc, wo, x, gate, g, bb, sh, sc)


def _split_bf16(x):
    hi = x.astype(BF16)
    lo = (x - hi.astype(F32)).astype(BF16)
    return hi, lo


def _dot3(a, b):
    ah, al = _split_bf16(a)
    bh, bl = _split_bf16(b)
    d = functools.partial(jnp.dot, preferred_element_type=F32)
    return d(ah, bh) + (d(ah, bl) + d(al, bh))


def _filter_kernel(z_ref, w1_ref, b1_ref, w2_ref, b2_ref, w3_ref, fr_ref, dec_ref, h_ref, s_ref, *, zero_block):
    i = pl.program_id(0)
    fr = fr_ref[...]
    hid = jnp.sin(fr * (_dot3(z_ref[...], w1_ref[...]) + b1_ref[...]))
    hid = jnp.sin(fr * (_dot3(hid, w2_ref[...]) + b2_ref[...]))
    h = _dot3(hid, w3_ref[0])
    dec = dec_ref[...]
    C = dec.shape[1]
    n_ord = h.shape[1] // C
    h = h * jnp.concatenate([dec] * n_ord, axis=1)

    @pl.when(i == 0)
    def _():
        s_ref[...] = jnp.zeros(s_ref.shape, F32)

    s_ref[...] += jnp.sum(jnp.abs(h), axis=0, keepdims=True)
    row = lax.broadcasted_iota(jnp.int32, h.shape, 0)
    h = jnp.where((row == 0) & (i == zero_block), 0.0, h)
    for o in range(n_ord):
        h_ref[o] = h[:, o * C:(o + 1) * C]


def hyena_filter_mlp(z2, w1, b1, w2, b2, w3_dir, freq, decay2):
    n2, E = z2.shape
    Hh = w2.shape[0]
    OC = w3_dir.shape[2]
    C = decay2.shape[1]
    n_ord = OC // C
    L = n2 // 2
    tr = _tile(L, 512, 8)
    nblk = n2 // tr
    c2 = lambda i: (0, 0)
    return pl.pallas_call(
        functools.partial(_filter_kernel, zero_block=L // tr),
        out_shape=(jax.ShapeDtypeStruct((n_ord, n2, C), F32), jax.ShapeDtypeStruct((1, OC), F32)),
        grid=(nblk,),
        in_specs=[pl.BlockSpec((tr, E), lambda i: (i, 0)), pl.BlockSpec((E, Hh), c2), pl.BlockSpec((1, Hh), c2),
                  pl.BlockSpec((Hh, Hh), c2), pl.BlockSpec((1, Hh), c2),
                  pl.BlockSpec((1, Hh, OC), lambda i: (i // (nblk // 2), 0, 0)),
                  pl.BlockSpec((1, Hh), c2), pl.BlockSpec((tr, C), lambda i: (i, 0))],
        out_specs=(pl.BlockSpec((n_ord, tr, C), lambda i: (0, i, 0)), pl.BlockSpec((1, OC), c2)),
        compiler_params=_cp(1), name="hyena_filter_mlp",
    )(z2, w1, b1, w2, b2, w3_dir, freq, decay2)


def _mm_split(m_hi, m_lo, x, passes):
    d = functools.partial(jnp.dot, preferred_element_type=F32)
    if passes == 1:
        return d(m_hi, x.astype(BF16))
    xh, xl = _split_bf16(x)
    return d(m_hi, xh) + (d(m_lo, xh) + d(m_hi, xl))


FFT_ROWS = 8


def _level1_kernel(mh_ref, ml_ref, x_ref, o_ref, *, passes):
    r_in, r_out = x_ref.shape[1], o_ref.shape[1]
    x = x_ref[0].reshape(r_in * FFT_ROWS, x_ref.shape[3])
    y = _mm_split(mh_ref[...], ml_ref[...], x, passes)
    o_ref[0] = y.reshape(r_out, FFT_ROWS, x_ref.shape[3]).astype(o_ref.dtype)


def _level1_gate_kernel(mh_ref, ml_ref, x_ref, z_ref, xg_ref, bias_ref, o_ref, *, passes):
    r_in, r_out = x_ref.shape[1], o_ref.shape[1]
    x = x_ref[0].reshape(r_in * FFT_ROWS, x_ref.shape[3])
    y = _mm_split(mh_ref[...], ml_ref[...], x, passes).reshape(r_out, FFT_ROWS, x_ref.shape[3])
    o_ref[0] = (xg_ref[0] * (y + bias_ref[...] * z_ref[0])).astype(o_ref.dtype)


def fft_level1(m_hi, m_lo, x, passes, gate_args=None, out_dtype=F32):
    P, R_in, Nb, C = x.shape
    R_out = m_hi.shape[0] // FFT_ROWS
    mspec = pl.BlockSpec(m_hi.shape, lambda p, j: (0, 0))
    xspec = pl.BlockSpec((1, R_in, FFT_ROWS, C), lambda p, j: (p, 0, j, 0))
    ospec = pl.BlockSpec((1, R_out, FFT_ROWS, C), lambda p, j: (p, 0, j, 0))
    if gate_args is None:
        kern = functools.partial(_level1_kernel, passes=passes)
        ins, specs = (m_hi, m_lo, x), [mspec, mspec, xspec]
    else:
        z, xg, bias = gate_args
        kern = functools.partial(_level1_gate_kernel, passes=passes)
        ins = (m_hi, m_lo, x, z, xg, bias)
        specs = [mspec, mspec, xspec, ospec, ospec, pl.BlockSpec((1, C), lambda p, j: (0, 0))]
    return pl.pallas_call(
        kern, out_shape=jax.ShapeDtypeStruct((P, R_out, Nb, C), out_dtype),
        grid=(P, Nb // FFT_ROWS), in_specs=specs, out_specs=ospec,
        compiler_params=_cp(2), name="fft_level1",
    )(*ins)


def _fft_mid_kernel(gh_ref, gl_ref, gih_ref, gil_ref, h_ref, a_ref, o_ref, *, passes):
    nb = a_ref.shape[3]
    x = a_ref[0, :, 0].reshape(2 * nb, a_ref.shape[4])
    X = _mm_split(gh_ref[0], gl_ref[0], x, passes)
    xr, xi = X[:nb], X[nb:]
    hr, hi = h_ref[0, 0], h_ref[1, 0]
    Y = jnp.concatenate([xr * hr - xi * hi, xr * hi + xi * hr], axis=0)
    Bv = _mm_split(gih_ref[0], gil_ref[0], Y, passes)
    o_ref[0, :, 0] = Bv.reshape(2, nb, a_ref.shape[4])


def fft_mid(g_hi, g_lo, gi_hi, gi_lo, hspec, a5, passes):
    P, _, Na, Nb, C = a5.shape
    gspec = pl.BlockSpec((1, 2 * Nb, 2 * Nb), lambda k, p: (k, 0, 0))
    return pl.pallas_call(
        functools.partial(_fft_mid_kernel, passes=passes),
        out_shape=jax.ShapeDtypeStruct(a5.shape, F32),
        grid=(Na, P),
        in_specs=[gspec, gspec, gspec, gspec,
                  pl.BlockSpec((2, 1, Nb, C), lambda k, p: (0, k, 0, 0)),
                  pl.BlockSpec((1, 2, 1, Nb, C), lambda k, p: (p, 0, k, 0, 0))],
        out_specs=pl.BlockSpec((1, 2, 1, Nb, C), lambda k, p: (p, 0, k, 0, 0)),
        compiler_params=_cp(2), name="fft_mid",
    )(g_hi, g_lo, gi_hi, gi_lo, hspec, a5)


def _fft_spec_kernel(gh_ref, gl_ref, inv_ref, a_ref, o_ref, *, passes):
    nb = a_ref.shape[3]
    x = a_ref[0, :, 0].reshape(2 * nb, a_ref.shape[4])
    X = _mm_split(gh_ref[0], gl_ref[0], x, passes) * inv_ref[0]
    o_ref[0, :, 0] = X.reshape(2, nb, a_ref.shape[4])


def fft_filter_spectrum(g_hi, g_lo, inv_norm, a5, passes):
    P, _, Na, Nb, C = a5.shape
    gspec = pl.BlockSpec((1, 2 * Nb, 2 * Nb), lambda k, p: (k, 0, 0))
    blk = pl.BlockSpec((1, 2, 1, Nb, C), lambda k, p: (p, 0, k, 0, 0))
    return pl.pallas_call(
        functools.partial(_fft_spec_kernel, passes=passes),
        out_shape=jax.ShapeDtypeStruct(a5.shape, F32),
        grid=(Na, P),
        in_specs=[gspec, gspec, pl.BlockSpec((1, 1, C), lambda k, p: (p, 0, 0)), blk],
        out_specs=blk,
        compiler_params=_cp(2), name="fft_filter_spectrum",
    )(g_hi, g_lo, inv_norm, a5)


def _fft_factors(n_fft):
    na = 1 << (int(math.log2(n_fft)) // 2)
    return na, n_fft // na


def _fft_tables(L):
    n_fft = 2 * L
    na, nb = _fft_factors(n_fft)

    def cis(num, den):
        ang = (2.0 * math.pi / den) * (num % den).astype(F32)
        return jnp.cos(ang), -jnp.sin(ang)

    ia = jnp.arange(na, dtype=jnp.int32)
    ib = jnp.arange(nb, dtype=jnp.int32)
    far, fai = cis(ia[:, None] * ia[None, :], na)
    fh_r, fh_i = far[:, :na // 2], fai[:, :na // 2]
    m_fwd = jnp.block([[fh_r, -fh_i], [fh_i, fh_r]])
    m_flt = jnp.concatenate([far, fai], axis=0)
    m_inv = jnp.block([[fh_r.T, fh_i.T], [-fh_i.T, fh_r.T]]) / n_fft
    num = ib[None, :, None] * ib[None, None, :] * na + ia[:, None, None] * ib[None, None, :]
    gr, gi = cis(num, n_fft)
    g = jnp.concatenate([jnp.concatenate([gr, -gi], axis=2), jnp.concatenate([gi, gr], axis=2)], axis=1)
    grt, git = jnp.swapaxes(gr, 1, 2), jnp.swapaxes(gi, 1, 2)
    ginv = jnp.concatenate([jnp.concatenate([grt, git], axis=2), jnp.concatenate([-git, grt], axis=2)], axis=1)
    split = lambda m: _split_bf16(m.astype(F32))
    kron = lambda m: split(jnp.kron(m, jnp.eye(FFT_ROWS, dtype=F32)))
    C = HY_WIDTH
    t = jnp.linspace(0.0, 1.0, L, dtype=F32)[:, None]
    f = jnp.linspace(1e-4, HY_BANDS - 1, HY_BANDS, dtype=F32)
    ang = (2.0 * math.pi / L) * jnp.arange(L, dtype=F32)[:, None] * f[None, :]
    z = jnp.concatenate([t, jnp.cos(ang), -jnp.sin(ang)], axis=-1)
    z = jnp.pad(z, ((0, 0), (0, HY_EMB_PAD - HY_EMB)))
    deltas = jnp.abs(jnp.linspace(HY_MIN_DECAY, HY_MAX_DECAY, C, dtype=F32))
    decay = jnp.exp(-t * deltas)
    circ = lambda a: jnp.concatenate([a, a[:1], a[:0:-1]], axis=0)
    return dict(na=na, nb=nb, fwd=kron(m_fwd), flt=kron(m_flt), inv=kron(m_inv), g=split(g), ginv=split(ginv),
                z2=circ(z), decay2=circ(decay))


def hyena_filter_spectra(w1, b1, w2, b2, w3, freq, tabs, passes):
    C = HY_WIDTH
    na, nb = tabs["na"], tabs["nb"]
    w1p = jnp.pad(w1, ((0, HY_EMB_PAD - HY_EMB), (0, 0)))
    w3_dir = jnp.transpose(w3.reshape(HY_HIDDEN, HY_ORDER, 2, C), (2, 0, 1, 3)).reshape(2, HY_HIDDEN, HY_ORDER * C)
    filt, s = hyena_filter_mlp(tabs["z2"], w1p, b1[None], w2, b2[None], w3_dir, freq[None], tabs["decay2"])
    inv_norm = (1.0 / s).reshape(HY_ORDER, 1, C)
    a = fft_level1(*tabs["flt"], filt.reshape(HY_ORDER, na, nb, C), passes)
    return fft_filter_spectrum(*tabs["g"], inv_norm, a.reshape(HY_ORDER, 2, na, nb, C), passes)


def hyena_long_conv_gate(z, xg, bias, hspec, tabs, passes):
    B, L, C = z.shape
    na, nb = tabs["na"], tabs["nb"]
    P = B // 2
    nat = lambda a: a.reshape(P, na, nb, C)
    a = fft_level1(*tabs["fwd"], nat(z), passes)
    bv = fft_mid(*tabs["g"], *tabs["ginv"], hspec, a.reshape(P, 2, na, nb, C), passes)
    y = fft_level1(*tabs["inv"], bv.reshape(P, 2 * na, nb, C), passes,
                   gate_args=(nat(z), nat(xg), bias[None]))
    return y.reshape(B, L, C)


def hyena_mix(u, spectra, conv_w, conv_b, bias, tabs, passes):
    v, x1, x2 = hyena_short_conv(u, conv_w, conv_b[None])
    zz = hyena_long_conv_gate(v, x1, bias[0], spectra[0], tabs, passes)
    return hyena_long_conv_gate(zz, x2, bias[1], spectra[1], tabs, passes)


FFT_PASSES = 1
FILTER_PASSES = 3


def _rope_tables_t(n):
    rows = n // GRID_W
    r = jnp.repeat(jnp.arange(rows, dtype=F32), GRID_W)
    col = jnp.tile(jnp.arange(GRID_W, dtype=F32), rows)
    axis_dim = HEAD_DIM // 2
    inv = ROPE_THETA ** (-jnp.arange(0, axis_dim, 2, dtype=F32) / axis_dim)
    ang = jnp.concatenate([r[:, None] * inv, col[:, None] * inv], axis=-1)
    return jnp.cos(ang).T, jnp.sin(ang).T


def _layer_weights(l, w_in, gq_qn, gq_kn):
    w = w_in[l]
    d_qk = DA_HEADS * 2 * HEAD_DIM
    gq_kv = GKV_HEADS * HEAD_DIM
    gq_q = GQ_HEADS * HEAD_DIM
    o = np.cumsum([0, d_qk, d_qk, gq_kv, gq_kv, d_qk, gq_q, 3 * HY_WIDTH])
    ka, va, kb, vb, qa, qb, hy = (w[:, o[i]:o[i + 1]] for i in range(7))
    gates = w[:, o[7]:]
    qscale = (HEAD_DIM ** -0.5) * LOG2E
    t = lambda parts: jnp.concatenate(parts, axis=1).T.astype(BF16)
    w_rope = t([ka, qa])
    g_rope = jnp.concatenate([jnp.ones((d_qk,), F32), jnp.full((d_qk,), qscale, F32)])[:, None]
    w_rms = t([kb, qb])
    g_rms = jnp.concatenate([jnp.tile(gq_kn[l], GKV_HEADS), jnp.tile(gq_qn[l], GQ_HEADS) * qscale])[:, None]
    w_v = t([va, vb])
    g_v = jnp.ones((d_qk + gq_kv, 1), F32)
    return dict(w_rope=w_rope, g_rope=g_rope, w_rms=w_rms, g_rms=g_rms, w_v=w_v, g_v=g_v,
                w_hy=hy.astype(BF16), w_gates=gates.astype(BF16), d_qk=d_qk, gq_kv=gq_kv)


def _project(h, lw, cos_t, sin_t, need_q):
    B, N, D = h.shape
    rope_o = proj_t(lw["w_rope"], h, lw["g_rope"], cos_t, sin_t, rms=False, rope=True)
    rms_o = proj_t(lw["w_rms"], h, lw["g_rms"], cos_t, sin_t, rms=True, rope=True)
    v_o = proj_t(lw["w_v"], h, lw["g_v"], cos_t, sin_t, rms=False, rope=False)
    d_qk, gq_kv = lw["d_qk"], lw["gq_kv"]
    out = dict(ka=rope_o[:, :d_qk], qa=rope_o, qa_blk=d_qk // 128, kb=rms_o[:, :gq_kv], qb=rms_o,
               qb_blk=gq_kv // 128, va=v_o[:, :d_qk], vb=v_o[:, d_qk:])
    if need_q:
        h2 = h.reshape(B * N, D)
        out["u"] = matmul(h2, lw["w_hy"], F32).reshape(B, N, -1)
        out["gates"] = matmul(h2, lw["w_gates"], BF16, act="sigmoid").reshape(B, N, -1)
    return out


def kernel(x, c, ctx, c_ctx, w_mod, b_mod, w_in, da_lq1, da_lk1, da_lq2, da_lk2, da_subln, gq_qn, gq_kn,
           hy_conv_w, hy_conv_b, hy_w1, hy_b1, hy_w2, hy_b2, hy_w3, hy_freq, hy_bias, w_pa, w_pb, w_pc, w_o,
           ln1_g, ln1_b, w_up, ffn_conv_w, ffn_conv_b, w_down, ln2_g, ln2_b):
    B, n_lat, D = x.shape
    n_ctx = ctx.shape[1]
    depth = w_in.shape[0]
    alpha = (2 * depth) ** 0.25
    cos_l, sin_l = _rope_tables_t(n_lat)
    cos_c, sin_c = jnp.ones((HALF, n_ctx), F32), jnp.zeros((HALF, n_ctx), F32)
    tabs_l = _fft_tables(n_lat)
    tabs_c = _fft_tables(n_ctx)
    xc = ctx

    cond = jnp.concatenate([c, c_ctx[None]], axis=0)
    cond = jnp.pad(jax.nn.silu(cond), ((0, 16 - (B + 1) % 16), (0, 0))).astype(BF16)

    for l in range(depth):
        last = l == depth - 1
        lam_init = 0.8 - 0.6 * math.exp(-0.3 * l)
        lam = (jnp.exp(jnp.sum(da_lq1[l] * da_lk1[l])) - jnp.exp(jnp.sum(da_lq2[l] * da_lk2[l])) + lam_init)
        lam = lam.reshape(1).astype(F32)
        mod = matmul(cond, w_mod[l].astype(BF16), F32, tn_pref=1024) + b_mod[l]
        sh1, sc1, g1, sh2, sc2, g2 = [m[:, None, :] for m in jnp.split(mod[:B], 6, axis=-1)]
        mc = [jnp.broadcast_to(m[None, None, :], (B, 1, D)) for m in jnp.split(mod[B], 6)]
        lw = _layer_weights(l, w_in, gq_qn, gq_kn)
        subln = da_subln[l][:, None]
        hy_mlp = (hy_w1[l], hy_b1[l], hy_w2[l], hy_b2[l], hy_w3[l], hy_freq[l])
        spectra_l = hyena_filter_spectra(*hy_mlp, tabs_l, FILTER_PASSES)
        wpa, wpb, wpc, wo = (w[l].astype(BF16) for w in (w_pa, w_pb, w_pc, w_o))
        wup, wdn = w_up[l].astype(BF16), w_down[l].astype(BF16)
        ln1 = (ln1_g[l][None], ln1_b[l][None])
        ln2 = (ln2_g[l][None], ln2_b[l][None])

        pc = _project(ln_mod(xc, mc[0], mc[1]), lw, cos_c, sin_c, need_q=not last)
        pl_ = _project(ln_mod(x, sh1, sc1), lw, cos_l, sin_l, need_q=True)

        cat = lambda name: jnp.concatenate([pc[name], pl_[name]], axis=2)
        ka_tok = jnp.swapaxes(cat("ka"), 1, 2)
        kb_tok = jnp.swapaxes(cat("kb"), 1, 2)
        a_l = flash_attention(pl_["qa"], ka_tok, cat("va"), lam, subln, mode="da", post_scale=1.0 - lam_init,
                              q_row_block=pl_["qa_blk"])
        b_l = flash_attention(pl_["qb"], kb_tok, cat("vb"), lam, subln, mode="gqa", q_row_block=pl_["qb_blk"])
        c_l = hyena_mix(pl_["u"], spectra_l, hy_conv_w[l], hy_conv_b[l], hy_bias[l], tabs_l, FFT_PASSES)
        x, h2 = merge_resid_ln(a_l, b_l, c_l, pl_["gates"], wpa, wpb, wpc, wo, x, g1, *ln1, sh2, sc2, alpha)
        up = matmul(h2.reshape(B * n_lat, D), wup, BF16).reshape(B, n_lat, -1)
        act = ffn_conv_act(up, ffn_conv_w[l], ffn_conv_b[l][None])
        x = down_resid_ln(act, wdn, x, g2, *ln2, alpha)

        if not last:
            spectra_c = hyena_filter_spectra(*hy_mlp, tabs_c, FILTER_PASSES)
            a_c = flash_attention(pc["qa"], jnp.swapaxes(pc["ka"], 1, 2), pc["va"], lam, subln, mode="da",
                                  post_scale=1.0 - lam_init, q_row_block=pc["qa_blk"])
            b_c = flash_attention(pc["qb"], jnp.swapaxes(pc["kb"], 1, 2), pc["vb"], lam, subln, mode="gqa",
                                  q_row_block=pc["qb_blk"])
            c_c = hyena_mix(pc["u"], spectra_c, hy_conv_w[l], hy_conv_b[l], hy_bias[l], tabs_c, FFT_PASSES)
            xc, hc2 = merge_resid_ln(a_c, b_c, c_c, pc["gates"], wpa, wpb, wpc, wo, xc, mc[2],
                                     *ln1, mc[3], mc[4], alpha)
            upc = matmul(hc2.reshape(B * n_ctx, D), wup, BF16).reshape(B, n_ctx, -1)
            actc = ffn_conv_act(upc, ffn_conv_w[l], ffn_conv_b[l][None])
            xc = down_resid_ln(actc, wdn, xc, mc[5], *ln2, alpha)
    return x
```

```python
import functools
import math

import numpy as np
import jax
import jax.numpy as jnp
from jax import lax
from jax.experimental import pallas as pl
from jax.experimental.pallas import tpu as pltpu

F32 = jnp.float32
BF16 = jnp.bfloat16

HEAD_DIM = 64
HALF = HEAD_DIM // 2
GRID_W = 64
ROPE_THETA = 10000.0
DA_HEADS = 4
GQ_HEADS = 8
GKV_HEADS = 2
HY_WIDTH = 512
HY_ORDER = 2
HY_EMB = 33
HY_EMB_PAD = 128
HY_BANDS = (HY_EMB - 1) // 2
HY_HIDDEN = 64
HY_MIN_DECAY = math.log(1e-2) / 1.5
HY_MAX_DECAY = math.log(1e-2) / 0.3
LN_EPS = 1e-6
LOG2E = 1.4426950408889634
NEG_BIG = -1e30

VMEM_LIMIT = 56 * 1024 * 1024


def _cp(n_axes):
    return pltpu.CompilerParams(dimension_semantics=("arbitrary",) * n_axes,
                                vmem_limit_bytes=VMEM_LIMIT)


def _tile(n, pref, mult):
    if n <= pref:
        return n
    t = (pref // mult) * mult
    while t >= mult:
        if n % t == 0:
            return t
        t -= mult
    return n


def _ln(x):
    mu = jnp.mean(x, axis=-1, keepdims=True)
    xc = x - mu
    return xc * lax.rsqrt(jnp.mean(xc * xc, axis=-1, keepdims=True) + LN_EPS)


def _ln_mod_kernel(x_ref, sh_ref, sc_ref, o_ref):
    o_ref[0] = (_ln(x_ref[0]) * (1.0 + sc_ref[0]) + sh_ref[0]).astype(o_ref.dtype)


def ln_mod(x, shift, scale):
    B, N, D = x.shape
    tm = _tile(N, 512, 8)
    return pl.pallas_call(
        _ln_mod_kernel,
        out_shape=jax.ShapeDtypeStruct((B, N, D), BF16),
        grid=(B, N // tm),
        in_specs=[pl.BlockSpec((1, tm, D), lambda b, i: (b, i, 0)),
                  pl.BlockSpec((1, 1, D), lambda b, i: (b, 0, 0)),
                  pl.BlockSpec((1, 1, D), lambda b, i: (b, 0, 0))],
        out_specs=pl.BlockSpec((1, tm, D), lambda b, i: (b, i, 0)),
        compiler_params=_cp(2), name="ln_mod",
    )(x, shift, scale)


def _resid_ln_tail(x, y, gate, g, b, alpha):
    return _ln(alpha * x + gate * y) * g + b


def _mm_kernel(a_ref, w_ref, o_ref, *, act):
    acc = jnp.dot(a_ref[...], w_ref[...], preferred_element_type=F32)
    if act == "sigmoid":
        acc = jax.nn.sigmoid(acc)
    o_ref[...] = acc.astype(o_ref.dtype)


def matmul(a, w, out_dtype, act=None, tm_pref=1024, tn_pref=512):
    M, K = a.shape
    Nn = w.shape[1]
    tm = _tile(M, tm_pref, 16)
    tn = _tile(Nn, tn_pref, 128)
    return pl.pallas_call(
        functools.partial(_mm_kernel, act=act),
        out_shape=jax.ShapeDtypeStruct((M, Nn), out_dtype),
        grid=(M // tm, Nn // tn),
        in_specs=[pl.BlockSpec((tm, K), lambda i, j: (i, 0)),
                  pl.BlockSpec((K, tn), lambda i, j: (0, j))],
        out_specs=pl.BlockSpec((tm, tn), lambda i, j: (i, j)),
        compiler_params=_cp(2), name="matmul",
    )(a, w)


def _proj_t_kernel(w_ref, h_ref, g_ref, cos_ref, sin_ref, o_ref, *, rms, rope):
    acc = lax.dot_general(w_ref[...], h_ref[0], (((1,), (1,)), ((), ())),
                          preferred_element_type=F32)
    n_heads = acc.shape[0] // HEAD_DIM
    if rope:
        c = cos_ref[...]
        s = sin_ref[...]
    for hd in range(n_heads):
        lo = hd * HEAD_DIM
        x = acc[lo:lo + HEAD_DIM]
        if rms:
            x = x * lax.rsqrt(jnp.mean(x * x, axis=0, keepdims=True) + LN_EPS)
        x = x * g_ref[lo:lo + HEAD_DIM]
        if rope:
            x1, x2 = x[:HALF], x[HALF:]
            o_ref[0, lo:lo + HALF] = (x1 * c - x2 * s).astype(o_ref.dtype)
            o_ref[0, lo + HALF:lo + HEAD_DIM] = (x1 * s + x2 * c).astype(o_ref.dtype)
        else:
            o_ref[0, lo:lo + HEAD_DIM] = x.astype(o_ref.dtype)


def proj_t(w_t, h, gain, cos_t, sin_t, *, rms, rope):
    B, N, K = h.shape
    Fdim = w_t.shape[0]
    tt = _tile(N, 512, 128)
    return pl.pallas_call(
        functools.partial(_proj_t_kernel, rms=rms, rope=rope),
        out_shape=jax.ShapeDtypeStruct((B, Fdim, N), BF16),
        grid=(B, N // tt),
        in_specs=[pl.BlockSpec((Fdim, K), lambda b, i: (0, 0)),
                  pl.BlockSpec((1, tt, K), lambda b, i: (b, i, 0)),
                  pl.BlockSpec((Fdim, 1), lambda b, i: (0, 0)),
                  pl.BlockSpec((HALF, tt), lambda b, i: (0, i)),
                  pl.BlockSpec((HALF, tt), lambda b, i: (0, i))],
        out_specs=pl.BlockSpec((1, Fdim, tt), lambda b, i: (b, 0, i)),
        compiler_params=_cp(2), name="proj_t",
    )(w_t, h, gain, cos_t, sin_t)


V_PAD = 16


FLASH_MIN_DENOM = 2.0 ** -60


def _flash_kernel(lam_ref, kmax_ref, q_ref, k_ref, v_ref, g_ref, o_ref, q_sc, p_buf, sh_sc, acc_sc,
                  *, mode, nj, dv, post_scale, n_kheads):
    b, h = pl.program_id(0), pl.program_id(1)
    qf = q_ref[0].astype(F32)
    tq = qf.shape[1]
    zero = jnp.zeros((HEAD_DIM, tq), F32)
    qa, qb = qf[:HEAD_DIM], qf[HEAD_DIM:]
    if mode == "da":
        q_sc[0] = jnp.concatenate([qa, zero], axis=0).astype(BF16)
        q_sc[1] = jnp.concatenate([zero, qb], axis=0).astype(BF16)
        k_heads = (2 * h, 2 * h + 1)
    else:
        group = h // 2
        for mi, qh in enumerate((qa, qb)):
            q_sc[mi] = jnp.where(group == 0, jnp.concatenate([qh, zero], axis=0),
                                 jnp.concatenate([zero, qh], axis=0)).astype(BF16)
        k_heads = (group, group)
    for mi, qh in enumerate((qa, qb)):
        q_norm = jnp.sqrt(jnp.sum(qh * qh, axis=0, keepdims=True))
        sh_sc[mi] = q_norm * kmax_ref[b * n_kheads + k_heads[mi]]

    def stage_exp(j, slot):
        kb = k_ref[0, j]
        for mi in range(2):
            s = jnp.dot(kb, q_sc[mi], preferred_element_type=F32)
            p_buf[slot, mi] = jnp.exp2(s - sh_sc[mi]).astype(BF16)

    def stage_values(j, slot):
        vb = v_ref[0, j]
        for mi in range(2):
            acc_sc[mi] += jnp.dot(vb, p_buf[slot, mi], preferred_element_type=F32)

    def accumulate():
        acc_sc[...] = jnp.zeros(acc_sc.shape, F32)
        stage_exp(0, 0)
        steady = list(range(1, nj))
        if len(steady) % 2 == 1:
            t = steady.pop(0)
            stage_exp(t, t % 2)
            stage_values(t - 1, (t - 1) % 2)
        if steady:
            t0 = steady[0]

            def body(i, carry):
                t = t0 + 2 * i
                stage_exp(t, t0 % 2)
                stage_values(t - 1, 1 - t0 % 2)
                stage_exp(t + 1, 1 - t0 % 2)
                stage_values(t, t0 % 2)
                return carry

            lax.fori_loop(0, len(steady) // 2, body, 0)
        stage_values(nj - 1, (nj - 1) % 2)

    accumulate()
    denom_min = jnp.minimum(jnp.min(acc_sc[0][dv:dv + 1]), jnp.min(acc_sc[1][dv:dv + 1]))

    @pl.when(jnp.logical_not(denom_min >= FLASH_MIN_DENOM))
    def _():
        for mi in range(2):
            def max_body(j, m, mi=mi):
                s = jnp.dot(k_ref[0, j], q_sc[mi], preferred_element_type=F32)
                return jnp.maximum(m, jnp.max(s, axis=0, keepdims=True))

            sh_sc[mi] = lax.fori_loop(0, nj, max_body, jnp.full((1, tq), NEG_BIG, F32))
        accumulate()

    outs = []
    for mi in range(2):
        acc = acc_sc[mi]
        outs.append(acc[:dv] / acc[dv:dv + 1])
    if mode == "da":
        o = outs[0] - lam_ref[0] * outs[1]
        o = o * lax.rsqrt(jnp.mean(o * o, axis=0, keepdims=True) + LN_EPS)
        o = o * (g_ref[...] * post_scale)
    else:
        o = jnp.concatenate(outs, axis=0)
    o_ref[0] = o.T.astype(o_ref.dtype)


def _knorm_kernel(k_ref, o_ref):
    k = k_ref[0].astype(F32)
    n2 = jnp.sum(k * k, axis=0, keepdims=True)
    o_ref[0, 0] = jnp.broadcast_to(jnp.max(n2, axis=1, keepdims=True), o_ref.shape[2:])


def key_norm_max(k_t):
    B, C, Nk = k_t.shape
    heads = C // HEAD_DIM
    out = pl.pallas_call(
        _knorm_kernel,
        out_shape=jax.ShapeDtypeStruct((B, heads, 8, 128), F32),
        grid=(B, heads),
        in_specs=[pl.BlockSpec((1, HEAD_DIM, Nk), lambda b, h: (b, h, 0))],
        out_specs=pl.BlockSpec((1, 1, 8, 128), lambda b, h: (b, h, 0, 0)),
        compiler_params=_cp(2), name="key_norm_max",
    )(k_t)
    return jnp.sqrt(out[:, :, 0, 0]).reshape(B * heads)


def flash_attention(q_t, k_t, v_t, lam, gain, *, mode, post_scale=1.0, q_row_block=0, tk_pref=768):
    B, _, Nq = q_t.shape
    Nk = k_t.shape[2]
    kmax = key_norm_max(k_t)
    n_kheads = k_t.shape[1] // HEAD_DIM
    k_tok = jnp.swapaxes(k_t, 1, 2)
    tq = _tile(Nq, 1024, 128)
    tk = _tile(Nk, tk_pref, 128)
    nj = Nk // tk
    dv = 2 * HEAD_DIM if mode == "da" else HEAD_DIM
    dvp = dv + V_PAD
    heads = v_t.shape[1] // dv
    ones_rows = jnp.zeros((B, heads, V_PAD, Nk), BF16).at[:, :, 0].set(1.0)
    v_aug = jnp.concatenate([v_t.reshape(B, heads, dv, Nk), ones_rows], axis=2)
    v4 = jnp.swapaxes(v_aug.reshape(B, heads * dvp, nj, tk), 1, 2)
    k4 = k_tok.reshape(B, nj, tk, k_tok.shape[2])
    if mode == "da":
        k_map = lambda b, h, i: (b, 0, 0, h)
        v_map = lambda b, h, i: (b, 0, h, 0)
    else:
        k_map = lambda b, h, i: (b, 0, 0, 0)
        v_map = lambda b, h, i: (b, 0, h // 2, 0)
    return pl.pallas_call(
        functools.partial(_flash_kernel, mode=mode, nj=nj, dv=dv, post_scale=post_scale, n_kheads=n_kheads),
        out_shape=jax.ShapeDtypeStruct((B, Nq, 4 * 128), BF16),
        grid=(B, 4, Nq // tq),
        in_specs=[pl.BlockSpec(memory_space=pltpu.SMEM),
                  pl.BlockSpec(memory_space=pltpu.SMEM),
                  pl.BlockSpec((1, 128, tq), lambda b, h, i: (b, h + q_row_block, i)),
                  pl.BlockSpec((1, nj, tk, 128), k_map),
                  pl.BlockSpec((1, nj, dvp, tk), v_map),
                  pl.BlockSpec((128, 1), lambda b, h, i: (0, 0))],
        out_specs=pl.BlockSpec((1, tq, 128), lambda b, h, i: (b, i, h)),
        scratch_shapes=[pltpu.VMEM((2, 128, tq), BF16),
                        pltpu.VMEM((2, 2, tk, tq), BF16),
                        pltpu.VMEM((2, 1, tq), F32),
                        pltpu.VMEM((2, dvp, tq), F32)],
        compiler_params=_cp(3), name="flash_" + mode,
    )(lam, kmax, q_t, k4, v4, gain)


def _shift_rows(x, prev_row, next_row):
    T = x.shape[0]
    row = lax.broadcasted_iota(jnp.int32, x.shape, 0)
    xp = jnp.where(row == 0, prev_row, pltpu.roll(x, 1, axis=0))
    xn = jnp.where(row == T - 1, next_row, pltpu.roll(x, T - 1, axis=0))
    return xp, xn


def _conv3_block(x_ref, p_ref, n_ref, w_ref, b_ref, halo):
    i = pl.program_id(1)
    last = pl.num_programs(1) - 1
    x = x_ref[0].astype(F32)
    prev_row = jnp.where(i > 0, p_ref[0].astype(F32)[halo - 1:halo], 0.0)
    next_row = jnp.where(i < last, n_ref[0].astype(F32)[0:1], 0.0)
    xp, xn = _shift_rows(x, prev_row, next_row)
    return xp * w_ref[0:1] + x * w_ref[1:2] + xn * w_ref[2:3] + b_ref[...]


def _halo_specs(tr, tc, halo, n_rows, col_of):
    per = tr // halo
    n_halo = n_rows // halo
    return [pl.BlockSpec((1, tr, tc), lambda b, i, j: (b, i, col_of(j))),
            pl.BlockSpec((1, halo, tc), lambda b, i, j: (b, jnp.maximum(i * per - 1, 0), col_of(j))),
            pl.BlockSpec((1, halo, tc), lambda b, i, j: (b, jnp.minimum((i + 1) * per, n_halo - 1), col_of(j)))]


def _dwconv_kernel(x_ref, p_ref, n_ref, w_ref, b_ref, v_ref, x1_ref, x2_ref):
    y = _conv3_block(x_ref, p_ref, n_ref, w_ref, b_ref, 8)
    C = v_ref.shape[2]
    v_ref[0] = y[:, :C]
    x1_ref[0] = y[:, C:2 * C]
    x2_ref[0] = y[:, 2 * C:]


def hyena_short_conv(u, w, b):
    B, L, C3 = u.shape
    C = C3 // 3
    tr = _tile(L, 512, 8)
    zero = lambda j: 0
    out = jax.ShapeDtypeStruct((B, L, C), F32)
    ospec = pl.BlockSpec((1, tr, C), lambda b, i, j: (b, i, 0))
    return pl.pallas_call(
        _dwconv_kernel,
        out_shape=(out, out, out),
        grid=(B, L // tr, 1),
        in_specs=_halo_specs(tr, C3, 8, L, zero) + [
            pl.BlockSpec((3, C3), lambda b, i, j: (0, 0)),
            pl.BlockSpec((1, C3), lambda b, i, j: (0, 0))],
        out_specs=(ospec, ospec, ospec),
        compiler_params=_cp(3), name="hyena_short_conv",
    )(u, u, u, w, b)


FFN_HALO = 16
FFN_CHUNK = 256


def _ffn_kernel(h_ref, hp_ref, hn_ref, wup_ref, cw_ref, cb_ref, wdn_ref, x_ref, gate_ref, g_ref, b_ref, o_ref,
                *, alpha):
    i = pl.program_id(1)
    tm = h_ref.shape[1]
    dff = wdn_ref.shape[0]
    rows = tm + 2 * FFN_HALO
    h_prev = jnp.where(i == 0, jnp.zeros_like(hp_ref[0]), hp_ref[0])
    h_next = jnp.where(i == pl.num_programs(1) - 1, jnp.zeros_like(hn_ref[0]), hn_ref[0])
    h_ext = jnp.concatenate([h_prev, h_ref[0], h_next], axis=0)

    def conv_cols(c0):
        u = jnp.dot(h_ext, wup_ref[:, c0:c0 + FFN_CHUNK], preferred_element_type=F32)
        mid = slice(FFN_HALO, FFN_HALO + tm)
        u_prev = pltpu.roll(u, 1, axis=0)[mid]
        u_next = pltpu.roll(u, rows - 1, axis=0)[mid]
        w = cw_ref[:, c0:c0 + FFN_CHUNK]
        return u_prev * w[0:1] + u[mid] * w[1:2] + u_next * w[2:3] + cb_ref[:, c0:c0 + FFN_CHUNK]

    y = jnp.zeros((tm, wdn_ref.shape[1]), F32)
    for c0 in range(0, dff, FFN_CHUNK):
        val = conv_cols(c0)
        gate = conv_cols(dff + c0)
        act = (gate * jax.nn.sigmoid(gate) * val).astype(BF16)
        y = y + jnp.dot(act, wdn_ref[c0:c0 + FFN_CHUNK, :], preferred_element_type=F32)
    o_ref[0] = _resid_ln_tail(x_ref[0], y, gate_ref[0], g_ref[...], b_ref[...], alpha)


def conv_ffn_resid_ln(h, w_up, conv_w, conv_b, w_down, x, gate, g, b, alpha):
    B, N, D = h.shape
    dff = w_down.shape[0]
    tm = _tile(N, 512, FFN_HALO)
    per = tm // FFN_HALO
    n_halo = N // FFN_HALO
    row = lambda b_, i: (b_, i, 0)
    per_b = lambda b_, i: (b_, 0, 0)
    const2 = lambda b_, i: (0, 0)
    resident = dict(pipeline_mode=pl.Buffered(1))
    return pl.pallas_call(
        functools.partial(_ffn_kernel, alpha=alpha),
        out_shape=jax.ShapeDtypeStruct((B, N, D), F32),
        grid=(B, N // tm),
        in_specs=[pl.BlockSpec((1, tm, D), row),
                  pl.BlockSpec((1, FFN_HALO, D), lambda b_, i: (b_, jnp.maximum(i * per - 1, 0), 0)),
                  pl.BlockSpec((1, FFN_HALO, D), lambda b_, i: (b_, jnp.minimum((i + 1) * per, n_halo - 1), 0)),
                  pl.BlockSpec((D, 2 * dff), const2, **resident),
                  pl.BlockSpec((3, 2 * dff), const2), pl.BlockSpec((1, 2 * dff), const2),
                  pl.BlockSpec((dff, D), const2, **resident),
                  pl.BlockSpec((1, tm, D), row), pl.BlockSpec((1, 1, D), per_b),
                  pl.BlockSpec((1, D), const2), pl.BlockSpec((1, D), const2)],
        out_specs=pl.BlockSpec((1, tm, D), row),
        compiler_params=_cp(2), name="conv_ffn_resid_ln",
    )(h, h, h, w_up, conv_w, conv_b, w_down, x, gate, g, b)


def _merge_kernel(a_ref, b_ref, c_ref, gt_ref, wpa_ref, wpb_ref, wpc_ref, wo_ref, x_ref, gate_ref,
                  g_ref, bb_ref, sh_ref, sc_ref, ox_ref, oh_ref, *, alpha):
    D = wo_ref.shape[0]
    gt = gt_ref[0]
    m = gt[:, :D].astype(F32) * jnp.dot(a_ref[0], wpa_ref[...], preferred_element_type=F32)
    m = m + gt[:, D:2 * D].astype(F32) * jnp.dot(b_ref[0], wpb_ref[...], preferred_element_type=F32)
    m = m + gt[:, 2 * D:].astype(F32) * jnp.dot(c_ref[0].astype(BF16), wpc_ref[...],
                                                preferred_element_type=F32)
    y = jnp.dot(m.astype(BF16), wo_ref[...], preferred_element_type=F32)
    xn = _resid_ln_tail(x_ref[0], y, gate_ref[0], g_ref[...], bb_ref[...], alpha)
    ox_ref[0] = xn
    oh_ref[0] = (_ln(xn) * (1.0 + sc_ref[0]) + sh_ref[0]).astype(oh_ref.dtype)


def merge_resid_ln(a, b, c, gates, wpa, wpb, wpc, wo, x, gate, g, bb, sh, sc, alpha):
    B, N, D = x.shape
    W = a.shape[2]
    tm = _tile(N, 512, 16)
    row = lambda b_, i: (b_, i, 0)
    per_b = lambda b_, i: (b_, 0, 0)
    const2 = lambda b_, i: (0, 0)
    return pl.pallas_call(
        functools.partial(_merge_kernel, alpha=alpha),
        out_shape=(jax.ShapeDtypeStruct((B, N, D), F32), jax.ShapeDtypeStruct((B, N, D), BF16)),
        grid=(B, N // tm),
        in_specs=[pl.BlockSpec((1, tm, W), row), pl.BlockSpec((1, tm, W), row), pl.BlockSpec((1, tm, W), row),
                  pl.BlockSpec((1, tm, 3 * D), row),
                  pl.BlockSpec((W, D), const2), pl.BlockSpec((W, D), const2), pl.BlockSpec((W, D), const2),
                  pl.BlockSpec((D, D), const2),
                  pl.BlockSpec((1, tm, D), row), pl.BlockSpec((1, 1, D), per_b),
                  pl.BlockSpec((1, D), const2), pl.BlockSpec((1, D), const2),
                  pl.BlockSpec((1, 1, D), per_b), pl.BlockSpec((1, 1, D), per_b)],
        out_specs=(pl.BlockSpec((1, tm, D), row), pl.BlockSpec((1, tm, D), row)),
        compiler_params=_cp(2), name="merge_resid_ln",
    )(a, b, c, gates, wpa, wpb, wpc, wo, x, gate, g, bb, sh, sc)


def _split_bf16(x):
    hi = x.astype(BF16)
    lo = (x - hi.astype(F32)).astype(BF16)
    return hi, lo


def _dot3(a, b):
    ah, al = _split_bf16(a)
    bh, bl = _split_bf16(b)
    d = functools.partial(jnp.dot, preferred_element_type=F32)
    return d(ah, bh) + (d(ah, bl) + d(al, bh))


def _filter_kernel(z_ref, w1_ref, b1_ref, w2_ref, b2_ref, w3_ref, fr_ref, dec_ref, h_ref, s_ref, *, zero_block):
    i = pl.program_id(0)
    fr = fr_ref[...]
    hid = jnp.sin(fr * (_dot3(z_ref[...], w1_ref[...]) + b1_ref[...]))
    hid = jnp.sin(fr * (_dot3(hid, w2_ref[...]) + b2_ref[...]))
    h = _dot3(hid, w3_ref[0])
    dec = dec_ref[...]
    C = dec.shape[1]
    n_ord = h.shape[1] // C
    h = h * jnp.concatenate([dec] * n_ord, axis=1)

    @pl.when(i == 0)
    def _():
        s_ref[...] = jnp.zeros(s_ref.shape, F32)

    s_ref[...] += jnp.sum(jnp.abs(h), axis=0, keepdims=True)
    row = lax.broadcasted_iota(jnp.int32, h.shape, 0)
    h = jnp.where((row == 0) & (i == zero_block), 0.0, h)
    for o in range(n_ord):
        h_ref[o] = h[:, o * C:(o + 1) * C]


def hyena_filter_mlp(z2, w1, b1, w2, b2, w3_dir, freq, decay2):
    n2, E = z2.shape
    Hh = w2.shape[0]
    OC = w3_dir.shape[2]
    C = decay2.shape[1]
    n_ord = OC // C
    L = n2 // 2
    tr = _tile(L, 512, 8)
    nblk = n2 // tr
    c2 = lambda i: (0, 0)
    return pl.pallas_call(
        functools.partial(_filter_kernel, zero_block=L // tr),
        out_shape=(jax.ShapeDtypeStruct((n_ord, n2, C), F32), jax.ShapeDtypeStruct((1, OC), F32)),
        grid=(nblk,),
        in_specs=[pl.BlockSpec((tr, E), lambda i: (i, 0)), pl.BlockSpec((E, Hh), c2), pl.BlockSpec((1, Hh), c2),
                  pl.BlockSpec((Hh, Hh), c2), pl.BlockSpec((1, Hh), c2),
                  pl.BlockSpec((1, Hh, OC), lambda i: (i // (nblk // 2), 0, 0)),
                  pl.BlockSpec((1, Hh), c2), pl.BlockSpec((tr, C), lambda i: (i, 0))],
        out_specs=(pl.BlockSpec((n_ord, tr, C), lambda i: (0, i, 0)), pl.BlockSpec((1, OC), c2)),
        compiler_params=_cp(1), name="hyena_filter_mlp",
    )(z2, w1, b1, w2, b2, w3_dir, freq, decay2)


def _mm_split(m_hi, m_lo, x, passes):
    d = functools.partial(jnp.dot, preferred_element_type=F32)
    if passes == 1:
        return d(m_hi, x.astype(BF16))
    xh, xl = _split_bf16(x)
    return d(m_hi, xh) + (d(m_lo, xh) + d(m_hi, xl))


FFT_ROWS = 8


def _level1_kernel(mh_ref, ml_ref, x_ref, o_ref, *, passes):
    r_in, r_out = x_ref.shape[1], o_ref.shape[1]
    x = x_ref[0].reshape(r_in * FFT_ROWS, x_ref.shape[3])
    y = _mm_split(mh_ref[...], ml_ref[...], x, passes)
    o_ref[0] = y.reshape(r_out, FFT_ROWS, x_ref.shape[3]).astype(o_ref.dtype)


def _level1_gate_kernel(mh_ref, ml_ref, x_ref, z_ref, xg_ref, bias_ref, o_ref, *, passes):
    r_in, r_out = x_ref.shape[1], o_ref.shape[1]
    x = x_ref[0].reshape(r_in * FFT_ROWS, x_ref.shape[3])
    y = _mm_split(mh_ref[...], ml_ref[...], x, passes).reshape(r_out, FFT_ROWS, x_ref.shape[3])
    o_ref[0] = (xg_ref[0] * (y + bias_ref[...] * z_ref[0])).astype(o_ref.dtype)


def fft_level1(m_hi, m_lo, x, passes, gate_args=None, out_dtype=F32):
    P, R_in, Nb, C = x.shape
    R_out = m_hi.shape[0] // FFT_ROWS
    mspec = pl.BlockSpec(m_hi.shape, lambda p, j: (0, 0))
    xspec = pl.BlockSpec((1, R_in, FFT_ROWS, C), lambda p, j: (p, 0, j, 0))
    ospec = pl.BlockSpec((1, R_out, FFT_ROWS, C), lambda p, j: (p, 0, j, 0))
    if gate_args is None:
        kern = functools.partial(_level1_kernel, passes=passes)
        ins, specs = (m_hi, m_lo, x), [mspec, mspec, xspec]
    else:
        z, xg, bias = gate_args
        kern = functools.partial(_level1_gate_kernel, passes=passes)
        ins = (m_hi, m_lo, x, z, xg, bias)
        specs = [mspec, mspec, xspec, ospec, ospec, pl.BlockSpec((1, C), lambda p, j: (0, 0))]
    return pl.pallas_call(
        kern, out_shape=jax.ShapeDtypeStruct((P, R_out, Nb, C), out_dtype),
        grid=(P, Nb // FFT_ROWS), in_specs=specs, out_specs=ospec,
        compiler_params=_cp(2), name="fft_level1",
    )(*ins)


FFT_MID_K1 = 2


def _fft_mid_kernel(*refs, passes):
    n_mat = 2 if passes == 1 else 4
    gh_ref, gih_ref = refs[0], refs[1]
    gl_ref, gil_ref = (refs[2], refs[3]) if passes > 1 else (None, None)
    h_ref, a_ref, o_ref = refs[n_mat:]
    n_pairs, _, kb, nb, C = a_ref.shape
    for kk in range(kb):
        hr, hi = h_ref[0, kk], h_ref[1, kk]
        for p in range(n_pairs):
            x = a_ref[p, :, kk].reshape(2 * nb, C)
            X = _mm_split(gh_ref[kk], None if gl_ref is None else gl_ref[kk], x, passes)
            xr, xi = X[:nb], X[nb:]
            Y = jnp.concatenate([xr * hr - xi * hi, xr * hi + xi * hr], axis=0)
            Bv = _mm_split(gih_ref[kk], None if gil_ref is None else gil_ref[kk], Y, passes)
            o_ref[p, :, kk] = Bv.reshape(2, nb, C)


def fft_mid(g_hi, g_lo, gi_hi, gi_lo, hspec, a5, passes):
    P, _, Na, Nb, C = a5.shape
    kb = FFT_MID_K1
    gspec = pl.BlockSpec((kb, 2 * Nb, 2 * Nb), lambda k: (k, 0, 0))
    mats = (g_hi, gi_hi) if passes == 1 else (g_hi, gi_hi, g_lo, gi_lo)
    blk = pl.BlockSpec((P, 2, kb, Nb, C), lambda k: (0, 0, k, 0, 0))
    return pl.pallas_call(
        functools.partial(_fft_mid_kernel, passes=passes),
        out_shape=jax.ShapeDtypeStruct(a5.shape, F32),
        grid=(Na // kb,),
        in_specs=[gspec] * len(mats) + [pl.BlockSpec((2, kb, Nb, C), lambda k: (0, k, 0, 0)), blk],
        out_specs=blk,
        compiler_params=_cp(1), name="fft_mid",
    )(*mats, hspec, a5)


def _fft_spec_kernel(gh_ref, gl_ref, inv_ref, a_ref, o_ref, *, passes):
    nb = a_ref.shape[3]
    x = a_ref[0, :, 0].reshape(2 * nb, a_ref.shape[4])
    X = _mm_split(gh_ref[0], gl_ref[0], x, passes) * inv_ref[0]
    o_ref[0, :, 0] = X.reshape(2, nb, a_ref.shape[4])


def fft_filter_spectrum(g_hi, g_lo, inv_norm, a5, passes):
    P, _, Na, Nb, C = a5.shape
    gspec = pl.BlockSpec((1, 2 * Nb, 2 * Nb), lambda k, p: (k, 0, 0))
    blk = pl.BlockSpec((1, 2, 1, Nb, C), lambda k, p: (p, 0, k, 0, 0))
    return pl.pallas_call(
        functools.partial(_fft_spec_kernel, passes=passes),
        out_shape=jax.ShapeDtypeStruct(a5.shape, F32),
        grid=(Na, P),
        in_specs=[gspec, gspec, pl.BlockSpec((1, 1, C), lambda k, p: (p, 0, 0)), blk],
        out_specs=blk,
        compiler_params=_cp(2), name="fft_filter_spectrum",
    )(g_hi, g_lo, inv_norm, a5)


def _fft_factors(n_fft):
    na = 1 << (int(math.log2(n_fft)) // 2)
    return na, n_fft // na


def _fft_tables(L):
    n_fft = 2 * L
    na, nb = _fft_factors(n_fft)

    def cis(num, den):
        ang = (2.0 * math.pi / den) * (num % den).astype(F32)
        return jnp.cos(ang), -jnp.sin(ang)

    ia = jnp.arange(na, dtype=jnp.int32)
    ib = jnp.arange(nb, dtype=jnp.int32)
    far, fai = cis(ia[:, None] * ia[None, :], na)
    fh_r, fh_i = far[:, :na // 2], fai[:, :na // 2]
    m_fwd = jnp.block([[fh_r, -fh_i], [fh_i, fh_r]])
    m_flt = jnp.concatenate([far, fai], axis=0)
    m_inv = jnp.block([[fh_r.T, fh_i.T], [-fh_i.T, fh_r.T]]) / n_fft
    num = ib[None, :, None] * ib[None, None, :] * na + ia[:, None, None] * ib[None, None, :]
    gr, gi = cis(num, n_fft)
    g = jnp.concatenate([jnp.concatenate([gr, -gi], axis=2), jnp.concatenate([gi, gr], axis=2)], axis=1)
    grt, git = jnp.swapaxes(gr, 1, 2), jnp.swapaxes(gi, 1, 2)
    ginv = jnp.concatenate([jnp.concatenate([grt, git], axis=2), jnp.concatenate([-git, grt], axis=2)], axis=1)
    split = lambda m: _split_bf16(m.astype(F32))
    kron = lambda m: split(jnp.kron(m, jnp.eye(FFT_ROWS, dtype=F32)))
    C = HY_WIDTH
    t = jnp.linspace(0.0, 1.0, L, dtype=F32)[:, None]
    f = jnp.linspace(1e-4, HY_BANDS - 1, HY_BANDS, dtype=F32)
    ang = (2.0 * math.pi / L) * jnp.arange(L, dtype=F32)[:, None] * f[None, :]
    z = jnp.concatenate([t, jnp.cos(ang), -jnp.sin(ang)], axis=-1)
    z = jnp.pad(z, ((0, 0), (0, HY_EMB_PAD - HY_EMB)))
    deltas = jnp.abs(jnp.linspace(HY_MIN_DECAY, HY_MAX_DECAY, C, dtype=F32))
    decay = jnp.exp(-t * deltas)
    circ = lambda a: jnp.concatenate([a, a[:1], a[:0:-1]], axis=0)
    return dict(na=na, nb=nb, fwd=kron(m_fwd), flt=kron(m_flt), inv=kron(m_inv), g=split(g), ginv=split(ginv),
                z2=circ(z), decay2=circ(decay))


def hyena_filter_spectra(w1, b1, w2, b2, w3, freq, tabs, passes):
    C = HY_WIDTH
    na, nb = tabs["na"], tabs["nb"]
    w1p = jnp.pad(w1, ((0, HY_EMB_PAD - HY_EMB), (0, 0)))
    w3_dir = jnp.transpose(w3.reshape(HY_HIDDEN, HY_ORDER, 2, C), (2, 0, 1, 3)).reshape(2, HY_HIDDEN, HY_ORDER * C)
    filt, s = hyena_filter_mlp(tabs["z2"], w1p, b1[None], w2, b2[None], w3_dir, freq[None], tabs["decay2"])
    inv_norm = (1.0 / s).reshape(HY_ORDER, 1, C)
    a = fft_level1(*tabs["flt"], filt.reshape(HY_ORDER, na, nb, C), passes)
    return fft_filter_spectrum(*tabs["g"], inv_norm, a.reshape(HY_ORDER, 2, na, nb, C), passes)


def hyena_long_conv_gate(z, xg, bias, hspec, tabs, passes):
    B, L, C = z.shape
    na, nb = tabs["na"], tabs["nb"]
    P = B // 2
    nat = lambda a: a.reshape(P, na, nb, C)
    a = fft_level1(*tabs["fwd"], nat(z), passes)
    bv = fft_mid(*tabs["g"], *tabs["ginv"], hspec, a.reshape(P, 2, na, nb, C), passes)
    y = fft_level1(*tabs["inv"], bv.reshape(P, 2 * na, nb, C), passes,
                   gate_args=(nat(z), nat(xg), bias[None]))
    return y.reshape(B, L, C)


def hyena_mix(u, spectra, conv_w, conv_b, bias, tabs, passes):
    v, x1, x2 = hyena_short_conv(u, conv_w, conv_b[None])
    zz = hyena_long_conv_gate(v, x1, bias[0], spectra[0], tabs, passes)
    return hyena_long_conv_gate(zz, x2, bias[1], spectra[1], tabs, passes)


FFT_PASSES = 1
FILTER_PASSES = 3


def _rope_tables_t(n):
    rows = n // GRID_W
    r = jnp.repeat(jnp.arange(rows, dtype=F32), GRID_W)
    col = jnp.tile(jnp.arange(GRID_W, dtype=F32), rows)
    axis_dim = HEAD_DIM // 2
    inv = ROPE_THETA ** (-jnp.arange(0, axis_dim, 2, dtype=F32) / axis_dim)
    ang = jnp.concatenate([r[:, None] * inv, col[:, None] * inv], axis=-1)
    return jnp.cos(ang).T, jnp.sin(ang).T


def _layer_weights(l, w_in, gq_qn, gq_kn):
    w = w_in[l]
    d_qk = DA_HEADS * 2 * HEAD_DIM
    gq_kv = GKV_HEADS * HEAD_DIM
    gq_q = GQ_HEADS * HEAD_DIM
    o = np.cumsum([0, d_qk, d_qk, gq_kv, gq_kv, d_qk, gq_q, 3 * HY_WIDTH])
    ka, va, kb, vb, qa, qb, hy = (w[:, o[i]:o[i + 1]] for i in range(7))
    gates = w[:, o[7]:]
    qscale = (HEAD_DIM ** -0.5) * LOG2E
    t = lambda parts: jnp.concatenate(parts, axis=1).T.astype(BF16)
    w_rope = t([ka, qa])
    g_rope = jnp.concatenate([jnp.ones((d_qk,), F32), jnp.full((d_qk,), qscale, F32)])[:, None]
    w_rms = t([kb, qb])
    g_rms = jnp.concatenate([jnp.tile(gq_kn[l], GKV_HEADS), jnp.tile(gq_qn[l], GQ_HEADS) * qscale])[:, None]
    w_v = t([va, vb])
    g_v = jnp.ones((d_qk + gq_kv, 1), F32)
    return dict(w_rope=w_rope, g_rope=g_rope, w_rms=w_rms, g_rms=g_rms, w_v=w_v, g_v=g_v,
                w_hy=hy.astype(BF16), w_gates=gates.astype(BF16), d_qk=d_qk, gq_kv=gq_kv)


def _project(h, lw, cos_t, sin_t, need_q):
    B, N, D = h.shape
    rope_o = proj_t(lw["w_rope"], h, lw["g_rope"], cos_t, sin_t, rms=False, rope=True)
    rms_o = proj_t(lw["w_rms"], h, lw["g_rms"], cos_t, sin_t, rms=True, rope=True)
    v_o = proj_t(lw["w_v"], h, lw["g_v"], cos_t, sin_t, rms=False, rope=False)
    d_qk, gq_kv = lw["d_qk"], lw["gq_kv"]
    out = dict(ka=rope_o[:, :d_qk], qa=rope_o, qa_blk=d_qk // 128, kb=rms_o[:, :gq_kv], qb=rms_o,
               qb_blk=gq_kv // 128, va=v_o[:, :d_qk], vb=v_o[:, d_qk:])
    if need_q:
        h2 = h.reshape(B * N, D)
        out["u"] = matmul(h2, lw["w_hy"], F32).reshape(B, N, -1)
        out["gates"] = matmul(h2, lw["w_gates"], BF16, act="sigmoid").reshape(B, N, -1)
    return out


def kernel(x, c, ctx, c_ctx, w_mod, b_mod, w_in, da_lq1, da_lk1, da_lq2, da_lk2, da_subln, gq_qn, gq_kn,
           hy_conv_w, hy_conv_b, hy_w1, hy_b1, hy_w2, hy_b2, hy_w3, hy_freq, hy_bias, w_pa, w_pb, w_pc, w_o,
           ln1_g, ln1_b, w_up, ffn_conv_w, ffn_conv_b, w_down, ln2_g, ln2_b):
    B, n_lat, D = x.shape
    n_ctx = ctx.shape[1]
    depth = w_in.shape[0]
    alpha = (2 * depth) ** 0.25
    cos_l, sin_l = _rope_tables_t(n_lat)
    cos_c, sin_c = jnp.ones((HALF, n_ctx), F32), jnp.zeros((HALF, n_ctx), F32)
    tabs_l = _fft_tables(n_lat)
    tabs_c = _fft_tables(n_ctx)
    xc = ctx

    cond = jnp.concatenate([c, c_ctx[None]], axis=0)
    cond = jnp.pad(jax.nn.silu(cond), ((0, 16 - (B + 1) % 16), (0, 0))).astype(BF16)

    for l in range(depth):
        last = l == depth - 1
        lam_init = 0.8 - 0.6 * math.exp(-0.3 * l)
        lam = (jnp.exp(jnp.sum(da_lq1[l] * da_lk1[l])) - jnp.exp(jnp.sum(da_lq2[l] * da_lk2[l])) + lam_init)
        lam = lam.reshape(1).astype(F32)
        mod = matmul(cond, w_mod[l].astype(BF16), F32, tn_pref=1024) + b_mod[l]
        sh1, sc1, g1, sh2, sc2, g2 = [m[:, None, :] for m in jnp.split(mod[:B], 6, axis=-1)]
        mc = [jnp.broadcast_to(m[None, None, :], (B, 1, D)) for m in jnp.split(mod[B], 6)]
        lw = _layer_weights(l, w_in, gq_qn, gq_kn)
        subln = da_subln[l][:, None]
        hy_mlp = (hy_w1[l], hy_b1[l], hy_w2[l], hy_b2[l], hy_w3[l], hy_freq[l])
        spectra_l = hyena_filter_spectra(*hy_mlp, tabs_l, FILTER_PASSES)
        wpa, wpb, wpc, wo = (w[l].astype(BF16) for w in (w_pa, w_pb, w_pc, w_o))
        wup, wdn = w_up[l].astype(BF16), w_down[l].astype(BF16)
        ln1 = (ln1_g[l][None], ln1_b[l][None])
        ln2 = (ln2_g[l][None], ln2_b[l][None])

        pc = _project(ln_mod(xc, mc[0], mc[1]), lw, cos_c, sin_c, need_q=not last)
        pl_ = _project(ln_mod(x, sh1, sc1), lw, cos_l, sin_l, need_q=True)

        cat = lambda name: jnp.concatenate([pc[name], pl_[name]], axis=2)
        a_l = flash_attention(pl_["qa"], cat("ka"), cat("va"), lam, subln, mode="da", post_scale=1.0 - lam_init,
                              q_row_block=pl_["qa_blk"])
        b_l = flash_attention(pl_["qb"], cat("kb"), cat("vb"), lam, subln, mode="gqa", q_row_block=pl_["qb_blk"])
        c_l = hyena_mix(pl_["u"], spectra_l, hy_conv_w[l], hy_conv_b[l], hy_bias[l], tabs_l, FFT_PASSES)
        x, h2 = merge_resid_ln(a_l, b_l, c_l, pl_["gates"], wpa, wpb, wpc, wo, x, g1, *ln1, sh2, sc2, alpha)
        x = conv_ffn_resid_ln(h2, wup, ffn_conv_w[l], ffn_conv_b[l][None], wdn, x, g2, *ln2, alpha)

        if not last:
            spectra_c = hyena_filter_spectra(*hy_mlp, tabs_c, FILTER_PASSES)
            a_c = flash_attention(pc["qa"], pc["ka"], pc["va"], lam, subln, mode="da",
                                  post_scale=1.0 - lam_init, q_row_block=pc["qa_blk"])
            b_c = flash_attention(pc["qb"], pc["kb"], pc["vb"], lam, subln, mode="gqa",
                                  q_row_block=pc["qb_blk"])
            c_c = hyena_mix(pc["u"], spectra_c, hy_conv_w[l], hy_conv_b[l], hy_bias[l], tabs_c, FFT_PASSES)
            xc, hc2 = merge_resid_ln(a_c, b_c, c_c, pc["gates"], wpa, wpb, wpc, wo, xc, mc[2],
                                     *ln1, mc[3], mc[4], alpha)
            xc = conv_ffn_resid_ln(hc2, wup, ffn_conv_w[l], ffn_conv_b[l][None], wdn, xc, mc[5], *ln2, alpha)
    return x
```

```python
import functools
import math

import numpy as np
import jax
import jax.numpy as jnp
from jax import lax
from jax.experimental import pallas as pl
from jax.experimental.pallas import tpu as pltpu

F32 = jnp.float32
BF16 = jnp.bfloat16

HEAD_DIM = 64
HALF = HEAD_DIM // 2
GRID_W = 64
ROPE_THETA = 10000.0
DA_HEADS = 4
GQ_HEADS = 8
GKV_HEADS = 2
HY_WIDTH = 512
HY_ORDER = 2
HY_EMB = 33
HY_EMB_PAD = 128
HY_BANDS = (HY_EMB - 1) // 2
HY_HIDDEN = 64
HY_MIN_DECAY = math.log(1e-2) / 1.5
HY_MAX_DECAY = math.log(1e-2) / 0.3
LN_EPS = 1e-6
LOG2E = 1.4426950408889634
NEG_BIG = -1e30

VMEM_LIMIT = 56 * 1024 * 1024


def _cp(n_axes):
    return pltpu.CompilerParams(dimension_semantics=("arbitrary",) * n_axes,
                                vmem_limit_bytes=VMEM_LIMIT)


def _tile(n, pref, mult):
    if n <= pref:
        return n
    t = (pref // mult) * mult
    while t >= mult:
        if n % t == 0:
            return t
        t -= mult
    return n


def _ln(x):
    mu = jnp.mean(x, axis=-1, keepdims=True)
    xc = x - mu
    return xc * lax.rsqrt(jnp.mean(xc * xc, axis=-1, keepdims=True) + LN_EPS)


def _ln_mod_kernel(x_ref, xc_ref, sh_ref, sc_ref, shc_ref, scc_ref, o_ref, *, n_lat_tiles):
    is_latent = pl.program_id(1) < n_lat_tiles

    @pl.when(is_latent)
    def _():
        o_ref[0] = (_ln(x_ref[0]) * (1.0 + sc_ref[0]) + sh_ref[0]).astype(o_ref.dtype)

    @pl.when(jnp.logical_not(is_latent))
    def _():
        o_ref[0] = (_ln(xc_ref[0]) * (1.0 + scc_ref[0]) + shc_ref[0]).astype(o_ref.dtype)


def ln_mod(x, xc, shift, scale, shift_c, scale_c):
    B, N, D = x.shape
    Nc = xc.shape[1]
    tm = _tile(Nc, 256, 16)
    assert N % tm == 0
    nl, nc = N // tm, Nc // tm
    per_b = lambda b, i: (b, 0, 0)
    return pl.pallas_call(
        functools.partial(_ln_mod_kernel, n_lat_tiles=nl),
        out_shape=jax.ShapeDtypeStruct((B, N + Nc, D), BF16),
        grid=(B, nl + nc),
        in_specs=[pl.BlockSpec((1, tm, D), lambda b, i: (b, jnp.minimum(i, nl - 1), 0)),
                  pl.BlockSpec((1, tm, D), lambda b, i: (b, jnp.maximum(i - nl, 0), 0)),
                  pl.BlockSpec((1, 1, D), per_b), pl.BlockSpec((1, 1, D), per_b),
                  pl.BlockSpec((1, 1, D), per_b), pl.BlockSpec((1, 1, D), per_b)],
        out_specs=pl.BlockSpec((1, tm, D), lambda b, i: (b, i, 0)),
        compiler_params=_cp(2), name="ln_mod",
    )(x, xc, shift, scale, shift_c, scale_c)


def _resid_ln_tail(x, y, gate, g, b, alpha):
    return _ln(alpha * x + gate * y) * g + b


def _mm_kernel(a_ref, w_ref, o_ref, *, act):
    acc = jnp.dot(a_ref[...], w_ref[...], preferred_element_type=F32)
    if act == "sigmoid":
        acc = jax.nn.sigmoid(acc)
    o_ref[...] = acc.astype(o_ref.dtype)


def matmul(a, w, out_dtype, act=None, tm_pref=1024, tn_pref=512):
    M, K = a.shape
    Nn = w.shape[1]
    tm = _tile(M, tm_pref, 16)
    tn = _tile(Nn, tn_pref, 128)
    return pl.pallas_call(
        functools.partial(_mm_kernel, act=act),
        out_shape=jax.ShapeDtypeStruct((M, Nn), out_dtype),
        grid=(M // tm, Nn // tn),
        in_specs=[pl.BlockSpec((tm, K), lambda i, j: (i, 0)),
                  pl.BlockSpec((K, tn), lambda i, j: (0, j))],
        out_specs=pl.BlockSpec((tm, tn), lambda i, j: (i, j)),
        compiler_params=_cp(2), name="matmul",
    )(a, w)


def _proj_t_kernel(w_ref, h_ref, g_ref, cos_ref, sin_ref, o_ref, *, rms, rope):
    acc = lax.dot_general(w_ref[...], h_ref[0], (((1,), (1,)), ((), ())),
                          preferred_element_type=F32)
    n_heads = acc.shape[0] // HEAD_DIM
    if rope:
        c = cos_ref[...]
        s = sin_ref[...]
    for hd in range(n_heads):
        lo = hd * HEAD_DIM
        x = acc[lo:lo + HEAD_DIM]
        if rms:
            x = x * lax.rsqrt(jnp.mean(x * x, axis=0, keepdims=True) + LN_EPS)
        x = x * g_ref[lo:lo + HEAD_DIM]
        if rope:
            x1, x2 = x[:HALF], x[HALF:]
            o_ref[0, lo:lo + HALF] = (x1 * c - x2 * s).astype(o_ref.dtype)
            o_ref[0, lo + HALF:lo + HEAD_DIM] = (x1 * s + x2 * c).astype(o_ref.dtype)
        else:
            o_ref[0, lo:lo + HEAD_DIM] = x.astype(o_ref.dtype)


def proj_t(w_t, h, gain, cos_t, sin_t, *, rms, rope):
    B, N, K = h.shape
    Fdim = w_t.shape[0]
    tt = _tile(N, 512, 128)
    return pl.pallas_call(
        functools.partial(_proj_t_kernel, rms=rms, rope=rope),
        out_shape=jax.ShapeDtypeStruct((B, Fdim, N), BF16),
        grid=(B, N // tt),
        in_specs=[pl.BlockSpec((Fdim, K), lambda b, i: (0, 0)),
                  pl.BlockSpec((1, tt, K), lambda b, i: (b, i, 0)),
                  pl.BlockSpec((Fdim, 1), lambda b, i: (0, 0)),
                  pl.BlockSpec((HALF, tt), lambda b, i: (0, i)),
                  pl.BlockSpec((HALF, tt), lambda b, i: (0, i))],
        out_specs=pl.BlockSpec((1, Fdim, tt), lambda b, i: (b, 0, i)),
        compiler_params=_cp(2), name="proj_t",
    )(w_t, h, gain, cos_t, sin_t)


V_PAD = 16


def _proj_v_kernel(w_ref, h_ref, ones_ref, o_ref):
    acc = lax.dot_general(w_ref[...], h_ref[0], (((1,), (1,)), ((), ())), preferred_element_type=F32)
    o_ref[0] = (acc + ones_ref[...]).astype(o_ref.dtype)


def proj_v(w_t, h, dv):
    B, N, K = h.shape
    heads = w_t.shape[0] // dv
    dvp = dv + V_PAD
    w_aug = jnp.pad(w_t.reshape(heads, dv, K), ((0, 0), (0, V_PAD), (0, 0))).reshape(heads * dvp, K)
    ones_col = jnp.zeros((heads, dvp, 1), F32).at[:, dv].set(1.0).reshape(heads * dvp, 1)
    tt = _tile(N, 512, 128)
    return pl.pallas_call(
        _proj_v_kernel,
        out_shape=jax.ShapeDtypeStruct((B, heads * dvp, N), BF16),
        grid=(B, N // tt),
        in_specs=[pl.BlockSpec((heads * dvp, K), lambda b, i: (0, 0)),
                  pl.BlockSpec((1, tt, K), lambda b, i: (b, i, 0)),
                  pl.BlockSpec((heads * dvp, 1), lambda b, i: (0, 0))],
        out_specs=pl.BlockSpec((1, heads * dvp, tt), lambda b, i: (b, 0, i)),
        compiler_params=_cp(2), name="proj_v",
    )(w_aug, h, ones_col)


FLASH_MIN_DENOM = 2.0 ** -60


def _flash_kernel(lam_ref, kmax_ref, q_ref, k_ref, v_ref, g_ref, o_ref, q_sc, p_buf, sh_sc, acc_sc,
                  *, mode, nj, tk, dv, post_scale, n_kheads, khead0):
    b, h = pl.program_id(0), pl.program_id(1)
    qf = q_ref[0].astype(F32)
    tq = qf.shape[1]
    zero = jnp.zeros((HEAD_DIM, tq), F32)
    qa, qb = qf[:HEAD_DIM], qf[HEAD_DIM:]
    if mode == "da":
        q_sc[0] = jnp.concatenate([qa, zero], axis=0).astype(BF16)
        q_sc[1] = jnp.concatenate([zero, qb], axis=0).astype(BF16)
        k_heads = (khead0 + 2 * h, khead0 + 2 * h + 1)
    else:
        group = h // 2
        for mi, qh in enumerate((qa, qb)):
            q_sc[mi] = jnp.where(group == 0, jnp.concatenate([qh, zero], axis=0),
                                 jnp.concatenate([zero, qh], axis=0)).astype(BF16)
        k_heads = (khead0 + group, khead0 + group)
    for mi, qh in enumerate((qa, qb)):
        q_norm = jnp.sqrt(jnp.sum(qh * qh, axis=0, keepdims=True))
        sh_sc[mi] = q_norm * kmax_ref[b * n_kheads + k_heads[mi]]

    def chunk(j):
        start = j * tk
        return pl.ds(start if isinstance(start, int) else pl.multiple_of(start, tk), tk)

    def stage_exp(j, slot):
        kb = k_ref[0, chunk(j), :]
        for mi in range(2):
            s = jnp.dot(kb, q_sc[mi], preferred_element_type=F32)
            p_buf[slot, mi] = jnp.exp2(s - sh_sc[mi]).astype(BF16)

    def stage_values(j, slot):
        vb = v_ref[0, :, chunk(j)]
        for mi in range(2):
            acc_sc[mi] += jnp.dot(vb, p_buf[slot, mi], preferred_element_type=F32)

    def accumulate():
        acc_sc[...] = jnp.zeros(acc_sc.shape, F32)
        stage_exp(0, 0)
        steady = list(range(1, nj))
        if len(steady) % 2 == 1:
            t = steady.pop(0)
            stage_exp(t, t % 2)
            stage_values(t - 1, (t - 1) % 2)
        if steady:
            t0 = steady[0]

            def body(i, carry):
                t = t0 + 2 * i
                stage_exp(t, t0 % 2)
                stage_values(t - 1, 1 - t0 % 2)
                stage_exp(t + 1, 1 - t0 % 2)
                stage_values(t, t0 % 2)
                return carry

            lax.fori_loop(0, len(steady) // 2, body, 0)
        stage_values(nj - 1, (nj - 1) % 2)

    accumulate()
    denom_min = jnp.minimum(jnp.min(acc_sc[0][dv:dv + 1]), jnp.min(acc_sc[1][dv:dv + 1]))

    @pl.when(jnp.logical_not(denom_min >= FLASH_MIN_DENOM))
    def _():
        for mi in range(2):
            def max_body(j, m, mi=mi):
                s = jnp.dot(k_ref[0, chunk(j), :], q_sc[mi], preferred_element_type=F32)
                return jnp.maximum(m, jnp.max(s, axis=0, keepdims=True))

            sh_sc[mi] = lax.fori_loop(0, nj, max_body, jnp.full((1, tq), NEG_BIG, F32))
        accumulate()

    outs = []
    for mi in range(2):
        acc = acc_sc[mi]
        outs.append(acc[:dv] / acc[dv:dv + 1])
    if mode == "da":
        o = outs[0] - lam_ref[0] * outs[1]
        o = o * lax.rsqrt(jnp.mean(o * o, axis=0, keepdims=True) + LN_EPS)
        o = o * (g_ref[...] * post_scale)
    else:
        o = jnp.concatenate(outs, axis=0)
    o_ref[0] = o.T.astype(o_ref.dtype)


def _knorm_kernel(k_ref, o_ref):
    k = k_ref[0].astype(F32)
    n2 = jnp.sum(k * k, axis=0, keepdims=True)
    o_ref[0, 0] = jnp.broadcast_to(jnp.max(n2, axis=1, keepdims=True), o_ref.shape[2:])


def key_norm_max(k_t):
    B, C, Nk = k_t.shape
    heads = C // HEAD_DIM
    out = pl.pallas_call(
        _knorm_kernel,
        out_shape=jax.ShapeDtypeStruct((B, heads, 8, 128), F32),
        grid=(B, heads),
        in_specs=[pl.BlockSpec((1, HEAD_DIM, Nk), lambda b, h: (b, h, 0))],
        out_specs=pl.BlockSpec((1, 1, 8, 128), lambda b, h: (b, h, 0, 0)),
        compiler_params=_cp(2), name="key_norm_max",
    )(k_t)
    return jnp.sqrt(out[:, :, 0, 0]).reshape(B * heads)


def flash_attention(q_t, k_tok, v_aug, kmax, lam, gain, *, mode, q_span, kv_span, q_row_block, k_col_block,
                    khead0, post_scale=1.0, tk_pref=768):
    B = q_t.shape[0]
    q0, Nq = q_span
    k0, Nk = kv_span
    tq = _tile(Nq, 1024, 128)
    tk = _tile(Nk, tk_pref, 128)
    assert q0 % tq == 0 and k0 % Nk == 0
    nj = Nk // tk
    dv = 2 * HEAD_DIM if mode == "da" else HEAD_DIM
    dvp = dv + V_PAD
    n_kheads = kmax.shape[0] // B
    if mode == "da":
        k_map = lambda b, h, i: (b, k0 // Nk, k_col_block + h)
        v_map = lambda b, h, i: (b, h, k0 // Nk)
    else:
        k_map = lambda b, h, i: (b, k0 // Nk, k_col_block)
        v_map = lambda b, h, i: (b, h // 2, k0 // Nk)
    return pl.pallas_call(
        functools.partial(_flash_kernel, mode=mode, nj=nj, tk=tk, dv=dv, post_scale=post_scale,
                          n_kheads=n_kheads, khead0=khead0),
        out_shape=jax.ShapeDtypeStruct((B, Nq, 4 * 128), BF16),
        grid=(B, 4, Nq // tq),
        in_specs=[pl.BlockSpec(memory_space=pltpu.SMEM),
                  pl.BlockSpec(memory_space=pltpu.SMEM),
                  pl.BlockSpec((1, 128, tq), lambda b, h, i: (b, h + q_row_block, i + q0 // tq)),
                  pl.BlockSpec((1, Nk, 128), k_map),
                  pl.BlockSpec((1, dvp, Nk), v_map),
                  pl.BlockSpec((128, 1), lambda b, h, i: (0, 0))],
        out_specs=pl.BlockSpec((1, tq, 128), lambda b, h, i: (b, i, h)),
        scratch_shapes=[pltpu.VMEM((2, 128, tq), BF16),
                        pltpu.VMEM((2, 2, tk, tq), BF16),
                        pltpu.VMEM((2, 1, tq), F32),
                        pltpu.VMEM((2, dvp, tq), F32)],
        compiler_params=_cp(3), name="flash_" + mode,
    )(lam, kmax, q_t, k_tok, v_aug, gain)


def _shift_rows(x, prev_row, next_row):
    T = x.shape[0]
    row = lax.broadcasted_iota(jnp.int32, x.shape, 0)
    xp = jnp.where(row == 0, prev_row, pltpu.roll(x, 1, axis=0))
    xn = jnp.where(row == T - 1, next_row, pltpu.roll(x, T - 1, axis=0))
    return xp, xn


def _conv3_block(x_ref, p_ref, n_ref, w_ref, b_ref, halo):
    i = pl.program_id(1)
    last = pl.num_programs(1) - 1
    x = x_ref[0].astype(F32)
    prev_row = jnp.where(i > 0, p_ref[0].astype(F32)[halo - 1:halo], 0.0)
    next_row = jnp.where(i < last, n_ref[0].astype(F32)[0:1], 0.0)
    xp, xn = _shift_rows(x, prev_row, next_row)
    return xp * w_ref[0:1] + x * w_ref[1:2] + xn * w_ref[2:3] + b_ref[...]


def _halo_specs(tr, tc, halo, row0, n_rows, col_of):
    per = tr // halo
    n_halo = n_rows // halo
    m0, h0 = row0 // tr, row0 // halo
    return [pl.BlockSpec((1, tr, tc), lambda b, i, j: (b, m0 + i, col_of(j))),
            pl.BlockSpec((1, halo, tc), lambda b, i, j: (b, h0 + jnp.maximum(i * per - 1, 0), col_of(j))),
            pl.BlockSpec((1, halo, tc),
                         lambda b, i, j: (b, h0 + jnp.minimum((i + 1) * per, n_halo - 1), col_of(j)))]


def _dwconv_kernel(x_ref, p_ref, n_ref, w_ref, b_ref, v_ref, x1_ref, x2_ref):
    y = _conv3_block(x_ref, p_ref, n_ref, w_ref, b_ref, 8)
    C = v_ref.shape[2]
    v_ref[0] = y[:, :C]
    x1_ref[0] = y[:, C:2 * C]
    x2_ref[0] = y[:, 2 * C:]


def hyena_short_conv(u, w, b, row0, L):
    B, _, C3 = u.shape
    C = C3 // 3
    tr = _tile(L, 512, 8)
    assert row0 % tr == 0
    zero = lambda j: 0
    out = jax.ShapeDtypeStruct((B, L, C), F32)
    ospec = pl.BlockSpec((1, tr, C), lambda b, i, j: (b, i, 0))
    return pl.pallas_call(
        _dwconv_kernel,
        out_shape=(out, out, out),
        grid=(B, L // tr, 1),
        in_specs=_halo_specs(tr, C3, 8, row0, L, zero) + [
            pl.BlockSpec((3, C3), lambda b, i, j: (0, 0)),
            pl.BlockSpec((1, C3), lambda b, i, j: (0, 0))],
        out_specs=(ospec, ospec, ospec),
        compiler_params=_cp(3), name="hyena_short_conv",
    )(u, u, u, w, b)


FFN_HALO = 16
FFN_CHUNK = 256


def _ffn_kernel(h_ref, hp_ref, hn_ref, wup_ref, cw_ref, cb_ref, wdn_ref, x_ref, gate_ref, g_ref, b_ref, o_ref,
                *, alpha):
    i = pl.program_id(1)
    tm = h_ref.shape[1]
    dff = wdn_ref.shape[0]
    rows = tm + 2 * FFN_HALO
    h_prev = jnp.where(i == 0, jnp.zeros_like(hp_ref[0]), hp_ref[0])
    h_next = jnp.where(i == pl.num_programs(1) - 1, jnp.zeros_like(hn_ref[0]), hn_ref[0])
    h_ext = jnp.concatenate([h_prev, h_ref[0], h_next], axis=0)

    def conv_cols(c0):
        u = jnp.dot(h_ext, wup_ref[:, c0:c0 + FFN_CHUNK], preferred_element_type=F32)
        mid = slice(FFN_HALO, FFN_HALO + tm)
        u_prev = pltpu.roll(u, 1, axis=0)[mid]
        u_next = pltpu.roll(u, rows - 1, axis=0)[mid]
        w = cw_ref[:, c0:c0 + FFN_CHUNK]
        return u_prev * w[0:1] + u[mid] * w[1:2] + u_next * w[2:3] + cb_ref[:, c0:c0 + FFN_CHUNK]

    y = jnp.zeros((tm, wdn_ref.shape[1]), F32)
    for c0 in range(0, dff, FFN_CHUNK):
        val = conv_cols(c0)
        gate = conv_cols(dff + c0)
        act = (gate * jax.nn.sigmoid(gate) * val).astype(BF16)
        y = y + jnp.dot(act, wdn_ref[c0:c0 + FFN_CHUNK, :], preferred_element_type=F32)
    o_ref[0] = _resid_ln_tail(x_ref[0], y, gate_ref[0], g_ref[...], b_ref[...], alpha)


def conv_ffn_resid_ln(h, w_up, conv_w, conv_b, w_down, x, gate, g, b, alpha):
    B, N, D = h.shape
    dff = w_down.shape[0]
    tm = _tile(N, 512, FFN_HALO)
    per = tm // FFN_HALO
    n_halo = N // FFN_HALO
    row = lambda b_, i: (b_, i, 0)
    per_b = lambda b_, i: (b_, 0, 0)
    const2 = lambda b_, i: (0, 0)
    resident = dict(pipeline_mode=pl.Buffered(1))
    return pl.pallas_call(
        functools.partial(_ffn_kernel, alpha=alpha),
        out_shape=jax.ShapeDtypeStruct((B, N, D), F32),
        grid=(B, N // tm),
        in_specs=[pl.BlockSpec((1, tm, D), row),
                  pl.BlockSpec((1, FFN_HALO, D), lambda b_, i: (b_, jnp.maximum(i * per - 1, 0), 0)),
                  pl.BlockSpec((1, FFN_HALO, D), lambda b_, i: (b_, jnp.minimum((i + 1) * per, n_halo - 1), 0)),
                  pl.BlockSpec((D, 2 * dff), const2, **resident),
                  pl.BlockSpec((3, 2 * dff), const2), pl.BlockSpec((1, 2 * dff), const2),
                  pl.BlockSpec((dff, D), const2, **resident),
                  pl.BlockSpec((1, tm, D), row), pl.BlockSpec((1, 1, D), per_b),
                  pl.BlockSpec((1, D), const2), pl.BlockSpec((1, D), const2)],
        out_specs=pl.BlockSpec((1, tm, D), row),
        compiler_params=_cp(2), name="conv_ffn_resid_ln",
    )(h, h, h, w_up, conv_w, conv_b, w_down, x, gate, g, b)


def _merge_kernel(a_ref, b_ref, c_ref, gt_ref, wpa_ref, wpb_ref, wpc_ref, wo_ref, x_ref, gate_ref,
                  g_ref, bb_ref, sh_ref, sc_ref, ox_ref, oh_ref, *, alpha):
    D = wo_ref.shape[0]
    gt = gt_ref[0]
    m = gt[:, :D].astype(F32) * jnp.dot(a_ref[0], wpa_ref[...], preferred_element_type=F32)
    m = m + gt[:, D:2 * D].astype(F32) * jnp.dot(b_ref[0], wpb_ref[...], preferred_element_type=F32)
    m = m + gt[:, 2 * D:].astype(F32) * jnp.dot(c_ref[0].astype(BF16), wpc_ref[...],
                                                preferred_element_type=F32)
    y = jnp.dot(m.astype(BF16), wo_ref[...], preferred_element_type=F32)
    xn = _resid_ln_tail(x_ref[0], y, gate_ref[0], g_ref[...], bb_ref[...], alpha)
    ox_ref[0] = xn
    oh_ref[0] = (_ln(xn) * (1.0 + sc_ref[0]) + sh_ref[0]).astype(oh_ref.dtype)


def merge_resid_ln(a, b, c, gates, tok0, wpa, wpb, wpc, wo, x, gate, g, bb, sh, sc, alpha):
    B, N, D = x.shape
    W = a.shape[2]
    tm = _tile(N, 512, 16)
    assert tok0 % tm == 0
    row = lambda b_, i: (b_, i, 0)
    per_b = lambda b_, i: (b_, 0, 0)
    const2 = lambda b_, i: (0, 0)
    return pl.pallas_call(
        functools.partial(_merge_kernel, alpha=alpha),
        out_shape=(jax.ShapeDtypeStruct((B, N, D), F32), jax.ShapeDtypeStruct((B, N, D), BF16)),
        grid=(B, N // tm),
        in_specs=[pl.BlockSpec((1, tm, W), row), pl.BlockSpec((1, tm, W), row), pl.BlockSpec((1, tm, W), row),
                  pl.BlockSpec((1, tm, 3 * D), lambda b_, i: (b_, tok0 // tm + i, 0)),
                  pl.BlockSpec((W, D), const2), pl.BlockSpec((W, D), const2), pl.BlockSpec((W, D), const2),
                  pl.BlockSpec((D, D), const2),
                  pl.BlockSpec((1, tm, D), row), pl.BlockSpec((1, 1, D), per_b),
                  pl.BlockSpec((1, D), const2), pl.BlockSpec((1, D), const2),
                  pl.BlockSpec((1, 1, D), per_b), pl.BlockSpec((1, 1, D), per_b)],
        out_specs=(pl.BlockSpec((1, tm, D), row), pl.BlockSpec((1, tm, D), row)),
        compiler_params=_cp(2), name="merge_resid_ln",
    )(a, b, c, gates, wpa, wpb, wpc, wo, x, gate, g, bb, sh, sc)


def _split_bf16(x):
    hi = x.astype(BF16)
    lo = (x - hi.astype(F32)).astype(BF16)
    return hi, lo


def _dot3(a, b):
    ah, al = _split_bf16(a)
    bh, bl = _split_bf16(b)
    d = functools.partial(jnp.dot, preferred_element_type=F32)
    return d(ah, bh) + (d(ah, bl) + d(al, bh))


def _filter_kernel(z_ref, w1_ref, b1_ref, w2_ref, b2_ref, w3_ref, fr_ref, dec_ref, h_ref, s_ref, *, zero_block):
    i = pl.program_id(0)
    fr = fr_ref[...]
    hid = jnp.sin(fr * (_dot3(z_ref[...], w1_ref[...]) + b1_ref[...]))
    hid = jnp.sin(fr * (_dot3(hid, w2_ref[...]) + b2_ref[...]))
    h = _dot3(hid, w3_ref[0])
    dec = dec_ref[...]
    C = dec.shape[1]
    n_ord = h.shape[1] // C
    h = h * jnp.concatenate([dec] * n_ord, axis=1)

    @pl.when(i == 0)
    def _():
        s_ref[...] = jnp.zeros(s_ref.shape, F32)

    s_ref[...] += jnp.sum(jnp.abs(h), axis=0, keepdims=True)
    row = lax.broadcasted_iota(jnp.int32, h.shape, 0)
    h = jnp.where((row == 0) & (i == zero_block), 0.0, h)
    for o in range(n_ord):
        h_ref[o] = h[:, o * C:(o + 1) * C]


def hyena_filter_mlp(z2, w1, b1, w2, b2, w3_dir, freq, decay2):
    n2, E = z2.shape
    Hh = w2.shape[0]
    OC = w3_dir.shape[2]
    C = decay2.shape[1]
    n_ord = OC // C
    L = n2 // 2
    tr = _tile(L, 512, 8)
    nblk = n2 // tr
    c2 = lambda i: (0, 0)
    return pl.pallas_call(
        functools.partial(_filter_kernel, zero_block=L // tr),
        out_shape=(jax.ShapeDtypeStruct((n_ord, n2, C), F32), jax.ShapeDtypeStruct((1, OC), F32)),
        grid=(nblk,),
        in_specs=[pl.BlockSpec((tr, E), lambda i: (i, 0)), pl.BlockSpec((E, Hh), c2), pl.BlockSpec((1, Hh), c2),
                  pl.BlockSpec((Hh, Hh), c2), pl.BlockSpec((1, Hh), c2),
                  pl.BlockSpec((1, Hh, OC), lambda i: (i // (nblk // 2), 0, 0)),
                  pl.BlockSpec((1, Hh), c2), pl.BlockSpec((tr, C), lambda i: (i, 0))],
        out_specs=(pl.BlockSpec((n_ord, tr, C), lambda i: (0, i, 0)), pl.BlockSpec((1, OC), c2)),
        compiler_params=_cp(1), name="hyena_filter_mlp",
    )(z2, w1, b1, w2, b2, w3_dir, freq, decay2)


def _mm_split(m_hi, m_lo, x, passes):
    d = functools.partial(jnp.dot, preferred_element_type=F32)
    if passes == 1:
        return d(m_hi, x.astype(BF16))
    xh, xl = _split_bf16(x)
    return d(m_hi, xh) + (d(m_lo, xh) + d(m_hi, xl))


FFT_ROWS = 8


def _level1_kernel(mh_ref, ml_ref, x_ref, o_ref, *, passes):
    r_in, r_out = x_ref.shape[1], o_ref.shape[1]
    x = x_ref[0].reshape(r_in * FFT_ROWS, x_ref.shape[3])
    y = _mm_split(mh_ref[...], ml_ref[...], x, passes)
    o_ref[0] = y.reshape(r_out, FFT_ROWS, x_ref.shape[3]).astype(o_ref.dtype)


def _level1_gate_kernel(mh_ref, ml_ref, x_ref, z_ref, xg_ref, bias_ref, o_ref, *, passes):
    r_in, r_out = x_ref.shape[1], o_ref.shape[1]
    x = x_ref[0].reshape(r_in * FFT_ROWS, x_ref.shape[3])
    y = _mm_split(mh_ref[...], ml_ref[...], x, passes).reshape(r_out, FFT_ROWS, x_ref.shape[3])
    o_ref[0] = (xg_ref[0] * (y + bias_ref[...] * z_ref[0])).astype(o_ref.dtype)


def fft_level1(m_hi, m_lo, x, passes, gate_args=None, out_dtype=F32):
    P, R_in, Nb, C = x.shape
    R_out = m_hi.shape[0] // FFT_ROWS
    mspec = pl.BlockSpec(m_hi.shape, lambda p, j: (0, 0))
    xspec = pl.BlockSpec((1, R_in, FFT_ROWS, C), lambda p, j: (p, 0, j, 0))
    ospec = pl.BlockSpec((1, R_out, FFT_ROWS, C), lambda p, j: (p, 0, j, 0))
    if gate_args is None:
        kern = functools.partial(_level1_kernel, passes=passes)
        ins, specs = (m_hi, m_lo, x), [mspec, mspec, xspec]
    else:
        z, xg, bias = gate_args
        kern = functools.partial(_level1_gate_kernel, passes=passes)
        ins = (m_hi, m_lo, x, z, xg, bias)
        specs = [mspec, mspec, xspec, ospec, ospec, pl.BlockSpec((1, C), lambda p, j: (0, 0))]
    return pl.pallas_call(
        kern, out_shape=jax.ShapeDtypeStruct((P, R_out, Nb, C), out_dtype),
        grid=(P, Nb // FFT_ROWS), in_specs=specs, out_specs=ospec,
        compiler_params=_cp(2), name="fft_level1",
    )(*ins)


FFT_MID_K1 = 2


def _fft_mid_kernel(*refs, passes):
    n_mat = 2 if passes == 1 else 4
    gh_ref, gih_ref = refs[0], refs[1]
    gl_ref, gil_ref = (refs[2], refs[3]) if passes > 1 else (None, None)
    h_ref, a_ref, o_ref = refs[n_mat:]
    n_pairs, _, kb, nb, C = a_ref.shape
    for kk in range(kb):
        hr, hi = h_ref[0, kk], h_ref[1, kk]
        for p in range(n_pairs):
            x = a_ref[p, :, kk].reshape(2 * nb, C)
            X = _mm_split(gh_ref[kk], None if gl_ref is None else gl_ref[kk], x, passes)
            xr, xi = X[:nb], X[nb:]
            Y = jnp.concatenate([xr * hr - xi * hi, xr * hi + xi * hr], axis=0)
            Bv = _mm_split(gih_ref[kk], None if gil_ref is None else gil_ref[kk], Y, passes)
            o_ref[p, :, kk] = Bv.reshape(2, nb, C)


def fft_mid(g_hi, g_lo, gi_hi, gi_lo, hspec, a5, passes):
    P, _, Na, Nb, C = a5.shape
    kb = FFT_MID_K1
    gspec = pl.BlockSpec((kb, 2 * Nb, 2 * Nb), lambda k: (k, 0, 0))
    mats = (g_hi, gi_hi) if passes == 1 else (g_hi, gi_hi, g_lo, gi_lo)
    blk = pl.BlockSpec((P, 2, kb, Nb, C), lambda k: (0, 0, k, 0, 0))
    return pl.pallas_call(
        functools.partial(_fft_mid_kernel, passes=passes),
        out_shape=jax.ShapeDtypeStruct(a5.shape, F32),
        grid=(Na // kb,),
        in_specs=[gspec] * len(mats) + [pl.BlockSpec((2, kb, Nb, C), lambda k: (0, k, 0, 0)), blk],
        out_specs=blk,
        compiler_params=_cp(1), name="fft_mid",
    )(*mats, hspec, a5)


def _fft_spec_kernel(gh_ref, gl_ref, inv_ref, a_ref, o_ref, *, passes):
    nb = a_ref.shape[3]
    x = a_ref[0, :, 0].reshape(2 * nb, a_ref.shape[4])
    X = _mm_split(gh_ref[0], gl_ref[0], x, passes) * inv_ref[0]
    o_ref[0, :, 0] = X.reshape(2, nb, a_ref.shape[4])


def fft_filter_spectrum(g_hi, g_lo, inv_norm, a5, passes):
    P, _, Na, Nb, C = a5.shape
    gspec = pl.BlockSpec((1, 2 * Nb, 2 * Nb), lambda k, p: (k, 0, 0))
    blk = pl.BlockSpec((1, 2, 1, Nb, C), lambda k, p: (p, 0, k, 0, 0))
    return pl.pallas_call(
        functools.partial(_fft_spec_kernel, passes=passes),
        out_shape=jax.ShapeDtypeStruct(a5.shape, F32),
        grid=(Na, P),
        in_specs=[gspec, gspec, pl.BlockSpec((1, 1, C), lambda k, p: (p, 0, 0)), blk],
        out_specs=blk,
        compiler_params=_cp(2), name="fft_filter_spectrum",
    )(g_hi, g_lo, inv_norm, a5)


def _fft_factors(n_fft):
    na = 1 << (int(math.log2(n_fft)) // 2)
    return na, n_fft // na


def _fft_tables(L):
    n_fft = 2 * L
    na, nb = _fft_factors(n_fft)

    def cis(num, den):
        ang = (2.0 * math.pi / den) * (num % den).astype(F32)
        return jnp.cos(ang), -jnp.sin(ang)

    ia = jnp.arange(na, dtype=jnp.int32)
    ib = jnp.arange(nb, dtype=jnp.int32)
    far, fai = cis(ia[:, None] * ia[None, :], na)
    fh_r, fh_i = far[:, :na // 2], fai[:, :na // 2]
    m_fwd = jnp.block([[fh_r, -fh_i], [fh_i, fh_r]])
    m_flt = jnp.concatenate([far, fai], axis=0)
    m_inv = jnp.block([[fh_r.T, fh_i.T], [-fh_i.T, fh_r.T]]) / n_fft
    num = ib[None, :, None] * ib[None, None, :] * na + ia[:, None, None] * ib[None, None, :]
    gr, gi = cis(num, n_fft)
    g = jnp.concatenate([jnp.concatenate([gr, -gi], axis=2), jnp.concatenate([gi, gr], axis=2)], axis=1)
    grt, git = jnp.swapaxes(gr, 1, 2), jnp.swapaxes(gi, 1, 2)
    ginv = jnp.concatenate([jnp.concatenate([grt, git], axis=2), jnp.concatenate([-git, grt], axis=2)], axis=1)
    split = lambda m: _split_bf16(m.astype(F32))
    kron = lambda m: split(jnp.kron(m, jnp.eye(FFT_ROWS, dtype=F32)))
    C = HY_WIDTH
    t = jnp.linspace(0.0, 1.0, L, dtype=F32)[:, None]
    f = jnp.linspace(1e-4, HY_BANDS - 1, HY_BANDS, dtype=F32)
    ang = (2.0 * math.pi / L) * jnp.arange(L, dtype=F32)[:, None] * f[None, :]
    z = jnp.concatenate([t, jnp.cos(ang), -jnp.sin(ang)], axis=-1)
    z = jnp.pad(z, ((0, 0), (0, HY_EMB_PAD - HY_EMB)))
    deltas = jnp.abs(jnp.linspace(HY_MIN_DECAY, HY_MAX_DECAY, C, dtype=F32))
    decay = jnp.exp(-t * deltas)
    circ = lambda a: jnp.concatenate([a, a[:1], a[:0:-1]], axis=0)
    return dict(na=na, nb=nb, fwd=kron(m_fwd), flt=kron(m_flt), inv=kron(m_inv), g=split(g), ginv=split(ginv),
                z2=circ(z), decay2=circ(decay))


def hyena_filter_spectra(w1, b1, w2, b2, w3, freq, tabs, passes):
    C = HY_WIDTH
    na, nb = tabs["na"], tabs["nb"]
    w1p = jnp.pad(w1, ((0, HY_EMB_PAD - HY_EMB), (0, 0)))
    w3_dir = jnp.transpose(w3.reshape(HY_HIDDEN, HY_ORDER, 2, C), (2, 0, 1, 3)).reshape(2, HY_HIDDEN, HY_ORDER * C)
    filt, s = hyena_filter_mlp(tabs["z2"], w1p, b1[None], w2, b2[None], w3_dir, freq[None], tabs["decay2"])
    inv_norm = (1.0 / s).reshape(HY_ORDER, 1, C)
    a = fft_level1(*tabs["flt"], filt.reshape(HY_ORDER, na, nb, C), passes)
    return fft_filter_spectrum(*tabs["g"], inv_norm, a.reshape(HY_ORDER, 2, na, nb, C), passes)


def hyena_long_conv_gate(z, xg, bias, hspec, tabs, passes):
    B, L, C = z.shape
    na, nb = tabs["na"], tabs["nb"]
    P = B // 2
    nat = lambda a: a.reshape(P, na, nb, C)
    a = fft_level1(*tabs["fwd"], nat(z), passes)
    bv = fft_mid(*tabs["g"], *tabs["ginv"], hspec, a.reshape(P, 2, na, nb, C), passes)
    y = fft_level1(*tabs["inv"], bv.reshape(P, 2 * na, nb, C), passes,
                   gate_args=(nat(z), nat(xg), bias[None]))
    return y.reshape(B, L, C)


def hyena_mix(u, row0, L, spectra, conv_w, conv_b, bias, tabs, passes):
    v, x1, x2 = hyena_short_conv(u, conv_w, conv_b[None], row0, L)
    zz = hyena_long_conv_gate(v, x1, bias[0], spectra[0], tabs, passes)
    return hyena_long_conv_gate(zz, x2, bias[1], spectra[1], tabs, passes)


FFT_PASSES = 1
FILTER_PASSES = 3


def _rope_tables_t(n):
    rows = n // GRID_W
    r = jnp.repeat(jnp.arange(rows, dtype=F32), GRID_W)
    col = jnp.tile(jnp.arange(GRID_W, dtype=F32), rows)
    axis_dim = HEAD_DIM // 2
    inv = ROPE_THETA ** (-jnp.arange(0, axis_dim, 2, dtype=F32) / axis_dim)
    ang = jnp.concatenate([r[:, None] * inv, col[:, None] * inv], axis=-1)
    return jnp.cos(ang).T, jnp.sin(ang).T


def _layer_weights(l, w_in, gq_qn, gq_kn):
    w = w_in[l]
    d_qk = DA_HEADS * 2 * HEAD_DIM
    gq_kv = GKV_HEADS * HEAD_DIM
    gq_q = GQ_HEADS * HEAD_DIM
    o = np.cumsum([0, d_qk, d_qk, gq_kv, gq_kv, d_qk, gq_q, 3 * HY_WIDTH])
    ka, va, kb, vb, qa, qb, hy = (w[:, o[i]:o[i + 1]] for i in range(7))
    gates = w[:, o[7]:]
    qscale = (HEAD_DIM ** -0.5) * LOG2E
    t = lambda parts: jnp.concatenate(parts, axis=1).T.astype(BF16)
    w_rope = t([ka, qa])
    g_rope = jnp.concatenate([jnp.ones((d_qk,), F32), jnp.full((d_qk,), qscale, F32)])[:, None]
    w_rms = t([kb, qb])
    g_rms = jnp.concatenate([jnp.tile(gq_kn[l], GKV_HEADS), jnp.tile(gq_qn[l], GQ_HEADS) * qscale])[:, None]
    return dict(w_rope=w_rope, g_rope=g_rope, w_rms=w_rms, g_rms=g_rms, w_va=va.T.astype(BF16),
                w_vb=vb.T.astype(BF16), w_hy=hy.astype(BF16), w_gates=gates.astype(BF16), d_qk=d_qk, gq_kv=gq_kv)


def _project(h, lw, cos_t, sin_t):
    B, N, D = h.shape
    d_qk, gq_kv = lw["d_qk"], lw["gq_kv"]
    rope_o = proj_t(lw["w_rope"], h, lw["g_rope"], cos_t, sin_t, rms=False, rope=True)
    rms_o = proj_t(lw["w_rms"], h, lw["g_rms"], cos_t, sin_t, rms=True, rope=True)
    k_t = jnp.concatenate([rope_o[:, :d_qk], rms_o[:, :gq_kv]], axis=1)
    h2 = h.reshape(B * N, D)
    return dict(
        q_da=rope_o, q_da_blk=d_qk // 128, q_gq=rms_o, q_gq_blk=gq_kv // 128,
        k_tok=jnp.swapaxes(k_t, 1, 2), kmax=key_norm_max(k_t), k_gq_blk=d_qk // 128, khead_gq=d_qk // HEAD_DIM,
        va=proj_v(lw["w_va"], h, 2 * HEAD_DIM), vb=proj_v(lw["w_vb"], h, HEAD_DIM),
        u=matmul(h2, lw["w_hy"], F32).reshape(B, N, -1),
        gates=matmul(h2, lw["w_gates"], BF16, act="sigmoid").reshape(B, N, -1))


def kernel(x, c, ctx, c_ctx, w_mod, b_mod, w_in, da_lq1, da_lk1, da_lq2, da_lk2, da_subln, gq_qn, gq_kn,
           hy_conv_w, hy_conv_b, hy_w1, hy_b1, hy_w2, hy_b2, hy_w3, hy_freq, hy_bias, w_pa, w_pb, w_pc, w_o,
           ln1_g, ln1_b, w_up, ffn_conv_w, ffn_conv_b, w_down, ln2_g, ln2_b):
    B, n_lat, D = x.shape
    n_ctx = ctx.shape[1]
    depth = w_in.shape[0]
    alpha = (2 * depth) ** 0.25
    cos_l, sin_l = _rope_tables_t(n_lat)
    cos_t = jnp.concatenate([cos_l, jnp.ones((HALF, n_ctx), F32)], axis=1)
    sin_t = jnp.concatenate([sin_l, jnp.zeros((HALF, n_ctx), F32)], axis=1)
    lat, ctx_span, all_span = (0, n_lat), (n_lat, n_ctx), (0, n_lat + n_ctx)
    tabs_l = _fft_tables(n_lat)
    tabs_c = _fft_tables(n_ctx)
    xc = ctx

    cond = jnp.concatenate([c, c_ctx[None]], axis=0)
    cond = jnp.pad(jax.nn.silu(cond), ((0, 16 - (B + 1) % 16), (0, 0))).astype(BF16)

    for l in range(depth):
        last = l == depth - 1
        lam_init = 0.8 - 0.6 * math.exp(-0.3 * l)
        lam = (jnp.exp(jnp.sum(da_lq1[l] * da_lk1[l])) - jnp.exp(jnp.sum(da_lq2[l] * da_lk2[l])) + lam_init)
        lam = lam.reshape(1).astype(F32)
        mod = matmul(cond, w_mod[l].astype(BF16), F32, tn_pref=1024) + b_mod[l]
        sh1, sc1, g1, sh2, sc2, g2 = [m[:, None, :] for m in jnp.split(mod[:B], 6, axis=-1)]
        mc = [jnp.broadcast_to(m[None, None, :], (B, 1, D)) for m in jnp.split(mod[B], 6)]
        lw = _layer_weights(l, w_in, gq_qn, gq_kn)
        subln = da_subln[l][:, None]
        hy_mlp = (hy_w1[l], hy_b1[l], hy_w2[l], hy_b2[l], hy_w3[l], hy_freq[l])
        spectra_l = hyena_filter_spectra(*hy_mlp, tabs_l, FILTER_PASSES)
        wpa, wpb, wpc, wo = (w[l].astype(BF16) for w in (w_pa, w_pb, w_pc, w_o))
        wup, wdn = w_up[l].astype(BF16), w_down[l].astype(BF16)
        ln1 = (ln1_g[l][None], ln1_b[l][None])
        ln2 = (ln2_g[l][None], ln2_b[l][None])

        p = _project(ln_mod(x, xc, sh1, sc1, mc[0], mc[1]), lw, cos_t, sin_t)

        def attend(q_span, kv_span):
            da = flash_attention(p["q_da"], p["k_tok"], p["va"], p["kmax"], lam, subln, mode="da",
                                 q_span=q_span, kv_span=kv_span, q_row_block=p["q_da_blk"], k_col_block=0,
                                 khead0=0, post_scale=1.0 - lam_init)
            gq = flash_attention(p["q_gq"], p["k_tok"], p["vb"], p["kmax"], lam, subln, mode="gqa",
                                 q_span=q_span, kv_span=kv_span, q_row_block=p["q_gq_blk"],
                                 k_col_block=p["k_gq_blk"], khead0=p["khead_gq"])
            return da, gq

        hy_args = (hy_conv_w[l], hy_conv_b[l], hy_bias[l])
        a_l, b_l = attend(lat, all_span)
        c_l = hyena_mix(p["u"], *lat, spectra_l, *hy_args, tabs_l, FFT_PASSES)
        x_new, h2 = merge_resid_ln(a_l, b_l, c_l, p["gates"], lat[0], wpa, wpb, wpc, wo, x, g1, *ln1,
                                   sh2, sc2, alpha)
        x_new = conv_ffn_resid_ln(h2, wup, ffn_conv_w[l], ffn_conv_b[l][None], wdn, x_new, g2, *ln2, alpha)

        if not last:
            spectra_c = hyena_filter_spectra(*hy_mlp, tabs_c, FILTER_PASSES)
            a_c, b_c = attend(ctx_span, ctx_span)
            c_c = hyena_mix(p["u"], *ctx_span, spectra_c, *hy_args, tabs_c, FFT_PASSES)
            xc, hc2 = merge_resid_ln(a_c, b_c, c_c, p["gates"], ctx_span[0], wpa, wpb, wpc, wo, xc, mc[2],
                                     *ln1, mc[3], mc[4], alpha)
            xc = conv_ffn_resid_ln(hc2, wup, ffn_conv_w[l], ffn_conv_b[l][None], wdn, xc, mc[5], *ln2, alpha)
        x = x_new
    return x
```

```python
import functools
import math

import numpy as np
import jax
import jax.numpy as jnp
from jax import lax
from jax.experimental import pallas as pl
from jax.experimental.pallas import tpu as pltpu

F32 = jnp.float32
BF16 = jnp.bfloat16

HEAD_DIM = 64
HALF = HEAD_DIM // 2
GRID_W = 64
ROPE_THETA = 10000.0
DA_HEADS = 4
GQ_HEADS = 8
GKV_HEADS = 2
HY_WIDTH = 512
HY_ORDER = 2
HY_EMB = 33
HY_EMB_PAD = 128
HY_BANDS = (HY_EMB - 1) // 2
HY_HIDDEN = 64
HY_MIN_DECAY = math.log(1e-2) / 1.5
HY_MAX_DECAY = math.log(1e-2) / 0.3
LN_EPS = 1e-6
LOG2E = 1.4426950408889634
NEG_BIG = -1e30

VMEM_LIMIT = 56 * 1024 * 1024


def _cp(n_axes):
    return pltpu.CompilerParams(dimension_semantics=("arbitrary",) * n_axes,
                                vmem_limit_bytes=VMEM_LIMIT)


def _tile(n, pref, mult):
    if n <= pref:
        return n
    t = (pref // mult) * mult
    while t >= mult:
        if n % t == 0:
            return t
        t -= mult
    return n


def _ln(x):
    mu = jnp.mean(x, axis=-1, keepdims=True)
    xc = x - mu
    return xc * lax.rsqrt(jnp.mean(xc * xc, axis=-1, keepdims=True) + LN_EPS)


V_PAD = 16
N_KHEADS = DA_HEADS * 2 + GKV_HEADS
KNORM_ROWS = 16


def _proj_all_kernel(x_ref, xc_ref, sh_ref, sc_ref, shc_ref, scc_ref, cos_ref, sin_ref, wqk_ref, gqk_ref,
                     wva_ref, ova_ref, wvb_ref, ovb_ref, wtok_ref,
                     q_ref, k_ref, kn_ref, va_ref, vb_ref, u_ref, gt_ref, kt_sc,
                     *, n_lat_tiles, rms_heads, n_hy):
    i = pl.program_id(1)
    is_latent = i < n_lat_tiles
    xin = jnp.where(is_latent, x_ref[0], xc_ref[0])
    shift = jnp.where(is_latent, sh_ref[0], shc_ref[0])
    scale = jnp.where(is_latent, sc_ref[0], scc_ref[0])
    h = (_ln(xin) * (1.0 + scale) + shift).astype(BF16)
    nt = (((1,), (1,)), ((), ()))

    acc = lax.dot_general(wqk_ref[...], h, nt, preferred_element_type=F32)
    c, s = cos_ref[...], sin_ref[...]
    n_k_rows = kt_sc.shape[0]
    for hd in range(acc.shape[0] // HEAD_DIM):
        lo = hd * HEAD_DIM
        xh = acc[lo:lo + HEAD_DIM]
        if hd in rms_heads:
            xh = xh * lax.rsqrt(jnp.mean(xh * xh, axis=0, keepdims=True) + LN_EPS)
        xh = xh * gqk_ref[lo:lo + HEAD_DIM]
        x1, x2 = xh[:HALF], xh[HALF:]
        r1, r2 = x1 * c - x2 * s, x1 * s + x2 * c
        if lo < n_k_rows:
            kt_sc[lo:lo + HALF] = r1
            kt_sc[lo + HALF:lo + HEAD_DIM] = r2
        else:
            q_ref[0, lo - n_k_rows:lo - n_k_rows + HALF] = r1.astype(q_ref.dtype)
            q_ref[0, lo - n_k_rows + HALF:lo - n_k_rows + HEAD_DIM] = r2.astype(q_ref.dtype)

    k_bf = kt_sc[...].astype(BF16)
    k_ref[0] = k_bf.astype(F32).T.astype(BF16)

    @pl.when(i == 0)
    def _():
        kn_ref[...] = jnp.zeros(kn_ref.shape, F32)

    k_sq = k_bf.astype(F32) ** 2
    for hd in range(n_k_rows // HEAD_DIM):
        n2 = jnp.sum(k_sq[hd * HEAD_DIM:(hd + 1) * HEAD_DIM], axis=0, keepdims=True)
        kn_ref[0, hd:hd + 1] = jnp.maximum(kn_ref[0, hd:hd + 1], jnp.max(n2, axis=1, keepdims=True))

    va = lax.dot_general(wva_ref[...], h, nt, preferred_element_type=F32)
    va_ref[0] = (va + ova_ref[...]).astype(va_ref.dtype)
    vb = lax.dot_general(wvb_ref[...], h, nt, preferred_element_type=F32)
    vb_ref[0] = (vb + ovb_ref[...]).astype(vb_ref.dtype)

    tok = jnp.dot(h, wtok_ref[...], preferred_element_type=F32)
    u_ref[0] = tok[:, :n_hy]
    gt_ref[0] = jax.nn.sigmoid(tok[:, n_hy:]).astype(gt_ref.dtype)


def _augment_values(w_t, dv):
    heads, K = w_t.shape[0] // dv, w_t.shape[1]
    dvp = dv + V_PAD
    w_aug = jnp.pad(w_t.reshape(heads, dv, K), ((0, 0), (0, V_PAD), (0, 0))).reshape(heads * dvp, K)
    ones_col = jnp.zeros((heads, dvp, 1), F32).at[:, dv].set(1.0).reshape(heads * dvp, 1)
    return w_aug, ones_col


def proj_all(x, xc, shift, scale, shift_c, scale_c, cos_t, sin_t, lw):
    B, N, D = x.shape
    Nc = xc.shape[1]
    ntot = N + Nc
    tt = _tile(Nc, 256, 128)
    assert N % tt == 0
    nl = N // tt
    n_k_rows = N_KHEADS * HEAD_DIM
    n_q_rows = lw["w_qk"].shape[0] - n_k_rows
    n_va, n_vb = lw["w_va"].shape[0], lw["w_vb"].shape[0]
    n_tok = lw["w_tok"].shape[1]
    n_hy = 3 * HY_WIDTH
    per_b = lambda b, i: (b, 0, 0)
    const = lambda b, i: (0, 0)
    resident = dict(pipeline_mode=pl.Buffered(1))
    feat = lambda rows: pl.BlockSpec((1, rows, tt), lambda b, i: (b, 0, i))
    tokm = lambda cols: pl.BlockSpec((1, tt, cols), lambda b, i: (b, i, 0))
    sds = jax.ShapeDtypeStruct
    return pl.pallas_call(
        functools.partial(_proj_all_kernel, n_lat_tiles=nl, rms_heads=lw["rms_heads"], n_hy=n_hy),
        out_shape=(sds((B, n_q_rows, ntot), BF16), sds((B, ntot, n_k_rows), BF16),
                   sds((B, KNORM_ROWS, 128), F32), sds((B, n_va, ntot), BF16), sds((B, n_vb, ntot), BF16),
                   sds((B, ntot, n_hy), F32), sds((B, ntot, n_tok - n_hy), BF16)),
        grid=(B, ntot // tt),
        in_specs=[pl.BlockSpec((1, tt, D), lambda b, i: (b, jnp.minimum(i, nl - 1), 0)),
                  pl.BlockSpec((1, tt, D), lambda b, i: (b, jnp.maximum(i - nl, 0), 0)),
                  pl.BlockSpec((1, 1, D), per_b), pl.BlockSpec((1, 1, D), per_b),
                  pl.BlockSpec((1, 1, D), per_b), pl.BlockSpec((1, 1, D), per_b),
                  pl.BlockSpec((HALF, tt), lambda b, i: (0, i)), pl.BlockSpec((HALF, tt), lambda b, i: (0, i)),
                  pl.BlockSpec(lw["w_qk"].shape, const, **resident), pl.BlockSpec((lw["w_qk"].shape[0], 1), const),
                  pl.BlockSpec(lw["w_va"].shape, const, **resident), pl.BlockSpec((n_va, 1), const),
                  pl.BlockSpec(lw["w_vb"].shape, const, **resident), pl.BlockSpec((n_vb, 1), const),
                  pl.BlockSpec(lw["w_tok"].shape, const, **resident)],
        out_specs=(feat(n_q_rows), tokm(n_k_rows), pl.BlockSpec((1, KNORM_ROWS, 128), per_b),
                   feat(n_va), feat(n_vb), tokm(n_hy), tokm(n_tok - n_hy)),
        scratch_shapes=[pltpu.VMEM((n_k_rows, tt), F32)],
        compiler_params=_cp(2), name="proj_all",
    )(x, xc, shift, scale, shift_c, scale_c, cos_t, sin_t, lw["w_qk"], lw["g_qk"], lw["w_va"], lw["ones_va"],
      lw["w_vb"], lw["ones_vb"], lw["w_tok"])


def _resid_ln_tail(x, y, gate, g, b, alpha):
    return _ln(alpha * x + gate * y) * g + b


def _mm_kernel(a_ref, w_ref, o_ref, *, act):
    acc = jnp.dot(a_ref[...], w_ref[...], preferred_element_type=F32)
    if act == "sigmoid":
        acc = jax.nn.sigmoid(acc)
    o_ref[...] = acc.astype(o_ref.dtype)


def matmul(a, w, out_dtype, act=None, tm_pref=1024, tn_pref=512):
    M, K = a.shape
    Nn = w.shape[1]
    tm = _tile(M, tm_pref, 16)
    tn = _tile(Nn, tn_pref, 128)
    return pl.pallas_call(
        functools.partial(_mm_kernel, act=act),
        out_shape=jax.ShapeDtypeStruct((M, Nn), out_dtype),
        grid=(M // tm, Nn // tn),
        in_specs=[pl.BlockSpec((tm, K), lambda i, j: (i, 0)),
                  pl.BlockSpec((K, tn), lambda i, j: (0, j))],
        out_specs=pl.BlockSpec((tm, tn), lambda i, j: (i, j)),
        compiler_params=_cp(2), name="matmul",
    )(a, w)


FLASH_MIN_DENOM = 2.0 ** -60


def _flash_kernel(lam_ref, kmax_ref, q_ref, k_ref, v_ref, g_ref, o_ref, q_sc, p_buf, sh_sc, acc_sc,
                  *, mode, nj, tk, dv, post_scale, n_kheads, khead0):
    b, h = pl.program_id(0), pl.program_id(1)
    qf = q_ref[0].astype(F32)
    tq = qf.shape[1]
    zero = jnp.zeros((HEAD_DIM, tq), F32)
    qa, qb = qf[:HEAD_DIM], qf[HEAD_DIM:]
    if mode == "da":
        q_sc[0] = jnp.concatenate([qa, zero], axis=0).astype(BF16)
        q_sc[1] = jnp.concatenate([zero, qb], axis=0).astype(BF16)
        k_heads = (khead0 + 2 * h, khead0 + 2 * h + 1)
    else:
        group = h // 2
        for mi, qh in enumerate((qa, qb)):
            q_sc[mi] = jnp.where(group == 0, jnp.concatenate([qh, zero], axis=0),
                                 jnp.concatenate([zero, qh], axis=0)).astype(BF16)
        k_heads = (khead0 + group, khead0 + group)
    for mi, qh in enumerate((qa, qb)):
        q_norm = jnp.sqrt(jnp.sum(qh * qh, axis=0, keepdims=True))
        sh_sc[mi] = q_norm * kmax_ref[b * n_kheads + k_heads[mi]]

    def chunk(j):
        start = j * tk
        return pl.ds(start if isinstance(start, int) else pl.multiple_of(start, tk), tk)

    def stage_exp(j, slot):
        kb = k_ref[0, chunk(j), :]
        for mi in range(2):
            s = jnp.dot(kb, q_sc[mi], preferred_element_type=F32)
            p_buf[slot, mi] = jnp.exp2(s - sh_sc[mi]).astype(BF16)

    def stage_values(j, slot):
        vb = v_ref[0, :, chunk(j)]
        for mi in range(2):
            acc_sc[mi] += jnp.dot(vb, p_buf[slot, mi], preferred_element_type=F32)

    def accumulate():
        acc_sc[...] = jnp.zeros(acc_sc.shape, F32)
        stage_exp(0, 0)
        steady = list(range(1, nj))
        if len(steady) % 2 == 1:
            t = steady.pop(0)
            stage_exp(t, t % 2)
            stage_values(t - 1, (t - 1) % 2)
        if steady:
            t0 = steady[0]

            def body(i, carry):
                t = t0 + 2 * i
                stage_exp(t, t0 % 2)
                stage_values(t - 1, 1 - t0 % 2)
                stage_exp(t + 1, 1 - t0 % 2)
                stage_values(t, t0 % 2)
                return carry

            lax.fori_loop(0, len(steady) // 2, body, 0)
        stage_values(nj - 1, (nj - 1) % 2)

    accumulate()
    denom_min = jnp.minimum(jnp.min(acc_sc[0][dv:dv + 1]), jnp.min(acc_sc[1][dv:dv + 1]))

    @pl.when(jnp.logical_not(denom_min >= FLASH_MIN_DENOM))
    def _():
        for mi in range(2):
            def max_body(j, m, mi=mi):
                s = jnp.dot(k_ref[0, chunk(j), :], q_sc[mi], preferred_element_type=F32)
                return jnp.maximum(m, jnp.max(s, axis=0, keepdims=True))

            sh_sc[mi] = lax.fori_loop(0, nj, max_body, jnp.full((1, tq), NEG_BIG, F32))
        accumulate()

    outs = []
    for mi in range(2):
        acc = acc_sc[mi]
        outs.append(acc[:dv] / acc[dv:dv + 1])
    if mode == "da":
        o = outs[0] - lam_ref[0] * outs[1]
        o = o * lax.rsqrt(jnp.mean(o * o, axis=0, keepdims=True) + LN_EPS)
        o = o * (g_ref[...] * post_scale)
    else:
        o = jnp.concatenate(outs, axis=0)
    o_ref[0] = o.T.astype(o_ref.dtype)


def flash_attention(q_t, k_tok, v_aug, kmax, lam, gain, *, mode, q_span, kv_span, q_row_block, k_col_block,
                    khead0, post_scale=1.0, tk_pref=768):
    B = q_t.shape[0]
    q0, Nq = q_span
    k0, Nk = kv_span
    tq = _tile(Nq, 1024, 128)
    tk = _tile(Nk, tk_pref, 128)
    assert q0 % tq == 0 and k0 % Nk == 0
    nj = Nk // tk
    dv = 2 * HEAD_DIM if mode == "da" else HEAD_DIM
    dvp = dv + V_PAD
    n_kheads = kmax.shape[0] // B
    if mode == "da":
        k_map = lambda b, h, i: (b, k0 // Nk, k_col_block + h)
        v_map = lambda b, h, i: (b, h, k0 // Nk)
    else:
        k_map = lambda b, h, i: (b, k0 // Nk, k_col_block)
        v_map = lambda b, h, i: (b, h // 2, k0 // Nk)
    return pl.pallas_call(
        functools.partial(_flash_kernel, mode=mode, nj=nj, tk=tk, dv=dv, post_scale=post_scale,
                          n_kheads=n_kheads, khead0=khead0),
        out_shape=jax.ShapeDtypeStruct((B, Nq, 4 * 128), BF16),
        grid=(B, 4, Nq // tq),
        in_specs=[pl.BlockSpec(memory_space=pltpu.SMEM),
                  pl.BlockSpec(memory_space=pltpu.SMEM),
                  pl.BlockSpec((1, 128, tq), lambda b, h, i: (b, h + q_row_block, i + q0 // tq)),
                  pl.BlockSpec((1, Nk, 128), k_map),
                  pl.BlockSpec((1, dvp, Nk), v_map),
                  pl.BlockSpec((128, 1), lambda b, h, i: (0, 0))],
        out_specs=pl.BlockSpec((1, tq, 128), lambda b, h, i: (b, i, h)),
        scratch_shapes=[pltpu.VMEM((2, 128, tq), BF16),
                        pltpu.VMEM((2, 2, tk, tq), BF16),
                        pltpu.VMEM((2, 1, tq), F32),
                        pltpu.VMEM((2, dvp, tq), F32)],
        compiler_params=_cp(3), name="flash_" + mode,
    )(lam, kmax, q_t, k_tok, v_aug, gain)


def _shift_rows(x, prev_row, next_row):
    T = x.shape[0]
    row = lax.broadcasted_iota(jnp.int32, x.shape, 0)
    xp = jnp.where(row == 0, prev_row, pltpu.roll(x, 1, axis=0))
    xn = jnp.where(row == T - 1, next_row, pltpu.roll(x, T - 1, axis=0))
    return xp, xn


def _conv3_block(x_ref, p_ref, n_ref, w_ref, b_ref, halo):
    i = pl.program_id(1)
    last = pl.num_programs(1) - 1
    x = x_ref[0].astype(F32)
    prev_row = jnp.where(i > 0, p_ref[0].astype(F32)[halo - 1:halo], 0.0)
    next_row = jnp.where(i < last, n_ref[0].astype(F32)[0:1], 0.0)
    xp, xn = _shift_rows(x, prev_row, next_row)
    return xp * w_ref[0:1] + x * w_ref[1:2] + xn * w_ref[2:3] + b_ref[...]


def _halo_specs(tr, tc, halo, row0, n_rows, col_of):
    per = tr // halo
    n_halo = n_rows // halo
    m0, h0 = row0 // tr, row0 // halo
    return [pl.BlockSpec((1, tr, tc), lambda b, i, j: (b, m0 + i, col_of(j))),
            pl.BlockSpec((1, halo, tc), lambda b, i, j: (b, h0 + jnp.maximum(i * per - 1, 0), col_of(j))),
            pl.BlockSpec((1, halo, tc),
                         lambda b, i, j: (b, h0 + jnp.minimum((i + 1) * per, n_halo - 1), col_of(j)))]


def _dwconv_kernel(x_ref, p_ref, n_ref, w_ref, b_ref, v_ref, x1_ref, x2_ref):
    y = _conv3_block(x_ref, p_ref, n_ref, w_ref, b_ref, 8)
    C = v_ref.shape[2]
    v_ref[0] = y[:, :C]
    x1_ref[0] = y[:, C:2 * C]
    x2_ref[0] = y[:, 2 * C:]


def hyena_short_conv(u, w, b, row0, L):
    B, _, C3 = u.shape
    C = C3 // 3
    tr = _tile(L, 512, 8)
    assert row0 % tr == 0
    zero = lambda j: 0
    out = jax.ShapeDtypeStruct((B, L, C), F32)
    ospec = pl.BlockSpec((1, tr, C), lambda b, i, j: (b, i, 0))
    return pl.pallas_call(
        _dwconv_kernel,
        out_shape=(out, out, out),
        grid=(B, L // tr, 1),
        in_specs=_halo_specs(tr, C3, 8, row0, L, zero) + [
            pl.BlockSpec((3, C3), lambda b, i, j: (0, 0)),
            pl.BlockSpec((1, C3), lambda b, i, j: (0, 0))],
        out_specs=(ospec, ospec, ospec),
        compiler_params=_cp(3), name="hyena_short_conv",
    )(u, u, u, w, b)


FFN_HALO = 16
FFN_CHUNK = 256


def _ffn_kernel(h_ref, hp_ref, hn_ref, wup_ref, cw_ref, cb_ref, wdn_ref, x_ref, gate_ref, g_ref, b_ref, o_ref,
                *, alpha):
    i = pl.program_id(1)
    tm = h_ref.shape[1]
    dff = wdn_ref.shape[0]
    rows = tm + 2 * FFN_HALO
    h_prev = jnp.where(i == 0, jnp.zeros_like(hp_ref[0]), hp_ref[0])
    h_next = jnp.where(i == pl.num_programs(1) - 1, jnp.zeros_like(hn_ref[0]), hn_ref[0])
    h_ext = jnp.concatenate([h_prev, h_ref[0], h_next], axis=0)

    def conv_cols(c0):
        u = jnp.dot(h_ext, wup_ref[:, c0:c0 + FFN_CHUNK], preferred_element_type=F32)
        mid = slice(FFN_HALO, FFN_HALO + tm)
        u_prev = pltpu.roll(u, 1, axis=0)[mid]
        u_next = pltpu.roll(u, rows - 1, axis=0)[mid]
        w = cw_ref[:, c0:c0 + FFN_CHUNK]
        return u_prev * w[0:1] + u[mid] * w[1:2] + u_next * w[2:3] + cb_ref[:, c0:c0 + FFN_CHUNK]

    y = jnp.zeros((tm, wdn_ref.shape[1]), F32)
    for c0 in range(0, dff, FFN_CHUNK):
        val = conv_cols(c0)
        gate = conv_cols(dff + c0)
        act = (gate * jax.nn.sigmoid(gate) * val).astype(BF16)
        y = y + jnp.dot(act, wdn_ref[c0:c0 + FFN_CHUNK, :], preferred_element_type=F32)
    o_ref[0] = _resid_ln_tail(x_ref[0], y, gate_ref[0], g_ref[...], b_ref[...], alpha)


def conv_ffn_resid_ln(h, w_up, conv_w, conv_b, w_down, x, gate, g, b, alpha):
    B, N, D = h.shape
    dff = w_down.shape[0]
    tm = _tile(N, 512, FFN_HALO)
    per = tm // FFN_HALO
    n_halo = N // FFN_HALO
    row = lambda b_, i: (b_, i, 0)
    per_b = lambda b_, i: (b_, 0, 0)
    const2 = lambda b_, i: (0, 0)
    resident = dict(pipeline_mode=pl.Buffered(1))
    return pl.pallas_call(
        functools.partial(_ffn_kernel, alpha=alpha),
        out_shape=jax.ShapeDtypeStruct((B, N, D), F32),
        grid=(B, N // tm),
        in_specs=[pl.BlockSpec((1, tm, D), row),
                  pl.BlockSpec((1, FFN_HALO, D), lambda b_, i: (b_, jnp.maximum(i * per - 1, 0), 0)),
                  pl.BlockSpec((1, FFN_HALO, D), lambda b_, i: (b_, jnp.minimum((i + 1) * per, n_halo - 1), 0)),
                  pl.BlockSpec((D, 2 * dff), const2, **resident),
                  pl.BlockSpec((3, 2 * dff), const2), pl.BlockSpec((1, 2 * dff), const2),
                  pl.BlockSpec((dff, D), const2, **resident),
                  pl.BlockSpec((1, tm, D), row), pl.BlockSpec((1, 1, D), per_b),
                  pl.BlockSpec((1, D), const2), pl.BlockSpec((1, D), const2)],
        out_specs=pl.BlockSpec((1, tm, D), row),
        compiler_params=_cp(2), name="conv_ffn_resid_ln",
    )(h, h, h, w_up, conv_w, conv_b, w_down, x, gate, g, b)


def _merge_kernel(a_ref, b_ref, c_ref, gt_ref, wpa_ref, wpb_ref, wpc_ref, wo_ref, x_ref, gate_ref,
                  g_ref, bb_ref, sh_ref, sc_ref, ox_ref, oh_ref, *, alpha):
    D = wo_ref.shape[0]
    gt = gt_ref[0]
    m = gt[:, :D].astype(F32) * jnp.dot(a_ref[0], wpa_ref[...], preferred_element_type=F32)
    m = m + gt[:, D:2 * D].astype(F32) * jnp.dot(b_ref[0], wpb_ref[...], preferred_element_type=F32)
    m = m + gt[:, 2 * D:].astype(F32) * jnp.dot(c_ref[0].astype(BF16), wpc_ref[...],
                                                preferred_element_type=F32)
    y = jnp.dot(m.astype(BF16), wo_ref[...], preferred_element_type=F32)
    xn = _resid_ln_tail(x_ref[0], y, gate_ref[0], g_ref[...], bb_ref[...], alpha)
    ox_ref[0] = xn
    oh_ref[0] = (_ln(xn) * (1.0 + sc_ref[0]) + sh_ref[0]).astype(oh_ref.dtype)


def merge_resid_ln(a, b, c, gates, tok0, wpa, wpb, wpc, wo, x, gate, g, bb, sh, sc, alpha):
    B, N, D = x.shape
    W = a.shape[2]
    tm = _tile(N, 512, 16)
    assert tok0 % tm == 0
    row = lambda b_, i: (b_, i, 0)
    per_b = lambda b_, i: (b_, 0, 0)
    const2 = lambda b_, i: (0, 0)
    return pl.pallas_call(
        functools.partial(_merge_kernel, alpha=alpha),
        out_shape=(jax.ShapeDtypeStruct((B, N, D), F32), jax.ShapeDtypeStruct((B, N, D), BF16)),
        grid=(B, N // tm),
        in_specs=[pl.BlockSpec((1, tm, W), row), pl.BlockSpec((1, tm, W), row), pl.BlockSpec((1, tm, W), row),
                  pl.BlockSpec((1, tm, 3 * D), lambda b_, i: (b_, tok0 // tm + i, 0)),
                  pl.BlockSpec((W, D), const2), pl.BlockSpec((W, D), const2), pl.BlockSpec((W, D), const2),
                  pl.BlockSpec((D, D), const2),
                  pl.BlockSpec((1, tm, D), row), pl.BlockSpec((1, 1, D), per_b),
                  pl.BlockSpec((1, D), const2), pl.BlockSpec((1, D), const2),
                  pl.BlockSpec((1, 1, D), per_b), pl.BlockSpec((1, 1, D), per_b)],
        out_specs=(pl.BlockSpec((1, tm, D), row), pl.BlockSpec((1, tm, D), row)),
        compiler_params=_cp(2), name="merge_resid_ln",
    )(a, b, c, gates, wpa, wpb, wpc, wo, x, gate, g, bb, sh, sc)


def _split_bf16(x):
    hi = x.astype(BF16)
    lo = (x - hi.astype(F32)).astype(BF16)
    return hi, lo


def _dot3(a, b):
    ah, al = _split_bf16(a)
    bh, bl = _split_bf16(b)
    d = functools.partial(jnp.dot, preferred_element_type=F32)
    return d(ah, bh) + (d(ah, bl) + d(al, bh))


def _filter_kernel(z_ref, w1_ref, b1_ref, w2_ref, b2_ref, w3_ref, fr_ref, dec_ref, h_ref, s_ref, *, zero_block):
    i = pl.program_id(0)
    fr = fr_ref[...]
    hid = jnp.sin(fr * (_dot3(z_ref[...], w1_ref[...]) + b1_ref[...]))
    hid = jnp.sin(fr * (_dot3(hid, w2_ref[...]) + b2_ref[...]))
    h = _dot3(hid, w3_ref[0])
    dec = dec_ref[...]
    C = dec.shape[1]
    n_ord = h.shape[1] // C
    h = h * jnp.concatenate([dec] * n_ord, axis=1)

    @pl.when(i == 0)
    def _():
        s_ref[...] = jnp.zeros(s_ref.shape, F32)

    s_ref[...] += jnp.sum(jnp.abs(h), axis=0, keepdims=True)
    row = lax.broadcasted_iota(jnp.int32, h.shape, 0)
    h = jnp.where((row == 0) & (i == zero_block), 0.0, h)
    for o in range(n_ord):
        h_ref[o] = h[:, o * C:(o + 1) * C]


def hyena_filter_mlp(z2, w1, b1, w2, b2, w3_dir, freq, decay2):
    n2, E = z2.shape
    Hh = w2.shape[0]
    OC = w3_dir.shape[2]
    C = decay2.shape[1]
    n_ord = OC // C
    L = n2 // 2
    tr = _tile(L, 512, 8)
    nblk = n2 // tr
    c2 = lambda i: (0, 0)
    return pl.pallas_call(
        functools.partial(_filter_kernel, zero_block=L // tr),
        out_shape=(jax.ShapeDtypeStruct((n_ord, n2, C), F32), jax.ShapeDtypeStruct((1, OC), F32)),
        grid=(nblk,),
        in_specs=[pl.BlockSpec((tr, E), lambda i: (i, 0)), pl.BlockSpec((E, Hh), c2), pl.BlockSpec((1, Hh), c2),
                  pl.BlockSpec((Hh, Hh), c2), pl.BlockSpec((1, Hh), c2),
                  pl.BlockSpec((1, Hh, OC), lambda i: (i // (nblk // 2), 0, 0)),
                  pl.BlockSpec((1, Hh), c2), pl.BlockSpec((tr, C), lambda i: (i, 0))],
        out_specs=(pl.BlockSpec((n_ord, tr, C), lambda i: (0, i, 0)), pl.BlockSpec((1, OC), c2)),
        compiler_params=_cp(1), name="hyena_filter_mlp",
    )(z2, w1, b1, w2, b2, w3_dir, freq, decay2)


def _mm_split(m_hi, m_lo, x, passes):
    d = functools.partial(jnp.dot, preferred_element_type=F32)
    if passes == 1:
        return d(m_hi, x.astype(BF16))
    xh, xl = _split_bf16(x)
    return d(m_hi, xh) + (d(m_lo, xh) + d(m_hi, xl))


FFT_ROWS = 8


def _level1_kernel(mh_ref, ml_ref, x_ref, o_ref, *, passes):
    r_in, r_out = x_ref.shape[1], o_ref.shape[1]
    x = x_ref[0].reshape(r_in * FFT_ROWS, x_ref.shape[3])
    y = _mm_split(mh_ref[...], ml_ref[...], x, passes)
    o_ref[0] = y.reshape(r_out, FFT_ROWS, x_ref.shape[3]).astype(o_ref.dtype)


def _level1_gate_kernel(mh_ref, ml_ref, x_ref, z_ref, xg_ref, bias_ref, o_ref, *, passes):
    r_in, r_out = x_ref.shape[1], o_ref.shape[1]
    x = x_ref[0].reshape(r_in * FFT_ROWS, x_ref.shape[3])
    y = _mm_split(mh_ref[...], ml_ref[...], x, passes).reshape(r_out, FFT_ROWS, x_ref.shape[3])
    o_ref[0] = (xg_ref[0] * (y + bias_ref[...] * z_ref[0])).astype(o_ref.dtype)


def fft_level1(m_hi, m_lo, x, passes, gate_args=None, out_dtype=F32):
    P, R_in, Nb, C = x.shape
    R_out = m_hi.shape[0] // FFT_ROWS
    mspec = pl.BlockSpec(m_hi.shape, lambda p, j: (0, 0))
    xspec = pl.BlockSpec((1, R_in, FFT_ROWS, C), lambda p, j: (p, 0, j, 0))
    ospec = pl.BlockSpec((1, R_out, FFT_ROWS, C), lambda p, j: (p, 0, j, 0))
    if gate_args is None:
        kern = functools.partial(_level1_kernel, passes=passes)
        ins, specs = (m_hi, m_lo, x), [mspec, mspec, xspec]
    else:
        z, xg, bias = gate_args
        kern = functools.partial(_level1_gate_kernel, passes=passes)
        ins = (m_hi, m_lo, x, z, xg, bias)
        specs = [mspec, mspec, xspec, ospec, ospec, pl.BlockSpec((1, C), lambda p, j: (0, 0))]
    return pl.pallas_call(
        kern, out_shape=jax.ShapeDtypeStruct((P, R_out, Nb, C), out_dtype),
        grid=(P, Nb // FFT_ROWS), in_specs=specs, out_specs=ospec,
        compiler_params=_cp(2), name="fft_level1",
    )(*ins)


FFT_MID_K1 = 2


def _fft_mid_kernel(*refs, passes):
    n_mat = 2 if passes == 1 else 4
    gh_ref, gih_ref = refs[0], refs[1]
    gl_ref, gil_ref = (refs[2], refs[3]) if passes > 1 else (None, None)
    h_ref, a_ref, o_ref = refs[n_mat:]
    n_pairs, _, kb, nb, C = a_ref.shape
    for kk in range(kb):
        hr, hi = h_ref[0, 0, kk], h_ref[0, 1, kk]
        for p in range(n_pairs):
            x = a_ref[p, :, kk].reshape(2 * nb, C)
            X = _mm_split(gh_ref[kk], None if gl_ref is None else gl_ref[kk], x, passes)
            xr, xi = X[:nb], X[nb:]
            Y = jnp.concatenate([xr * hr - xi * hi, xr * hi + xi * hr], axis=0)
            Bv = _mm_split(gih_ref[kk], None if gil_ref is None else gil_ref[kk], Y, passes)
            o_ref[p, :, kk] = Bv.reshape(2, nb, C)


def fft_mid(g_hi, g_lo, gi_hi, gi_lo, spectra, order, a5, passes):
    P, _, Na, Nb, C = a5.shape
    kb = FFT_MID_K1
    gspec = pl.BlockSpec((kb, 2 * Nb, 2 * Nb), lambda k: (k, 0, 0))
    mats = (g_hi, gi_hi) if passes == 1 else (g_hi, gi_hi, g_lo, gi_lo)
    blk = pl.BlockSpec((P, 2, kb, Nb, C), lambda k: (0, 0, k, 0, 0))
    return pl.pallas_call(
        functools.partial(_fft_mid_kernel, passes=passes),
        out_shape=jax.ShapeDtypeStruct(a5.shape, F32),
        grid=(Na // kb,),
        in_specs=[gspec] * len(mats) + [pl.BlockSpec((1, 2, kb, Nb, C), lambda k: (order, 0, k, 0, 0)), blk],
        out_specs=blk,
        compiler_params=_cp(1), name="fft_mid",
    )(*mats, spectra, a5)


def _fft_spec_kernel(gh_ref, gl_ref, inv_ref, a_ref, o_ref, *, passes):
    nb = a_ref.shape[3]
    x = a_ref[0, :, 0].reshape(2 * nb, a_ref.shape[4])
    X = _mm_split(gh_ref[0], gl_ref[0], x, passes) * inv_ref[0]
    o_ref[0, :, 0] = X.reshape(2, nb, a_ref.shape[4])


def fft_filter_spectrum(g_hi, g_lo, inv_norm, a5, passes):
    P, _, Na, Nb, C = a5.shape
    gspec = pl.BlockSpec((1, 2 * Nb, 2 * Nb), lambda k, p: (k, 0, 0))
    blk = pl.BlockSpec((1, 2, 1, Nb, C), lambda k, p: (p, 0, k, 0, 0))
    return pl.pallas_call(
        functools.partial(_fft_spec_kernel, passes=passes),
        out_shape=jax.ShapeDtypeStruct(a5.shape, F32),
        grid=(Na, P),
        in_specs=[gspec, gspec, pl.BlockSpec((1, 1, C), lambda k, p: (p, 0, 0)), blk],
        out_specs=blk,
        compiler_params=_cp(2), name="fft_filter_spectrum",
    )(g_hi, g_lo, inv_norm, a5)


def _fft_factors(n_fft):
    na = 1 << (int(math.log2(n_fft)) // 2)
    return na, n_fft // na


def _fft_tables(L):
    n_fft = 2 * L
    na, nb = _fft_factors(n_fft)

    def cis(num, den):
        ang = (2.0 * math.pi / den) * (num % den).astype(F32)
        return jnp.cos(ang), -jnp.sin(ang)

    ia = jnp.arange(na, dtype=jnp.int32)
    ib = jnp.arange(nb, dtype=jnp.int32)
    far, fai = cis(ia[:, None] * ia[None, :], na)
    fh_r, fh_i = far[:, :na // 2], fai[:, :na // 2]
    m_fwd = jnp.block([[fh_r, -fh_i], [fh_i, fh_r]])
    m_flt = jnp.concatenate([far, fai], axis=0)
    m_inv = jnp.block([[fh_r.T, fh_i.T], [-fh_i.T, fh_r.T]]) / n_fft
    num = ib[None, :, None] * ib[None, None, :] * na + ia[:, None, None] * ib[None, None, :]
    gr, gi = cis(num, n_fft)
    g = jnp.concatenate([jnp.concatenate([gr, -gi], axis=2), jnp.concatenate([gi, gr], axis=2)], axis=1)
    grt, git = jnp.swapaxes(gr, 1, 2), jnp.swapaxes(gi, 1, 2)
    ginv = jnp.concatenate([jnp.concatenate([grt, git], axis=2), jnp.concatenate([-git, grt], axis=2)], axis=1)
    split = lambda m: _split_bf16(m.astype(F32))
    kron = lambda m: split(jnp.kron(m, jnp.eye(FFT_ROWS, dtype=F32)))
    C = HY_WIDTH
    t = jnp.linspace(0.0, 1.0, L, dtype=F32)[:, None]
    f = jnp.linspace(1e-4, HY_BANDS - 1, HY_BANDS, dtype=F32)
    ang = (2.0 * math.pi / L) * jnp.arange(L, dtype=F32)[:, None] * f[None, :]
    z = jnp.concatenate([t, jnp.cos(ang), -jnp.sin(ang)], axis=-1)
    z = jnp.pad(z, ((0, 0), (0, HY_EMB_PAD - HY_EMB)))
    deltas = jnp.abs(jnp.linspace(HY_MIN_DECAY, HY_MAX_DECAY, C, dtype=F32))
    decay = jnp.exp(-t * deltas)
    circ = lambda a: jnp.concatenate([a, a[:1], a[:0:-1]], axis=0)
    return dict(na=na, nb=nb, fwd=kron(m_fwd), flt=kron(m_flt), inv=kron(m_inv), g=split(g), ginv=split(ginv),
                z2=circ(z), decay2=circ(decay))


def hyena_filter_spectra(w1, b1, w2, b2, w3, freq, tabs, passes):
    C = HY_WIDTH
    na, nb = tabs["na"], tabs["nb"]
    w1p = jnp.pad(w1, ((0, HY_EMB_PAD - HY_EMB), (0, 0)))
    w3_dir = jnp.transpose(w3.reshape(HY_HIDDEN, HY_ORDER, 2, C), (2, 0, 1, 3)).reshape(2, HY_HIDDEN, HY_ORDER * C)
    filt, s = hyena_filter_mlp(tabs["z2"], w1p, b1[None], w2, b2[None], w3_dir, freq[None], tabs["decay2"])
    inv_norm = (1.0 / s).reshape(HY_ORDER, 1, C)
    a = fft_level1(*tabs["flt"], filt.reshape(HY_ORDER, na, nb, C), passes)
    return fft_filter_spectrum(*tabs["g"], inv_norm, a.reshape(HY_ORDER, 2, na, nb, C), passes)


def hyena_long_conv_gate(z, xg, bias, spectra, order, tabs, passes):
    B, L, C = z.shape
    na, nb = tabs["na"], tabs["nb"]
    P = B // 2
    nat = lambda a: a.reshape(P, na, nb, C)
    a = fft_level1(*tabs["fwd"], nat(z), passes)
    bv = fft_mid(*tabs["g"], *tabs["ginv"], spectra, order, a.reshape(P, 2, na, nb, C), passes)
    y = fft_level1(*tabs["inv"], bv.reshape(P, 2 * na, nb, C), passes,
                   gate_args=(nat(z), nat(xg), bias[None]))
    return y.reshape(B, L, C)


def hyena_mix(u, row0, L, spectra, conv_w, conv_b, bias, tabs, passes):
    v, x1, x2 = hyena_short_conv(u, conv_w, conv_b[None], row0, L)
    zz = hyena_long_conv_gate(v, x1, bias[0], spectra, 0, tabs, passes)
    return hyena_long_conv_gate(zz, x2, bias[1], spectra, 1, tabs, passes)


FFT_PASSES = 1
FILTER_PASSES = 3


def _rope_tables_t(n):
    rows = n // GRID_W
    r = jnp.repeat(jnp.arange(rows, dtype=F32), GRID_W)
    col = jnp.tile(jnp.arange(GRID_W, dtype=F32), rows)
    axis_dim = HEAD_DIM // 2
    inv = ROPE_THETA ** (-jnp.arange(0, axis_dim, 2, dtype=F32) / axis_dim)
    ang = jnp.concatenate([r[:, None] * inv, col[:, None] * inv], axis=-1)
    return jnp.cos(ang).T, jnp.sin(ang).T


def _layer_weights(l, w_in, gq_qn, gq_kn):
    w = w_in[l]
    d_qk = DA_HEADS * 2 * HEAD_DIM
    gq_kv = GKV_HEADS * HEAD_DIM
    gq_q = GQ_HEADS * HEAD_DIM
    o = np.cumsum([0, d_qk, d_qk, gq_kv, gq_kv, d_qk, gq_q, 3 * HY_WIDTH])
    ka, va, kb, vb, qa, qb, hy = (w[:, o[i]:o[i + 1]] for i in range(7))
    gates = w[:, o[7]:]
    qscale = (HEAD_DIM ** -0.5) * LOG2E
    w_qk = jnp.concatenate([ka, kb, qa, qb], axis=1).T.astype(BF16)
    g_qk = jnp.concatenate([jnp.ones((d_qk,), F32), jnp.tile(gq_kn[l], GKV_HEADS),
                            jnp.full((d_qk,), qscale, F32), jnp.tile(gq_qn[l], GQ_HEADS) * qscale])[:, None]
    n_ka, n_kb, n_qa = d_qk // HEAD_DIM, gq_kv // HEAD_DIM, d_qk // HEAD_DIM
    rms_heads = frozenset(range(n_ka, n_ka + n_kb)) | frozenset(range(n_ka + n_kb + n_qa,
                                                                       n_ka + n_kb + n_qa + gq_q // HEAD_DIM))
    w_va, ones_va = _augment_values(va.T.astype(BF16), 2 * HEAD_DIM)
    w_vb, ones_vb = _augment_values(vb.T.astype(BF16), HEAD_DIM)
    return dict(w_qk=w_qk, g_qk=g_qk, rms_heads=rms_heads, w_va=w_va, ones_va=ones_va, w_vb=w_vb,
                ones_vb=ones_vb, w_tok=jnp.concatenate([hy, gates], axis=1).astype(BF16))


def kernel(x, c, ctx, c_ctx, w_mod, b_mod, w_in, da_lq1, da_lk1, da_lq2, da_lk2, da_subln, gq_qn, gq_kn,
           hy_conv_w, hy_conv_b, hy_w1, hy_b1, hy_w2, hy_b2, hy_w3, hy_freq, hy_bias, w_pa, w_pb, w_pc, w_o,
           ln1_g, ln1_b, w_up, ffn_conv_w, ffn_conv_b, w_down, ln2_g, ln2_b):
    B, n_lat, D = x.shape
    n_ctx = ctx.shape[1]
    depth = w_in.shape[0]
    alpha = (2 * depth) ** 0.25
    cos_l, sin_l = _rope_tables_t(n_lat)
    cos_t = jnp.concatenate([cos_l, jnp.ones((HALF, n_ctx), F32)], axis=1)
    sin_t = jnp.concatenate([sin_l, jnp.zeros((HALF, n_ctx), F32)], axis=1)
    lat, ctx_span, all_span = (0, n_lat), (n_lat, n_ctx), (0, n_lat + n_ctx)
    tabs_l = _fft_tables(n_lat)
    tabs_c = _fft_tables(n_ctx)
    xc = ctx

    cond = jnp.concatenate([c, c_ctx[None]], axis=0)
    cond = jnp.pad(jax.nn.silu(cond), ((0, 16 - (B + 1) % 16), (0, 0))).astype(BF16)

    for l in range(depth):
        last = l == depth - 1
        lam_init = 0.8 - 0.6 * math.exp(-0.3 * l)
        lam = (jnp.exp(jnp.sum(da_lq1[l] * da_lk1[l])) - jnp.exp(jnp.sum(da_lq2[l] * da_lk2[l])) + lam_init)
        lam = lam.reshape(1).astype(F32)
        mod = matmul(cond, w_mod[l].astype(BF16), F32, tn_pref=1024) + b_mod[l]
        sh1, sc1, g1, sh2, sc2, g2 = [m[:, None, :] for m in jnp.split(mod[:B], 6, axis=-1)]
        mc = [jnp.broadcast_to(m[None, None, :], (B, 1, D)) for m in jnp.split(mod[B], 6)]
        lw = _layer_weights(l, w_in, gq_qn, gq_kn)
        subln = da_subln[l][:, None]
        hy_mlp = (hy_w1[l], hy_b1[l], hy_w2[l], hy_b2[l], hy_w3[l], hy_freq[l])
        spectra_l = hyena_filter_spectra(*hy_mlp, tabs_l, FILTER_PASSES)
        wpa, wpb, wpc, wo = (w[l].astype(BF16) for w in (w_pa, w_pb, w_pc, w_o))
        wup, wdn = w_up[l].astype(BF16), w_down[l].astype(BF16)
        ln1 = (ln1_g[l][None], ln1_b[l][None])
        ln2 = (ln2_g[l][None], ln2_b[l][None])

        q_all, k_tok, k_n2, va, vb, u, gates = proj_all(x, xc, sh1, sc1, mc[0], mc[1], cos_t, sin_t, lw)
        kmax = jnp.sqrt(k_n2[:, :N_KHEADS, 0]).reshape(B * N_KHEADS)
        n_da_maps = DA_HEADS * 2

        def attend(q_span, kv_span):
            da = flash_attention(q_all, k_tok, va, kmax, lam, subln, mode="da", q_span=q_span, kv_span=kv_span,
                                 q_row_block=0, k_col_block=0, khead0=0, post_scale=1.0 - lam_init)
            gq = flash_attention(q_all, k_tok, vb, kmax, lam, subln, mode="gqa", q_span=q_span, kv_span=kv_span,
                                 q_row_block=n_da_maps * HEAD_DIM // 128, k_col_block=n_da_maps * HEAD_DIM // 128,
                                 khead0=n_da_maps)
            return da, gq

        hy_args = (hy_conv_w[l], hy_conv_b[l], hy_bias[l])
        a_l, b_l = attend(lat, all_span)
        c_l = hyena_mix(u, *lat, spectra_l, *hy_args, tabs_l, FFT_PASSES)
        x_new, h2 = merge_resid_ln(a_l, b_l, c_l, gates, lat[0], wpa, wpb, wpc, wo, x, g1, *ln1,
                                   sh2, sc2, alpha)
        x_new = conv_ffn_resid_ln(h2, wup, ffn_conv_w[l], ffn_conv_b[l][None], wdn, x_new, g2, *ln2, alpha)

        if not last:
            spectra_c = hyena_filter_spectra(*hy_mlp, tabs_c, FILTER_PASSES)
            a_c, b_c = attend(ctx_span, ctx_span)
            c_c = hyena_mix(u, *ctx_span, spectra_c, *hy_args, tabs_c, FFT_PASSES)
            xc, hc2 = merge_resid_ln(a_c, b_c, c_c, gates, ctx_span[0], wpa, wpb, wpc, wo, xc, mc[2],
                                     *ln1, mc[3], mc[4], alpha)
            xc = conv_ffn_resid_ln(hc2, wup, ffn_conv_w[l], ffn_conv_b[l][None], wdn, xc, mc[5], *ln2, alpha)
        x = x_new
    return x
```

```python
import functools
import math

import numpy as np
import jax
import jax.numpy as jnp
from jax import lax
from jax.experimental import pallas as pl
from jax.experimental.pallas import tpu as pltpu

F32 = jnp.float32
BF16 = jnp.bfloat16

HEAD_DIM = 64
HALF = HEAD_DIM // 2
GRID_W = 64
ROPE_THETA = 10000.0
DA_HEADS = 4
GQ_HEADS = 8
GKV_HEADS = 2
HY_WIDTH = 512
HY_ORDER = 2
HY_EMB = 33
HY_EMB_PAD = 128
HY_BANDS = (HY_EMB - 1) // 2
HY_HIDDEN = 64
HY_MIN_DECAY = math.log(1e-2) / 1.5
HY_MAX_DECAY = math.log(1e-2) / 0.3
LN_EPS = 1e-6
LOG2E = 1.4426950408889634
NEG_BIG = -1e30

VMEM_LIMIT = 56 * 1024 * 1024


def _cp(n_axes):
    return pltpu.CompilerParams(dimension_semantics=("arbitrary",) * n_axes,
                                vmem_limit_bytes=VMEM_LIMIT)


def _tile(n, pref, mult):
    if n <= pref:
        return n
    t = (pref // mult) * mult
    while t >= mult:
        if n % t == 0:
            return t
        t -= mult
    return n


def _ln(x):
    mu = jnp.mean(x, axis=-1, keepdims=True)
    xc = x - mu
    return xc * lax.rsqrt(jnp.mean(xc * xc, axis=-1, keepdims=True) + LN_EPS)


V_PAD = 16
N_KHEADS = DA_HEADS * 2 + GKV_HEADS
KNORM_ROWS = 16


def _proj_all_kernel(x_ref, xc_ref, sh_ref, sc_ref, shc_ref, scc_ref, cos_ref, sin_ref, wqk_ref, gqk_ref,
                     wva_ref, ova_ref, wvb_ref, ovb_ref, wtok_ref,
                     q_ref, k_ref, kn_ref, va_ref, vb_ref, u_ref, gt_ref, kt_sc,
                     *, n_lat_tiles, rms_heads, n_hy):
    i = pl.program_id(1)
    is_latent = i < n_lat_tiles
    xin = jnp.where(is_latent, x_ref[0], xc_ref[0])
    shift = jnp.where(is_latent, sh_ref[0], shc_ref[0])
    scale = jnp.where(is_latent, sc_ref[0], scc_ref[0])
    h = (_ln(xin) * (1.0 + scale) + shift).astype(BF16)
    nt = (((1,), (1,)), ((), ()))

    acc = lax.dot_general(wqk_ref[...], h, nt, preferred_element_type=F32)
    c, s = cos_ref[...], sin_ref[...]
    n_k_rows = kt_sc.shape[0]
    for hd in range(acc.shape[0] // HEAD_DIM):
        lo = hd * HEAD_DIM
        xh = acc[lo:lo + HEAD_DIM]
        if hd in rms_heads:
            xh = xh * lax.rsqrt(jnp.mean(xh * xh, axis=0, keepdims=True) + LN_EPS)
        xh = xh * gqk_ref[lo:lo + HEAD_DIM]
        x1, x2 = xh[:HALF], xh[HALF:]
        r1, r2 = x1 * c - x2 * s, x1 * s + x2 * c
        if lo < n_k_rows:
            kt_sc[lo:lo + HALF] = r1
            kt_sc[lo + HALF:lo + HEAD_DIM] = r2
        else:
            q_ref[0, lo - n_k_rows:lo - n_k_rows + HALF] = r1.astype(q_ref.dtype)
            q_ref[0, lo - n_k_rows + HALF:lo - n_k_rows + HEAD_DIM] = r2.astype(q_ref.dtype)

    k_bf = kt_sc[...].astype(BF16)
    k_ref[0] = k_bf.astype(F32).T.astype(BF16)

    @pl.when(i == 0)
    def _():
        kn_ref[...] = jnp.zeros(kn_ref.shape, F32)

    k_sq = k_bf.astype(F32) ** 2
    for hd in range(n_k_rows // HEAD_DIM):
        n2 = jnp.sum(k_sq[hd * HEAD_DIM:(hd + 1) * HEAD_DIM], axis=0, keepdims=True)
        kn_ref[0, hd:hd + 1] = jnp.maximum(kn_ref[0, hd:hd + 1], jnp.max(n2, axis=1, keepdims=True))

    va = lax.dot_general(wva_ref[...], h, nt, preferred_element_type=F32)
    va_ref[0] = (va + ova_ref[...]).astype(va_ref.dtype)
    vb = lax.dot_general(wvb_ref[...], h, nt, preferred_element_type=F32)
    vb_ref[0] = (vb + ovb_ref[...]).astype(vb_ref.dtype)

    tok = jnp.dot(h, wtok_ref[...], preferred_element_type=F32)
    u_ref[0] = tok[:, :n_hy]
    gt_ref[0] = jax.nn.sigmoid(tok[:, n_hy:]).astype(gt_ref.dtype)


def _augment_values(w_t, dv):
    heads, K = w_t.shape[0] // dv, w_t.shape[1]
    dvp = dv + V_PAD
    w_aug = jnp.pad(w_t.reshape(heads, dv, K), ((0, 0), (0, V_PAD), (0, 0))).reshape(heads * dvp, K)
    ones_col = jnp.zeros((heads, dvp, 1), F32).at[:, dv].set(1.0).reshape(heads * dvp, 1)
    return w_aug, ones_col


def proj_all(x, xc, shift, scale, shift_c, scale_c, cos_t, sin_t, lw):
    B, N, D = x.shape
    Nc = xc.shape[1]
    ntot = N + Nc
    tt = _tile(Nc, 256, 128)
    assert N % tt == 0
    nl = N // tt
    n_k_rows = N_KHEADS * HEAD_DIM
    n_q_rows = lw["w_qk"].shape[0] - n_k_rows
    n_va, n_vb = lw["w_va"].shape[0], lw["w_vb"].shape[0]
    n_tok = lw["w_tok"].shape[1]
    n_hy = 3 * HY_WIDTH
    per_b = lambda b, i: (b, 0, 0)
    const = lambda b, i: (0, 0)
    resident = dict(pipeline_mode=pl.Buffered(1))
    feat = lambda rows: pl.BlockSpec((1, rows, tt), lambda b, i: (b, 0, i))
    tokm = lambda cols: pl.BlockSpec((1, tt, cols), lambda b, i: (b, i, 0))
    sds = jax.ShapeDtypeStruct
    return pl.pallas_call(
        functools.partial(_proj_all_kernel, n_lat_tiles=nl, rms_heads=lw["rms_heads"], n_hy=n_hy),
        out_shape=(sds((B, n_q_rows, ntot), BF16), sds((B, ntot, n_k_rows), BF16),
                   sds((B, KNORM_ROWS, 128), F32), sds((B, n_va, ntot), BF16), sds((B, n_vb, ntot), BF16),
                   sds((B, ntot, n_hy), F32), sds((B, ntot, n_tok - n_hy), BF16)),
        grid=(B, ntot // tt),
        in_specs=[pl.BlockSpec((1, tt, D), lambda b, i: (b, jnp.minimum(i, nl - 1), 0)),
                  pl.BlockSpec((1, tt, D), lambda b, i: (b, jnp.maximum(i - nl, 0), 0)),
                  pl.BlockSpec((1, 1, D), per_b), pl.BlockSpec((1, 1, D), per_b),
                  pl.BlockSpec((1, 1, D), per_b), pl.BlockSpec((1, 1, D), per_b),
                  pl.BlockSpec((HALF, tt), lambda b, i: (0, i)), pl.BlockSpec((HALF, tt), lambda b, i: (0, i)),
                  pl.BlockSpec(lw["w_qk"].shape, const, **resident), pl.BlockSpec((lw["w_qk"].shape[0], 1), const),
                  pl.BlockSpec(lw["w_va"].shape, const, **resident), pl.BlockSpec((n_va, 1), const),
                  pl.BlockSpec(lw["w_vb"].shape, const, **resident), pl.BlockSpec((n_vb, 1), const),
                  pl.BlockSpec(lw["w_tok"].shape, const, **resident)],
        out_specs=(feat(n_q_rows), tokm(n_k_rows), pl.BlockSpec((1, KNORM_ROWS, 128), per_b),
                   feat(n_va), feat(n_vb), tokm(n_hy), tokm(n_tok - n_hy)),
        scratch_shapes=[pltpu.VMEM((n_k_rows, tt), F32)],
        compiler_params=_cp(2), name="proj_all",
    )(x, xc, shift, scale, shift_c, scale_c, cos_t, sin_t, lw["w_qk"], lw["g_qk"], lw["w_va"], lw["ones_va"],
      lw["w_vb"], lw["ones_vb"], lw["w_tok"])


def _resid_ln_tail(x, y, gate, g, b, alpha):
    return _ln(alpha * x + gate * y) * g + b


def _mm_kernel(a_ref, w_ref, o_ref, *, act):
    acc = jnp.dot(a_ref[...], w_ref[...], preferred_element_type=F32)
    if act == "sigmoid":
        acc = jax.nn.sigmoid(acc)
    o_ref[...] = acc.astype(o_ref.dtype)


def matmul(a, w, out_dtype, act=None, tm_pref=1024, tn_pref=512):
    M, K = a.shape
    Nn = w.shape[1]
    tm = _tile(M, tm_pref, 16)
    tn = _tile(Nn, tn_pref, 128)
    return pl.pallas_call(
        functools.partial(_mm_kernel, act=act),
        out_shape=jax.ShapeDtypeStruct((M, Nn), out_dtype),
        grid=(M // tm, Nn // tn),
        in_specs=[pl.BlockSpec((tm, K), lambda i, j: (i, 0)),
                  pl.BlockSpec((K, tn), lambda i, j: (0, j))],
        out_specs=pl.BlockSpec((tm, tn), lambda i, j: (i, j)),
        compiler_params=_cp(2), name="matmul",
    )(a, w)


FLASH_MIN_DENOM = 2.0 ** -60


def _flash_kernel(lam_ref, kmax_ref, q_ref, k_ref, v_ref, g_ref, o_ref, q_sc, p_buf, sh_sc, acc_sc,
                  *, mode, nj, tk, dv, post_scale, n_kheads, khead0):
    b, h = pl.program_id(0), pl.program_id(1)
    qf = q_ref[0].astype(F32)
    tq = qf.shape[1]
    zero = jnp.zeros((HEAD_DIM, tq), F32)
    qa, qb = qf[:HEAD_DIM], qf[HEAD_DIM:]
    if mode == "da":
        q_sc[0] = jnp.concatenate([qa, zero], axis=0).astype(BF16)
        q_sc[1] = jnp.concatenate([zero, qb], axis=0).astype(BF16)
        k_heads = (khead0 + 2 * h, khead0 + 2 * h + 1)
    else:
        group = h // 2
        for mi, qh in enumerate((qa, qb)):
            q_sc[mi] = jnp.where(group == 0, jnp.concatenate([qh, zero], axis=0),
                                 jnp.concatenate([zero, qh], axis=0)).astype(BF16)
        k_heads = (khead0 + group, khead0 + group)
    for mi, qh in enumerate((qa, qb)):
        q_norm = jnp.sqrt(jnp.sum(qh * qh, axis=0, keepdims=True))
        sh_sc[mi] = q_norm * kmax_ref[b * n_kheads + k_heads[mi]]

    def chunk(j):
        start = j * tk
        return pl.ds(start if isinstance(start, int) else pl.multiple_of(start, tk), tk)

    def stage_exp(j, slot):
        kb = k_ref[0, chunk(j), :]
        for mi in range(2):
            s = jnp.dot(kb, q_sc[mi], preferred_element_type=F32)
            p_buf[slot, mi] = jnp.exp2(s - sh_sc[mi]).astype(BF16)

    def stage_values(j, slot):
        vb = v_ref[0, :, chunk(j)]
        for mi in range(2):
            acc_sc[mi] += jnp.dot(vb, p_buf[slot, mi], preferred_element_type=F32)

    def accumulate():
        acc_sc[...] = jnp.zeros(acc_sc.shape, F32)
        stage_exp(0, 0)
        steady = list(range(1, nj))
        if len(steady) % 2 == 1:
            t = steady.pop(0)
            stage_exp(t, t % 2)
            stage_values(t - 1, (t - 1) % 2)
        if steady:
            t0 = steady[0]

            def body(i, carry):
                t = t0 + 2 * i
                stage_exp(t, t0 % 2)
                stage_values(t - 1, 1 - t0 % 2)
                stage_exp(t + 1, 1 - t0 % 2)
                stage_values(t, t0 % 2)
                return carry

            lax.fori_loop(0, len(steady) // 2, body, 0)
        stage_values(nj - 1, (nj - 1) % 2)

    accumulate()
    denom_min = jnp.minimum(jnp.min(acc_sc[0][dv:dv + 1]), jnp.min(acc_sc[1][dv:dv + 1]))

    @pl.when(jnp.logical_not(denom_min >= FLASH_MIN_DENOM))
    def _():
        for mi in range(2):
            def max_body(j, m, mi=mi):
                s = jnp.dot(k_ref[0, chunk(j), :], q_sc[mi], preferred_element_type=F32)
                return jnp.maximum(m, jnp.max(s, axis=0, keepdims=True))

            sh_sc[mi] = lax.fori_loop(0, nj, max_body, jnp.full((1, tq), NEG_BIG, F32))
        accumulate()

    outs = []
    for mi in range(2):
        acc = acc_sc[mi]
        outs.append(acc[:dv] / acc[dv:dv + 1])
    if mode == "da":
        o = outs[0] - lam_ref[0] * outs[1]
        o = o * lax.rsqrt(jnp.mean(o * o, axis=0, keepdims=True) + LN_EPS)
        o = o * (g_ref[...] * post_scale)
    else:
        o = jnp.concatenate(outs, axis=0)
    o_ref[0] = o.T.astype(o_ref.dtype)


def flash_attention(q_t, k_tok, v_aug, kmax, lam, gain, *, mode, q_span, kv_span, q_row_block, k_col_block,
                    khead0, post_scale=1.0, tk_pref=768):
    B = q_t.shape[0]
    q0, Nq = q_span
    k0, Nk = kv_span
    tq = _tile(Nq, 1024, 128)
    tk = _tile(Nk, tk_pref, 128)
    assert q0 % tq == 0 and k0 % Nk == 0
    nj = Nk // tk
    dv = 2 * HEAD_DIM if mode == "da" else HEAD_DIM
    dvp = dv + V_PAD
    n_kheads = kmax.shape[0] // B
    if mode == "da":
        k_map = lambda b, h, i: (b, k0 // Nk, k_col_block + h)
        v_map = lambda b, h, i: (b, h, k0 // Nk)
    else:
        k_map = lambda b, h, i: (b, k0 // Nk, k_col_block)
        v_map = lambda b, h, i: (b, h // 2, k0 // Nk)
    return pl.pallas_call(
        functools.partial(_flash_kernel, mode=mode, nj=nj, tk=tk, dv=dv, post_scale=post_scale,
                          n_kheads=n_kheads, khead0=khead0),
        out_shape=jax.ShapeDtypeStruct((B, Nq, 4 * 128), BF16),
        grid=(B, 4, Nq // tq),
        in_specs=[pl.BlockSpec(memory_space=pltpu.SMEM),
                  pl.BlockSpec(memory_space=pltpu.SMEM),
                  pl.BlockSpec((1, 128, tq), lambda b, h, i: (b, h + q_row_block, i + q0 // tq)),
                  pl.BlockSpec((1, Nk, 128), k_map),
                  pl.BlockSpec((1, dvp, Nk), v_map),
                  pl.BlockSpec((128, 1), lambda b, h, i: (0, 0))],
        out_specs=pl.BlockSpec((1, tq, 128), lambda b, h, i: (b, i, h)),
        scratch_shapes=[pltpu.VMEM((2, 128, tq), BF16),
                        pltpu.VMEM((2, 2, tk, tq), BF16),
                        pltpu.VMEM((2, 1, tq), F32),
                        pltpu.VMEM((2, dvp, tq), F32)],
        compiler_params=_cp(3), name="flash_" + mode,
    )(lam, kmax, q_t, k_tok, v_aug, gain)


def _shift_rows(x, prev_row, next_row):
    T = x.shape[0]
    row = lax.broadcasted_iota(jnp.int32, x.shape, 0)
    xp = jnp.where(row == 0, prev_row, pltpu.roll(x, 1, axis=0))
    xn = jnp.where(row == T - 1, next_row, pltpu.roll(x, T - 1, axis=0))
    return xp, xn


def _conv3_block(x_ref, p_ref, n_ref, w_ref, b_ref, halo):
    i = pl.program_id(1)
    last = pl.num_programs(1) - 1
    x = x_ref[0].astype(F32)
    prev_row = jnp.where(i > 0, p_ref[0].astype(F32)[halo - 1:halo], 0.0)
    next_row = jnp.where(i < last, n_ref[0].astype(F32)[0:1], 0.0)
    xp, xn = _shift_rows(x, prev_row, next_row)
    return xp * w_ref[0:1] + x * w_ref[1:2] + xn * w_ref[2:3] + b_ref[...]


def _halo_specs(tr, tc, halo, row0, n_rows, col_of):
    per = tr // halo
    n_halo = n_rows // halo
    m0, h0 = row0 // tr, row0 // halo
    return [pl.BlockSpec((1, tr, tc), lambda b, i, j: (b, m0 + i, col_of(j))),
            pl.BlockSpec((1, halo, tc), lambda b, i, j: (b, h0 + jnp.maximum(i * per - 1, 0), col_of(j))),
            pl.BlockSpec((1, halo, tc),
                         lambda b, i, j: (b, h0 + jnp.minimum((i + 1) * per, n_halo - 1), col_of(j)))]


def _dwconv_kernel(x_ref, p_ref, n_ref, w_ref, b_ref, v_ref, x1_ref, x2_ref):
    y = _conv3_block(x_ref, p_ref, n_ref, w_ref, b_ref, 8)
    C = v_ref.shape[2]
    v_ref[0] = y[:, :C]
    x1_ref[0] = y[:, C:2 * C]
    x2_ref[0] = y[:, 2 * C:]


def hyena_short_conv(u, w, b, row0, L):
    B, _, C3 = u.shape
    C = C3 // 3
    tr = _tile(L, 512, 8)
    assert row0 % tr == 0
    zero = lambda j: 0
    out = jax.ShapeDtypeStruct((B, L, C), F32)
    ospec = pl.BlockSpec((1, tr, C), lambda b, i, j: (b, i, 0))
    return pl.pallas_call(
        _dwconv_kernel,
        out_shape=(out, out, out),
        grid=(B, L // tr, 1),
        in_specs=_halo_specs(tr, C3, 8, row0, L, zero) + [
            pl.BlockSpec((3, C3), lambda b, i, j: (0, 0)),
            pl.BlockSpec((1, C3), lambda b, i, j: (0, 0))],
        out_specs=(ospec, ospec, ospec),
        compiler_params=_cp(3), name="hyena_short_conv",
    )(u, u, u, w, b)


FFN_HALO = 16
FFN_CHUNK = 256


def _ffn_kernel(h_ref, hp_ref, hn_ref, wup_ref, cw_ref, cb_ref, wdn_ref, x_ref, gate_ref, g_ref, b_ref, o_ref,
                act_sc, *, alpha):
    i = pl.program_id(1)
    tm = h_ref.shape[1]
    dff = wdn_ref.shape[0]
    rows = tm + 2 * FFN_HALO
    h_prev = jnp.where(i == 0, jnp.zeros_like(hp_ref[0]), hp_ref[0])
    h_next = jnp.where(i == pl.num_programs(1) - 1, jnp.zeros_like(hn_ref[0]), hn_ref[0])
    h_ext = jnp.concatenate([h_prev, h_ref[0], h_next], axis=0)

    def conv_cols(c0):
        u = jnp.dot(h_ext, wup_ref[:, c0:c0 + FFN_CHUNK], preferred_element_type=F32)
        mid = slice(FFN_HALO, FFN_HALO + tm)
        u_prev = pltpu.roll(u, 1, axis=0)[mid]
        u_next = pltpu.roll(u, rows - 1, axis=0)[mid]
        w = cw_ref[:, c0:c0 + FFN_CHUNK]
        return u_prev * w[0:1] + u[mid] * w[1:2] + u_next * w[2:3] + cb_ref[:, c0:c0 + FFN_CHUNK]

    for c0 in range(0, dff, FFN_CHUNK):
        val = conv_cols(c0)
        gate = conv_cols(dff + c0)
        act_sc[:, c0:c0 + FFN_CHUNK] = (gate * jax.nn.sigmoid(gate) * val).astype(BF16)
    y = jnp.dot(act_sc[...], wdn_ref[...], preferred_element_type=F32)
    o_ref[0] = _resid_ln_tail(x_ref[0], y, gate_ref[0], g_ref[...], b_ref[...], alpha)


def conv_ffn_resid_ln(h, w_up, conv_w, conv_b, w_down, x, gate, g, b, alpha):
    B, N, D = h.shape
    dff = w_down.shape[0]
    tm = _tile(N, 512, FFN_HALO)
    per = tm // FFN_HALO
    n_halo = N // FFN_HALO
    row = lambda b_, i: (b_, i, 0)
    per_b = lambda b_, i: (b_, 0, 0)
    const2 = lambda b_, i: (0, 0)
    resident = dict(pipeline_mode=pl.Buffered(1))
    return pl.pallas_call(
        functools.partial(_ffn_kernel, alpha=alpha),
        out_shape=jax.ShapeDtypeStruct((B, N, D), F32),
        grid=(B, N // tm),
        in_specs=[pl.BlockSpec((1, tm, D), row),
                  pl.BlockSpec((1, FFN_HALO, D), lambda b_, i: (b_, jnp.maximum(i * per - 1, 0), 0)),
                  pl.BlockSpec((1, FFN_HALO, D), lambda b_, i: (b_, jnp.minimum((i + 1) * per, n_halo - 1), 0)),
                  pl.BlockSpec((D, 2 * dff), const2, **resident),
                  pl.BlockSpec((3, 2 * dff), const2), pl.BlockSpec((1, 2 * dff), const2),
                  pl.BlockSpec((dff, D), const2, **resident),
                  pl.BlockSpec((1, tm, D), row), pl.BlockSpec((1, 1, D), per_b),
                  pl.BlockSpec((1, D), const2), pl.BlockSpec((1, D), const2)],
        out_specs=pl.BlockSpec((1, tm, D), row),
        scratch_shapes=[pltpu.VMEM((tm, dff), BF16)],
        compiler_params=_cp(2), name="conv_ffn_resid_ln",
    )(h, h, h, w_up, conv_w, conv_b, w_down, x, gate, g, b)


def _merge_kernel(a_ref, b_ref, c_ref, gt_ref, wpa_ref, wpb_ref, wpc_ref, wo_ref, x_ref, gate_ref,
                  g_ref, bb_ref, sh_ref, sc_ref, ox_ref, oh_ref, *, alpha):
    D = wo_ref.shape[0]
    gt = gt_ref[0]
    m = gt[:, :D].astype(F32) * jnp.dot(a_ref[0], wpa_ref[...], preferred_element_type=F32)
    m = m + gt[:, D:2 * D].astype(F32) * jnp.dot(b_ref[0], wpb_ref[...], preferred_element_type=F32)
    m = m + gt[:, 2 * D:].astype(F32) * jnp.dot(c_ref[0].astype(BF16), wpc_ref[...],
                                                preferred_element_type=F32)
    y = jnp.dot(m.astype(BF16), wo_ref[...], preferred_element_type=F32)
    xn = _resid_ln_tail(x_ref[0], y, gate_ref[0], g_ref[...], bb_ref[...], alpha)
    ox_ref[0] = xn
    oh_ref[0] = (_ln(xn) * (1.0 + sc_ref[0]) + sh_ref[0]).astype(oh_ref.dtype)


def merge_resid_ln(a, b, c, gates, tok0, wpa, wpb, wpc, wo, x, gate, g, bb, sh, sc, alpha):
    B, N, D = x.shape
    W = a.shape[2]
    tm = _tile(N, 512, 16)
    assert tok0 % tm == 0
    row = lambda b_, i: (b_, i, 0)
    per_b = lambda b_, i: (b_, 0, 0)
    const2 = lambda b_, i: (0, 0)
    return pl.pallas_call(
        functools.partial(_merge_kernel, alpha=alpha),
        out_shape=(jax.ShapeDtypeStruct((B, N, D), F32), jax.ShapeDtypeStruct((B, N, D), BF16)),
        grid=(B, N // tm),
        in_specs=[pl.BlockSpec((1, tm, W), row), pl.BlockSpec((1, tm, W), row), pl.BlockSpec((1, tm, W), row),
                  pl.BlockSpec((1, tm, 3 * D), lambda b_, i: (b_, tok0 // tm + i, 0)),
                  pl.BlockSpec((W, D), const2), pl.BlockSpec((W, D), const2), pl.BlockSpec((W, D), const2),
                  pl.BlockSpec((D, D), const2),
                  pl.BlockSpec((1, tm, D), row), pl.BlockSpec((1, 1, D), per_b),
                  pl.BlockSpec((1, D), const2), pl.BlockSpec((1, D), const2),
                  pl.BlockSpec((1, 1, D), per_b), pl.BlockSpec((1, 1, D), per_b)],
        out_specs=(pl.BlockSpec((1, tm, D), row), pl.BlockSpec((1, tm, D), row)),
        compiler_params=_cp(2), name="merge_resid_ln",
    )(a, b, c, gates, wpa, wpb, wpc, wo, x, gate, g, bb, sh, sc)


def _split_bf16(x):
    hi = x.astype(BF16)
    lo = (x - hi.astype(F32)).astype(BF16)
    return hi, lo


def _dot3(a, b):
    ah, al = _split_bf16(a)
    bh, bl = _split_bf16(b)
    d = functools.partial(jnp.dot, preferred_element_type=F32)
    return d(ah, bh) + (d(ah, bl) + d(al, bh))


def _filter_kernel(z_ref, w1_ref, b1_ref, w2_ref, b2_ref, w3_ref, fr_ref, dec_ref, h_ref, s_ref, *, zero_block):
    i = pl.program_id(0)
    fr = fr_ref[...]
    hid = jnp.sin(fr * (_dot3(z_ref[...], w1_ref[...]) + b1_ref[...]))
    hid = jnp.sin(fr * (_dot3(hid, w2_ref[...]) + b2_ref[...]))
    h = _dot3(hid, w3_ref[0])
    dec = dec_ref[...]
    C = dec.shape[1]
    n_ord = h.shape[1] // C
    h = h * jnp.concatenate([dec] * n_ord, axis=1)

    @pl.when(i == 0)
    def _():
        s_ref[...] = jnp.zeros(s_ref.shape, F32)

    s_ref[...] += jnp.sum(jnp.abs(h), axis=0, keepdims=True)
    row = lax.broadcasted_iota(jnp.int32, h.shape, 0)
    h = jnp.where((row == 0) & (i == zero_block), 0.0, h)
    for o in range(n_ord):
        h_ref[o] = h[:, o * C:(o + 1) * C]


def hyena_filter_mlp(z2, w1, b1, w2, b2, w3_dir, freq, decay2):
    n2, E = z2.shape
    Hh = w2.shape[0]
    OC = w3_dir.shape[2]
    C = decay2.shape[1]
    n_ord = OC // C
    L = n2 // 2
    tr = _tile(L, 512, 8)
    nblk = n2 // tr
    c2 = lambda i: (0, 0)
    return pl.pallas_call(
        functools.partial(_filter_kernel, zero_block=L // tr),
        out_shape=(jax.ShapeDtypeStruct((n_ord, n2, C), F32), jax.ShapeDtypeStruct((1, OC), F32)),
        grid=(nblk,),
        in_specs=[pl.BlockSpec((tr, E), lambda i: (i, 0)), pl.BlockSpec((E, Hh), c2), pl.BlockSpec((1, Hh), c2),
                  pl.BlockSpec((Hh, Hh), c2), pl.BlockSpec((1, Hh), c2),
                  pl.BlockSpec((1, Hh, OC), lambda i: (i // (nblk // 2), 0, 0)),
                  pl.BlockSpec((1, Hh), c2), pl.BlockSpec((tr, C), lambda i: (i, 0))],
        out_specs=(pl.BlockSpec((n_ord, tr, C), lambda i: (0, i, 0)), pl.BlockSpec((1, OC), c2)),
        compiler_params=_cp(1), name="hyena_filter_mlp",
    )(z2, w1, b1, w2, b2, w3_dir, freq, decay2)


def _mm_split(m_hi, m_lo, x, passes):
    d = functools.partial(jnp.dot, preferred_element_type=F32)
    if passes == 1:
        return d(m_hi, x.astype(BF16))
    xh, xl = _split_bf16(x)
    return d(m_hi, xh) + (d(m_lo, xh) + d(m_hi, xl))


FFT_ROWS = 8
FFT_BLOCK = 16


def _level1_apply(m_refs, x, r_out, passes):
    r_in, _, C = x.shape
    x = x.astype(F32)
    m_hi = m_refs[0][...]
    m_lo = m_refs[1][...] if passes > 1 else None
    groups = []
    for g in range(FFT_BLOCK // FFT_ROWS):
        xg = x[:, g * FFT_ROWS:(g + 1) * FFT_ROWS].reshape(r_in * FFT_ROWS, C)
        groups.append(_mm_split(m_hi, m_lo, xg, passes).reshape(r_out, FFT_ROWS, C))
    return jnp.concatenate(groups, axis=1)


def _level1_kernel(*refs, passes):
    n_mat = 1 if passes == 1 else 2
    x_ref, o_ref = refs[n_mat:]
    o_ref[0] = _level1_apply(refs[:n_mat], x_ref[0], o_ref.shape[1], passes).astype(o_ref.dtype)


def _level1_gate_kernel(*refs, passes):
    n_mat = 1 if passes == 1 else 2
    x_ref, z_ref, xg_ref, bias_ref, o_ref = refs[n_mat:]
    y = _level1_apply(refs[:n_mat], x_ref[0], o_ref.shape[1], passes)
    o_ref[0] = (xg_ref[0] * (y + bias_ref[...] * z_ref[0])).astype(o_ref.dtype)


def fft_level1(m_hi, m_lo, x, passes, gate_args=None, out_dtype=F32):
    P, R_in, Nb, C = x.shape
    R_out = m_hi.shape[0] // FFT_ROWS
    mats = (m_hi,) if passes == 1 else (m_hi, m_lo)
    mspec = pl.BlockSpec(m_hi.shape, lambda p, j: (0, 0), pipeline_mode=pl.Buffered(1))
    xspec = pl.BlockSpec((1, R_in, FFT_BLOCK, C), lambda p, j: (p, 0, j, 0))
    ospec = pl.BlockSpec((1, R_out, FFT_BLOCK, C), lambda p, j: (p, 0, j, 0))
    if gate_args is None:
        kern = functools.partial(_level1_kernel, passes=passes)
        ins, specs = (*mats, x), [mspec] * len(mats) + [xspec]
    else:
        z, xg, bias = gate_args
        kern = functools.partial(_level1_gate_kernel, passes=passes)
        ins = (*mats, x, z, xg, bias)
        specs = [mspec] * len(mats) + [xspec, ospec, ospec, pl.BlockSpec((1, C), lambda p, j: (0, 0))]
    return pl.pallas_call(
        kern, out_shape=jax.ShapeDtypeStruct((P, R_out, Nb, C), out_dtype),
        grid=(P, Nb // FFT_BLOCK), in_specs=specs, out_specs=ospec,
        compiler_params=_cp(2), name="fft_level1",
    )(*ins)


FFT_MID_K1 = 2


def _fft_mid_kernel(*refs, passes):
    n_mat = 2 if passes == 1 else 4
    gh_ref, gih_ref = refs[0], refs[1]
    gl_ref, gil_ref = (refs[2], refs[3]) if passes > 1 else (None, None)
    h_ref, a_ref, o_ref = refs[n_mat:]
    n_pairs, _, kb, nb, C = a_ref.shape
    for kk in range(kb):
        hr, hi = h_ref[0, 0, kk], h_ref[0, 1, kk]
        for p in range(n_pairs):
            x = a_ref[p, :, kk].reshape(2 * nb, C)
            X = _mm_split(gh_ref[kk], None if gl_ref is None else gl_ref[kk], x, passes)
            xr, xi = X[:nb], X[nb:]
            Y = jnp.concatenate([xr * hr - xi * hi, xr * hi + xi * hr], axis=0)
            Bv = _mm_split(gih_ref[kk], None if gil_ref is None else gil_ref[kk], Y, passes)
            o_ref[p, :, kk] = Bv.reshape(2, nb, C).astype(o_ref.dtype)


def fft_mid(g_hi, g_lo, gi_hi, gi_lo, spectra, order, a5, passes):
    P, _, Na, Nb, C = a5.shape
    kb = FFT_MID_K1
    gspec = pl.BlockSpec((kb, 2 * Nb, 2 * Nb), lambda k: (k, 0, 0))
    mats = (g_hi, gi_hi) if passes == 1 else (g_hi, gi_hi, g_lo, gi_lo)
    blk = pl.BlockSpec((P, 2, kb, Nb, C), lambda k: (0, 0, k, 0, 0))
    return pl.pallas_call(
        functools.partial(_fft_mid_kernel, passes=passes),
        out_shape=jax.ShapeDtypeStruct(a5.shape, a5.dtype),
        grid=(Na // kb,),
        in_specs=[gspec] * len(mats) + [pl.BlockSpec((1, 2, kb, Nb, C), lambda k: (order, 0, k, 0, 0)), blk],
        out_specs=blk,
        compiler_params=_cp(1), name="fft_mid",
    )(*mats, spectra, a5)


def _fft_spec_kernel(*refs, passes):
    n_mat = 1 if passes == 1 else 2
    gh_ref = refs[0]
    gl_ref = refs[1] if passes > 1 else None
    inv_ref, a_ref, o_ref = refs[n_mat:]
    n_filt, _, kb, nb, C = a_ref.shape
    for kk in range(kb):
        for p in range(n_filt):
            x = a_ref[p, :, kk].reshape(2 * nb, C)
            X = _mm_split(gh_ref[kk], None if gl_ref is None else gl_ref[kk], x, passes) * inv_ref[p]
            o_ref[p, :, kk] = X.reshape(2, nb, C)


def fft_filter_spectrum(g_hi, g_lo, inv_norm, a5, passes):
    P, _, Na, Nb, C = a5.shape
    kb = FFT_MID_K1
    mats = (g_hi,) if passes == 1 else (g_hi, g_lo)
    gspec = pl.BlockSpec((kb, 2 * Nb, 2 * Nb), lambda k: (k, 0, 0))
    blk = pl.BlockSpec((P, 2, kb, Nb, C), lambda k: (0, 0, k, 0, 0))
    return pl.pallas_call(
        functools.partial(_fft_spec_kernel, passes=passes),
        out_shape=jax.ShapeDtypeStruct(a5.shape, F32),
        grid=(Na // kb,),
        in_specs=[gspec] * len(mats) + [pl.BlockSpec((P, 1, C), lambda k: (0, 0, 0)), blk],
        out_specs=blk,
        compiler_params=_cp(1), name="fft_filter_spectrum",
    )(*mats, inv_norm, a5)


def _fft_factors(n_fft):
    na = 1 << (int(math.log2(n_fft)) // 2)
    return na, n_fft // na


def _fft_tables(L):
    n_fft = 2 * L
    na, nb = _fft_factors(n_fft)

    def cis(num, den):
        ang = (2.0 * math.pi / den) * (num % den).astype(F32)
        return jnp.cos(ang), -jnp.sin(ang)

    ia = jnp.arange(na, dtype=jnp.int32)
    ib = jnp.arange(nb, dtype=jnp.int32)
    far, fai = cis(ia[:, None] * ia[None, :], na)
    fh_r, fh_i = far[:, :na // 2], fai[:, :na // 2]
    m_fwd = jnp.block([[fh_r, -fh_i], [fh_i, fh_r]])
    m_flt = jnp.concatenate([far, fai], axis=0)
    m_inv = jnp.block([[fh_r.T, fh_i.T], [-fh_i.T, fh_r.T]]) / n_fft
    num = ib[None, :, None] * ib[None, None, :] * na + ia[:, None, None] * ib[None, None, :]
    gr, gi = cis(num, n_fft)
    g = jnp.concatenate([jnp.concatenate([gr, -gi], axis=2), jnp.concatenate([gi, gr], axis=2)], axis=1)
    grt, git = jnp.swapaxes(gr, 1, 2), jnp.swapaxes(gi, 1, 2)
    ginv = jnp.concatenate([jnp.concatenate([grt, git], axis=2), jnp.concatenate([-git, grt], axis=2)], axis=1)
    split = lambda m: _split_bf16(m.astype(F32))
    kron = lambda m: split(jnp.kron(m, jnp.eye(FFT_ROWS, dtype=F32)))
    C = HY_WIDTH
    t = jnp.linspace(0.0, 1.0, L, dtype=F32)[:, None]
    f = jnp.linspace(1e-4, HY_BANDS - 1, HY_BANDS, dtype=F32)
    ang = (2.0 * math.pi / L) * jnp.arange(L, dtype=F32)[:, None] * f[None, :]
    z = jnp.concatenate([t, jnp.cos(ang), -jnp.sin(ang)], axis=-1)
    z = jnp.pad(z, ((0, 0), (0, HY_EMB_PAD - HY_EMB)))
    deltas = jnp.abs(jnp.linspace(HY_MIN_DECAY, HY_MAX_DECAY, C, dtype=F32))
    decay = jnp.exp(-t * deltas)
    circ = lambda a: jnp.concatenate([a, a[:1], a[:0:-1]], axis=0)
    return dict(na=na, nb=nb, fwd=kron(m_fwd), flt=kron(m_flt), inv=kron(m_inv), g=split(g), ginv=split(ginv),
                z2=circ(z), decay2=circ(decay))


def hyena_filter_spectra(w1, b1, w2, b2, w3, freq, tabs, passes):
    C = HY_WIDTH
    na, nb = tabs["na"], tabs["nb"]
    w1p = jnp.pad(w1, ((0, HY_EMB_PAD - HY_EMB), (0, 0)))
    w3_dir = jnp.transpose(w3.reshape(HY_HIDDEN, HY_ORDER, 2, C), (2, 0, 1, 3)).reshape(2, HY_HIDDEN, HY_ORDER * C)
    filt, s = hyena_filter_mlp(tabs["z2"], w1p, b1[None], w2, b2[None], w3_dir, freq[None], tabs["decay2"])
    inv_norm = (1.0 / s).reshape(HY_ORDER, 1, C)
    a = fft_level1(*tabs["flt"], filt.reshape(HY_ORDER, na, nb, C), passes,
                   out_dtype=BF16 if passes == 1 else F32)
    return fft_filter_spectrum(*tabs["g"], inv_norm, a.reshape(HY_ORDER, 2, na, nb, C), passes)


def hyena_long_conv_gate(z, xg, bias, spectra, order, tabs, passes):
    B, L, C = z.shape
    na, nb = tabs["na"], tabs["nb"]
    P = B // 2
    nat = lambda a: a.reshape(P, na, nb, C)
    mid_dtype = BF16 if passes == 1 else F32
    a = fft_level1(*tabs["fwd"], nat(z), passes, out_dtype=mid_dtype)
    bv = fft_mid(*tabs["g"], *tabs["ginv"], spectra, order, a.reshape(P, 2, na, nb, C), passes)
    y = fft_level1(*tabs["inv"], bv.reshape(P, 2 * na, nb, C), passes,
                   gate_args=(nat(z), nat(xg), bias[None]))
    return y.reshape(B, L, C)


def hyena_mix(u, row0, L, spectra, conv_w, conv_b, bias, tabs, passes):
    v, x1, x2 = hyena_short_conv(u, conv_w, conv_b[None], row0, L)
    zz = hyena_long_conv_gate(v, x1, bias[0], spectra, 0, tabs, passes)
    return hyena_long_conv_gate(zz, x2, bias[1], spectra, 1, tabs, passes)


FFT_PASSES = 1
FILTER_PASSES = 1


def _rope_tables_t(n):
    rows = n // GRID_W
    r = jnp.repeat(jnp.arange(rows, dtype=F32), GRID_W)
    col = jnp.tile(jnp.arange(GRID_W, dtype=F32), rows)
    axis_dim = HEAD_DIM // 2
    inv = ROPE_THETA ** (-jnp.arange(0, axis_dim, 2, dtype=F32) / axis_dim)
    ang = jnp.concatenate([r[:, None] * inv, col[:, None] * inv], axis=-1)
    return jnp.cos(ang).T, jnp.sin(ang).T


def _layer_weights(l, w_in, gq_qn, gq_kn):
    w = w_in[l]
    d_qk = DA_HEADS * 2 * HEAD_DIM
    gq_kv = GKV_HEADS * HEAD_DIM
    gq_q = GQ_HEADS * HEAD_DIM
    o = np.cumsum([0, d_qk, d_qk, gq_kv, gq_kv, d_qk, gq_q, 3 * HY_WIDTH])
    ka, va, kb, vb, qa, qb, hy = (w[:, o[i]:o[i + 1]] for i in range(7))
    gates = w[:, o[7]:]
    qscale = (HEAD_DIM ** -0.5) * LOG2E
    w_qk = jnp.concatenate([ka, kb, qa, qb], axis=1).T.astype(BF16)
    g_qk = jnp.concatenate([jnp.ones((d_qk,), F32), jnp.tile(gq_kn[l], GKV_HEADS),
                            jnp.full((d_qk,), qscale, F32), jnp.tile(gq_qn[l], GQ_HEADS) * qscale])[:, None]
    n_ka, n_kb, n_qa = d_qk // HEAD_DIM, gq_kv // HEAD_DIM, d_qk // HEAD_DIM
    rms_heads = frozenset(range(n_ka, n_ka + n_kb)) | frozenset(range(n_ka + n_kb + n_qa,
                                                                       n_ka + n_kb + n_qa + gq_q // HEAD_DIM))
    w_va, ones_va = _augment_values(va.T.astype(BF16), 2 * HEAD_DIM)
    w_vb, ones_vb = _augment_values(vb.T.astype(BF16), HEAD_DIM)
    return dict(w_qk=w_qk, g_qk=g_qk, rms_heads=rms_heads, w_va=w_va, ones_va=ones_va, w_vb=w_vb,
                ones_vb=ones_vb, w_tok=jnp.concatenate([hy, gates], axis=1).astype(BF16))


def kernel(x, c, ctx, c_ctx, w_mod, b_mod, w_in, da_lq1, da_lk1, da_lq2, da_lk2, da_subln, gq_qn, gq_kn,
           hy_conv_w, hy_conv_b, hy_w1, hy_b1, hy_w2, hy_b2, hy_w3, hy_freq, hy_bias, w_pa, w_pb, w_pc, w_o,
           ln1_g, ln1_b, w_up, ffn_conv_w, ffn_conv_b, w_down, ln2_g, ln2_b):
    B, n_lat, D = x.shape
    n_ctx = ctx.shape[1]
    depth = w_in.shape[0]
    alpha = (2 * depth) ** 0.25
    cos_l, sin_l = _rope_tables_t(n_lat)
    cos_t = jnp.concatenate([cos_l, jnp.ones((HALF, n_ctx), F32)], axis=1)
    sin_t = jnp.concatenate([sin_l, jnp.zeros((HALF, n_ctx), F32)], axis=1)
    lat, ctx_span, all_span = (0, n_lat), (n_lat, n_ctx), (0, n_lat + n_ctx)
    tabs_l = _fft_tables(n_lat)
    tabs_c = _fft_tables(n_ctx)
    xc = ctx

    cond = jnp.concatenate([c, c_ctx[None]], axis=0)
    cond = jnp.pad(jax.nn.silu(cond), ((0, 16 - (B + 1) % 16), (0, 0))).astype(BF16)

    for l in range(depth):
        last = l == depth - 1
        lam_init = 0.8 - 0.6 * math.exp(-0.3 * l)
        lam = (jnp.exp(jnp.sum(da_lq1[l] * da_lk1[l])) - jnp.exp(jnp.sum(da_lq2[l] * da_lk2[l])) + lam_init)
        lam = lam.reshape(1).astype(F32)
        mod = matmul(cond, w_mod[l].astype(BF16), F32, tn_pref=1024) + b_mod[l]
        sh1, sc1, g1, sh2, sc2, g2 = [m[:, None, :] for m in jnp.split(mod[:B], 6, axis=-1)]
        mc = [jnp.broadcast_to(m[None, None, :], (B, 1, D)) for m in jnp.split(mod[B], 6)]
        lw = _layer_weights(l, w_in, gq_qn, gq_kn)
        subln = da_subln[l][:, None]
        hy_mlp = (hy_w1[l], hy_b1[l], hy_w2[l], hy_b2[l], hy_w3[l], hy_freq[l])
        spectra_l = hyena_filter_spectra(*hy_mlp, tabs_l, FILTER_PASSES)
        wpa, wpb, wpc, wo = (w[l].astype(BF16) for w in (w_pa, w_pb, w_pc, w_o))
        wup, wdn = w_up[l].astype(BF16), w_down[l].astype(BF16)
        ln1 = (ln1_g[l][None], ln1_b[l][None])
        ln2 = (ln2_g[l][None], ln2_b[l][None])

        q_all, k_tok, k_n2, va, vb, u, gates = proj_all(x, xc, sh1, sc1, mc[0], mc[1], cos_t, sin_t, lw)
        kmax = jnp.sqrt(k_n2[:, :N_KHEADS, 0]).reshape(B * N_KHEADS)
        n_da_maps = DA_HEADS * 2

        def attend(q_span, kv_span):
            da = flash_attention(q_all, k_tok, va, kmax, lam, subln, mode="da", q_span=q_span, kv_span=kv_span,
                                 q_row_block=0, k_col_block=0, khead0=0, post_scale=1.0 - lam_init)
            gq = flash_attention(q_all, k_tok, vb, kmax, lam, subln, mode="gqa", q_span=q_span, kv_span=kv_span,
                                 q_row_block=n_da_maps * HEAD_DIM // 128, k_col_block=n_da_maps * HEAD_DIM // 128,
                                 khead0=n_da_maps)
            return da, gq

        hy_args = (hy_conv_w[l], hy_conv_b[l], hy_bias[l])
        a_l, b_l = attend(lat, all_span)
        c_l = hyena_mix(u, *lat, spectra_l, *hy_args, tabs_l, FFT_PASSES)
        x_new, h2 = merge_resid_ln(a_l, b_l, c_l, gates, lat[0], wpa, wpb, wpc, wo, x, g1, *ln1,
                                   sh2, sc2, alpha)
        x_new = conv_ffn_resid_ln(h2, wup, ffn_conv_w[l], ffn_conv_b[l][None], wdn, x_new, g2, *ln2, alpha)

        if not last:
            spectra_c = hyena_filter_spectra(*hy_mlp, tabs_c, FILTER_PASSES)
            a_c, b_c = attend(ctx_span, ctx_span)
            c_c = hyena_mix(u, *ctx_span, spectra_c, *hy_args, tabs_c, FFT_PASSES)
            xc, hc2 = merge_resid_ln(a_c, b_c, c_c, gates, ctx_span[0], wpa, wpb, wpc, wo, xc, mc[2],
                                     *ln1, mc[3], mc[4], alpha)
            xc = conv_ffn_resid_ln(hc2, wup, ffn_conv_w[l], ffn_conv_b[l][None], wdn, xc, mc[5], *ln2, alpha)
        x = x_new
    return x
```

```python
import functools
import math

import numpy as np
import jax
import jax.numpy as jnp
from jax import lax
from jax.experimental import pallas as pl
from jax.experimental.pallas import tpu as pltpu

F32 = jnp.float32
BF16 = jnp.bfloat16

HEAD_DIM = 64
HALF = HEAD_DIM // 2
GRID_W = 64
ROPE_THETA = 10000.0
DA_HEADS = 4
GQ_HEADS = 8
GKV_HEADS = 2
HY_WIDTH = 512
HY_ORDER = 2
HY_EMB = 33
HY_EMB_PAD = 128
HY_BANDS = (HY_EMB - 1) // 2
HY_HIDDEN = 64
HY_MIN_DECAY = math.log(1e-2) / 1.5
HY_MAX_DECAY = math.log(1e-2) / 0.3
LN_EPS = 1e-6
LOG2E = 1.4426950408889634
NEG_BIG = -1e30

VMEM_LIMIT = 56 * 1024 * 1024


def _cp(n_axes):
    return pltpu.CompilerParams(dimension_semantics=("arbitrary",) * n_axes,
                                vmem_limit_bytes=VMEM_LIMIT)


def _tile(n, pref, mult):
    if n <= pref:
        return n
    t = (pref // mult) * mult
    while t >= mult:
        if n % t == 0:
            return t
        t -= mult
    return n


def _ln(x):
    mu = jnp.mean(x, axis=-1, keepdims=True)
    xc = x - mu
    return xc * lax.rsqrt(jnp.mean(xc * xc, axis=-1, keepdims=True) + LN_EPS)


N_KHEADS = DA_HEADS * 2 + GKV_HEADS
KNORM_ROWS = 16


def _proj_all_kernel(x_ref, xc_ref, sh_ref, sc_ref, shc_ref, scc_ref, cos_ref, sin_ref, wqk_ref, gqk_ref,
                     wva_ref, wvb_ref, wtok_ref,
                     q_ref, k_ref, kn_ref, va_ref, vb_ref, u_ref, gt_ref, kt_sc,
                     *, n_lat_tiles, rms_heads, n_hy):
    i = pl.program_id(1)
    is_latent = i < n_lat_tiles
    xin = jnp.where(is_latent, x_ref[0], xc_ref[0])
    shift = jnp.where(is_latent, sh_ref[0], shc_ref[0])
    scale = jnp.where(is_latent, sc_ref[0], scc_ref[0])
    h = (_ln(xin) * (1.0 + scale) + shift).astype(BF16)
    nt = (((1,), (1,)), ((), ()))

    acc = lax.dot_general(wqk_ref[...], h, nt, preferred_element_type=F32)
    c, s = cos_ref[...], sin_ref[...]
    n_k_rows = kt_sc.shape[0]
    for hd in range(acc.shape[0] // HEAD_DIM):
        lo = hd * HEAD_DIM
        xh = acc[lo:lo + HEAD_DIM]
        if hd in rms_heads:
            xh = xh * lax.rsqrt(jnp.mean(xh * xh, axis=0, keepdims=True) + LN_EPS)
        xh = xh * gqk_ref[lo:lo + HEAD_DIM]
        x1, x2 = xh[:HALF], xh[HALF:]
        r1, r2 = x1 * c - x2 * s, x1 * s + x2 * c
        if lo < n_k_rows:
            kt_sc[lo:lo + HALF] = r1
            kt_sc[lo + HALF:lo + HEAD_DIM] = r2
        else:
            q_ref[0, lo - n_k_rows:lo - n_k_rows + HALF] = r1.astype(q_ref.dtype)
            q_ref[0, lo - n_k_rows + HALF:lo - n_k_rows + HEAD_DIM] = r2.astype(q_ref.dtype)

    k_bf = kt_sc[...].astype(BF16)
    k_ref[0] = k_bf.astype(F32).T.astype(BF16)

    @pl.when(i == 0)
    def _():
        kn_ref[...] = jnp.zeros(kn_ref.shape, F32)

    k_sq = k_bf.astype(F32) ** 2
    for hd in range(n_k_rows // HEAD_DIM):
        n2 = jnp.sum(k_sq[hd * HEAD_DIM:(hd + 1) * HEAD_DIM], axis=0, keepdims=True)
        kn_ref[0, hd:hd + 1] = jnp.maximum(kn_ref[0, hd:hd + 1], jnp.max(n2, axis=1, keepdims=True))

    va_ref[0] = lax.dot_general(wva_ref[...], h, nt, preferred_element_type=F32).astype(va_ref.dtype)
    vb_ref[0] = lax.dot_general(wvb_ref[...], h, nt, preferred_element_type=F32).astype(vb_ref.dtype)

    tok = jnp.dot(h, wtok_ref[...], preferred_element_type=F32)
    u_ref[0] = tok[:, :n_hy]
    gt_ref[0] = jax.nn.sigmoid(tok[:, n_hy:]).astype(gt_ref.dtype)


def proj_all(x, xc, shift, scale, shift_c, scale_c, cos_t, sin_t, lw):
    B, N, D = x.shape
    Nc = xc.shape[1]
    ntot = N + Nc
    tt = _tile(Nc, 256, 128)
    assert N % tt == 0
    nl = N // tt
    n_k_rows = N_KHEADS * HEAD_DIM
    n_q_rows = lw["w_qk"].shape[0] - n_k_rows
    n_va, n_vb = lw["w_va"].shape[0], lw["w_vb"].shape[0]
    n_tok = lw["w_tok"].shape[1]
    n_hy = 3 * HY_WIDTH
    per_b = lambda b, i: (b, 0, 0)
    const = lambda b, i: (0, 0)
    resident = dict(pipeline_mode=pl.Buffered(1))
    feat = lambda rows: pl.BlockSpec((1, rows, tt), lambda b, i: (b, 0, i))
    tokm = lambda cols: pl.BlockSpec((1, tt, cols), lambda b, i: (b, i, 0))
    sds = jax.ShapeDtypeStruct
    return pl.pallas_call(
        functools.partial(_proj_all_kernel, n_lat_tiles=nl, rms_heads=lw["rms_heads"], n_hy=n_hy),
        out_shape=(sds((B, n_q_rows, ntot), BF16), sds((B, ntot, n_k_rows), BF16),
                   sds((B, KNORM_ROWS, 128), F32), sds((B, n_va, ntot), BF16), sds((B, n_vb, ntot), BF16),
                   sds((B, ntot, n_hy), F32), sds((B, ntot, n_tok - n_hy), BF16)),
        grid=(B, ntot // tt),
        in_specs=[pl.BlockSpec((1, tt, D), lambda b, i: (b, jnp.minimum(i, nl - 1), 0)),
                  pl.BlockSpec((1, tt, D), lambda b, i: (b, jnp.maximum(i - nl, 0), 0)),
                  pl.BlockSpec((1, 1, D), per_b), pl.BlockSpec((1, 1, D), per_b),
                  pl.BlockSpec((1, 1, D), per_b), pl.BlockSpec((1, 1, D), per_b),
                  pl.BlockSpec((HALF, tt), lambda b, i: (0, i)), pl.BlockSpec((HALF, tt), lambda b, i: (0, i)),
                  pl.BlockSpec(lw["w_qk"].shape, const, **resident), pl.BlockSpec((lw["w_qk"].shape[0], 1), const),
                  pl.BlockSpec(lw["w_va"].shape, const, **resident),
                  pl.BlockSpec(lw["w_vb"].shape, const, **resident),
                  pl.BlockSpec(lw["w_tok"].shape, const, **resident)],
        out_specs=(feat(n_q_rows), tokm(n_k_rows), pl.BlockSpec((1, KNORM_ROWS, 128), per_b),
                   feat(n_va), feat(n_vb), tokm(n_hy), tokm(n_tok - n_hy)),
        scratch_shapes=[pltpu.VMEM((n_k_rows, tt), F32)],
        compiler_params=_cp(2), name="proj_all",
    )(x, xc, shift, scale, shift_c, scale_c, cos_t, sin_t, lw["w_qk"], lw["g_qk"], lw["w_va"], lw["w_vb"],
      lw["w_tok"])


def _resid_ln_tail(x, y, gate, g, b, alpha):
    return _ln(alpha * x + gate * y) * g + b


def _mm_kernel(a_ref, w_ref, o_ref, *, act):
    acc = jnp.dot(a_ref[...], w_ref[...], preferred_element_type=F32)
    if act == "sigmoid":
        acc = jax.nn.sigmoid(acc)
    o_ref[...] = acc.astype(o_ref.dtype)


def matmul(a, w, out_dtype, act=None, tm_pref=1024, tn_pref=512):
    M, K = a.shape
    Nn = w.shape[1]
    tm = _tile(M, tm_pref, 16)
    tn = _tile(Nn, tn_pref, 128)
    return pl.pallas_call(
        functools.partial(_mm_kernel, act=act),
        out_shape=jax.ShapeDtypeStruct((M, Nn), out_dtype),
        grid=(M // tm, Nn // tn),
        in_specs=[pl.BlockSpec((tm, K), lambda i, j: (i, 0)),
                  pl.BlockSpec((K, tn), lambda i, j: (0, j))],
        out_specs=pl.BlockSpec((tm, tn), lambda i, j: (i, j)),
        compiler_params=_cp(2), name="matmul",
    )(a, w)


FLASH_MIN_DENOM = 2.0 ** -60
FLASH_UNROLL = 4


def _flash_kernel(lam_ref, kmax_ref, q_ref, k_ref, v_ref, g_ref, o_ref, q_sc, p_buf, sh_sc, acc_sc, l_sc,
                  *, mode, nj, tk, post_scale, n_kheads, khead0):
    b, h = pl.program_id(0), pl.program_id(1)
    qf = q_ref[0].astype(F32)
    tq = qf.shape[1]
    zero = jnp.zeros((HEAD_DIM, tq), F32)
    qa, qb = qf[:HEAD_DIM], qf[HEAD_DIM:]
    if mode == "da":
        q_sc[0] = jnp.concatenate([qa, zero], axis=0).astype(BF16)
        q_sc[1] = jnp.concatenate([zero, qb], axis=0).astype(BF16)
        k_heads = (khead0 + 2 * h, khead0 + 2 * h + 1)
    else:
        group = h // 2
        for mi, qh in enumerate((qa, qb)):
            q_sc[mi] = jnp.where(group == 0, jnp.concatenate([qh, zero], axis=0),
                                 jnp.concatenate([zero, qh], axis=0)).astype(BF16)
        k_heads = (khead0 + group, khead0 + group)
    for mi, qh in enumerate((qa, qb)):
        q_norm = jnp.sqrt(jnp.sum(qh * qh, axis=0, keepdims=True))
        sh_sc[mi] = q_norm * kmax_ref[b * n_kheads + k_heads[mi]]

    def chunk(j):
        start = j * tk
        return pl.ds(start if isinstance(start, int) else pl.multiple_of(start, tk), tk)

    def stage_exp(j, slot):
        kb = k_ref[0, chunk(j), :]
        for mi in range(2):
            s = jnp.dot(kb, q_sc[mi], preferred_element_type=F32)
            p = jnp.exp2(s - sh_sc[mi])
            p_buf[slot, mi] = p.astype(BF16)
            l_sc[mi] += jnp.sum(p.reshape(tk // 8, 8, tq), axis=0)

    def stage_values(j, slot):
        vb = v_ref[0, :, chunk(j)]
        for mi in range(2):
            acc_sc[mi] += jnp.dot(vb, p_buf[slot, mi], preferred_element_type=F32)

    def accumulate():
        acc_sc[...] = jnp.zeros(acc_sc.shape, F32)
        l_sc[...] = jnp.zeros(l_sc.shape, F32)
        stage_exp(0, 0)
        steady = list(range(1, nj))
        while len(steady) % FLASH_UNROLL:
            t = steady.pop(0)
            stage_exp(t, t % 2)
            stage_values(t - 1, (t - 1) % 2)
        if steady:
            t0 = steady[0]

            def body(i, carry):
                for d in range(FLASH_UNROLL):
                    stage_exp(t0 + FLASH_UNROLL * i + d, (t0 + d) % 2)
                    stage_values(t0 + FLASH_UNROLL * i + d - 1, (t0 + d - 1) % 2)
                return carry

            lax.fori_loop(0, len(steady) // FLASH_UNROLL, body, 0)
        stage_values(nj - 1, (nj - 1) % 2)

    denominators = lambda: [jnp.sum(l_sc[mi], axis=0, keepdims=True) for mi in range(2)]
    accumulate()
    denom_min = jnp.min(jnp.minimum(*denominators()))

    @pl.when(jnp.logical_not(denom_min >= FLASH_MIN_DENOM))
    def _():
        for mi in range(2):
            def max_body(j, m, mi=mi):
                s = jnp.dot(k_ref[0, chunk(j), :], q_sc[mi], preferred_element_type=F32)
                return jnp.maximum(m, jnp.max(s, axis=0, keepdims=True))

            sh_sc[mi] = lax.fori_loop(0, nj, max_body, jnp.full((1, tq), NEG_BIG, F32))
        accumulate()

    outs = [acc_sc[mi] / l for mi, l in enumerate(denominators())]
    if mode == "da":
        o = outs[0] - lam_ref[0] * outs[1]
        o = o * lax.rsqrt(jnp.mean(o * o, axis=0, keepdims=True) + LN_EPS)
        o = o * (g_ref[...] * post_scale)
    else:
        o = jnp.concatenate(outs, axis=0)
    o_ref[0] = o.T.astype(o_ref.dtype)


def flash_attention(q_t, k_tok, v_aug, kmax, lam, gain, *, mode, q_span, kv_span, q_row_block, k_col_block,
                    khead0, post_scale=1.0, tk_pref=768):
    B = q_t.shape[0]
    q0, Nq = q_span
    k0, Nk = kv_span
    tq = _tile(Nq, 1024, 128)
    tk = _tile(Nk, tk_pref, 128)
    assert q0 % tq == 0 and k0 % Nk == 0
    nj = Nk // tk
    dv = 2 * HEAD_DIM if mode == "da" else HEAD_DIM
    n_kheads = kmax.shape[0] // B
    if mode == "da":
        k_map = lambda b, h, i: (b, k0 // Nk, k_col_block + h)
        v_map = lambda b, h, i: (b, h, k0 // Nk)
    else:
        k_map = lambda b, h, i: (b, k0 // Nk, k_col_block)
        v_map = lambda b, h, i: (b, h // 2, k0 // Nk)
    return pl.pallas_call(
        functools.partial(_flash_kernel, mode=mode, nj=nj, tk=tk, post_scale=post_scale,
                          n_kheads=n_kheads, khead0=khead0),
        out_shape=jax.ShapeDtypeStruct((B, Nq, 4 * 128), BF16),
        grid=(B, 4, Nq // tq),
        in_specs=[pl.BlockSpec(memory_space=pltpu.SMEM),
                  pl.BlockSpec(memory_space=pltpu.SMEM),
                  pl.BlockSpec((1, 128, tq), lambda b, h, i: (b, h + q_row_block, i + q0 // tq)),
                  pl.BlockSpec((1, Nk, 128), k_map),
                  pl.BlockSpec((1, dv, Nk), v_map),
                  pl.BlockSpec((128, 1), lambda b, h, i: (0, 0))],
        out_specs=pl.BlockSpec((1, tq, 128), lambda b, h, i: (b, i, h)),
        scratch_shapes=[pltpu.VMEM((2, 128, tq), BF16),
                        pltpu.VMEM((2, 2, tk, tq), BF16),
                        pltpu.VMEM((2, 1, tq), F32),
                        pltpu.VMEM((2, dv, tq), F32),
                        pltpu.VMEM((2, 8, tq), F32)],
        compiler_params=_cp(3), name="flash_" + mode,
    )(lam, kmax, q_t, k_tok, v_aug, gain)


def _shift_rows(x, prev_row, next_row):
    T = x.shape[0]
    row = lax.broadcasted_iota(jnp.int32, x.shape, 0)
    xp = jnp.where(row == 0, prev_row, pltpu.roll(x, 1, axis=0))
    xn = jnp.where(row == T - 1, next_row, pltpu.roll(x, T - 1, axis=0))
    return xp, xn


def _conv3_block(x_ref, p_ref, n_ref, w_ref, b_ref, halo):
    i = pl.program_id(1)
    last = pl.num_programs(1) - 1
    x = x_ref[0].astype(F32)
    prev_row = jnp.where(i > 0, p_ref[0].astype(F32)[halo - 1:halo], 0.0)
    next_row = jnp.where(i < last, n_ref[0].astype(F32)[0:1], 0.0)
    xp, xn = _shift_rows(x, prev_row, next_row)
    return xp * w_ref[0:1] + x * w_ref[1:2] + xn * w_ref[2:3] + b_ref[...]


def _halo_specs(tr, tc, halo, row0, n_rows, col_of):
    per = tr // halo
    n_halo = n_rows // halo
    m0, h0 = row0 // tr, row0 // halo
    return [pl.BlockSpec((1, tr, tc), lambda b, i, j: (b, m0 + i, col_of(j))),
            pl.BlockSpec((1, halo, tc), lambda b, i, j: (b, h0 + jnp.maximum(i * per - 1, 0), col_of(j))),
            pl.BlockSpec((1, halo, tc),
                         lambda b, i, j: (b, h0 + jnp.minimum((i + 1) * per, n_halo - 1), col_of(j)))]


def _dwconv_kernel(x_ref, p_ref, n_ref, w_ref, b_ref, v_ref, x1_ref, x2_ref):
    y = _conv3_block(x_ref, p_ref, n_ref, w_ref, b_ref, 8)
    C = v_ref.shape[2]
    v_ref[0] = y[:, :C]
    x1_ref[0] = y[:, C:2 * C]
    x2_ref[0] = y[:, 2 * C:]


def hyena_short_conv(u, w, b, row0, L):
    B, _, C3 = u.shape
    C = C3 // 3
    tr = _tile(L, 512, 8)
    assert row0 % tr == 0
    zero = lambda j: 0
    out = jax.ShapeDtypeStruct((B, L, C), F32)
    ospec = pl.BlockSpec((1, tr, C), lambda b, i, j: (b, i, 0))
    return pl.pallas_call(
        _dwconv_kernel,
        out_shape=(out, out, out),
        grid=(B, L // tr, 1),
        in_specs=_halo_specs(tr, C3, 8, row0, L, zero) + [
            pl.BlockSpec((3, C3), lambda b, i, j: (0, 0)),
            pl.BlockSpec((1, C3), lambda b, i, j: (0, 0))],
        out_specs=(ospec, ospec, ospec),
        compiler_params=_cp(3), name="hyena_short_conv",
    )(u, u, u, w, b)


FFN_HALO = 16
FFN_CHUNK = 256


def _ffn_kernel(h_ref, hp_ref, hn_ref, wup_ref, cw_ref, cb_ref, wdn_ref, x_ref, gate_ref, g_ref, b_ref, o_ref,
                act_sc, *, alpha):
    i = pl.program_id(1)
    tm = h_ref.shape[1]
    dff = wdn_ref.shape[0]
    rows = tm + 2 * FFN_HALO
    h_prev = jnp.where(i == 0, jnp.zeros_like(hp_ref[0]), hp_ref[0])
    h_next = jnp.where(i == pl.num_programs(1) - 1, jnp.zeros_like(hn_ref[0]), hn_ref[0])
    h_ext = jnp.concatenate([h_prev, h_ref[0], h_next], axis=0)

    def conv_cols(c0):
        u = jnp.dot(h_ext, wup_ref[:, c0:c0 + FFN_CHUNK], preferred_element_type=F32)
        mid = slice(FFN_HALO, FFN_HALO + tm)
        u_prev = pltpu.roll(u, 1, axis=0)[mid]
        u_next = pltpu.roll(u, rows - 1, axis=0)[mid]
        w = cw_ref[:, c0:c0 + FFN_CHUNK]
        return u_prev * w[0:1] + u[mid] * w[1:2] + u_next * w[2:3] + cb_ref[:, c0:c0 + FFN_CHUNK]

    for c0 in range(0, dff, FFN_CHUNK):
        val = conv_cols(c0)
        gate = conv_cols(dff + c0)
        act_sc[:, c0:c0 + FFN_CHUNK] = (gate * jax.nn.sigmoid(gate) * val).astype(BF16)
    y = jnp.dot(act_sc[...], wdn_ref[...], preferred_element_type=F32)
    o_ref[0] = _resid_ln_tail(x_ref[0], y, gate_ref[0], g_ref[...], b_ref[...], alpha)


def conv_ffn_resid_ln(h, w_up, conv_w, conv_b, w_down, x, gate, g, b, alpha):
    B, N, D = h.shape
    dff = w_down.shape[0]
    tm = _tile(N, 512, FFN_HALO)
    per = tm // FFN_HALO
    n_halo = N // FFN_HALO
    row = lambda b_, i: (b_, i, 0)
    per_b = lambda b_, i: (b_, 0, 0)
    const2 = lambda b_, i: (0, 0)
    resident = dict(pipeline_mode=pl.Buffered(1))
    return pl.pallas_call(
        functools.partial(_ffn_kernel, alpha=alpha),
        out_shape=jax.ShapeDtypeStruct((B, N, D), F32),
        grid=(B, N // tm),
        in_specs=[pl.BlockSpec((1, tm, D), row),
                  pl.BlockSpec((1, FFN_HALO, D), lambda b_, i: (b_, jnp.maximum(i * per - 1, 0), 0)),
                  pl.BlockSpec((1, FFN_HALO, D), lambda b_, i: (b_, jnp.minimum((i + 1) * per, n_halo - 1), 0)),
                  pl.BlockSpec((D, 2 * dff), const2, **resident),
                  pl.BlockSpec((3, 2 * dff), const2), pl.BlockSpec((1, 2 * dff), const2),
                  pl.BlockSpec((dff, D), const2, **resident),
                  pl.BlockSpec((1, tm, D), row), pl.BlockSpec((1, 1, D), per_b),
                  pl.BlockSpec((1, D), const2), pl.BlockSpec((1, D), const2)],
        out_specs=pl.BlockSpec((1, tm, D), row),
        scratch_shapes=[pltpu.VMEM((tm, dff), BF16)],
        compiler_params=_cp(2), name="conv_ffn_resid_ln",
    )(h, h, h, w_up, conv_w, conv_b, w_down, x, gate, g, b)


def _merge_kernel(a_ref, b_ref, c_ref, gt_ref, wpa_ref, wpb_ref, wpc_ref, wo_ref, x_ref, gate_ref,
                  g_ref, bb_ref, sh_ref, sc_ref, ox_ref, oh_ref, *, alpha):
    D = wo_ref.shape[0]
    gt = gt_ref[0]
    m = gt[:, :D].astype(F32) * jnp.dot(a_ref[0], wpa_ref[...], preferred_element_type=F32)
    m = m + gt[:, D:2 * D].astype(F32) * jnp.dot(b_ref[0], wpb_ref[...], preferred_element_type=F32)
    m = m + gt[:, 2 * D:].astype(F32) * jnp.dot(c_ref[0].astype(BF16), wpc_ref[...],
                                                preferred_element_type=F32)
    y = jnp.dot(m.astype(BF16), wo_ref[...], preferred_element_type=F32)
    xn = _resid_ln_tail(x_ref[0], y, gate_ref[0], g_ref[...], bb_ref[...], alpha)
    ox_ref[0] = xn
    oh_ref[0] = (_ln(xn) * (1.0 + sc_ref[0]) + sh_ref[0]).astype(oh_ref.dtype)


def merge_resid_ln(a, b, c, gates, tok0, wpa, wpb, wpc, wo, x, gate, g, bb, sh, sc, alpha):
    B, N, D = x.shape
    W = a.shape[2]
    tm = _tile(N, 512, 16)
    assert tok0 % tm == 0
    row = lambda b_, i: (b_, i, 0)
    per_b = lambda b_, i: (b_, 0, 0)
    const2 = lambda b_, i: (0, 0)
    return pl.pallas_call(
        functools.partial(_merge_kernel, alpha=alpha),
        out_shape=(jax.ShapeDtypeStruct((B, N, D), F32), jax.ShapeDtypeStruct((B, N, D), BF16)),
        grid=(B, N // tm),
        in_specs=[pl.BlockSpec((1, tm, W), row), pl.BlockSpec((1, tm, W), row), pl.BlockSpec((1, tm, W), row),
                  pl.BlockSpec((1, tm, 3 * D), lambda b_, i: (b_, tok0 // tm + i, 0)),
                  pl.BlockSpec((W, D), const2), pl.BlockSpec((W, D), const2), pl.BlockSpec((W, D), const2),
                  pl.BlockSpec((D, D), const2),
                  pl.BlockSpec((1, tm, D), row), pl.BlockSpec((1, 1, D), per_b),
                  pl.BlockSpec((1, D), const2), pl.BlockSpec((1, D), const2),
                  pl.BlockSpec((1, 1, D), per_b), pl.BlockSpec((1, 1, D), per_b)],
        out_specs=(pl.BlockSpec((1, tm, D), row), pl.BlockSpec((1, tm, D), row)),
        compiler_params=_cp(2), name="merge_resid_ln",
    )(a, b, c, gates, wpa, wpb, wpc, wo, x, gate, g, bb, sh, sc)


def _split_bf16(x):
    hi = x.astype(BF16)
    lo = (x - hi.astype(F32)).astype(BF16)
    return hi, lo


def _dot3(a, b):
    ah, al = _split_bf16(a)
    bh, bl = _split_bf16(b)
    d = functools.partial(jnp.dot, preferred_element_type=F32)
    return d(ah, bh) + (d(ah, bl) + d(al, bh))


def _filter_kernel(z_ref, w1_ref, b1_ref, w2_ref, b2_ref, w3_ref, fr_ref, dec_ref, h_ref, s_ref, *, zero_block):
    i = pl.program_id(0)
    fr = fr_ref[...]
    hid = jnp.sin(fr * (_dot3(z_ref[...], w1_ref[...]) + b1_ref[...]))
    hid = jnp.sin(fr * (_dot3(hid, w2_ref[...]) + b2_ref[...]))
    h = _dot3(hid, w3_ref[0])
    dec = dec_ref[...]
    C = dec.shape[1]
    n_ord = h.shape[1] // C
    h = h * jnp.concatenate([dec] * n_ord, axis=1)

    @pl.when(i == 0)
    def _():
        s_ref[...] = jnp.zeros(s_ref.shape, F32)

    s_ref[...] += jnp.sum(jnp.abs(h), axis=0, keepdims=True)
    row = lax.broadcasted_iota(jnp.int32, h.shape, 0)
    h = jnp.where((row == 0) & (i == zero_block), 0.0, h)
    for o in range(n_ord):
        h_ref[o] = h[:, o * C:(o + 1) * C]


def hyena_filter_mlp(z2, w1, b1, w2, b2, w3_dir, freq, decay2):
    n2, E = z2.shape
    Hh = w2.shape[0]
    OC = w3_dir.shape[2]
    C = decay2.shape[1]
    n_ord = OC // C
    L = n2 // 2
    tr = _tile(L, 512, 8)
    nblk = n2 // tr
    c2 = lambda i: (0, 0)
    return pl.pallas_call(
        functools.partial(_filter_kernel, zero_block=L // tr),
        out_shape=(jax.ShapeDtypeStruct((n_ord, n2, C), F32), jax.ShapeDtypeStruct((1, OC), F32)),
        grid=(nblk,),
        in_specs=[pl.BlockSpec((tr, E), lambda i: (i, 0)), pl.BlockSpec((E, Hh), c2), pl.BlockSpec((1, Hh), c2),
                  pl.BlockSpec((Hh, Hh), c2), pl.BlockSpec((1, Hh), c2),
                  pl.BlockSpec((1, Hh, OC), lambda i: (i // (nblk // 2), 0, 0)),
                  pl.BlockSpec((1, Hh), c2), pl.BlockSpec((tr, C), lambda i: (i, 0))],
        out_specs=(pl.BlockSpec((n_ord, tr, C), lambda i: (0, i, 0)), pl.BlockSpec((1, OC), c2)),
        compiler_params=_cp(1), name="hyena_filter_mlp",
    )(z2, w1, b1, w2, b2, w3_dir, freq, decay2)


def _mm_split(m_hi, m_lo, x, passes):
    d = functools.partial(jnp.dot, preferred_element_type=F32)
    if passes == 1:
        return d(m_hi, x.astype(BF16))
    xh, xl = _split_bf16(x)
    return d(m_hi, xh) + (d(m_lo, xh) + d(m_hi, xl))


FFT_ROWS = 8
FFT_BLOCK = 16


def _level1_apply(m_refs, x, r_out, passes):
    r_in, _, C = x.shape
    x = x.astype(F32)
    m_hi = m_refs[0][...]
    m_lo = m_refs[1][...] if passes > 1 else None
    groups = []
    for g in range(FFT_BLOCK // FFT_ROWS):
        xg = x[:, g * FFT_ROWS:(g + 1) * FFT_ROWS].reshape(r_in * FFT_ROWS, C)
        groups.append(_mm_split(m_hi, m_lo, xg, passes).reshape(r_out, FFT_ROWS, C))
    return jnp.concatenate(groups, axis=1)


def _level1_kernel(*refs, passes):
    n_mat = 1 if passes == 1 else 2
    x_ref, o_ref = refs[n_mat:]
    o_ref[0] = _level1_apply(refs[:n_mat], x_ref[0], o_ref.shape[1], passes).astype(o_ref.dtype)


def _level1_gate_kernel(*refs, passes):
    n_mat = 1 if passes == 1 else 2
    x_ref, z_ref, xg_ref, bias_ref, o_ref = refs[n_mat:]
    y = _level1_apply(refs[:n_mat], x_ref[0], o_ref.shape[1], passes)
    o_ref[0] = (xg_ref[0] * (y + bias_ref[...] * z_ref[0])).astype(o_ref.dtype)


def fft_level1(m_hi, m_lo, x, passes, gate_args=None, out_dtype=F32):
    P, R_in, Nb, C = x.shape
    R_out = m_hi.shape[0] // FFT_ROWS
    mats = (m_hi,) if passes == 1 else (m_hi, m_lo)
    mspec = pl.BlockSpec(m_hi.shape, lambda p, j: (0, 0), pipeline_mode=pl.Buffered(1))
    xspec = pl.BlockSpec((1, R_in, FFT_BLOCK, C), lambda p, j: (p, 0, j, 0))
    ospec = pl.BlockSpec((1, R_out, FFT_BLOCK, C), lambda p, j: (p, 0, j, 0))
    if gate_args is None:
        kern = functools.partial(_level1_kernel, passes=passes)
        ins, specs = (*mats, x), [mspec] * len(mats) + [xspec]
    else:
        z, xg, bias = gate_args
        kern = functools.partial(_level1_gate_kernel, passes=passes)
        ins = (*mats, x, z, xg, bias)
        specs = [mspec] * len(mats) + [xspec, ospec, ospec, pl.BlockSpec((1, C), lambda p, j: (0, 0))]
    return pl.pallas_call(
        kern, out_shape=jax.ShapeDtypeStruct((P, R_out, Nb, C), out_dtype),
        grid=(P, Nb // FFT_BLOCK), in_specs=specs, out_specs=ospec,
        compiler_params=_cp(2), name="fft_level1",
    )(*ins)


FFT_MID_K1 = 2


def _fft_mid_kernel(*refs, passes):
    n_mat = 2 if passes == 1 else 4
    gh_ref, gih_ref = refs[0], refs[1]
    gl_ref, gil_ref = (refs[2], refs[3]) if passes > 1 else (None, None)
    h_ref, a_ref, o_ref = refs[n_mat:]
    n_pairs, _, kb, nb, C = a_ref.shape
    for kk in range(kb):
        hr, hi = h_ref[0, 0, kk], h_ref[0, 1, kk]
        for p in range(n_pairs):
            x = a_ref[p, :, kk].reshape(2 * nb, C)
            X = _mm_split(gh_ref[kk], None if gl_ref is None else gl_ref[kk], x, passes)
            xr, xi = X[:nb], X[nb:]
            Y = jnp.concatenate([xr * hr - xi * hi, xr * hi + xi * hr], axis=0)
            Bv = _mm_split(gih_ref[kk], None if gil_ref is None else gil_ref[kk], Y, passes)
            o_ref[p, :, kk] = Bv.reshape(2, nb, C).astype(o_ref.dtype)


def fft_mid(g_hi, g_lo, gi_hi, gi_lo, spectra, order, a5, passes):
    P, _, Na, Nb, C = a5.shape
    kb = FFT_MID_K1
    gspec = pl.BlockSpec((kb, 2 * Nb, 2 * Nb), lambda k: (k, 0, 0))
    mats = (g_hi, gi_hi) if passes == 1 else (g_hi, gi_hi, g_lo, gi_lo)
    blk = pl.BlockSpec((P, 2, kb, Nb, C), lambda k: (0, 0, k, 0, 0))
    return pl.pallas_call(
        functools.partial(_fft_mid_kernel, passes=passes),
        out_shape=jax.ShapeDtypeStruct(a5.shape, a5.dtype),
        grid=(Na // kb,),
        in_specs=[gspec] * len(mats) + [pl.BlockSpec((1, 2, kb, Nb, C), lambda k: (order, 0, k, 0, 0)), blk],
        out_specs=blk,
        compiler_params=_cp(1), name="fft_mid",
    )(*mats, spectra, a5)


def _fft_spec_kernel(*refs, passes):
    n_mat = 1 if passes == 1 else 2
    gh_ref = refs[0]
    gl_ref = refs[1] if passes > 1 else None
    inv_ref, a_ref, o_ref = refs[n_mat:]
    n_filt, _, kb, nb, C = a_ref.shape
    for kk in range(kb):
        for p in range(n_filt):
            x = a_ref[p, :, kk].reshape(2 * nb, C)
            X = _mm_split(gh_ref[kk], None if gl_ref is None else gl_ref[kk], x, passes) * inv_ref[p]
            o_ref[p, :, kk] = X.reshape(2, nb, C)


def fft_filter_spectrum(g_hi, g_lo, inv_norm, a5, passes):
    P, _, Na, Nb, C = a5.shape
    kb = FFT_MID_K1
    mats = (g_hi,) if passes == 1 else (g_hi, g_lo)
    gspec = pl.BlockSpec((kb, 2 * Nb, 2 * Nb), lambda k: (k, 0, 0))
    blk = pl.BlockSpec((P, 2, kb, Nb, C), lambda k: (0, 0, k, 0, 0))
    return pl.pallas_call(
        functools.partial(_fft_spec_kernel, passes=passes),
        out_shape=jax.ShapeDtypeStruct(a5.shape, F32),
        grid=(Na // kb,),
        in_specs=[gspec] * len(mats) + [pl.BlockSpec((P, 1, C), lambda k: (0, 0, 0)), blk],
        out_specs=blk,
        compiler_params=_cp(1), name="fft_filter_spectrum",
    )(*mats, inv_norm, a5)


def _fft_factors(n_fft):
    na = 1 << (int(math.log2(n_fft)) // 2)
    return na, n_fft // na


def _fft_tables(L):
    n_fft = 2 * L
    na, nb = _fft_factors(n_fft)
    nh = na // 2
    iota = lambda shape, axis: lax.broadcasted_iota(jnp.int32, shape, axis)

    def cis(num, den):
        ang = (2.0 * math.pi / den) * (num % den).astype(F32)
        return jnp.cos(ang), -jnp.sin(ang)

    def block(diag_sign, pr, pc, cr, ci):
        return jnp.where(pr == pc, cr, jnp.where(pr == 1, diag_sign * ci, -diag_sign * ci))

    def kron_matrix(rows, cols, entry):
        shape = (rows * FFT_ROWS, cols * FFT_ROWS)
        I, J = iota(shape, 0), iota(shape, 1)
        m = entry(I // FFT_ROWS, J // FFT_ROWS)
        return _split_bf16(jnp.where(I % FFT_ROWS == J % FFT_ROWS, m, 0.0))

    def fwd_entry(i, j):
        cr, ci = cis((i % na) * (j % nh), na)
        return block(1, i // na, j // nh, cr, ci)

    def flt_entry(i, j):
        cr, ci = cis((i % na) * j, na)
        return jnp.where(i // na == 0, cr, ci)

    def inv_entry(i, j):
        cr, ci = cis((i % nh) * (j % na), na)
        return block(-1, i // nh, j // na, cr, ci) / n_fft

    gshape = (na, 2 * nb, 2 * nb)
    k1, gi_, gj_ = iota(gshape, 0), iota(gshape, 1), iota(gshape, 2)
    cr, ci = cis((gi_ % nb) * (gj_ % nb) * na + k1 * (gj_ % nb), n_fft)
    g = _split_bf16(block(1, gi_ // nb, gj_ // nb, cr, ci))
    cr, ci = cis((gj_ % nb) * (gi_ % nb) * na + k1 * (gi_ % nb), n_fft)
    ginv = _split_bf16(block(-1, gi_ // nb, gj_ // nb, cr, ci))

    C = HY_WIDTH
    n = jnp.arange(n_fft, dtype=jnp.int32)
    pos = jnp.where(n < L, n, jnp.where(n == L, 0, n_fft - n)).astype(F32)[:, None]
    t = pos * (1.0 / (L - 1))
    f = jnp.linspace(1e-4, HY_BANDS - 1, HY_BANDS, dtype=F32)
    ang = (2.0 * math.pi / L) * pos * f[None, :]
    z2 = jnp.concatenate([t, jnp.cos(ang), -jnp.sin(ang), jnp.zeros((n_fft, HY_EMB_PAD - HY_EMB), F32)], axis=-1)
    deltas = jnp.abs(jnp.linspace(HY_MIN_DECAY, HY_MAX_DECAY, C, dtype=F32))
    decay2 = jnp.exp(-t * deltas)
    return dict(na=na, nb=nb, fwd=kron_matrix(2 * na, na, fwd_entry), flt=kron_matrix(2 * na, na, flt_entry),
                inv=kron_matrix(na, 2 * na, inv_entry), g=g, ginv=ginv, z2=z2, decay2=decay2)


def hyena_filter_spectra(w1, b1, w2, b2, w3, freq, tabs, passes):
    C = HY_WIDTH
    na, nb = tabs["na"], tabs["nb"]
    w1p = jnp.pad(w1, ((0, HY_EMB_PAD - HY_EMB), (0, 0)))
    w3_dir = jnp.transpose(w3.reshape(HY_HIDDEN, HY_ORDER, 2, C), (2, 0, 1, 3)).reshape(2, HY_HIDDEN, HY_ORDER * C)
    filt, s = hyena_filter_mlp(tabs["z2"], w1p, b1[None], w2, b2[None], w3_dir, freq[None], tabs["decay2"])
    inv_norm = (1.0 / s).reshape(HY_ORDER, 1, C)
    a = fft_level1(*tabs["flt"], filt.reshape(HY_ORDER, na, nb, C), passes,
                   out_dtype=BF16 if passes == 1 else F32)
    return fft_filter_spectrum(*tabs["g"], inv_norm, a.reshape(HY_ORDER, 2, na, nb, C), passes)


def hyena_long_conv_gate(z, xg, bias, spectra, order, tabs, passes):
    B, L, C = z.shape
    na, nb = tabs["na"], tabs["nb"]
    P = B // 2
    nat = lambda a: a.reshape(P, na, nb, C)
    mid_dtype = BF16 if passes == 1 else F32
    a = fft_level1(*tabs["fwd"], nat(z), passes, out_dtype=mid_dtype)
    bv = fft_mid(*tabs["g"], *tabs["ginv"], spectra, order, a.reshape(P, 2, na, nb, C), passes)
    y = fft_level1(*tabs["inv"], bv.reshape(P, 2 * na, nb, C), passes,
                   gate_args=(nat(z), nat(xg), bias[None]))
    return y.reshape(B, L, C)


def hyena_mix(u, row0, L, spectra, conv_w, conv_b, bias, tabs, passes):
    v, x1, x2 = hyena_short_conv(u, conv_w, conv_b[None], row0, L)
    zz = hyena_long_conv_gate(v, x1, bias[0], spectra, 0, tabs, passes)
    return hyena_long_conv_gate(zz, x2, bias[1], spectra, 1, tabs, passes)


FFT_PASSES = 1
FILTER_PASSES = 1


def _rope_tables_t(n):
    rows = n // GRID_W
    r = jnp.repeat(jnp.arange(rows, dtype=F32), GRID_W)
    col = jnp.tile(jnp.arange(GRID_W, dtype=F32), rows)
    axis_dim = HEAD_DIM // 2
    inv = ROPE_THETA ** (-jnp.arange(0, axis_dim, 2, dtype=F32) / axis_dim)
    ang = jnp.concatenate([r[:, None] * inv, col[:, None] * inv], axis=-1)
    return jnp.cos(ang).T, jnp.sin(ang).T


def _layer_weights(l, w_in, gq_qn, gq_kn):
    w = w_in[l]
    d_qk = DA_HEADS * 2 * HEAD_DIM
    gq_kv = GKV_HEADS * HEAD_DIM
    gq_q = GQ_HEADS * HEAD_DIM
    o = np.cumsum([0, d_qk, d_qk, gq_kv, gq_kv, d_qk, gq_q, 3 * HY_WIDTH])
    ka, va, kb, vb, qa, qb, hy = (w[:, o[i]:o[i + 1]] for i in range(7))
    gates = w[:, o[7]:]
    qscale = (HEAD_DIM ** -0.5) * LOG2E
    w_qk = jnp.concatenate([ka, kb, qa, qb], axis=1).T.astype(BF16)
    g_qk = jnp.concatenate([jnp.ones((d_qk,), F32), jnp.tile(gq_kn[l], GKV_HEADS),
                            jnp.full((d_qk,), qscale, F32), jnp.tile(gq_qn[l], GQ_HEADS) * qscale])[:, None]
    n_ka, n_kb, n_qa = d_qk // HEAD_DIM, gq_kv // HEAD_DIM, d_qk // HEAD_DIM
    rms_heads = frozenset(range(n_ka, n_ka + n_kb)) | frozenset(range(n_ka + n_kb + n_qa,
                                                                       n_ka + n_kb + n_qa + gq_q // HEAD_DIM))
    return dict(w_qk=w_qk, g_qk=g_qk, rms_heads=rms_heads, w_va=va.T.astype(BF16), w_vb=vb.T.astype(BF16),
                w_tok=jnp.concatenate([hy, gates], axis=1).astype(BF16))


def kernel(x, c, ctx, c_ctx, w_mod, b_mod, w_in, da_lq1, da_lk1, da_lq2, da_lk2, da_subln, gq_qn, gq_kn,
           hy_conv_w, hy_conv_b, hy_w1, hy_b1, hy_w2, hy_b2, hy_w3, hy_freq, hy_bias, w_pa, w_pb, w_pc, w_o,
           ln1_g, ln1_b, w_up, ffn_conv_w, ffn_conv_b, w_down, ln2_g, ln2_b):
    B, n_lat, D = x.shape
    n_ctx = ctx.shape[1]
    depth = w_in.shape[0]
    alpha = (2 * depth) ** 0.25
    cos_l, sin_l = _rope_tables_t(n_lat)
    cos_t = jnp.concatenate([cos_l, jnp.ones((HALF, n_ctx), F32)], axis=1)
    sin_t = jnp.concatenate([sin_l, jnp.zeros((HALF, n_ctx), F32)], axis=1)
    lat, ctx_span, all_span = (0, n_lat), (n_lat, n_ctx), (0, n_lat + n_ctx)
    tabs_l = _fft_tables(n_lat)
    tabs_c = _fft_tables(n_ctx)
    xc = ctx

    cond = jnp.concatenate([c, c_ctx[None]], axis=0)
    cond = jnp.pad(jax.nn.silu(cond), ((0, 16 - (B + 1) % 16), (0, 0))).astype(BF16)

    for l in range(depth):
        last = l == depth - 1
        lam_init = 0.8 - 0.6 * math.exp(-0.3 * l)
        lam = (jnp.exp(jnp.sum(da_lq1[l] * da_lk1[l])) - jnp.exp(jnp.sum(da_lq2[l] * da_lk2[l])) + lam_init)
        lam = lam.reshape(1).astype(F32)
        mod = matmul(cond, w_mod[l].astype(BF16), F32, tn_pref=1024) + b_mod[l]
        sh1, sc1, g1, sh2, sc2, g2 = [m[:, None, :] for m in jnp.split(mod[:B], 6, axis=-1)]
        mc = [jnp.broadcast_to(m[None, None, :], (B, 1, D)) for m in jnp.split(mod[B], 6)]
        lw = _layer_weights(l, w_in, gq_qn, gq_kn)
        subln = da_subln[l][:, None]
        hy_mlp = (hy_w1[l], hy_b1[l], hy_w2[l], hy_b2[l], hy_w3[l], hy_freq[l])
        spectra_l = hyena_filter_spectra(*hy_mlp, tabs_l, FILTER_PASSES)
        wpa, wpb, wpc, wo = (w[l].astype(BF16) for w in (w_pa, w_pb, w_pc, w_o))
        wup, wdn = w_up[l].astype(BF16), w_down[l].astype(BF16)
        ln1 = (ln1_g[l][None], ln1_b[l][None])
        ln2 = (ln2_g[l][None], ln2_b[l][None])

        q_all, k_tok, k_n2, va, vb, u, gates = proj_all(x, xc, sh1, sc1, mc[0], mc[1], cos_t, sin_t, lw)
        kmax = jnp.sqrt(k_n2[:, :N_KHEADS, 0]).reshape(B * N_KHEADS)
        n_da_maps = DA_HEADS * 2

        def attend(q_span, kv_span):
            da = flash_attention(q_all, k_tok, va, kmax, lam, subln, mode="da", q_span=q_span, kv_span=kv_span,
                                 q_row_block=0, k_col_block=0, khead0=0, post_scale=1.0 - lam_init)
            gq = flash_attention(q_all, k_tok, vb, kmax, lam, subln, mode="gqa", q_span=q_span, kv_span=kv_span,
                                 q_row_block=n_da_maps * HEAD_DIM // 128, k_col_block=n_da_maps * HEAD_DIM // 128,
                                 khead0=n_da_maps)
            return da, gq

        hy_args = (hy_conv_w[l], hy_conv_b[l], hy_bias[l])
        a_l, b_l = attend(lat, all_span)
        c_l = hyena_mix(u, *lat, spectra_l, *hy_args, tabs_l, FFT_PASSES)
        x_new, h2 = merge_resid_ln(a_l, b_l, c_l, gates, lat[0], wpa, wpb, wpc, wo, x, g1, *ln1,
                                   sh2, sc2, alpha)
        x_new = conv_ffn_resid_ln(h2, wup, ffn_conv_w[l], ffn_conv_b[l][None], wdn, x_new, g2, *ln2, alpha)

        if not last:
            spectra_c = hyena_filter_spectra(*hy_mlp, tabs_c, FILTER_PASSES)
            a_c, b_c = attend(ctx_span, ctx_span)
            c_c = hyena_mix(u, *ctx_span, spectra_c, *hy_args, tabs_c, FFT_PASSES)
            xc, hc2 = merge_resid_ln(a_c, b_c, c_c, gates, ctx_span[0], wpa, wpb, wpc, wo, xc, mc[2],
                                     *ln1, mc[3], mc[4], alpha)
            xc = conv_ffn_resid_ln(hc2, wup, ffn_conv_w[l], ffn_conv_b[l][None], wdn, xc, mc[5], *ln2, alpha)
        x = x_new
    return x
```

```python
import functools
import math

import numpy as np
import jax
import jax.numpy as jnp
from jax import lax
from jax.experimental import pallas as pl
from jax.experimental.pallas import tpu as pltpu

F32 = jnp.float32
BF16 = jnp.bfloat16

HEAD_DIM = 64
HALF = HEAD_DIM // 2
GRID_W = 64
ROPE_THETA = 10000.0
DA_HEADS = 4
GQ_HEADS = 8
GKV_HEADS = 2
HY_WIDTH = 512
HY_ORDER = 2
HY_EMB = 33
HY_EMB_PAD = 128
HY_BANDS = (HY_EMB - 1) // 2
HY_HIDDEN = 64
HY_MIN_DECAY = math.log(1e-2) / 1.5
HY_MAX_DECAY = math.log(1e-2) / 0.3
LN_EPS = 1e-6
LOG2E = 1.4426950408889634
NEG_BIG = -1e30

VMEM_LIMIT = 56 * 1024 * 1024


def _cp(n_axes):
    return pltpu.CompilerParams(dimension_semantics=("arbitrary",) * n_axes,
                                vmem_limit_bytes=VMEM_LIMIT)


def _tile(n, pref, mult):
    if n <= pref:
        return n
    t = (pref // mult) * mult
    while t >= mult:
        if n % t == 0:
            return t
        t -= mult
    return n


def _ln(x):
    mu = jnp.mean(x, axis=-1, keepdims=True)
    xc = x - mu
    return xc * lax.rsqrt(jnp.mean(xc * xc, axis=-1, keepdims=True) + LN_EPS)


N_KHEADS = DA_HEADS * 2 + GKV_HEADS
KNORM_ROWS = 16


def _proj_all_kernel(x_ref, xc_ref, sh_ref, sc_ref, shc_ref, scc_ref, cos_ref, sin_ref, wqk_ref, gqk_ref,
                     wva_ref, wvb_ref, wtok_ref,
                     q_ref, k_ref, kn_ref, va_ref, vb_ref, u_ref, gt_ref, kt_sc,
                     *, n_lat_tiles, rms_heads, n_hy):
    i = pl.program_id(1)

    @pl.when(i == 0)
    def _():
        kn_ref[...] = jnp.zeros(kn_ref.shape, F32)

    is_latent = i < n_lat_tiles
    xin = jnp.where(is_latent, x_ref[0], xc_ref[0])
    shift = jnp.where(is_latent, sh_ref[0], shc_ref[0])
    scale = jnp.where(is_latent, sc_ref[0], scc_ref[0])
    h = (_ln(xin) * (1.0 + scale) + shift).astype(BF16)
    nt = (((1,), (1,)), ((), ()))

    acc = lax.dot_general(wqk_ref[...], h, nt, preferred_element_type=F32)
    va = lax.dot_general(wva_ref[...], h, nt, preferred_element_type=F32)
    vb = lax.dot_general(wvb_ref[...], h, nt, preferred_element_type=F32)
    tok = jnp.dot(h, wtok_ref[...], preferred_element_type=F32)
    va_ref[0] = va.astype(va_ref.dtype)
    vb_ref[0] = vb.astype(vb_ref.dtype)
    c, s = cos_ref[...], sin_ref[...]
    n_k_rows = kt_sc.shape[0]
    for hd in range(acc.shape[0] // HEAD_DIM):
        lo = hd * HEAD_DIM
        xh = acc[lo:lo + HEAD_DIM]
        if hd in rms_heads:
            xh = xh * lax.rsqrt(jnp.mean(xh * xh, axis=0, keepdims=True) + LN_EPS)
        xh = xh * gqk_ref[lo:lo + HEAD_DIM]
        x1, x2 = xh[:HALF], xh[HALF:]
        r1, r2 = x1 * c - x2 * s, x1 * s + x2 * c
        if lo < n_k_rows:
            kt_sc[lo:lo + HALF] = r1
            kt_sc[lo + HALF:lo + HEAD_DIM] = r2
        else:
            q_ref[0, lo - n_k_rows:lo - n_k_rows + HALF] = r1.astype(q_ref.dtype)
            q_ref[0, lo - n_k_rows + HALF:lo - n_k_rows + HEAD_DIM] = r2.astype(q_ref.dtype)

    k_bf = kt_sc[...].astype(BF16)
    k_ref[0] = k_bf.astype(F32).T.astype(BF16)
    k_sq = k_bf.astype(F32) ** 2
    for hd in range(n_k_rows // HEAD_DIM):
        n2 = jnp.sum(k_sq[hd * HEAD_DIM:(hd + 1) * HEAD_DIM], axis=0, keepdims=True)
        kn_ref[0, hd:hd + 1] = jnp.maximum(kn_ref[0, hd:hd + 1], jnp.max(n2, axis=1, keepdims=True))

    u_ref[0] = tok[:, :n_hy]
    gt_ref[0] = jax.nn.sigmoid(tok[:, n_hy:]).astype(gt_ref.dtype)


def proj_all(x, xc, shift, scale, shift_c, scale_c, cos_t, sin_t, lw):
    B, N, D = x.shape
    Nc = xc.shape[1]
    ntot = N + Nc
    tt = _tile(Nc, 256, 128)
    assert N % tt == 0
    nl = N // tt
    n_k_rows = N_KHEADS * HEAD_DIM
    n_q_rows = lw["w_qk"].shape[0] - n_k_rows
    n_va, n_vb = lw["w_va"].shape[0], lw["w_vb"].shape[0]
    n_tok = lw["w_tok"].shape[1]
    n_hy = 3 * HY_WIDTH
    per_b = lambda b, i: (b, 0, 0)
    const = lambda b, i: (0, 0)
    resident = dict(pipeline_mode=pl.Buffered(1))
    feat = lambda rows: pl.BlockSpec((1, rows, tt), lambda b, i: (b, 0, i))
    tokm = lambda cols: pl.BlockSpec((1, tt, cols), lambda b, i: (b, i, 0))
    sds = jax.ShapeDtypeStruct
    return pl.pallas_call(
        functools.partial(_proj_all_kernel, n_lat_tiles=nl, rms_heads=lw["rms_heads"], n_hy=n_hy),
        out_shape=(sds((B, n_q_rows, ntot), BF16), sds((B, ntot, n_k_rows), BF16),
                   sds((B, KNORM_ROWS, 128), F32), sds((B, n_va, ntot), BF16), sds((B, n_vb, ntot), BF16),
                   sds((B, ntot, n_hy), F32), sds((B, ntot, n_tok - n_hy), BF16)),
        grid=(B, ntot // tt),
        in_specs=[pl.BlockSpec((1, tt, D), lambda b, i: (b, jnp.minimum(i, nl - 1), 0)),
                  pl.BlockSpec((1, tt, D), lambda b, i: (b, jnp.maximum(i - nl, 0), 0)),
                  pl.BlockSpec((1, 1, D), per_b), pl.BlockSpec((1, 1, D), per_b),
                  pl.BlockSpec((1, 1, D), per_b), pl.BlockSpec((1, 1, D), per_b),
                  pl.BlockSpec((HALF, tt), lambda b, i: (0, i)), pl.BlockSpec((HALF, tt), lambda b, i: (0, i)),
                  pl.BlockSpec(lw["w_qk"].shape, const, **resident), pl.BlockSpec((lw["w_qk"].shape[0], 1), const),
                  pl.BlockSpec(lw["w_va"].shape, const, **resident),
                  pl.BlockSpec(lw["w_vb"].shape, const, **resident),
                  pl.BlockSpec(lw["w_tok"].shape, const, **resident)],
        out_specs=(feat(n_q_rows), tokm(n_k_rows), pl.BlockSpec((1, KNORM_ROWS, 128), per_b),
                   feat(n_va), feat(n_vb), tokm(n_hy), tokm(n_tok - n_hy)),
        scratch_shapes=[pltpu.VMEM((n_k_rows, tt), F32)],
        compiler_params=_cp(2), name="proj_all",
    )(x, xc, shift, scale, shift_c, scale_c, cos_t, sin_t, lw["w_qk"], lw["g_qk"], lw["w_va"], lw["w_vb"],
      lw["w_tok"])


def _resid_ln_tail(x, y, gate, g, b, alpha):
    return _ln(alpha * x + gate * y) * g + b


def _mm_kernel(a_ref, w_ref, o_ref, *, act):
    acc = jnp.dot(a_ref[...], w_ref[...], preferred_element_type=F32)
    if act == "sigmoid":
        acc = jax.nn.sigmoid(acc)
    o_ref[...] = acc.astype(o_ref.dtype)


def matmul(a, w, out_dtype, act=None, tm_pref=1024, tn_pref=512):
    M, K = a.shape
    Nn = w.shape[1]
    tm = _tile(M, tm_pref, 16)
    tn = _tile(Nn, tn_pref, 128)
    return pl.pallas_call(
        functools.partial(_mm_kernel, act=act),
        out_shape=jax.ShapeDtypeStruct((M, Nn), out_dtype),
        grid=(M // tm, Nn // tn),
        in_specs=[pl.BlockSpec((tm, K), lambda i, j: (i, 0)),
                  pl.BlockSpec((K, tn), lambda i, j: (0, j))],
        out_specs=pl.BlockSpec((tm, tn), lambda i, j: (i, j)),
        compiler_params=_cp(2), name="matmul",
    )(a, w)


FLASH_MIN_DENOM = 2.0 ** -60
FLASH_UNROLL = 4


def _flash_kernel(lam_ref, kmax_ref, q_ref, k_ref, v_ref, g_ref, o_ref, q_sc, p_buf, sh_sc, acc_sc, l_sc,
                  *, mode, nj, tk, post_scale, n_kheads, khead0):
    b, h = pl.program_id(0), pl.program_id(1)
    qf = q_ref[0].astype(F32)
    tq = qf.shape[1]
    zero = jnp.zeros((HEAD_DIM, tq), F32)
    qa, qb = qf[:HEAD_DIM], qf[HEAD_DIM:]
    if mode == "da":
        q_sc[0] = jnp.concatenate([qa, zero], axis=0).astype(BF16)
        q_sc[1] = jnp.concatenate([zero, qb], axis=0).astype(BF16)
        k_heads = (khead0 + 2 * h, khead0 + 2 * h + 1)
    else:
        group = h // 2
        for mi, qh in enumerate((qa, qb)):
            q_sc[mi] = jnp.where(group == 0, jnp.concatenate([qh, zero], axis=0),
                                 jnp.concatenate([zero, qh], axis=0)).astype(BF16)
        k_heads = (khead0 + group, khead0 + group)
    for mi, qh in enumerate((qa, qb)):
        q_norm = jnp.sqrt(jnp.sum(qh * qh, axis=0, keepdims=True))
        sh_sc[mi] = q_norm * kmax_ref[b * n_kheads + k_heads[mi]]

    def chunk(j):
        start = j * tk
        return pl.ds(start if isinstance(start, int) else pl.multiple_of(start, tk), tk)

    def stage_exp(j, slot):
        kb = k_ref[0, chunk(j), :]
        for mi in range(2):
            s = jnp.dot(kb, q_sc[mi], preferred_element_type=F32)
            p = jnp.exp2(s - sh_sc[mi])
            p_buf[slot, mi] = p.astype(BF16)
            l_sc[mi] += jnp.sum(p.reshape(tk // 8, 8, tq), axis=0)

    def stage_values(j, slot):
        vb = v_ref[0, :, chunk(j)]
        for mi in range(2):
            acc_sc[mi] += jnp.dot(vb, p_buf[slot, mi], preferred_element_type=F32)

    def accumulate():
        acc_sc[...] = jnp.zeros(acc_sc.shape, F32)
        l_sc[...] = jnp.zeros(l_sc.shape, F32)
        stage_exp(0, 0)
        steady = list(range(1, nj))
        while len(steady) % FLASH_UNROLL:
            t = steady.pop(0)
            stage_exp(t, t % 2)
            stage_values(t - 1, (t - 1) % 2)
        if steady:
            t0 = steady[0]

            def body(i, carry):
                for d in range(FLASH_UNROLL):
                    stage_exp(t0 + FLASH_UNROLL * i + d, (t0 + d) % 2)
                    stage_values(t0 + FLASH_UNROLL * i + d - 1, (t0 + d - 1) % 2)
                return carry

            lax.fori_loop(0, len(steady) // FLASH_UNROLL, body, 0)
        stage_values(nj - 1, (nj - 1) % 2)

    denominators = lambda: [jnp.sum(l_sc[mi], axis=0, keepdims=True) for mi in range(2)]
    accumulate()
    denom_min = jnp.min(jnp.minimum(*denominators()))

    @pl.when(jnp.logical_not(denom_min >= FLASH_MIN_DENOM))
    def _():
        for mi in range(2):
            def max_body(j, m, mi=mi):
                s = jnp.dot(k_ref[0, chunk(j), :], q_sc[mi], preferred_element_type=F32)
                return jnp.maximum(m, jnp.max(s, axis=0, keepdims=True))

            sh_sc[mi] = lax.fori_loop(0, nj, max_body, jnp.full((1, tq), NEG_BIG, F32))
        accumulate()

    outs = [acc_sc[mi] / l for mi, l in enumerate(denominators())]
    if mode == "da":
        o = outs[0] - lam_ref[0] * outs[1]
        o = o * lax.rsqrt(jnp.mean(o * o, axis=0, keepdims=True) + LN_EPS)
        o = o * (g_ref[...] * post_scale)
    else:
        o = jnp.concatenate(outs, axis=0)
    o_ref[0] = o.T.astype(o_ref.dtype)


def flash_attention(q_t, k_tok, v_aug, kmax, lam, gain, *, mode, q_span, kv_span, q_row_block, k_col_block,
                    khead0, post_scale=1.0, tk_pref=768):
    B = q_t.shape[0]
    q0, Nq = q_span
    k0, Nk = kv_span
    tq = _tile(Nq, 1024, 128)
    tk = _tile(Nk, tk_pref, 128)
    assert q0 % tq == 0 and k0 % Nk == 0
    nj = Nk // tk
    dv = 2 * HEAD_DIM if mode == "da" else HEAD_DIM
    n_kheads = kmax.shape[0] // B
    if mode == "da":
        k_map = lambda b, h, i: (b, k0 // Nk, k_col_block + h)
        v_map = lambda b, h, i: (b, h, k0 // Nk)
    else:
        k_map = lambda b, h, i: (b, k0 // Nk, k_col_block)
        v_map = lambda b, h, i: (b, h // 2, k0 // Nk)
    return pl.pallas_call(
        functools.partial(_flash_kernel, mode=mode, nj=nj, tk=tk, post_scale=post_scale,
                          n_kheads=n_kheads, khead0=khead0),
        out_shape=jax.ShapeDtypeStruct((B, Nq, 4 * 128), BF16),
        grid=(B, 4, Nq // tq),
        in_specs=[pl.BlockSpec(memory_space=pltpu.SMEM),
                  pl.BlockSpec(memory_space=pltpu.SMEM),
                  pl.BlockSpec((1, 128, tq), lambda b, h, i: (b, h + q_row_block, i + q0 // tq)),
                  pl.BlockSpec((1, Nk, 128), k_map),
                  pl.BlockSpec((1, dv, Nk), v_map),
                  pl.BlockSpec((128, 1), lambda b, h, i: (0, 0))],
        out_specs=pl.BlockSpec((1, tq, 128), lambda b, h, i: (b, i, h)),
        scratch_shapes=[pltpu.VMEM((2, 128, tq), BF16),
                        pltpu.VMEM((2, 2, tk, tq), BF16),
                        pltpu.VMEM((2, 1, tq), F32),
                        pltpu.VMEM((2, dv, tq), F32),
                        pltpu.VMEM((2, 8, tq), F32)],
        compiler_params=_cp(3), name="flash_" + mode,
    )(lam, kmax, q_t, k_tok, v_aug, gain)


def _shift_rows(x, prev_row, next_row):
    T = x.shape[0]
    row = lax.broadcasted_iota(jnp.int32, x.shape, 0)
    xp = jnp.where(row == 0, prev_row, pltpu.roll(x, 1, axis=0))
    xn = jnp.where(row == T - 1, next_row, pltpu.roll(x, T - 1, axis=0))
    return xp, xn


def _conv3_block(x_ref, p_ref, n_ref, w_ref, b_ref, halo):
    i = pl.program_id(1)
    last = pl.num_programs(1) - 1
    x = x_ref[0].astype(F32)
    prev_row = jnp.where(i > 0, p_ref[0].astype(F32)[halo - 1:halo], 0.0)
    next_row = jnp.where(i < last, n_ref[0].astype(F32)[0:1], 0.0)
    xp, xn = _shift_rows(x, prev_row, next_row)
    return xp * w_ref[0:1] + x * w_ref[1:2] + xn * w_ref[2:3] + b_ref[...]


def _halo_specs(tr, tc, halo, row0, n_rows, col_of):
    per = tr // halo
    n_halo = n_rows // halo
    m0, h0 = row0 // tr, row0 // halo
    return [pl.BlockSpec((1, tr, tc), lambda b, i, j: (b, m0 + i, col_of(j))),
            pl.BlockSpec((1, halo, tc), lambda b, i, j: (b, h0 + jnp.maximum(i * per - 1, 0), col_of(j))),
            pl.BlockSpec((1, halo, tc),
                         lambda b, i, j: (b, h0 + jnp.minimum((i + 1) * per, n_halo - 1), col_of(j)))]


def _dwconv_kernel(x_ref, p_ref, n_ref, w_ref, b_ref, v_ref, x1_ref, x2_ref):
    y = _conv3_block(x_ref, p_ref, n_ref, w_ref, b_ref, 8)
    C = v_ref.shape[2]
    v_ref[0] = y[:, :C]
    x1_ref[0] = y[:, C:2 * C]
    x2_ref[0] = y[:, 2 * C:]


def hyena_short_conv(u, w, b, row0, L):
    B, _, C3 = u.shape
    C = C3 // 3
    tr = _tile(L, 512, 8)
    assert row0 % tr == 0
    zero = lambda j: 0
    out = jax.ShapeDtypeStruct((B, L, C), F32)
    ospec = pl.BlockSpec((1, tr, C), lambda b, i, j: (b, i, 0))
    return pl.pallas_call(
        _dwconv_kernel,
        out_shape=(out, out, out),
        grid=(B, L // tr, 1),
        in_specs=_halo_specs(tr, C3, 8, row0, L, zero) + [
            pl.BlockSpec((3, C3), lambda b, i, j: (0, 0)),
            pl.BlockSpec((1, C3), lambda b, i, j: (0, 0))],
        out_specs=(ospec, ospec, ospec),
        compiler_params=_cp(3), name="hyena_short_conv",
    )(u, u, u, w, b)


FFN_HALO = 16
FFN_CHUNK = 256


def _ffn_kernel(h_ref, hp_ref, hn_ref, wup_ref, cw_ref, cb_ref, wdn_ref, x_ref, gate_ref, g_ref, b_ref, o_ref,
                act_sc, *, alpha):
    i = pl.program_id(1)
    tm = h_ref.shape[1]
    dff = wdn_ref.shape[0]
    rows = tm + 2 * FFN_HALO
    h_prev = jnp.where(i == 0, jnp.zeros_like(hp_ref[0]), hp_ref[0])
    h_next = jnp.where(i == pl.num_programs(1) - 1, jnp.zeros_like(hn_ref[0]), hn_ref[0])
    h_ext = jnp.concatenate([h_prev, h_ref[0], h_next], axis=0)

    def conv_cols(c0):
        u = jnp.dot(h_ext, wup_ref[:, c0:c0 + FFN_CHUNK], preferred_element_type=F32)
        mid = slice(FFN_HALO, FFN_HALO + tm)
        u_prev = pltpu.roll(u, 1, axis=0)[mid]
        u_next = pltpu.roll(u, rows - 1, axis=0)[mid]
        w = cw_ref[:, c0:c0 + FFN_CHUNK]
        return u_prev * w[0:1] + u[mid] * w[1:2] + u_next * w[2:3] + cb_ref[:, c0:c0 + FFN_CHUNK]

    for c0 in range(0, dff, FFN_CHUNK):
        val = conv_cols(c0)
        gate = conv_cols(dff + c0)
        act_sc[:, c0:c0 + FFN_CHUNK] = (gate * jax.nn.sigmoid(gate) * val).astype(BF16)
    y = jnp.dot(act_sc[...], wdn_ref[...], preferred_element_type=F32)
    o_ref[0] = _resid_ln_tail(x_ref[0], y, gate_ref[0], g_ref[...], b_ref[...], alpha)


def conv_ffn_resid_ln(h, w_up, conv_w, conv_b, w_down, x, gate, g, b, alpha):
    B, N, D = h.shape
    dff = w_down.shape[0]
    tm = _tile(N, 512, FFN_HALO)
    per = tm // FFN_HALO
    n_halo = N // FFN_HALO
    row = lambda b_, i: (b_, i, 0)
    per_b = lambda b_, i: (b_, 0, 0)
    const2 = lambda b_, i: (0, 0)
    resident = dict(pipeline_mode=pl.Buffered(1))
    return pl.pallas_call(
        functools.partial(_ffn_kernel, alpha=alpha),
        out_shape=jax.ShapeDtypeStruct((B, N, D), F32),
        grid=(B, N // tm),
        in_specs=[pl.BlockSpec((1, tm, D), row),
                  pl.BlockSpec((1, FFN_HALO, D), lambda b_, i: (b_, jnp.maximum(i * per - 1, 0), 0)),
                  pl.BlockSpec((1, FFN_HALO, D), lambda b_, i: (b_, jnp.minimum((i + 1) * per, n_halo - 1), 0)),
                  pl.BlockSpec((D, 2 * dff), const2, **resident),
                  pl.BlockSpec((3, 2 * dff), const2), pl.BlockSpec((1, 2 * dff), const2),
                  pl.BlockSpec((dff, D), const2, **resident),
                  pl.BlockSpec((1, tm, D), row), pl.BlockSpec((1, 1, D), per_b),
                  pl.BlockSpec((1, D), const2), pl.BlockSpec((1, D), const2)],
        out_specs=pl.BlockSpec((1, tm, D), row),
        scratch_shapes=[pltpu.VMEM((tm, dff), BF16)],
        compiler_params=_cp(2), name="conv_ffn_resid_ln",
    )(h, h, h, w_up, conv_w, conv_b, w_down, x, gate, g, b)


def _merge_kernel(a_ref, b_ref, c_ref, gt_ref, wpa_ref, wpb_ref, wpc_ref, wo_ref, x_ref, gate_ref,
                  g_ref, bb_ref, sh_ref, sc_ref, ox_ref, oh_ref, *, alpha):
    D = wo_ref.shape[0]
    tm = x_ref.shape[1]
    n_sub = 2 if tm % 32 == 0 else 1
    for r in range(n_sub):
        rows = slice(r * tm // n_sub, (r + 1) * tm // n_sub)
        gt = gt_ref[0, rows]
        m = gt[:, :D].astype(F32) * jnp.dot(a_ref[0, rows], wpa_ref[...], preferred_element_type=F32)
        m = m + gt[:, D:2 * D].astype(F32) * jnp.dot(b_ref[0, rows], wpb_ref[...], preferred_element_type=F32)
        m = m + gt[:, 2 * D:].astype(F32) * jnp.dot(c_ref[0, rows].astype(BF16), wpc_ref[...],
                                                    preferred_element_type=F32)
        y = jnp.dot(m.astype(BF16), wo_ref[...], preferred_element_type=F32)
        xn = _resid_ln_tail(x_ref[0, rows], y, gate_ref[0], g_ref[...], bb_ref[...], alpha)
        ox_ref[0, rows] = xn
        oh_ref[0, rows] = (_ln(xn) * (1.0 + sc_ref[0]) + sh_ref[0]).astype(oh_ref.dtype)


def merge_resid_ln(a, b, c, gates, tok0, wpa, wpb, wpc, wo, x, gate, g, bb, sh, sc, alpha):
    B, N, D = x.shape
    W = a.shape[2]
    tm = _tile(N, 512, 16)
    assert tok0 % tm == 0
    row = lambda b_, i: (b_, i, 0)
    per_b = lambda b_, i: (b_, 0, 0)
    const2 = lambda b_, i: (0, 0)
    return pl.pallas_call(
        functools.partial(_merge_kernel, alpha=alpha),
        out_shape=(jax.ShapeDtypeStruct((B, N, D), F32), jax.ShapeDtypeStruct((B, N, D), BF16)),
        grid=(B, N // tm),
        in_specs=[pl.BlockSpec((1, tm, W), row), pl.BlockSpec((1, tm, W), row), pl.BlockSpec((1, tm, W), row),
                  pl.BlockSpec((1, tm, 3 * D), lambda b_, i: (b_, tok0 // tm + i, 0)),
                  pl.BlockSpec((W, D), const2), pl.BlockSpec((W, D), const2), pl.BlockSpec((W, D), const2),
                  pl.BlockSpec((D, D), const2),
                  pl.BlockSpec((1, tm, D), row), pl.BlockSpec((1, 1, D), per_b),
                  pl.BlockSpec((1, D), const2), pl.BlockSpec((1, D), const2),
                  pl.BlockSpec((1, 1, D), per_b), pl.BlockSpec((1, 1, D), per_b)],
        out_specs=(pl.BlockSpec((1, tm, D), row), pl.BlockSpec((1, tm, D), row)),
        compiler_params=_cp(2), name="merge_resid_ln",
    )(a, b, c, gates, wpa, wpb, wpc, wo, x, gate, g, bb, sh, sc)


def _split_bf16(x):
    hi = x.astype(BF16)
    lo = (x - hi.astype(F32)).astype(BF16)
    return hi, lo


def _dot3(a, b):
    ah, al = _split_bf16(a)
    bh, bl = _split_bf16(b)
    d = functools.partial(jnp.dot, preferred_element_type=F32)
    return d(ah, bh) + (d(ah, bl) + d(al, bh))


def _filter_kernel(z_ref, w1_ref, b1_ref, w2_ref, b2_ref, w3_ref, fr_ref, dec_ref, h_ref, s_ref, *, zero_block):
    i = pl.program_id(0)

    @pl.when(i == 0)
    def _():
        s_ref[...] = jnp.zeros(s_ref.shape, F32)

    fr = fr_ref[...]
    hid = jnp.sin(fr * (_dot3(z_ref[...], w1_ref[...]) + b1_ref[...]))
    hid = jnp.sin(fr * (_dot3(hid, w2_ref[...]) + b2_ref[...]))
    h = _dot3(hid, w3_ref[0])
    dec = dec_ref[...]
    C = dec.shape[1]
    n_ord = h.shape[1] // C
    h = h * jnp.concatenate([dec] * n_ord, axis=1)
    s_ref[...] += jnp.sum(jnp.abs(h), axis=0, keepdims=True)
    row = lax.broadcasted_iota(jnp.int32, h.shape, 0)
    h = jnp.where((row == 0) & (i == zero_block), 0.0, h)
    for o in range(n_ord):
        h_ref[o] = h[:, o * C:(o + 1) * C]


def hyena_filter_mlp(z2, w1, b1, w2, b2, w3_dir, freq, decay2):
    n2, E = z2.shape
    Hh = w2.shape[0]
    OC = w3_dir.shape[2]
    C = decay2.shape[1]
    n_ord = OC // C
    L = n2 // 2
    tr = _tile(L, 512, 8)
    nblk = n2 // tr
    c2 = lambda i: (0, 0)
    return pl.pallas_call(
        functools.partial(_filter_kernel, zero_block=L // tr),
        out_shape=(jax.ShapeDtypeStruct((n_ord, n2, C), F32), jax.ShapeDtypeStruct((1, OC), F32)),
        grid=(nblk,),
        in_specs=[pl.BlockSpec((tr, E), lambda i: (i, 0)), pl.BlockSpec((E, Hh), c2), pl.BlockSpec((1, Hh), c2),
                  pl.BlockSpec((Hh, Hh), c2), pl.BlockSpec((1, Hh), c2),
                  pl.BlockSpec((1, Hh, OC), lambda i: (i // (nblk // 2), 0, 0)),
                  pl.BlockSpec((1, Hh), c2), pl.BlockSpec((tr, C), lambda i: (i, 0))],
        out_specs=(pl.BlockSpec((n_ord, tr, C), lambda i: (0, i, 0)), pl.BlockSpec((1, OC), c2)),
        compiler_params=_cp(1), name="hyena_filter_mlp",
    )(z2, w1, b1, w2, b2, w3_dir, freq, decay2)


def _mm_split(m_hi, m_lo, x, passes):
    d = functools.partial(jnp.dot, preferred_element_type=F32)
    if passes == 1:
        return d(m_hi, x.astype(BF16))
    xh, xl = _split_bf16(x)
    return d(m_hi, xh) + (d(m_lo, xh) + d(m_hi, xl))


FFT_ROWS = 8
FFT_BLOCK = 16


def _level1_apply(m_refs, x, r_out, passes):
    r_in, _, C = x.shape
    x = x.astype(F32)
    m_hi = m_refs[0][...]
    m_lo = m_refs[1][...] if passes > 1 else None
    groups = []
    for g in range(FFT_BLOCK // FFT_ROWS):
        xg = x[:, g * FFT_ROWS:(g + 1) * FFT_ROWS].reshape(r_in * FFT_ROWS, C)
        groups.append(_mm_split(m_hi, m_lo, xg, passes).reshape(r_out, FFT_ROWS, C))
    return jnp.concatenate(groups, axis=1)


def _level1_kernel(*refs, passes):
    n_mat = 1 if passes == 1 else 2
    x_ref, o_ref = refs[n_mat:]
    o_ref[0] = _level1_apply(refs[:n_mat], x_ref[0], o_ref.shape[1], passes).astype(o_ref.dtype)


def _level1_gate_kernel(*refs, passes):
    n_mat = 1 if passes == 1 else 2
    x_ref, z_ref, xg_ref, bias_ref, o_ref = refs[n_mat:]
    y = _level1_apply(refs[:n_mat], x_ref[0], o_ref.shape[1], passes)
    o_ref[0] = (xg_ref[0] * (y + bias_ref[...] * z_ref[0])).astype(o_ref.dtype)


def fft_level1(m_hi, m_lo, x, passes, gate_args=None, out_dtype=F32):
    P, R_in, Nb, C = x.shape
    R_out = m_hi.shape[0] // FFT_ROWS
    mats = (m_hi,) if passes == 1 else (m_hi, m_lo)
    mspec = pl.BlockSpec(m_hi.shape, lambda p, j: (0, 0), pipeline_mode=pl.Buffered(1))
    xspec = pl.BlockSpec((1, R_in, FFT_BLOCK, C), lambda p, j: (p, 0, j, 0))
    ospec = pl.BlockSpec((1, R_out, FFT_BLOCK, C), lambda p, j: (p, 0, j, 0))
    if gate_args is None:
        kern = functools.partial(_level1_kernel, passes=passes)
        ins, specs = (*mats, x), [mspec] * len(mats) + [xspec]
    else:
        z, xg, bias = gate_args
        kern = functools.partial(_level1_gate_kernel, passes=passes)
        ins = (*mats, x, z, xg, bias)
        specs = [mspec] * len(mats) + [xspec, ospec, ospec, pl.BlockSpec((1, C), lambda p, j: (0, 0))]
    return pl.pallas_call(
        kern, out_shape=jax.ShapeDtypeStruct((P, R_out, Nb, C), out_dtype),
        grid=(P, Nb // FFT_BLOCK), in_specs=specs, out_specs=ospec,
        compiler_params=_cp(2), name="fft_level1",
    )(*ins)


FFT_MID_K1 = 2


def _fft_mid_kernel(*refs, passes):
    n_mat = 2 if passes == 1 else 4
    gh_ref, gih_ref = refs[0], refs[1]
    gl_ref, gil_ref = (refs[2], refs[3]) if passes > 1 else (None, None)
    h_ref, a_ref, o_ref = refs[n_mat:]
    n_pairs, _, kb, nb, C = a_ref.shape
    for kk in range(kb):
        hr, hi = h_ref[0, 0, kk], h_ref[0, 1, kk]
        for p in range(n_pairs):
            x = a_ref[p, :, kk].reshape(2 * nb, C)
            X = _mm_split(gh_ref[kk], None if gl_ref is None else gl_ref[kk], x, passes)
            xr, xi = X[:nb], X[nb:]
            Y = jnp.concatenate([xr * hr - xi * hi, xr * hi + xi * hr], axis=0)
            Bv = _mm_split(gih_ref[kk], None if gil_ref is None else gil_ref[kk], Y, passes)
            o_ref[p, :, kk] = Bv.reshape(2, nb, C).astype(o_ref.dtype)


def fft_mid(g_hi, g_lo, gi_hi, gi_lo, spectra, order, a5, passes):
    P, _, Na, Nb, C = a5.shape
    kb = FFT_MID_K1
    gspec = pl.BlockSpec((kb, 2 * Nb, 2 * Nb), lambda k: (k, 0, 0))
    mats = (g_hi, gi_hi) if passes == 1 else (g_hi, gi_hi, g_lo, gi_lo)
    blk = pl.BlockSpec((P, 2, kb, Nb, C), lambda k: (0, 0, k, 0, 0))
    return pl.pallas_call(
        functools.partial(_fft_mid_kernel, passes=passes),
        out_shape=jax.ShapeDtypeStruct(a5.shape, a5.dtype),
        grid=(Na // kb,),
        in_specs=[gspec] * len(mats) + [pl.BlockSpec((1, 2, kb, Nb, C), lambda k: (order, 0, k, 0, 0)), blk],
        out_specs=blk,
        compiler_params=_cp(1), name="fft_mid",
    )(*mats, spectra, a5)


def _fft_spec_kernel(*refs, passes):
    n_mat = 1 if passes == 1 else 2
    gh_ref = refs[0]
    gl_ref = refs[1] if passes > 1 else None
    inv_ref, a_ref, o_ref = refs[n_mat:]
    n_filt, _, kb, nb, C = a_ref.shape
    for kk in range(kb):
        for p in range(n_filt):
            x = a_ref[p, :, kk].reshape(2 * nb, C)
            X = _mm_split(gh_ref[kk], None if gl_ref is None else gl_ref[kk], x, passes) * inv_ref[p]
            o_ref[p, :, kk] = X.reshape(2, nb, C)


def fft_filter_spectrum(g_hi, g_lo, inv_norm, a5, passes):
    P, _, Na, Nb, C = a5.shape
    kb = FFT_MID_K1
    mats = (g_hi,) if passes == 1 else (g_hi, g_lo)
    gspec = pl.BlockSpec((kb, 2 * Nb, 2 * Nb), lambda k: (k, 0, 0))
    blk = pl.BlockSpec((P, 2, kb, Nb, C), lambda k: (0, 0, k, 0, 0))
    return pl.pallas_call(
        functools.partial(_fft_spec_kernel, passes=passes),
        out_shape=jax.ShapeDtypeStruct(a5.shape, F32),
        grid=(Na // kb,),
        in_specs=[gspec] * len(mats) + [pl.BlockSpec((P, 1, C), lambda k: (0, 0, 0)), blk],
        out_specs=blk,
        compiler_params=_cp(1), name="fft_filter_spectrum",
    )(*mats, inv_norm, a5)


def _fft_factors(n_fft):
    na = 1 << (int(math.log2(n_fft)) // 2)
    return na, n_fft // na


def _fft_tables(L):
    n_fft = 2 * L
    na, nb = _fft_factors(n_fft)
    nh = na // 2
    iota = lambda shape, axis: lax.broadcasted_iota(jnp.int32, shape, axis)

    def cis(num, den):
        ang = (2.0 * math.pi / den) * (num % den).astype(F32)
        return jnp.cos(ang), -jnp.sin(ang)

    def block(diag_sign, pr, pc, cr, ci):
        return jnp.where(pr == pc, cr, jnp.where(pr == 1, diag_sign * ci, -diag_sign * ci))

    def kron_matrix(rows, cols, entry):
        shape = (rows * FFT_ROWS, cols * FFT_ROWS)
        I, J = iota(shape, 0), iota(shape, 1)
        m = entry(I // FFT_ROWS, J // FFT_ROWS)
        return _split_bf16(jnp.where(I % FFT_ROWS == J % FFT_ROWS, m, 0.0))

    def fwd_entry(i, j):
        cr, ci = cis((i % na) * (j % nh), na)
        return block(1, i // na, j // nh, cr, ci)

    def flt_entry(i, j):
        cr, ci = cis((i % na) * j, na)
        return jnp.where(i // na == 0, cr, ci)

    def inv_entry(i, j):
        cr, ci = cis((i % nh) * (j % na), na)
        return block(-1, i // nh, j // na, cr, ci) / n_fft

    gshape = (na, 2 * nb, 2 * nb)
    k1, gi_, gj_ = iota(gshape, 0), iota(gshape, 1), iota(gshape, 2)
    cr, ci = cis((gi_ % nb) * (gj_ % nb) * na + k1 * (gj_ % nb), n_fft)
    g = _split_bf16(block(1, gi_ // nb, gj_ // nb, cr, ci))
    cr, ci = cis((gj_ % nb) * (gi_ % nb) * na + k1 * (gi_ % nb), n_fft)
    ginv = _split_bf16(block(-1, gi_ // nb, gj_ // nb, cr, ci))

    C = HY_WIDTH
    n = jnp.arange(n_fft, dtype=jnp.int32)
    pos = jnp.where(n < L, n, jnp.where(n == L, 0, n_fft - n)).astype(F32)[:, None]
    t = pos * (1.0 / (L - 1))
    f = jnp.linspace(1e-4, HY_BANDS - 1, HY_BANDS, dtype=F32)
    ang = (2.0 * math.pi / L) * pos * f[None, :]
    z2 = jnp.concatenate([t, jnp.cos(ang), -jnp.sin(ang), jnp.zeros((n_fft, HY_EMB_PAD - HY_EMB), F32)], axis=-1)
    deltas = jnp.abs(jnp.linspace(HY_MIN_DECAY, HY_MAX_DECAY, C, dtype=F32))
    decay2 = jnp.exp(-t * deltas)
    return dict(na=na, nb=nb, fwd=kron_matrix(2 * na, na, fwd_entry), flt=kron_matrix(2 * na, na, flt_entry),
                inv=kron_matrix(na, 2 * na, inv_entry), g=g, ginv=ginv, z2=z2, decay2=decay2)


def hyena_filter_spectra(w1, b1, w2, b2, w3, freq, tabs, passes):
    C = HY_WIDTH
    na, nb = tabs["na"], tabs["nb"]
    w1p = jnp.pad(w1, ((0, HY_EMB_PAD - HY_EMB), (0, 0)))
    w3_dir = jnp.transpose(w3.reshape(HY_HIDDEN, HY_ORDER, 2, C), (2, 0, 1, 3)).reshape(2, HY_HIDDEN, HY_ORDER * C)
    filt, s = hyena_filter_mlp(tabs["z2"], w1p, b1[None], w2, b2[None], w3_dir, freq[None], tabs["decay2"])
    inv_norm = (1.0 / s).reshape(HY_ORDER, 1, C)
    a = fft_level1(*tabs["flt"], filt.reshape(HY_ORDER, na, nb, C), passes,
                   out_dtype=BF16 if passes == 1 else F32)
    return fft_filter_spectrum(*tabs["g"], inv_norm, a.reshape(HY_ORDER, 2, na, nb, C), passes)


def hyena_long_conv_gate(z, xg, bias, spectra, order, tabs, passes):
    B, L, C = z.shape
    na, nb = tabs["na"], tabs["nb"]
    P = B // 2
    nat = lambda a: a.reshape(P, na, nb, C)
    mid_dtype = BF16 if passes == 1 else F32
    a = fft_level1(*tabs["fwd"], nat(z), passes, out_dtype=mid_dtype)
    bv = fft_mid(*tabs["g"], *tabs["ginv"], spectra, order, a.reshape(P, 2, na, nb, C), passes)
    y = fft_level1(*tabs["inv"], bv.reshape(P, 2 * na, nb, C), passes,
                   gate_args=(nat(z), nat(xg), bias[None]))
    return y.reshape(B, L, C)


def hyena_mix(u, row0, L, spectra, conv_w, conv_b, bias, tabs, passes):
    v, x1, x2 = hyena_short_conv(u, conv_w, conv_b[None], row0, L)
    zz = hyena_long_conv_gate(v, x1, bias[0], spectra, 0, tabs, passes)
    return hyena_long_conv_gate(zz, x2, bias[1], spectra, 1, tabs, passes)


FFT_PASSES = 1
FILTER_PASSES = 1


def _rope_tables_t(n):
    rows = n // GRID_W
    r = jnp.repeat(jnp.arange(rows, dtype=F32), GRID_W)
    col = jnp.tile(jnp.arange(GRID_W, dtype=F32), rows)
    axis_dim = HEAD_DIM // 2
    inv = ROPE_THETA ** (-jnp.arange(0, axis_dim, 2, dtype=F32) / axis_dim)
    ang = jnp.concatenate([r[:, None] * inv, col[:, None] * inv], axis=-1)
    return jnp.cos(ang).T, jnp.sin(ang).T


def _layer_weights(l, w_in, gq_qn, gq_kn):
    w = w_in[l]
    d_qk = DA_HEADS * 2 * HEAD_DIM
    gq_kv = GKV_HEADS * HEAD_DIM
    gq_q = GQ_HEADS * HEAD_DIM
    o = np.cumsum([0, d_qk, d_qk, gq_kv, gq_kv, d_qk, gq_q, 3 * HY_WIDTH])
    ka, va, kb, vb, qa, qb, hy = (w[:, o[i]:o[i + 1]] for i in range(7))
    gates = w[:, o[7]:]
    qscale = (HEAD_DIM ** -0.5) * LOG2E
    w_qk = jnp.concatenate([ka, kb, qa, qb], axis=1).T.astype(BF16)
    g_qk = jnp.concatenate([jnp.ones((d_qk,), F32), jnp.tile(gq_kn[l], GKV_HEADS),
                            jnp.full((d_qk,), qscale, F32), jnp.tile(gq_qn[l], GQ_HEADS) * qscale])[:, None]
    n_ka, n_kb, n_qa = d_qk // HEAD_DIM, gq_kv // HEAD_DIM, d_qk // HEAD_DIM
    rms_heads = frozenset(range(n_ka, n_ka + n_kb)) | frozenset(range(n_ka + n_kb + n_qa,
                                                                       n_ka + n_kb + n_qa + gq_q // HEAD_DIM))
    return dict(w_qk=w_qk, g_qk=g_qk, rms_heads=rms_heads, w_va=va.T.astype(BF16), w_vb=vb.T.astype(BF16),
                w_tok=jnp.concatenate([hy, gates], axis=1).astype(BF16))


def kernel(x, c, ctx, c_ctx, w_mod, b_mod, w_in, da_lq1, da_lk1, da_lq2, da_lk2, da_subln, gq_qn, gq_kn,
           hy_conv_w, hy_conv_b, hy_w1, hy_b1, hy_w2, hy_b2, hy_w3, hy_freq, hy_bias, w_pa, w_pb, w_pc, w_o,
           ln1_g, ln1_b, w_up, ffn_conv_w, ffn_conv_b, w_down, ln2_g, ln2_b):
    B, n_lat, D = x.shape
    n_ctx = ctx.shape[1]
    depth = w_in.shape[0]
    alpha = (2 * depth) ** 0.25
    cos_l, sin_l = _rope_tables_t(n_lat)
    cos_t = jnp.concatenate([cos_l, jnp.ones((HALF, n_ctx), F32)], axis=1)
    sin_t = jnp.concatenate([sin_l, jnp.zeros((HALF, n_ctx), F32)], axis=1)
    lat, ctx_span, all_span = (0, n_lat), (n_lat, n_ctx), (0, n_lat + n_ctx)
    tabs_l = _fft_tables(n_lat)
    tabs_c = _fft_tables(n_ctx)
    xc = ctx

    cond = jnp.concatenate([c, c_ctx[None]], axis=0)
    cond = jnp.pad(jax.nn.silu(cond), ((0, 16 - (B + 1) % 16), (0, 0))).astype(BF16)

    for l in range(depth):
        last = l == depth - 1
        lam_init = 0.8 - 0.6 * math.exp(-0.3 * l)
        lam = (jnp.exp(jnp.sum(da_lq1[l] * da_lk1[l])) - jnp.exp(jnp.sum(da_lq2[l] * da_lk2[l])) + lam_init)
        lam = lam.reshape(1).astype(F32)
        mod = matmul(cond, w_mod[l].astype(BF16), F32, tn_pref=1024) + b_mod[l]
        sh1, sc1, g1, sh2, sc2, g2 = [m[:, None, :] for m in jnp.split(mod[:B], 6, axis=-1)]
        mc = [jnp.broadcast_to(m[None, None, :], (B, 1, D)) for m in jnp.split(mod[B], 6)]
        lw = _layer_weights(l, w_in, gq_qn, gq_kn)
        subln = da_subln[l][:, None]
        hy_mlp = (hy_w1[l], hy_b1[l], hy_w2[l], hy_b2[l], hy_w3[l], hy_freq[l])
        spectra_l = hyena_filter_spectra(*hy_mlp, tabs_l, FILTER_PASSES)
        wpa, wpb, wpc, wo = (w[l].astype(BF16) for w in (w_pa, w_pb, w_pc, w_o))
        wup, wdn = w_up[l].astype(BF16), w_down[l].astype(BF16)
        ln1 = (ln1_g[l][None], ln1_b[l][None])
        ln2 = (ln2_g[l][None], ln2_b[l][None])

        q_all, k_tok, k_n2, va, vb, u, gates = proj_all(x, xc, sh1, sc1, mc[0], mc[1], cos_t, sin_t, lw)
        kmax = jnp.sqrt(k_n2[:, :N_KHEADS, 0]).reshape(B * N_KHEADS)
        n_da_maps = DA_HEADS * 2

        def attend(q_span, kv_span):
            da = flash_attention(q_all, k_tok, va, kmax, lam, subln, mode="da", q_span=q_span, kv_span=kv_span,
                                 q_row_block=0, k_col_block=0, khead0=0, post_scale=1.0 - lam_init)
            gq = flash_attention(q_all, k_tok, vb, kmax, lam, subln, mode="gqa", q_span=q_span, kv_span=kv_span,
                                 q_row_block=n_da_maps * HEAD_DIM // 128, k_col_block=n_da_maps * HEAD_DIM // 128,
                                 khead0=n_da_maps)
            return da, gq

        hy_args = (hy_conv_w[l], hy_conv_b[l], hy_bias[l])
        a_l, b_l = attend(lat, all_span)
        c_l = hyena_mix(u, *lat, spectra_l, *hy_args, tabs_l, FFT_PASSES)
        x_new, h2 = merge_resid_ln(a_l, b_l, c_l, gates, lat[0], wpa, wpb, wpc, wo, x, g1, *ln1,
                                   sh2, sc2, alpha)
        x_new = conv_ffn_resid_ln(h2, wup, ffn_conv_w[l], ffn_conv_b[l][None], wdn, x_new, g2, *ln2, alpha)

        if not last:
            spectra_c = hyena_filter_spectra(*hy_mlp, tabs_c, FILTER_PASSES)
            a_c, b_c = attend(ctx_span, ctx_span)
            c_c = hyena_mix(u, *ctx_span, spectra_c, *hy_args, tabs_c, FFT_PASSES)
            xc, hc2 = merge_resid_ln(a_c, b_c, c_c, gates, ctx_span[0], wpa, wpb, wpc, wo, xc, mc[2],
                                     *ln1, mc[3], mc[4], alpha)
            xc = conv_ffn_resid_ln(hc2, wup, ffn_conv_w[l], ffn_conv_b[l][None], wdn, xc, mc[5], *ln2, alpha)
        x = x_new
    return x
```

```python
import functools
import math

import numpy as np
import jax
import jax.numpy as jnp
from jax import lax
from jax.experimental import pallas as pl
from jax.experimental.pallas import tpu as pltpu

F32 = jnp.float32
BF16 = jnp.bfloat16

HEAD_DIM = 64
HALF = HEAD_DIM // 2
GRID_W = 64
ROPE_THETA = 10000.0
DA_HEADS = 4
GQ_HEADS = 8
GKV_HEADS = 2
HY_WIDTH = 512
HY_ORDER = 2
HY_EMB = 33
HY_EMB_PAD = 128
HY_BANDS = (HY_EMB - 1) // 2
HY_HIDDEN = 64
HY_MIN_DECAY = math.log(1e-2) / 1.5
HY_MAX_DECAY = math.log(1e-2) / 0.3
LN_EPS = 1e-6
LOG2E = 1.4426950408889634
NEG_BIG = -1e30

VMEM_LIMIT = 56 * 1024 * 1024


def _cp(n_axes):
    return pltpu.CompilerParams(dimension_semantics=("arbitrary",) * n_axes,
                                vmem_limit_bytes=VMEM_LIMIT)


def _tile(n, pref, mult):
    if n <= pref:
        return n
    t = (pref // mult) * mult
    while t >= mult:
        if n % t == 0:
            return t
        t -= mult
    return n


def _ln(x):
    mu = jnp.mean(x, axis=-1, keepdims=True)
    xc = x - mu
    return xc * lax.rsqrt(jnp.mean(xc * xc, axis=-1, keepdims=True) + LN_EPS)


N_KHEADS = DA_HEADS * 2 + GKV_HEADS
KNORM_ROWS = 16


def _proj_all_kernel(x_ref, xc_ref, sh_ref, sc_ref, shc_ref, scc_ref, cos_ref, sin_ref, wqk_ref, gqk_ref,
                     wva_ref, wvb_ref, wtok_ref,
                     q_ref, k_ref, kn_ref, va_ref, vb_ref, u_ref, gt_ref, kt_sc,
                     *, n_lat_tiles, rms_heads, n_hy):
    i = pl.program_id(1)

    @pl.when(i == 0)
    def _():
        kn_ref[...] = jnp.zeros(kn_ref.shape, F32)

    is_latent = i < n_lat_tiles
    xin = jnp.where(is_latent, x_ref[0], xc_ref[0])
    shift = jnp.where(is_latent, sh_ref[0], shc_ref[0])
    scale = jnp.where(is_latent, sc_ref[0], scc_ref[0])
    h = (_ln(xin) * (1.0 + scale) + shift).astype(BF16)
    nt = (((1,), (1,)), ((), ()))

    acc = lax.dot_general(wqk_ref[...], h, nt, preferred_element_type=F32)
    va = lax.dot_general(wva_ref[...], h, nt, preferred_element_type=F32)
    vb = lax.dot_general(wvb_ref[...], h, nt, preferred_element_type=F32)
    tok = jnp.dot(h, wtok_ref[...], preferred_element_type=F32)
    va_ref[0] = va.astype(va_ref.dtype)
    vb_ref[0] = vb.astype(vb_ref.dtype)
    c, s = cos_ref[...], sin_ref[...]
    n_k_rows = kt_sc.shape[0]
    for hd in range(acc.shape[0] // HEAD_DIM):
        lo = hd * HEAD_DIM
        xh = acc[lo:lo + HEAD_DIM]
        if hd in rms_heads:
            xh = xh * lax.rsqrt(jnp.mean(xh * xh, axis=0, keepdims=True) + LN_EPS)
        xh = xh * gqk_ref[lo:lo + HEAD_DIM]
        x1, x2 = xh[:HALF], xh[HALF:]
        r1, r2 = x1 * c - x2 * s, x1 * s + x2 * c
        if lo < n_k_rows:
            kt_sc[lo:lo + HALF] = r1
            kt_sc[lo + HALF:lo + HEAD_DIM] = r2
        else:
            q_ref[0, lo - n_k_rows:lo - n_k_rows + HALF] = r1.astype(q_ref.dtype)
            q_ref[0, lo - n_k_rows + HALF:lo - n_k_rows + HEAD_DIM] = r2.astype(q_ref.dtype)

    k_bf = kt_sc[...].astype(BF16)
    k_ref[0] = k_bf.astype(F32).T.astype(BF16)
    k_sq = k_bf.astype(F32) ** 2
    for hd in range(n_k_rows // HEAD_DIM):
        n2 = jnp.sum(k_sq[hd * HEAD_DIM:(hd + 1) * HEAD_DIM], axis=0, keepdims=True)
        kn_ref[0, hd:hd + 1] = jnp.maximum(kn_ref[0, hd:hd + 1], jnp.max(n2, axis=1, keepdims=True))

    u_ref[0] = tok[:, :n_hy]
    gt_ref[0] = jax.nn.sigmoid(tok[:, n_hy:]).astype(gt_ref.dtype)


def proj_all(x, xc, shift, scale, shift_c, scale_c, cos_t, sin_t, lw):
    B, N, D = x.shape
    Nc = xc.shape[1]
    ntot = N + Nc
    tt = _tile(Nc, 256, 128)
    assert N % tt == 0
    nl = N // tt
    n_k_rows = N_KHEADS * HEAD_DIM
    n_q_rows = lw["w_qk"].shape[0] - n_k_rows
    n_va, n_vb = lw["w_va"].shape[0], lw["w_vb"].shape[0]
    n_tok = lw["w_tok"].shape[1]
    n_hy = 3 * HY_WIDTH
    per_b = lambda b, i: (b, 0, 0)
    const = lambda b, i: (0, 0)
    resident = dict(pipeline_mode=pl.Buffered(1))
    feat = lambda rows: pl.BlockSpec((1, rows, tt), lambda b, i: (b, 0, i))
    tokm = lambda cols: pl.BlockSpec((1, tt, cols), lambda b, i: (b, i, 0))
    sds = jax.ShapeDtypeStruct
    return pl.pallas_call(
        functools.partial(_proj_all_kernel, n_lat_tiles=nl, rms_heads=lw["rms_heads"], n_hy=n_hy),
        out_shape=(sds((B, n_q_rows, ntot), BF16), sds((B, ntot, n_k_rows), BF16),
                   sds((B, KNORM_ROWS, 128), F32), sds((B, n_va, ntot), BF16), sds((B, n_vb, ntot), BF16),
                   sds((B, ntot, n_hy), F32), sds((B, ntot, n_tok - n_hy), BF16)),
        grid=(B, ntot // tt),
        in_specs=[pl.BlockSpec((1, tt, D), lambda b, i: (b, jnp.minimum(i, nl - 1), 0)),
                  pl.BlockSpec((1, tt, D), lambda b, i: (b, jnp.maximum(i - nl, 0), 0)),
                  pl.BlockSpec((1, 1, D), per_b), pl.BlockSpec((1, 1, D), per_b),
                  pl.BlockSpec((1, 1, D), per_b), pl.BlockSpec((1, 1, D), per_b),
                  pl.BlockSpec((HALF, tt), lambda b, i: (0, i)), pl.BlockSpec((HALF, tt), lambda b, i: (0, i)),
                  pl.BlockSpec(lw["w_qk"].shape, const, **resident), pl.BlockSpec((lw["w_qk"].shape[0], 1), const),
                  pl.BlockSpec(lw["w_va"].shape, const, **resident),
                  pl.BlockSpec(lw["w_vb"].shape, const, **resident),
                  pl.BlockSpec(lw["w_tok"].shape, const, **resident)],
        out_specs=(feat(n_q_rows), tokm(n_k_rows), pl.BlockSpec((1, KNORM_ROWS, 128), per_b),
                   feat(n_va), feat(n_vb), tokm(n_hy), tokm(n_tok - n_hy)),
        scratch_shapes=[pltpu.VMEM((n_k_rows, tt), F32)],
        compiler_params=_cp(2), name="proj_all",
    )(x, xc, shift, scale, shift_c, scale_c, cos_t, sin_t, lw["w_qk"], lw["g_qk"], lw["w_va"], lw["w_vb"],
      lw["w_tok"])


def _resid_ln_tail(x, y, gate, g, b, alpha):
    return _ln(alpha * x + gate * y) * g + b


def _mm_kernel(a_ref, w_ref, o_ref, *, act):
    acc = jnp.dot(a_ref[...], w_ref[...], preferred_element_type=F32)
    if act == "sigmoid":
        acc = jax.nn.sigmoid(acc)
    o_ref[...] = acc.astype(o_ref.dtype)


def matmul(a, w, out_dtype, act=None, tm_pref=1024, tn_pref=512):
    M, K = a.shape
    Nn = w.shape[1]
    tm = _tile(M, tm_pref, 16)
    tn = _tile(Nn, tn_pref, 128)
    return pl.pallas_call(
        functools.partial(_mm_kernel, act=act),
        out_shape=jax.ShapeDtypeStruct((M, Nn), out_dtype),
        grid=(M // tm, Nn // tn),
        in_specs=[pl.BlockSpec((tm, K), lambda i, j: (i, 0)),
                  pl.BlockSpec((K, tn), lambda i, j: (0, j))],
        out_specs=pl.BlockSpec((tm, tn), lambda i, j: (i, j)),
        compiler_params=_cp(2), name="matmul",
    )(a, w)


FLASH_MIN_DENOM = 2.0 ** -60
FLASH_UNROLL = 4
FLASH_COL_SPLIT = 2


def _flash_kernel(lam_ref, kmax_ref, q_ref, k_ref, v_ref, g_ref, o_ref, q_sc, p_buf, sh_sc, acc_sc, l_sc,
                  *, mode, nj, tk, post_scale, n_kheads, khead0):
    b, h = pl.program_id(0), pl.program_id(1)
    qf = q_ref[0].astype(F32)
    tq = qf.shape[1]
    zero = jnp.zeros((HEAD_DIM, tq), F32)
    qa, qb = qf[:HEAD_DIM], qf[HEAD_DIM:]
    if mode == "da":
        q_sc[0] = jnp.concatenate([qa, zero], axis=0).astype(BF16)
        q_sc[1] = jnp.concatenate([zero, qb], axis=0).astype(BF16)
        k_heads = (khead0 + 2 * h, khead0 + 2 * h + 1)
    else:
        group = h // 2
        for mi, qh in enumerate((qa, qb)):
            q_sc[mi] = jnp.where(group == 0, jnp.concatenate([qh, zero], axis=0),
                                 jnp.concatenate([zero, qh], axis=0)).astype(BF16)
        k_heads = (khead0 + group, khead0 + group)
    for mi, qh in enumerate((qa, qb)):
        q_norm = jnp.sqrt(jnp.sum(qh * qh, axis=0, keepdims=True))
        sh_sc[mi] = q_norm * kmax_ref[b * n_kheads + k_heads[mi]]

    def chunk(j):
        start = j * tk
        return pl.ds(start if isinstance(start, int) else pl.multiple_of(start, tk), tk)

    n_split = FLASH_COL_SPLIT if tq % (128 * FLASH_COL_SPLIT) == 0 else 1
    tc = tq // n_split
    pieces = [(mi, slice(ci * tc, (ci + 1) * tc)) for mi in range(2) for ci in range(n_split)]

    def stage_exp(j, slot, mi, cols):
        s = jnp.dot(k_ref[0, chunk(j), :], q_sc[mi, :, cols], preferred_element_type=F32)
        p = jnp.exp2(s - sh_sc[mi, :, cols])
        p_buf[slot, mi, :, cols] = p.astype(BF16)
        l_sc[mi, :, cols] += jnp.sum(p.reshape(tk // 8, 8, tc), axis=0)

    def stage_values(j, slot, mi, cols):
        acc_sc[mi, :, cols] += jnp.dot(v_ref[0, :, chunk(j)], p_buf[slot, mi, :, cols],
                                       preferred_element_type=F32)

    def tick(t, parity, exp=True, values=True):
        for mi, cols in pieces:
            if exp:
                stage_exp(t, parity, mi, cols)
            if values:
                stage_values(t - 1, 1 - parity, mi, cols)

    def accumulate():
        acc_sc[...] = jnp.zeros(acc_sc.shape, F32)
        l_sc[...] = jnp.zeros(l_sc.shape, F32)
        tick(0, 0, values=False)
        steady = list(range(1, nj))
        while len(steady) % FLASH_UNROLL:
            t = steady.pop(0)
            tick(t, t % 2)
        if steady:
            t0 = steady[0]

            def body(i, carry):
                for d in range(FLASH_UNROLL):
                    tick(t0 + FLASH_UNROLL * i + d, (t0 + d) % 2)
                return carry

            lax.fori_loop(0, len(steady) // FLASH_UNROLL, body, 0)
        tick(nj, nj % 2, exp=False)

    denominators = lambda: [jnp.sum(l_sc[mi], axis=0, keepdims=True) for mi in range(2)]
    accumulate()
    denom_min = jnp.min(jnp.minimum(*denominators()))

    @pl.when(jnp.logical_not(denom_min >= FLASH_MIN_DENOM))
    def _():
        for mi in range(2):
            def max_body(j, m, mi=mi):
                s = jnp.dot(k_ref[0, chunk(j), :], q_sc[mi], preferred_element_type=F32)
                return jnp.maximum(m, jnp.max(s, axis=0, keepdims=True))

            sh_sc[mi] = lax.fori_loop(0, nj, max_body, jnp.full((1, tq), NEG_BIG, F32))
        accumulate()

    outs = [acc_sc[mi] / l for mi, l in enumerate(denominators())]
    if mode == "da":
        o = outs[0] - lam_ref[0] * outs[1]
        o = o * lax.rsqrt(jnp.mean(o * o, axis=0, keepdims=True) + LN_EPS)
        o = o * (g_ref[...] * post_scale)
    else:
        o = jnp.concatenate(outs, axis=0)
    o_ref[0] = o.T.astype(o_ref.dtype)


def flash_attention(q_t, k_tok, v_aug, kmax, lam, gain, *, mode, q_span, kv_span, q_row_block, k_col_block,
                    khead0, post_scale=1.0, tk_pref=768):
    B = q_t.shape[0]
    q0, Nq = q_span
    k0, Nk = kv_span
    tq = _tile(Nq, 1024, 128)
    tk = _tile(Nk, tk_pref, 128)
    assert q0 % tq == 0 and k0 % Nk == 0
    nj = Nk // tk
    dv = 2 * HEAD_DIM if mode == "da" else HEAD_DIM
    n_kheads = kmax.shape[0] // B
    if mode == "da":
        k_map = lambda b, h, i: (b, k0 // Nk, k_col_block + h)
        v_map = lambda b, h, i: (b, h, k0 // Nk)
    else:
        k_map = lambda b, h, i: (b, k0 // Nk, k_col_block)
        v_map = lambda b, h, i: (b, h // 2, k0 // Nk)
    return pl.pallas_call(
        functools.partial(_flash_kernel, mode=mode, nj=nj, tk=tk, post_scale=post_scale,
                          n_kheads=n_kheads, khead0=khead0),
        out_shape=jax.ShapeDtypeStruct((B, Nq, 4 * 128), BF16),
        grid=(B, 4, Nq // tq),
        in_specs=[pl.BlockSpec(memory_space=pltpu.SMEM),
                  pl.BlockSpec(memory_space=pltpu.SMEM),
                  pl.BlockSpec((1, 128, tq), lambda b, h, i: (b, h + q_row_block, i + q0 // tq)),
                  pl.BlockSpec((1, Nk, 128), k_map),
                  pl.BlockSpec((1, dv, Nk), v_map),
                  pl.BlockSpec((128, 1), lambda b, h, i: (0, 0))],
        out_specs=pl.BlockSpec((1, tq, 128), lambda b, h, i: (b, i, h)),
        scratch_shapes=[pltpu.VMEM((2, 128, tq), BF16),
                        pltpu.VMEM((2, 2, tk, tq), BF16),
                        pltpu.VMEM((2, 1, tq), F32),
                        pltpu.VMEM((2, dv, tq), F32),
                        pltpu.VMEM((2, 8, tq), F32)],
        compiler_params=_cp(3), name="flash_" + mode,
    )(lam, kmax, q_t, k_tok, v_aug, gain)


def _shift_rows(x, prev_row, next_row):
    T = x.shape[0]
    row = lax.broadcasted_iota(jnp.int32, x.shape, 0)
    xp = jnp.where(row == 0, prev_row, pltpu.roll(x, 1, axis=0))
    xn = jnp.where(row == T - 1, next_row, pltpu.roll(x, T - 1, axis=0))
    return xp, xn


def _conv3_block(x_ref, p_ref, n_ref, w_ref, b_ref, halo):
    i = pl.program_id(1)
    last = pl.num_programs(1) - 1
    x = x_ref[0].astype(F32)
    prev_row = jnp.where(i > 0, p_ref[0].astype(F32)[halo - 1:halo], 0.0)
    next_row = jnp.where(i < last, n_ref[0].astype(F32)[0:1], 0.0)
    xp, xn = _shift_rows(x, prev_row, next_row)
    return xp * w_ref[0:1] + x * w_ref[1:2] + xn * w_ref[2:3] + b_ref[...]


def _halo_specs(tr, tc, halo, row0, n_rows, col_of):
    per = tr // halo
    n_halo = n_rows // halo
    m0, h0 = row0 // tr, row0 // halo
    return [pl.BlockSpec((1, tr, tc), lambda b, i, j: (b, m0 + i, col_of(j))),
            pl.BlockSpec((1, halo, tc), lambda b, i, j: (b, h0 + jnp.maximum(i * per - 1, 0), col_of(j))),
            pl.BlockSpec((1, halo, tc),
                         lambda b, i, j: (b, h0 + jnp.minimum((i + 1) * per, n_halo - 1), col_of(j)))]


def _dwconv_kernel(x_ref, p_ref, n_ref, w_ref, b_ref, v_ref, x1_ref, x2_ref):
    y = _conv3_block(x_ref, p_ref, n_ref, w_ref, b_ref, 8)
    C = v_ref.shape[2]
    v_ref[0] = y[:, :C]
    x1_ref[0] = y[:, C:2 * C]
    x2_ref[0] = y[:, 2 * C:]


def hyena_short_conv(u, w, b, row0, L):
    B, _, C3 = u.shape
    C = C3 // 3
    tr = _tile(L, 512, 8)
    assert row0 % tr == 0
    zero = lambda j: 0
    out = jax.ShapeDtypeStruct((B, L, C), F32)
    ospec = pl.BlockSpec((1, tr, C), lambda b, i, j: (b, i, 0))
    return pl.pallas_call(
        _dwconv_kernel,
        out_shape=(out, out, out),
        grid=(B, L // tr, 1),
        in_specs=_halo_specs(tr, C3, 8, row0, L, zero) + [
            pl.BlockSpec((3, C3), lambda b, i, j: (0, 0)),
            pl.BlockSpec((1, C3), lambda b, i, j: (0, 0))],
        out_specs=(ospec, ospec, ospec),
        compiler_params=_cp(3), name="hyena_short_conv",
    )(u, u, u, w, b)


FFN_HALO = 16
FFN_CHUNK = 256


def _ffn_kernel(h_ref, hp_ref, hn_ref, wup_ref, cw_ref, cb_ref, wdn_ref, x_ref, gate_ref, g_ref, b_ref, o_ref,
                act_sc, *, alpha):
    i = pl.program_id(1)
    tm = h_ref.shape[1]
    dff = wdn_ref.shape[0]
    rows = tm + 2 * FFN_HALO
    h_prev = jnp.where(i == 0, jnp.zeros_like(hp_ref[0]), hp_ref[0])
    h_next = jnp.where(i == pl.num_programs(1) - 1, jnp.zeros_like(hn_ref[0]), hn_ref[0])
    h_ext = jnp.concatenate([h_prev, h_ref[0], h_next], axis=0)

    def conv_cols(c0):
        u = jnp.dot(h_ext, wup_ref[:, c0:c0 + FFN_CHUNK], preferred_element_type=F32)
        mid = slice(FFN_HALO, FFN_HALO + tm)
        u_prev = pltpu.roll(u, 1, axis=0)[mid]
        u_next = pltpu.roll(u, rows - 1, axis=0)[mid]
        w = cw_ref[:, c0:c0 + FFN_CHUNK]
        return u_prev * w[0:1] + u[mid] * w[1:2] + u_next * w[2:3] + cb_ref[:, c0:c0 + FFN_CHUNK]

    for c0 in range(0, dff, FFN_CHUNK):
        val = conv_cols(c0)
        gate = conv_cols(dff + c0)
        act_sc[:, c0:c0 + FFN_CHUNK] = (gate * jax.nn.sigmoid(gate) * val).astype(BF16)
    n_sub = 2 if tm % 32 == 0 else 1
    for r in range(n_sub):
        rows = slice(r * tm // n_sub, (r + 1) * tm // n_sub)
        y = jnp.dot(act_sc[rows], wdn_ref[...], preferred_element_type=F32)
        o_ref[0, rows] = _resid_ln_tail(x_ref[0, rows], y, gate_ref[0], g_ref[...], b_ref[...], alpha)


def conv_ffn_resid_ln(h, w_up, conv_w, conv_b, w_down, x, gate, g, b, alpha):
    B, N, D = h.shape
    dff = w_down.shape[0]
    tm = _tile(N, 512, FFN_HALO)
    per = tm // FFN_HALO
    n_halo = N // FFN_HALO
    row = lambda b_, i: (b_, i, 0)
    per_b = lambda b_, i: (b_, 0, 0)
    const2 = lambda b_, i: (0, 0)
    resident = dict(pipeline_mode=pl.Buffered(1))
    return pl.pallas_call(
        functools.partial(_ffn_kernel, alpha=alpha),
        out_shape=jax.ShapeDtypeStruct((B, N, D), F32),
        grid=(B, N // tm),
        in_specs=[pl.BlockSpec((1, tm, D), row),
                  pl.BlockSpec((1, FFN_HALO, D), lambda b_, i: (b_, jnp.maximum(i * per - 1, 0), 0)),
                  pl.BlockSpec((1, FFN_HALO, D), lambda b_, i: (b_, jnp.minimum((i + 1) * per, n_halo - 1), 0)),
                  pl.BlockSpec((D, 2 * dff), const2, **resident),
                  pl.BlockSpec((3, 2 * dff), const2), pl.BlockSpec((1, 2 * dff), const2),
                  pl.BlockSpec((dff, D), const2, **resident),
                  pl.BlockSpec((1, tm, D), row), pl.BlockSpec((1, 1, D), per_b),
                  pl.BlockSpec((1, D), const2), pl.BlockSpec((1, D), const2)],
        out_specs=pl.BlockSpec((1, tm, D), row),
        scratch_shapes=[pltpu.VMEM((tm, dff), BF16)],
        compiler_params=_cp(2), name="conv_ffn_resid_ln",
    )(h, h, h, w_up, conv_w, conv_b, w_down, x, gate, g, b)


def _merge_kernel(a_ref, b_ref, c_ref, gt_ref, wpa_ref, wpb_ref, wpc_ref, wo_ref, x_ref, gate_ref,
                  g_ref, bb_ref, sh_ref, sc_ref, ox_ref, oh_ref, *, alpha):
    D = wo_ref.shape[0]
    tm = x_ref.shape[1]
    n_sub = 2 if tm % 32 == 0 else 1
    for r in range(n_sub):
        rows = slice(r * tm // n_sub, (r + 1) * tm // n_sub)
        gt = gt_ref[0, rows]
        m = gt[:, :D].astype(F32) * jnp.dot(a_ref[0, rows], wpa_ref[...], preferred_element_type=F32)
        m = m + gt[:, D:2 * D].astype(F32) * jnp.dot(b_ref[0, rows], wpb_ref[...], preferred_element_type=F32)
        m = m + gt[:, 2 * D:].astype(F32) * jnp.dot(c_ref[0, rows].astype(BF16), wpc_ref[...],
                                                    preferred_element_type=F32)
        y = jnp.dot(m.astype(BF16), wo_ref[...], preferred_element_type=F32)
        xn = _resid_ln_tail(x_ref[0, rows], y, gate_ref[0], g_ref[...], bb_ref[...], alpha)
        ox_ref[0, rows] = xn
        oh_ref[0, rows] = (_ln(xn) * (1.0 + sc_ref[0]) + sh_ref[0]).astype(oh_ref.dtype)


def merge_resid_ln(a, b, c, gates, tok0, wpa, wpb, wpc, wo, x, gate, g, bb, sh, sc, alpha):
    B, N, D = x.shape
    W = a.shape[2]
    tm = _tile(N, 512, 16)
    assert tok0 % tm == 0
    row = lambda b_, i: (b_, i, 0)
    per_b = lambda b_, i: (b_, 0, 0)
    const2 = lambda b_, i: (0, 0)
    return pl.pallas_call(
        functools.partial(_merge_kernel, alpha=alpha),
        out_shape=(jax.ShapeDtypeStruct((B, N, D), F32), jax.ShapeDtypeStruct((B, N, D), BF16)),
        grid=(B, N // tm),
        in_specs=[pl.BlockSpec((1, tm, W), row), pl.BlockSpec((1, tm, W), row), pl.BlockSpec((1, tm, W), row),
                  pl.BlockSpec((1, tm, 3 * D), lambda b_, i: (b_, tok0 // tm + i, 0)),
                  pl.BlockSpec((W, D), const2), pl.BlockSpec((W, D), const2), pl.BlockSpec((W, D), const2),
                  pl.BlockSpec((D, D), const2),
                  pl.BlockSpec((1, tm, D), row), pl.BlockSpec((1, 1, D), per_b),
                  pl.BlockSpec((1, D), const2), pl.BlockSpec((1, D), const2),
                  pl.BlockSpec((1, 1, D), per_b), pl.BlockSpec((1, 1, D), per_b)],
        out_specs=(pl.BlockSpec((1, tm, D), row), pl.BlockSpec((1, tm, D), row)),
        compiler_params=_cp(2), name="merge_resid_ln",
    )(a, b, c, gates, wpa, wpb, wpc, wo, x, gate, g, bb, sh, sc)


def _split_bf16(x):
    hi = x.astype(BF16)
    lo = (x - hi.astype(F32)).astype(BF16)
    return hi, lo


def _dot3(a, b):
    ah, al = _split_bf16(a)
    bh, bl = _split_bf16(b)
    d = functools.partial(jnp.dot, preferred_element_type=F32)
    return d(ah, bh) + (d(ah, bl) + d(al, bh))


def _filter_kernel(z_ref, w1_ref, b1_ref, w2_ref, b2_ref, w3_ref, fr_ref, dec_ref, h_ref, s_ref, *, zero_block):
    i = pl.program_id(0)

    @pl.when(i == 0)
    def _():
        s_ref[...] = jnp.zeros(s_ref.shape, F32)

    fr = fr_ref[...]
    hid = jnp.sin(fr * (_dot3(z_ref[...], w1_ref[...]) + b1_ref[...]))
    hid = jnp.sin(fr * (_dot3(hid, w2_ref[...]) + b2_ref[...]))
    h = _dot3(hid, w3_ref[0])
    dec = dec_ref[...]
    C = dec.shape[1]
    n_ord = h.shape[1] // C
    h = h * jnp.concatenate([dec] * n_ord, axis=1)
    s_ref[...] += jnp.sum(jnp.abs(h), axis=0, keepdims=True)
    row = lax.broadcasted_iota(jnp.int32, h.shape, 0)
    h = jnp.where((row == 0) & (i == zero_block), 0.0, h)
    for o in range(n_ord):
        h_ref[o] = h[:, o * C:(o + 1) * C]


def hyena_filter_mlp(z2, w1, b1, w2, b2, w3_dir, freq, decay2):
    n2, E = z2.shape
    Hh = w2.shape[0]
    OC = w3_dir.shape[2]
    C = decay2.shape[1]
    n_ord = OC // C
    L = n2 // 2
    tr = _tile(L, 512, 8)
    nblk = n2 // tr
    c2 = lambda i: (0, 0)
    return pl.pallas_call(
        functools.partial(_filter_kernel, zero_block=L // tr),
        out_shape=(jax.ShapeDtypeStruct((n_ord, n2, C), F32), jax.ShapeDtypeStruct((1, OC), F32)),
        grid=(nblk,),
        in_specs=[pl.BlockSpec((tr, E), lambda i: (i, 0)), pl.BlockSpec((E, Hh), c2), pl.BlockSpec((1, Hh), c2),
                  pl.BlockSpec((Hh, Hh), c2), pl.BlockSpec((1, Hh), c2),
                  pl.BlockSpec((1, Hh, OC), lambda i: (i // (nblk // 2), 0, 0)),
                  pl.BlockSpec((1, Hh), c2), pl.BlockSpec((tr, C), lambda i: (i, 0))],
        out_specs=(pl.BlockSpec((n_ord, tr, C), lambda i: (0, i, 0)), pl.BlockSpec((1, OC), c2)),
        compiler_params=_cp(1), name="hyena_filter_mlp",
    )(z2, w1, b1, w2, b2, w3_dir, freq, decay2)


def _mm_split(m_hi, m_lo, x, passes):
    d = functools.partial(jnp.dot, preferred_element_type=F32)
    if passes == 1:
        return d(m_hi, x.astype(BF16))
    xh, xl = _split_bf16(x)
    return d(m_hi, xh) + (d(m_lo, xh) + d(m_hi, xl))


FFT_ROWS = 8
FFT_BLOCK = 16


def _level1_apply(m_refs, x, r_out, passes):
    r_in, _, C = x.shape
    x = x.astype(F32)
    m_hi = m_refs[0][...]
    m_lo = m_refs[1][...] if passes > 1 else None
    groups = []
    for g in range(FFT_BLOCK // FFT_ROWS):
        xg = x[:, g * FFT_ROWS:(g + 1) * FFT_ROWS].reshape(r_in * FFT_ROWS, C)
        groups.append(_mm_split(m_hi, m_lo, xg, passes).reshape(r_out, FFT_ROWS, C))
    return jnp.concatenate(groups, axis=1)


def _level1_kernel(*refs, passes):
    n_mat = 1 if passes == 1 else 2
    x_ref, o_ref = refs[n_mat:]
    o_ref[0] = _level1_apply(refs[:n_mat], x_ref[0], o_ref.shape[1], passes).astype(o_ref.dtype)


def _level1_gate_kernel(*refs, passes):
    n_mat = 1 if passes == 1 else 2
    x_ref, z_ref, xg_ref, bias_ref, o_ref = refs[n_mat:]
    y = _level1_apply(refs[:n_mat], x_ref[0], o_ref.shape[1], passes)
    o_ref[0] = (xg_ref[0] * (y + bias_ref[...] * z_ref[0])).astype(o_ref.dtype)


def fft_level1(m_hi, m_lo, x, passes, gate_args=None, out_dtype=F32):
    P, R_in, Nb, C = x.shape
    R_out = m_hi.shape[0] // FFT_ROWS
    mats = (m_hi,) if passes == 1 else (m_hi, m_lo)
    mspec = pl.BlockSpec(m_hi.shape, lambda p, j: (0, 0), pipeline_mode=pl.Buffered(1))
    xspec = pl.BlockSpec((1, R_in, FFT_BLOCK, C), lambda p, j: (p, 0, j, 0))
    ospec = pl.BlockSpec((1, R_out, FFT_BLOCK, C), lambda p, j: (p, 0, j, 0))
    if gate_args is None:
        kern = functools.partial(_level1_kernel, passes=passes)
        ins, specs = (*mats, x), [mspec] * len(mats) + [xspec]
    else:
        z, xg, bias = gate_args
        kern = functools.partial(_level1_gate_kernel, passes=passes)
        ins = (*mats, x, z, xg, bias)
        specs = [mspec] * len(mats) + [xspec, ospec, ospec, pl.BlockSpec((1, C), lambda p, j: (0, 0))]
    return pl.pallas_call(
        kern, out_shape=jax.ShapeDtypeStruct((P, R_out, Nb, C), out_dtype),
        grid=(P, Nb // FFT_BLOCK), in_specs=specs, out_specs=ospec,
        compiler_params=_cp(2), name="fft_level1",
    )(*ins)


FFT_MID_K1 = 2


def _fft_mid_kernel(*refs, passes):
    n_mat = 2 if passes == 1 else 4
    gh_ref, gih_ref = refs[0], refs[1]
    gl_ref, gil_ref = (refs[2], refs[3]) if passes > 1 else (None, None)
    h_ref, a_ref, o_ref = refs[n_mat:]
    n_pairs, _, kb, nb, C = a_ref.shape
    for kk in range(kb):
        hr, hi = h_ref[0, 0, kk], h_ref[0, 1, kk]
        for p in range(n_pairs):
            x = a_ref[p, :, kk].reshape(2 * nb, C)
            X = _mm_split(gh_ref[kk], None if gl_ref is None else gl_ref[kk], x, passes)
            xr, xi = X[:nb], X[nb:]
            Y = jnp.concatenate([xr * hr - xi * hi, xr * hi + xi * hr], axis=0)
            Bv = _mm_split(gih_ref[kk], None if gil_ref is None else gil_ref[kk], Y, passes)
            o_ref[p, :, kk] = Bv.reshape(2, nb, C).astype(o_ref.dtype)


def fft_mid(g_hi, g_lo, gi_hi, gi_lo, spectra, order, a5, passes):
    P, _, Na, Nb, C = a5.shape
    kb = FFT_MID_K1
    gspec = pl.BlockSpec((kb, 2 * Nb, 2 * Nb), lambda k: (k, 0, 0))
    mats = (g_hi, gi_hi) if passes == 1 else (g_hi, gi_hi, g_lo, gi_lo)
    blk = pl.BlockSpec((P, 2, kb, Nb, C), lambda k: (0, 0, k, 0, 0))
    return pl.pallas_call(
        functools.partial(_fft_mid_kernel, passes=passes),
        out_shape=jax.ShapeDtypeStruct(a5.shape, a5.dtype),
        grid=(Na // kb,),
        in_specs=[gspec] * len(mats) + [pl.BlockSpec((1, 2, kb, Nb, C), lambda k: (order, 0, k, 0, 0)), blk],
        out_specs=blk,
        compiler_params=_cp(1), name="fft_mid",
    )(*mats, spectra, a5)


def _fft_spec_kernel(*refs, passes):
    n_mat = 1 if passes == 1 else 2
    gh_ref = refs[0]
    gl_ref = refs[1] if passes > 1 else None
    inv_ref, a_ref, o_ref = refs[n_mat:]
    n_filt, _, kb, nb, C = a_ref.shape
    for kk in range(kb):
        for p in range(n_filt):
            x = a_ref[p, :, kk].reshape(2 * nb, C)
            X = _mm_split(gh_ref[kk], None if gl_ref is None else gl_ref[kk], x, passes) * inv_ref[p]
            o_ref[p, :, kk] = X.reshape(2, nb, C)


def fft_filter_spectrum(g_hi, g_lo, inv_norm, a5, passes):
    P, _, Na, Nb, C = a5.shape
    kb = FFT_MID_K1
    mats = (g_hi,) if passes == 1 else (g_hi, g_lo)
    gspec = pl.BlockSpec((kb, 2 * Nb, 2 * Nb), lambda k: (k, 0, 0))
    blk = pl.BlockSpec((P, 2, kb, Nb, C), lambda k: (0, 0, k, 0, 0))
    return pl.pallas_call(
        functools.partial(_fft_spec_kernel, passes=passes),
        out_shape=jax.ShapeDtypeStruct(a5.shape, F32),
        grid=(Na // kb,),
        in_specs=[gspec] * len(mats) + [pl.BlockSpec((P, 1, C), lambda k: (0, 0, 0)), blk],
        out_specs=blk,
        compiler_params=_cp(1), name="fft_filter_spectrum",
    )(*mats, inv_norm, a5)


def _fft_factors(n_fft):
    na = 1 << (int(math.log2(n_fft)) // 2)
    return na, n_fft // na


def _fft_tables(L):
    n_fft = 2 * L
    na, nb = _fft_factors(n_fft)
    nh = na // 2
    iota = lambda shape, axis: lax.broadcasted_iota(jnp.int32, shape, axis)

    def cis(num, den):
        ang = (2.0 * math.pi / den) * (num % den).astype(F32)
        return jnp.cos(ang), -jnp.sin(ang)

    def block(diag_sign, pr, pc, cr, ci):
        return jnp.where(pr == pc, cr, jnp.where(pr == 1, diag_sign * ci, -diag_sign * ci))

    def kron_matrix(rows, cols, entry):
        shape = (rows * FFT_ROWS, cols * FFT_ROWS)
        I, J = iota(shape, 0), iota(shape, 1)
        m = entry(I // FFT_ROWS, J // FFT_ROWS)
        return _split_bf16(jnp.where(I % FFT_ROWS == J % FFT_ROWS, m, 0.0))

    def fwd_entry(i, j):
        cr, ci = cis((i % na) * (j % nh), na)
        return block(1, i // na, j // nh, cr, ci)

    def flt_entry(i, j):
        cr, ci = cis((i % na) * j, na)
        return jnp.where(i // na == 0, cr, ci)

    def inv_entry(i, j):
        cr, ci = cis((i % nh) * (j % na), na)
        return block(-1, i // nh, j // na, cr, ci) / n_fft

    gshape = (na, 2 * nb, 2 * nb)
    k1, gi_, gj_ = iota(gshape, 0), iota(gshape, 1), iota(gshape, 2)
    cr, ci = cis((gi_ % nb) * (gj_ % nb) * na + k1 * (gj_ % nb), n_fft)
    g = _split_bf16(block(1, gi_ // nb, gj_ // nb, cr, ci))
    cr, ci = cis((gj_ % nb) * (gi_ % nb) * na + k1 * (gi_ % nb), n_fft)
    ginv = _split_bf16(block(-1, gi_ // nb, gj_ // nb, cr, ci))

    C = HY_WIDTH
    n = jnp.arange(n_fft, dtype=jnp.int32)
    pos = jnp.where(n < L, n, jnp.where(n == L, 0, n_fft - n)).astype(F32)[:, None]
    t = pos * (1.0 / (L - 1))
    f = jnp.linspace(1e-4, HY_BANDS - 1, HY_BANDS, dtype=F32)
    ang = (2.0 * math.pi / L) * pos * f[None, :]
    z2 = jnp.concatenate([t, jnp.cos(ang), -jnp.sin(ang), jnp.zeros((n_fft, HY_EMB_PAD - HY_EMB), F32)], axis=-1)
    deltas = jnp.abs(jnp.linspace(HY_MIN_DECAY, HY_MAX_DECAY, C, dtype=F32))
    decay2 = jnp.exp(-t * deltas)
    return dict(na=na, nb=nb, fwd=kron_matrix(2 * na, na, fwd_entry), flt=kron_matrix(2 * na, na, flt_entry),
                inv=kron_matrix(na, 2 * na, inv_entry), g=g, ginv=ginv, z2=z2, decay2=decay2)


def hyena_filter_spectra(w1, b1, w2, b2, w3, freq, tabs, passes):
    C = HY_WIDTH
    na, nb = tabs["na"], tabs["nb"]
    w1p = jnp.pad(w1, ((0, HY_EMB_PAD - HY_EMB), (0, 0)))
    w3_dir = jnp.transpose(w3.reshape(HY_HIDDEN, HY_ORDER, 2, C), (2, 0, 1, 3)).reshape(2, HY_HIDDEN, HY_ORDER * C)
    filt, s = hyena_filter_mlp(tabs["z2"], w1p, b1[None], w2, b2[None], w3_dir, freq[None], tabs["decay2"])
    inv_norm = (1.0 / s).reshape(HY_ORDER, 1, C)
    a = fft_level1(*tabs["flt"], filt.reshape(HY_ORDER, na, nb, C), passes,
                   out_dtype=BF16 if passes == 1 else F32)
    return fft_filter_spectrum(*tabs["g"], inv_norm, a.reshape(HY_ORDER, 2, na, nb, C), passes)


def hyena_long_conv_gate(z, xg, bias, spectra, order, tabs, passes):
    B, L, C = z.shape
    na, nb = tabs["na"], tabs["nb"]
    P = B // 2
    nat = lambda a: a.reshape(P, na, nb, C)
    mid_dtype = BF16 if passes == 1 else F32
    a = fft_level1(*tabs["fwd"], nat(z), passes, out_dtype=mid_dtype)
    bv = fft_mid(*tabs["g"], *tabs["ginv"], spectra, order, a.reshape(P, 2, na, nb, C), passes)
    y = fft_level1(*tabs["inv"], bv.reshape(P, 2 * na, nb, C), passes,
                   gate_args=(nat(z), nat(xg), bias[None]))
    return y.reshape(B, L, C)


def hyena_mix(u, row0, L, spectra, conv_w, conv_b, bias, tabs, passes):
    v, x1, x2 = hyena_short_conv(u, conv_w, conv_b[None], row0, L)
    zz = hyena_long_conv_gate(v, x1, bias[0], spectra, 0, tabs, passes)
    return hyena_long_conv_gate(zz, x2, bias[1], spectra, 1, tabs, passes)


FFT_PASSES = 1
FILTER_PASSES = 1


def _rope_tables_t(n):
    rows = n // GRID_W
    r = jnp.repeat(jnp.arange(rows, dtype=F32), GRID_W)
    col = jnp.tile(jnp.arange(GRID_W, dtype=F32), rows)
    axis_dim = HEAD_DIM // 2
    inv = ROPE_THETA ** (-jnp.arange(0, axis_dim, 2, dtype=F32) / axis_dim)
    ang = jnp.concatenate([r[:, None] * inv, col[:, None] * inv], axis=-1)
    return jnp.cos(ang).T, jnp.sin(ang).T


def _layer_weights(l, w_in, gq_qn, gq_kn):
    w = w_in[l]
    d_qk = DA_HEADS * 2 * HEAD_DIM
    gq_kv = GKV_HEADS * HEAD_DIM
    gq_q = GQ_HEADS * HEAD_DIM
    o = np.cumsum([0, d_qk, d_qk, gq_kv, gq_kv, d_qk, gq_q, 3 * HY_WIDTH])
    ka, va, kb, vb, qa, qb, hy = (w[:, o[i]:o[i + 1]] for i in range(7))
    gates = w[:, o[7]:]
    qscale = (HEAD_DIM ** -0.5) * LOG2E
    w_qk = jnp.concatenate([ka, kb, qa, qb], axis=1).T.astype(BF16)
    g_qk = jnp.concatenate([jnp.ones((d_qk,), F32), jnp.tile(gq_kn[l], GKV_HEADS),
                            jnp.full((d_qk,), qscale, F32), jnp.tile(gq_qn[l], GQ_HEADS) * qscale])[:, None]
    n_ka, n_kb, n_qa = d_qk // HEAD_DIM, gq_kv // HEAD_DIM, d_qk // HEAD_DIM
    rms_heads = frozenset(range(n_ka, n_ka + n_kb)) | frozenset(range(n_ka + n_kb + n_qa,
                                                                       n_ka + n_kb + n_qa + gq_q // HEAD_DIM))
    return dict(w_qk=w_qk, g_qk=g_qk, rms_heads=rms_heads, w_va=va.T.astype(BF16), w_vb=vb.T.astype(BF16),
                w_tok=jnp.concatenate([hy, gates], axis=1).astype(BF16))


def kernel(x, c, ctx, c_ctx, w_mod, b_mod, w_in, da_lq1, da_lk1, da_lq2, da_lk2, da_subln, gq_qn, gq_kn,
           hy_conv_w, hy_conv_b, hy_w1, hy_b1, hy_w2, hy_b2, hy_w3, hy_freq, hy_bias, w_pa, w_pb, w_pc, w_o,
           ln1_g, ln1_b, w_up, ffn_conv_w, ffn_conv_b, w_down, ln2_g, ln2_b):
    B, n_lat, D = x.shape
    n_ctx = ctx.shape[1]
    depth = w_in.shape[0]
    alpha = (2 * depth) ** 0.25
    cos_l, sin_l = _rope_tables_t(n_lat)
    cos_t = jnp.concatenate([cos_l, jnp.ones((HALF, n_ctx), F32)], axis=1)
    sin_t = jnp.concatenate([sin_l, jnp.zeros((HALF, n_ctx), F32)], axis=1)
    lat, ctx_span, all_span = (0, n_lat), (n_lat, n_ctx), (0, n_lat + n_ctx)
    tabs_l = _fft_tables(n_lat)
    tabs_c = _fft_tables(n_ctx)
    xc = ctx

    cond = jnp.concatenate([c, c_ctx[None]], axis=0)
    cond = jnp.pad(jax.nn.silu(cond), ((0, 16 - (B + 1) % 16), (0, 0))).astype(BF16)

    for l in range(depth):
        last = l == depth - 1
        lam_init = 0.8 - 0.6 * math.exp(-0.3 * l)
        lam = (jnp.exp(jnp.sum(da_lq1[l] * da_lk1[l])) - jnp.exp(jnp.sum(da_lq2[l] * da_lk2[l])) + lam_init)
        lam = lam.reshape(1).astype(F32)
        mod = matmul(cond, w_mod[l].astype(BF16), F32, tn_pref=1024) + b_mod[l]
        sh1, sc1, g1, sh2, sc2, g2 = [m[:, None, :] for m in jnp.split(mod[:B], 6, axis=-1)]
        mc = [jnp.broadcast_to(m[None, None, :], (B, 1, D)) for m in jnp.split(mod[B], 6)]
        lw = _layer_weights(l, w_in, gq_qn, gq_kn)
        subln = da_subln[l][:, None]
        hy_mlp = (hy_w1[l], hy_b1[l], hy_w2[l], hy_b2[l], hy_w3[l], hy_freq[l])
        spectra_l = hyena_filter_spectra(*hy_mlp, tabs_l, FILTER_PASSES)
        wpa, wpb, wpc, wo = (w[l].astype(BF16) for w in (w_pa, w_pb, w_pc, w_o))
        wup, wdn = w_up[l].astype(BF16), w_down[l].astype(BF16)
        ln1 = (ln1_g[l][None], ln1_b[l][None])
        ln2 = (ln2_g[l][None], ln2_b[l][None])

        q_all, k_tok, k_n2, va, vb, u, gates = proj_all(x, xc, sh1, sc1, mc[0], mc[1], cos_t, sin_t, lw)
        kmax = jnp.sqrt(k_n2[:, :N_KHEADS, 0]).reshape(B * N_KHEADS)
        n_da_maps = DA_HEADS * 2

        def attend(q_span, kv_span):
            da = flash_attention(q_all, k_tok, va, kmax, lam, subln, mode="da", q_span=q_span, kv_span=kv_span,
                                 q_row_block=0, k_col_block=0, khead0=0, post_scale=1.0 - lam_init)
            gq = flash_attention(q_all, k_tok, vb, kmax, lam, subln, mode="gqa", q_span=q_span, kv_span=kv_span,
                                 q_row_block=n_da_maps * HEAD_DIM // 128, k_col_block=n_da_maps * HEAD_DIM // 128,
                                 khead0=n_da_maps)
            return da, gq

        hy_args = (hy_conv_w[l], hy_conv_b[l], hy_bias[l])
        a_l, b_l = attend(lat, all_span)
        c_l = hyena_mix(u, *lat, spectra_l, *hy_args, tabs_l, FFT_PASSES)
        x_new, h2 = merge_resid_ln(a_l, b_l, c_l, gates, lat[0], wpa, wpb, wpc, wo, x, g1, *ln1,
                                   sh2, sc2, alpha)
        x_new = conv_ffn_resid_ln(h2, wup, ffn_conv_w[l], ffn_conv_b[l][None], wdn, x_new, g2, *ln2, alpha)

        if not last:
            spectra_c = hyena_filter_spectra(*hy_mlp, tabs_c, FILTER_PASSES)
            a_c, b_c = attend(ctx_span, ctx_span)
            c_c = hyena_mix(u, *ctx_span, spectra_c, *hy_args, tabs_c, FFT_PASSES)
            xc, hc2 = merge_resid_ln(a_c, b_c, c_c, gates, ctx_span[0], wpa, wpb, wpc, wo, xc, mc[2],
                                     *ln1, mc[3], mc[4], alpha)
            xc = conv_ffn_resid_ln(hc2, wup, ffn_conv_w[l], ffn_conv_b[l][None], wdn, xc, mc[5], *ln2, alpha)
        x = x_new
    return x
```

```python
import functools
import math

import numpy as np
import jax
import jax.numpy as jnp
from jax import lax
from jax.experimental import pallas as pl
from jax.experimental.pallas import tpu as pltpu

F32 = jnp.float32
BF16 = jnp.bfloat16

HEAD_DIM = 64
HALF = HEAD_DIM // 2
GRID_W = 64
ROPE_THETA = 10000.0
DA_HEADS = 4
GQ_HEADS = 8
GKV_HEADS = 2
HY_WIDTH = 512
HY_ORDER = 2
HY_EMB = 33
HY_EMB_PAD = 128
HY_BANDS = (HY_EMB - 1) // 2
HY_HIDDEN = 64
HY_MIN_DECAY = math.log(1e-2) / 1.5
HY_MAX_DECAY = math.log(1e-2) / 0.3
LN_EPS = 1e-6
LOG2E = 1.4426950408889634
NEG_BIG = -1e30

VMEM_LIMIT = 56 * 1024 * 1024


def _cp(n_axes):
    return pltpu.CompilerParams(dimension_semantics=("arbitrary",) * n_axes,
                                vmem_limit_bytes=VMEM_LIMIT)


def _tile(n, pref, mult):
    if n <= pref:
        return n
    t = (pref // mult) * mult
    while t >= mult:
        if n % t == 0:
            return t
        t -= mult
    return n


def _ln(x):
    mu = jnp.mean(x, axis=-1, keepdims=True)
    xc = x - mu
    return xc * lax.rsqrt(jnp.mean(xc * xc, axis=-1, keepdims=True) + LN_EPS)


N_KHEADS = DA_HEADS * 2 + GKV_HEADS
KNORM_ROWS = 16


def _proj_all_kernel(x_ref, xc_ref, sh_ref, sc_ref, shc_ref, scc_ref, cos_ref, sin_ref, wqk_ref, gqk_ref,
                     wva_ref, wvb_ref, wtok_ref,
                     q_ref, k_ref, kn_ref, va_ref, vb_ref, u_ref, gt_ref, kt_sc,
                     *, n_lat_tiles, rms_heads, n_hy):
    i = pl.program_id(1)

    @pl.when(i == 0)
    def _():
        kn_ref[...] = jnp.zeros(kn_ref.shape, F32)

    is_latent = i < n_lat_tiles
    xin = jnp.where(is_latent, x_ref[0], xc_ref[0])
    shift = jnp.where(is_latent, sh_ref[0], shc_ref[0])
    scale = jnp.where(is_latent, sc_ref[0], scc_ref[0])
    h = (_ln(xin) * (1.0 + scale) + shift).astype(BF16)
    nt = (((1,), (1,)), ((), ()))

    acc = lax.dot_general(wqk_ref[...], h, nt, preferred_element_type=F32)
    va = lax.dot_general(wva_ref[...], h, nt, preferred_element_type=F32)
    vb = lax.dot_general(wvb_ref[...], h, nt, preferred_element_type=F32)
    tok = jnp.dot(h, wtok_ref[...], preferred_element_type=F32)
    va_ref[0] = va.astype(va_ref.dtype)
    vb_ref[0] = vb.astype(vb_ref.dtype)
    c, s = cos_ref[...], sin_ref[...]
    n_k_rows = kt_sc.shape[0]
    for hd in range(acc.shape[0] // HEAD_DIM):
        lo = hd * HEAD_DIM
        xh = acc[lo:lo + HEAD_DIM]
        if hd in rms_heads:
            xh = xh * lax.rsqrt(jnp.mean(xh * xh, axis=0, keepdims=True) + LN_EPS)
        xh = xh * gqk_ref[lo:lo + HEAD_DIM]
        x1, x2 = xh[:HALF], xh[HALF:]
        r1, r2 = x1 * c - x2 * s, x1 * s + x2 * c
        if lo < n_k_rows:
            kt_sc[lo:lo + HALF] = r1
            kt_sc[lo + HALF:lo + HEAD_DIM] = r2
        else:
            q_ref[0, lo - n_k_rows:lo - n_k_rows + HALF] = r1.astype(q_ref.dtype)
            q_ref[0, lo - n_k_rows + HALF:lo - n_k_rows + HEAD_DIM] = r2.astype(q_ref.dtype)

    k_bf = kt_sc[...].astype(BF16)
    k_ref[0] = k_bf.astype(F32).T.astype(BF16)
    k_sq = k_bf.astype(F32) ** 2
    for hd in range(n_k_rows // HEAD_DIM):
        n2 = jnp.sum(k_sq[hd * HEAD_DIM:(hd + 1) * HEAD_DIM], axis=0, keepdims=True)
        kn_ref[0, hd:hd + 1] = jnp.maximum(kn_ref[0, hd:hd + 1], jnp.max(n2, axis=1, keepdims=True))

    u_ref[0] = tok[:, :n_hy].astype(u_ref.dtype)
    gt_ref[0] = jax.nn.sigmoid(tok[:, n_hy:]).astype(gt_ref.dtype)


def proj_all(x, xc, shift, scale, shift_c, scale_c, cos_t, sin_t, lw):
    B, N, D = x.shape
    Nc = xc.shape[1]
    ntot = N + Nc
    tt = _tile(Nc, 256, 128)
    assert N % tt == 0
    nl = N // tt
    n_k_rows = N_KHEADS * HEAD_DIM
    n_q_rows = lw["w_qk"].shape[0] - n_k_rows
    n_va, n_vb = lw["w_va"].shape[0], lw["w_vb"].shape[0]
    n_tok = lw["w_tok"].shape[1]
    n_hy = 3 * HY_WIDTH
    per_b = lambda b, i: (b, 0, 0)
    const = lambda b, i: (0, 0)
    resident = dict(pipeline_mode=pl.Buffered(1))
    feat = lambda rows: pl.BlockSpec((1, rows, tt), lambda b, i: (b, 0, i))
    tokm = lambda cols: pl.BlockSpec((1, tt, cols), lambda b, i: (b, i, 0))
    sds = jax.ShapeDtypeStruct
    return pl.pallas_call(
        functools.partial(_proj_all_kernel, n_lat_tiles=nl, rms_heads=lw["rms_heads"], n_hy=n_hy),
        out_shape=(sds((B, n_q_rows, ntot), BF16), sds((B, ntot, n_k_rows), BF16),
                   sds((B, KNORM_ROWS, 128), F32), sds((B, n_va, ntot), BF16), sds((B, n_vb, ntot), BF16),
                   sds((B, ntot, n_hy), BF16), sds((B, ntot, n_tok - n_hy), BF16)),
        grid=(B, ntot // tt),
        in_specs=[pl.BlockSpec((1, tt, D), lambda b, i: (b, jnp.minimum(i, nl - 1), 0)),
                  pl.BlockSpec((1, tt, D), lambda b, i: (b, jnp.maximum(i - nl, 0), 0)),
                  pl.BlockSpec((1, 1, D), per_b), pl.BlockSpec((1, 1, D), per_b),
                  pl.BlockSpec((1, 1, D), per_b), pl.BlockSpec((1, 1, D), per_b),
                  pl.BlockSpec((HALF, tt), lambda b, i: (0, i)), pl.BlockSpec((HALF, tt), lambda b, i: (0, i)),
                  pl.BlockSpec(lw["w_qk"].shape, const, **resident), pl.BlockSpec((lw["w_qk"].shape[0], 1), const),
                  pl.BlockSpec(lw["w_va"].shape, const, **resident),
                  pl.BlockSpec(lw["w_vb"].shape, const, **resident),
                  pl.BlockSpec(lw["w_tok"].shape, const, **resident)],
        out_specs=(feat(n_q_rows), tokm(n_k_rows), pl.BlockSpec((1, KNORM_ROWS, 128), per_b),
                   feat(n_va), feat(n_vb), tokm(n_hy), tokm(n_tok - n_hy)),
        scratch_shapes=[pltpu.VMEM((n_k_rows, tt), F32)],
        compiler_params=_cp(2), name="proj_all",
    )(x, xc, shift, scale, shift_c, scale_c, cos_t, sin_t, lw["w_qk"], lw["g_qk"], lw["w_va"], lw["w_vb"],
      lw["w_tok"])


def _resid_ln_tail(x, y, gate, g, b, alpha):
    return _ln(alpha * x + gate * y) * g + b


def _mm_kernel(a_ref, w_ref, o_ref, *, act):
    acc = jnp.dot(a_ref[...], w_ref[...], preferred_element_type=F32)
    if act == "sigmoid":
        acc = jax.nn.sigmoid(acc)
    o_ref[...] = acc.astype(o_ref.dtype)


def matmul(a, w, out_dtype, act=None, tm_pref=1024, tn_pref=512):
    M, K = a.shape
    Nn = w.shape[1]
    tm = _tile(M, tm_pref, 16)
    tn = _tile(Nn, tn_pref, 128)
    return pl.pallas_call(
        functools.partial(_mm_kernel, act=act),
        out_shape=jax.ShapeDtypeStruct((M, Nn), out_dtype),
        grid=(M // tm, Nn // tn),
        in_specs=[pl.BlockSpec((tm, K), lambda i, j: (i, 0)),
                  pl.BlockSpec((K, tn), lambda i, j: (0, j))],
        out_specs=pl.BlockSpec((tm, tn), lambda i, j: (i, j)),
        compiler_params=_cp(2), name="matmul",
    )(a, w)


FLASH_MIN_DENOM = 2.0 ** -60
FLASH_UNROLL = 4


def _flash_kernel(lam_ref, kmax_ref, q_ref, k_ref, v_ref, g_ref, o_ref, q_sc, p_buf, sh_sc, acc_sc, l_sc,
                  *, mode, nj, tk, post_scale, n_kheads, khead0):
    b, h = pl.program_id(0), pl.program_id(1)
    qf = q_ref[0].astype(F32)
    tq = qf.shape[1]
    zero = jnp.zeros((HEAD_DIM, tq), F32)
    qa, qb = qf[:HEAD_DIM], qf[HEAD_DIM:]
    if mode == "da":
        q_sc[0] = jnp.concatenate([qa, zero], axis=0).astype(BF16)
        q_sc[1] = jnp.concatenate([zero, qb], axis=0).astype(BF16)
        k_heads = (khead0 + 2 * h, khead0 + 2 * h + 1)
    else:
        group = h // 2
        for mi, qh in enumerate((qa, qb)):
            q_sc[mi] = jnp.where(group == 0, jnp.concatenate([qh, zero], axis=0),
                                 jnp.concatenate([zero, qh], axis=0)).astype(BF16)
        k_heads = (khead0 + group, khead0 + group)
    for mi, qh in enumerate((qa, qb)):
        q_norm = jnp.sqrt(jnp.sum(qh * qh, axis=0, keepdims=True))
        sh_sc[mi] = q_norm * kmax_ref[b * n_kheads + k_heads[mi]]

    def chunk(j):
        start = j * tk
        return pl.ds(start if isinstance(start, int) else pl.multiple_of(start, tk), tk)

    def stage_exp(j, slot):
        kb = k_ref[0, chunk(j), :]
        for mi in range(2):
            s = jnp.dot(kb, q_sc[mi], preferred_element_type=F32)
            p = jnp.exp2(s - sh_sc[mi])
            p_buf[slot, mi] = p.astype(BF16)
            l_sc[mi] += jnp.sum(p.reshape(tk // 8, 8, tq), axis=0)

    def stage_values(j, slot):
        vb = v_ref[0, :, chunk(j)]
        for mi in range(2):
            acc_sc[mi] += jnp.dot(vb, p_buf[slot, mi], preferred_element_type=F32)

    def accumulate():
        acc_sc[...] = jnp.zeros(acc_sc.shape, F32)
        l_sc[...] = jnp.zeros(l_sc.shape, F32)
        stage_exp(0, 0)
        steady = list(range(1, nj))
        while len(steady) % FLASH_UNROLL:
            t = steady.pop(0)
            stage_exp(t, t % 2)
            stage_values(t - 1, (t - 1) % 2)
        if steady:
            t0 = steady[0]

            def body(i, carry):
                for d in range(FLASH_UNROLL):
                    stage_exp(t0 + FLASH_UNROLL * i + d, (t0 + d) % 2)
                    stage_values(t0 + FLASH_UNROLL * i + d - 1, (t0 + d - 1) % 2)
                return carry

            lax.fori_loop(0, len(steady) // FLASH_UNROLL, body, 0)
        stage_values(nj - 1, (nj - 1) % 2)

    denominators = lambda: [jnp.sum(l_sc[mi], axis=0, keepdims=True) for mi in range(2)]
    accumulate()
    denom_min = jnp.min(jnp.minimum(*denominators()))

    @pl.when(jnp.logical_not(denom_min >= FLASH_MIN_DENOM))
    def _():
        for mi in range(2):
            def max_body(j, m, mi=mi):
                s = jnp.dot(k_ref[0, chunk(j), :], q_sc[mi], preferred_element_type=F32)
                return jnp.maximum(m, jnp.max(s, axis=0, keepdims=True))

            sh_sc[mi] = lax.fori_loop(0, nj, max_body, jnp.full((1, tq), NEG_BIG, F32))
        accumulate()

    outs = [acc_sc[mi] / l for mi, l in enumerate(denominators())]
    if mode == "da":
        o = outs[0] - lam_ref[0] * outs[1]
        o = o * lax.rsqrt(jnp.mean(o * o, axis=0, keepdims=True) + LN_EPS)
        o = o * (g_ref[...] * post_scale)
    else:
        o = jnp.concatenate(outs, axis=0)
    o_ref[0] = o.T.astype(o_ref.dtype)


def flash_attention(q_t, k_tok, v_aug, kmax, lam, gain, *, mode, q_span, kv_span, q_row_block, k_col_block,
                    khead0, post_scale=1.0, tk_pref=768):
    B = q_t.shape[0]
    q0, Nq = q_span
    k0, Nk = kv_span
    tq = _tile(Nq, 1024, 128)
    tk = _tile(Nk, tk_pref, 128)
    assert q0 % tq == 0 and k0 % Nk == 0
    nj = Nk // tk
    dv = 2 * HEAD_DIM if mode == "da" else HEAD_DIM
    n_kheads = kmax.shape[0] // B
    if mode == "da":
        k_map = lambda b, h, i: (b, k0 // Nk, k_col_block + h)
        v_map = lambda b, h, i: (b, h, k0 // Nk)
    else:
        k_map = lambda b, h, i: (b, k0 // Nk, k_col_block)
        v_map = lambda b, h, i: (b, h // 2, k0 // Nk)
    return pl.pallas_call(
        functools.partial(_flash_kernel, mode=mode, nj=nj, tk=tk, post_scale=post_scale,
                          n_kheads=n_kheads, khead0=khead0),
        out_shape=jax.ShapeDtypeStruct((B, Nq, 4 * 128), BF16),
        grid=(B, 4, Nq // tq),
        in_specs=[pl.BlockSpec(memory_space=pltpu.SMEM),
                  pl.BlockSpec(memory_space=pltpu.SMEM),
                  pl.BlockSpec((1, 128, tq), lambda b, h, i: (b, h + q_row_block, i + q0 // tq)),
                  pl.BlockSpec((1, Nk, 128), k_map),
                  pl.BlockSpec((1, dv, Nk), v_map),
                  pl.BlockSpec((128, 1), lambda b, h, i: (0, 0))],
        out_specs=pl.BlockSpec((1, tq, 128), lambda b, h, i: (b, i, h)),
        scratch_shapes=[pltpu.VMEM((2, 128, tq), BF16),
                        pltpu.VMEM((2, 2, tk, tq), BF16),
                        pltpu.VMEM((2, 1, tq), F32),
                        pltpu.VMEM((2, dv, tq), F32),
                        pltpu.VMEM((2, 8, tq), F32)],
        compiler_params=_cp(3), name="flash_" + mode,
    )(lam, kmax, q_t, k_tok, v_aug, gain)


def _shift_rows(x, prev_row, next_row):
    T = x.shape[0]
    row = lax.broadcasted_iota(jnp.int32, x.shape, 0)
    xp = jnp.where(row == 0, prev_row, pltpu.roll(x, 1, axis=0))
    xn = jnp.where(row == T - 1, next_row, pltpu.roll(x, T - 1, axis=0))
    return xp, xn


def _conv3_block(x_ref, p_ref, n_ref, w_ref, b_ref, halo):
    i = pl.program_id(1)
    last = pl.num_programs(1) - 1
    x = x_ref[0].astype(F32)
    prev_row = jnp.where(i > 0, p_ref[0].astype(F32)[halo - 1:halo], 0.0)
    next_row = jnp.where(i < last, n_ref[0].astype(F32)[0:1], 0.0)
    xp, xn = _shift_rows(x, prev_row, next_row)
    return xp * w_ref[0:1] + x * w_ref[1:2] + xn * w_ref[2:3] + b_ref[...]


def _halo_specs(tr, tc, halo, row0, n_rows, col_of):
    per = tr // halo
    n_halo = n_rows // halo
    m0, h0 = row0 // tr, row0 // halo
    return [pl.BlockSpec((1, tr, tc), lambda b, i, j: (b, m0 + i, col_of(j))),
            pl.BlockSpec((1, halo, tc), lambda b, i, j: (b, h0 + jnp.maximum(i * per - 1, 0), col_of(j))),
            pl.BlockSpec((1, halo, tc),
                         lambda b, i, j: (b, h0 + jnp.minimum((i + 1) * per, n_halo - 1), col_of(j)))]


HYENA_HALO = 16


def _dwconv_kernel(x_ref, p_ref, n_ref, w_ref, b_ref, v_ref, x1_ref, x2_ref):
    y = _conv3_block(x_ref, p_ref, n_ref, w_ref, b_ref, HYENA_HALO)
    C = v_ref.shape[2]
    v_ref[0] = y[:, :C].astype(v_ref.dtype)
    x1_ref[0] = y[:, C:2 * C].astype(x1_ref.dtype)
    x2_ref[0] = y[:, 2 * C:].astype(x2_ref.dtype)


def hyena_short_conv(u, w, b, row0, L):
    B, _, C3 = u.shape
    C = C3 // 3
    tr = _tile(L, 512, HYENA_HALO)
    assert row0 % tr == 0
    zero = lambda j: 0
    out = jax.ShapeDtypeStruct((B, L, C), BF16)
    ospec = pl.BlockSpec((1, tr, C), lambda b, i, j: (b, i, 0))
    return pl.pallas_call(
        _dwconv_kernel,
        out_shape=(out, out, out),
        grid=(B, L // tr, 1),
        in_specs=_halo_specs(tr, C3, HYENA_HALO, row0, L, zero) + [
            pl.BlockSpec((3, C3), lambda b, i, j: (0, 0)),
            pl.BlockSpec((1, C3), lambda b, i, j: (0, 0))],
        out_specs=(ospec, ospec, ospec),
        compiler_params=_cp(3), name="hyena_short_conv",
    )(u, u, u, w, b)


FFN_HALO = 16
FFN_CHUNK = 256


def _ffn_kernel(h_ref, hp_ref, hn_ref, wup_ref, cw_ref, cb_ref, wdn_ref, x_ref, gate_ref, g_ref, b_ref, o_ref,
                act_sc, *, alpha):
    i = pl.program_id(1)
    tm = h_ref.shape[1]
    dff = wdn_ref.shape[0]
    rows = tm + 2 * FFN_HALO
    h_prev = jnp.where(i == 0, jnp.zeros_like(hp_ref[0]), hp_ref[0])
    h_next = jnp.where(i == pl.num_programs(1) - 1, jnp.zeros_like(hn_ref[0]), hn_ref[0])
    h_ext = jnp.concatenate([h_prev, h_ref[0], h_next], axis=0)

    def conv_cols(c0):
        u = jnp.dot(h_ext, wup_ref[:, c0:c0 + FFN_CHUNK], preferred_element_type=F32)
        mid = slice(FFN_HALO, FFN_HALO + tm)
        u_prev = pltpu.roll(u, 1, axis=0)[mid]
        u_next = pltpu.roll(u, rows - 1, axis=0)[mid]
        w = cw_ref[:, c0:c0 + FFN_CHUNK]
        return u_prev * w[0:1] + u[mid] * w[1:2] + u_next * w[2:3] + cb_ref[:, c0:c0 + FFN_CHUNK]

    for c0 in range(0, dff, FFN_CHUNK):
        val = conv_cols(c0)
        gate = conv_cols(dff + c0)
        act_sc[:, c0:c0 + FFN_CHUNK] = (gate * jax.nn.sigmoid(gate) * val).astype(BF16)
    y = jnp.dot(act_sc[...], wdn_ref[...], preferred_element_type=F32)
    o_ref[0] = _resid_ln_tail(x_ref[0], y, gate_ref[0], g_ref[...], b_ref[...], alpha)


def conv_ffn_resid_ln(h, w_up, conv_w, conv_b, w_down, x, gate, g, b, alpha):
    B, N, D = h.shape
    dff = w_down.shape[0]
    tm = _tile(N, 512, FFN_HALO)
    per = tm // FFN_HALO
    n_halo = N // FFN_HALO
    row = lambda b_, i: (b_, i, 0)
    per_b = lambda b_, i: (b_, 0, 0)
    const2 = lambda b_, i: (0, 0)
    resident = dict(pipeline_mode=pl.Buffered(1))
    return pl.pallas_call(
        functools.partial(_ffn_kernel, alpha=alpha),
        out_shape=jax.ShapeDtypeStruct((B, N, D), F32),
        grid=(B, N // tm),
        in_specs=[pl.BlockSpec((1, tm, D), row),
                  pl.BlockSpec((1, FFN_HALO, D), lambda b_, i: (b_, jnp.maximum(i * per - 1, 0), 0)),
                  pl.BlockSpec((1, FFN_HALO, D), lambda b_, i: (b_, jnp.minimum((i + 1) * per, n_halo - 1), 0)),
                  pl.BlockSpec((D, 2 * dff), const2, **resident),
                  pl.BlockSpec((3, 2 * dff), const2), pl.BlockSpec((1, 2 * dff), const2),
                  pl.BlockSpec((dff, D), const2, **resident),
                  pl.BlockSpec((1, tm, D), row), pl.BlockSpec((1, 1, D), per_b),
                  pl.BlockSpec((1, D), const2), pl.BlockSpec((1, D), const2)],
        out_specs=pl.BlockSpec((1, tm, D), row),
        scratch_shapes=[pltpu.VMEM((tm, dff), BF16)],
        compiler_params=_cp(2), name="conv_ffn_resid_ln",
    )(h, h, h, w_up, conv_w, conv_b, w_down, x, gate, g, b)


def _merge_kernel(a_ref, b_ref, c_ref, gt_ref, wpa_ref, wpb_ref, wpc_ref, wo_ref, x_ref, gate_ref,
                  g_ref, bb_ref, sh_ref, sc_ref, ox_ref, oh_ref, *, alpha):
    D = wo_ref.shape[0]
    tm = x_ref.shape[1]
    n_sub = 2 if tm % 32 == 0 else 1
    for r in range(n_sub):
        rows = slice(r * tm // n_sub, (r + 1) * tm // n_sub)
        gt = gt_ref[0, rows]
        m = gt[:, :D].astype(F32) * jnp.dot(a_ref[0, rows], wpa_ref[...], preferred_element_type=F32)
        m = m + gt[:, D:2 * D].astype(F32) * jnp.dot(b_ref[0, rows], wpb_ref[...], preferred_element_type=F32)
        m = m + gt[:, 2 * D:].astype(F32) * jnp.dot(c_ref[0, rows].astype(BF16), wpc_ref[...],
                                                    preferred_element_type=F32)
        y = jnp.dot(m.astype(BF16), wo_ref[...], preferred_element_type=F32)
        xn = _resid_ln_tail(x_ref[0, rows], y, gate_ref[0], g_ref[...], bb_ref[...], alpha)
        ox_ref[0, rows] = xn
        oh_ref[0, rows] = (_ln(xn) * (1.0 + sc_ref[0]) + sh_ref[0]).astype(oh_ref.dtype)


def merge_resid_ln(a, b, c, gates, tok0, wpa, wpb, wpc, wo, x, gate, g, bb, sh, sc, alpha):
    B, N, D = x.shape
    W = a.shape[2]
    tm = _tile(N, 512, 16)
    assert tok0 % tm == 0
    row = lambda b_, i: (b_, i, 0)
    per_b = lambda b_, i: (b_, 0, 0)
    const2 = lambda b_, i: (0, 0)
    return pl.pallas_call(
        functools.partial(_merge_kernel, alpha=alpha),
        out_shape=(jax.ShapeDtypeStruct((B, N, D), F32), jax.ShapeDtypeStruct((B, N, D), BF16)),
        grid=(B, N // tm),
        in_specs=[pl.BlockSpec((1, tm, W), row), pl.BlockSpec((1, tm, W), row), pl.BlockSpec((1, tm, W), row),
                  pl.BlockSpec((1, tm, 3 * D), lambda b_, i: (b_, tok0 // tm + i, 0)),
                  pl.BlockSpec((W, D), const2), pl.BlockSpec((W, D), const2), pl.BlockSpec((W, D), const2),
                  pl.BlockSpec((D, D), const2),
                  pl.BlockSpec((1, tm, D), row), pl.BlockSpec((1, 1, D), per_b),
                  pl.BlockSpec((1, D), const2), pl.BlockSpec((1, D), const2),
                  pl.BlockSpec((1, 1, D), per_b), pl.BlockSpec((1, 1, D), per_b)],
        out_specs=(pl.BlockSpec((1, tm, D), row), pl.BlockSpec((1, tm, D), row)),
        compiler_params=_cp(2), name="merge_resid_ln",
    )(a, b, c, gates, wpa, wpb, wpc, wo, x, gate, g, bb, sh, sc)


def _split_bf16(x):
    hi = x.astype(BF16)
    lo = (x - hi.astype(F32)).astype(BF16)
    return hi, lo


def _dot3(a, b):
    ah, al = _split_bf16(a)
    bh, bl = _split_bf16(b)
    d = functools.partial(jnp.dot, preferred_element_type=F32)
    return d(ah, bh) + (d(ah, bl) + d(al, bh))


def _filter_kernel(z_ref, w1_ref, b1_ref, w2_ref, b2_ref, w3_ref, fr_ref, dec_ref, h_ref, s_ref, *, zero_block):
    i = pl.program_id(0)

    @pl.when(i == 0)
    def _():
        s_ref[...] = jnp.zeros(s_ref.shape, F32)

    fr = fr_ref[...]
    hid = jnp.sin(fr * (_dot3(z_ref[...], w1_ref[...]) + b1_ref[...]))
    hid = jnp.sin(fr * (_dot3(hid, w2_ref[...]) + b2_ref[...]))
    h = _dot3(hid, w3_ref[0])
    dec = dec_ref[...]
    C = dec.shape[1]
    n_ord = h.shape[1] // C
    h = h * jnp.concatenate([dec] * n_ord, axis=1)
    s_ref[...] += jnp.sum(jnp.abs(h), axis=0, keepdims=True)
    row = lax.broadcasted_iota(jnp.int32, h.shape, 0)
    h = jnp.where((row == 0) & (i == zero_block), 0.0, h)
    for o in range(n_ord):
        h_ref[o] = h[:, o * C:(o + 1) * C]


def hyena_filter_mlp(z2, w1, b1, w2, b2, w3_dir, freq, decay2):
    n2, E = z2.shape
    Hh = w2.shape[0]
    OC = w3_dir.shape[2]
    C = decay2.shape[1]
    n_ord = OC // C
    L = n2 // 2
    tr = _tile(L, 512, 8)
    nblk = n2 // tr
    c2 = lambda i: (0, 0)
    return pl.pallas_call(
        functools.partial(_filter_kernel, zero_block=L // tr),
        out_shape=(jax.ShapeDtypeStruct((n_ord, n2, C), F32), jax.ShapeDtypeStruct((1, OC), F32)),
        grid=(nblk,),
        in_specs=[pl.BlockSpec((tr, E), lambda i: (i, 0)), pl.BlockSpec((E, Hh), c2), pl.BlockSpec((1, Hh), c2),
                  pl.BlockSpec((Hh, Hh), c2), pl.BlockSpec((1, Hh), c2),
                  pl.BlockSpec((1, Hh, OC), lambda i: (i // (nblk // 2), 0, 0)),
                  pl.BlockSpec((1, Hh), c2), pl.BlockSpec((tr, C), lambda i: (i, 0))],
        out_specs=(pl.BlockSpec((n_ord, tr, C), lambda i: (0, i, 0)), pl.BlockSpec((1, OC), c2)),
        compiler_params=_cp(1), name="hyena_filter_mlp",
    )(z2, w1, b1, w2, b2, w3_dir, freq, decay2)


def _mm_split(m_hi, m_lo, x, passes):
    d = functools.partial(jnp.dot, preferred_element_type=F32)
    if passes == 1:
        return d(m_hi, x.astype(BF16))
    xh, xl = _split_bf16(x)
    return d(m_hi, xh) + (d(m_lo, xh) + d(m_hi, xl))


FFT_ROWS = 8
FFT_BLOCK = 16


def _level1_apply(m_refs, x, r_out, passes):
    r_in, _, C = x.shape
    x = x.astype(F32)
    m_hi = m_refs[0][...]
    m_lo = m_refs[1][...] if passes > 1 else None
    groups = []
    for g in range(FFT_BLOCK // FFT_ROWS):
        xg = x[:, g * FFT_ROWS:(g + 1) * FFT_ROWS].reshape(r_in * FFT_ROWS, C)
        groups.append(_mm_split(m_hi, m_lo, xg, passes).reshape(r_out, FFT_ROWS, C))
    return jnp.concatenate(groups, axis=1)


def _level1_kernel(*refs, passes):
    n_mat = 1 if passes == 1 else 2
    x_ref, o_ref = refs[n_mat:]
    o_ref[0] = _level1_apply(refs[:n_mat], x_ref[0], o_ref.shape[1], passes).astype(o_ref.dtype)


def _level1_gate_kernel(*refs, passes):
    n_mat = 1 if passes == 1 else 2
    x_ref, z_ref, xg_ref, bias_ref, o_ref = refs[n_mat:]
    y = _level1_apply(refs[:n_mat], x_ref[0], o_ref.shape[1], passes)
    o_ref[0] = (xg_ref[0].astype(F32) * (y + bias_ref[...] * z_ref[0].astype(F32))).astype(o_ref.dtype)


def fft_level1(m_hi, m_lo, x, passes, gate_args=None, out_dtype=F32):
    P, R_in, Nb, C = x.shape
    R_out = m_hi.shape[0] // FFT_ROWS
    mats = (m_hi,) if passes == 1 else (m_hi, m_lo)
    mspec = pl.BlockSpec(m_hi.shape, lambda p, j: (0, 0), pipeline_mode=pl.Buffered(1))
    xspec = pl.BlockSpec((1, R_in, FFT_BLOCK, C), lambda p, j: (p, 0, j, 0))
    ospec = pl.BlockSpec((1, R_out, FFT_BLOCK, C), lambda p, j: (p, 0, j, 0))
    if gate_args is None:
        kern = functools.partial(_level1_kernel, passes=passes)
        ins, specs = (*mats, x), [mspec] * len(mats) + [xspec]
    else:
        z, xg, bias = gate_args
        kern = functools.partial(_level1_gate_kernel, passes=passes)
        ins = (*mats, x, z, xg, bias)
        specs = [mspec] * len(mats) + [xspec, ospec, ospec, pl.BlockSpec((1, C), lambda p, j: (0, 0))]
    return pl.pallas_call(
        kern, out_shape=jax.ShapeDtypeStruct((P, R_out, Nb, C), out_dtype),
        grid=(P, Nb // FFT_BLOCK), in_specs=specs, out_specs=ospec,
        compiler_params=_cp(2), name="fft_level1",
    )(*ins)


FFT_MID_K1 = 2


def _fft_mid_kernel(*refs, passes):
    n_mat = 2 if passes == 1 else 4
    gh_ref, gih_ref = refs[0], refs[1]
    gl_ref, gil_ref = (refs[2], refs[3]) if passes > 1 else (None, None)
    h_ref, a_ref, o_ref = refs[n_mat:]
    n_pairs, _, kb, nb, C = a_ref.shape
    for kk in range(kb):
        hr, hi = h_ref[0, 0, kk], h_ref[0, 1, kk]
        for p in range(n_pairs):
            x = a_ref[p, :, kk].reshape(2 * nb, C)
            X = _mm_split(gh_ref[kk], None if gl_ref is None else gl_ref[kk], x, passes)
            xr, xi = X[:nb], X[nb:]
            Y = jnp.concatenate([xr * hr - xi * hi, xr * hi + xi * hr], axis=0)
            Bv = _mm_split(gih_ref[kk], None if gil_ref is None else gil_ref[kk], Y, passes)
            o_ref[p, :, kk] = Bv.reshape(2, nb, C).astype(o_ref.dtype)


def fft_mid(g_hi, g_lo, gi_hi, gi_lo, spectra, order, a5, passes):
    P, _, Na, Nb, C = a5.shape
    kb = FFT_MID_K1
    gspec = pl.BlockSpec((kb, 2 * Nb, 2 * Nb), lambda k: (k, 0, 0))
    mats = (g_hi, gi_hi) if passes == 1 else (g_hi, gi_hi, g_lo, gi_lo)
    blk = pl.BlockSpec((P, 2, kb, Nb, C), lambda k: (0, 0, k, 0, 0))
    return pl.pallas_call(
        functools.partial(_fft_mid_kernel, passes=passes),
        out_shape=jax.ShapeDtypeStruct(a5.shape, a5.dtype),
        grid=(Na // kb,),
        in_specs=[gspec] * len(mats) + [pl.BlockSpec((1, 2, kb, Nb, C), lambda k: (order, 0, k, 0, 0)), blk],
        out_specs=blk,
        compiler_params=_cp(1), name="fft_mid",
    )(*mats, spectra, a5)


def _fft_spec_kernel(*refs, passes):
    n_mat = 1 if passes == 1 else 2
    gh_ref = refs[0]
    gl_ref = refs[1] if passes > 1 else None
    inv_ref, a_ref, o_ref = refs[n_mat:]
    n_filt, _, kb, nb, C = a_ref.shape
    for kk in range(kb):
        for p in range(n_filt):
            x = a_ref[p, :, kk].reshape(2 * nb, C)
            X = _mm_split(gh_ref[kk], None if gl_ref is None else gl_ref[kk], x, passes) * inv_ref[p]
            o_ref[p, :, kk] = X.reshape(2, nb, C)


def fft_filter_spectrum(g_hi, g_lo, inv_norm, a5, passes):
    P, _, Na, Nb, C = a5.shape
    kb = FFT_MID_K1
    mats = (g_hi,) if passes == 1 else (g_hi, g_lo)
    gspec = pl.BlockSpec((kb, 2 * Nb, 2 * Nb), lambda k: (k, 0, 0))
    blk = pl.BlockSpec((P, 2, kb, Nb, C), lambda k: (0, 0, k, 0, 0))
    return pl.pallas_call(
        functools.partial(_fft_spec_kernel, passes=passes),
        out_shape=jax.ShapeDtypeStruct(a5.shape, F32),
        grid=(Na // kb,),
        in_specs=[gspec] * len(mats) + [pl.BlockSpec((P, 1, C), lambda k: (0, 0, 0)), blk],
        out_specs=blk,
        compiler_params=_cp(1), name="fft_filter_spectrum",
    )(*mats, inv_norm, a5)


def _fft_factors(n_fft):
    na = 1 << (int(math.log2(n_fft)) // 2)
    return na, n_fft // na


def _fft_tables(L):
    n_fft = 2 * L
    na, nb = _fft_factors(n_fft)
    nh = na // 2
    iota = lambda shape, axis: lax.broadcasted_iota(jnp.int32, shape, axis)

    def cis(num, den):
        ang = (2.0 * math.pi / den) * (num % den).astype(F32)
        return jnp.cos(ang), -jnp.sin(ang)

    def block(diag_sign, pr, pc, cr, ci):
        return jnp.where(pr == pc, cr, jnp.where(pr == 1, diag_sign * ci, -diag_sign * ci))

    def kron_matrix(rows, cols, entry):
        shape = (rows * FFT_ROWS, cols * FFT_ROWS)
        I, J = iota(shape, 0), iota(shape, 1)
        m = entry(I // FFT_ROWS, J // FFT_ROWS)
        return _split_bf16(jnp.where(I % FFT_ROWS == J % FFT_ROWS, m, 0.0))

    def fwd_entry(i, j):
        cr, ci = cis((i % na) * (j % nh), na)
        return block(1, i // na, j // nh, cr, ci)

    def flt_entry(i, j):
        cr, ci = cis((i % na) * j, na)
        return jnp.where(i // na == 0, cr, ci)

    def inv_entry(i, j):
        cr, ci = cis((i % nh) * (j % na), na)
        return block(-1, i // nh, j // na, cr, ci) / n_fft

    gshape = (na, 2 * nb, 2 * nb)
    k1, gi_, gj_ = iota(gshape, 0), iota(gshape, 1), iota(gshape, 2)
    cr, ci = cis((gi_ % nb) * (gj_ % nb) * na + k1 * (gj_ % nb), n_fft)
    g = _split_bf16(block(1, gi_ // nb, gj_ // nb, cr, ci))
    cr, ci = cis((gj_ % nb) * (gi_ % nb) * na + k1 * (gi_ % nb), n_fft)
    ginv = _split_bf16(block(-1, gi_ // nb, gj_ // nb, cr, ci))

    C = HY_WIDTH
    n = jnp.arange(n_fft, dtype=jnp.int32)
    pos = jnp.where(n < L, n, jnp.where(n == L, 0, n_fft - n)).astype(F32)[:, None]
    t = pos * (1.0 / (L - 1))
    f = jnp.linspace(1e-4, HY_BANDS - 1, HY_BANDS, dtype=F32)
    ang = (2.0 * math.pi / L) * pos * f[None, :]
    z2 = jnp.concatenate([t, jnp.cos(ang), -jnp.sin(ang), jnp.zeros((n_fft, HY_EMB_PAD - HY_EMB), F32)], axis=-1)
    deltas = jnp.abs(jnp.linspace(HY_MIN_DECAY, HY_MAX_DECAY, C, dtype=F32))
    decay2 = jnp.exp(-t * deltas)
    return dict(na=na, nb=nb, fwd=kron_matrix(2 * na, na, fwd_entry), flt=kron_matrix(2 * na, na, flt_entry),
                inv=kron_matrix(na, 2 * na, inv_entry), g=g, ginv=ginv, z2=z2, decay2=decay2)


def hyena_filter_spectra(w1, b1, w2, b2, w3, freq, tabs, passes):
    C = HY_WIDTH
    na, nb = tabs["na"], tabs["nb"]
    w1p = jnp.pad(w1, ((0, HY_EMB_PAD - HY_EMB), (0, 0)))
    w3_dir = jnp.transpose(w3.reshape(HY_HIDDEN, HY_ORDER, 2, C), (2, 0, 1, 3)).reshape(2, HY_HIDDEN, HY_ORDER * C)
    filt, s = hyena_filter_mlp(tabs["z2"], w1p, b1[None], w2, b2[None], w3_dir, freq[None], tabs["decay2"])
    inv_norm = (1.0 / s).reshape(HY_ORDER, 1, C)
    a = fft_level1(*tabs["flt"], filt.reshape(HY_ORDER, na, nb, C), passes,
                   out_dtype=BF16 if passes == 1 else F32)
    return fft_filter_spectrum(*tabs["g"], inv_norm, a.reshape(HY_ORDER, 2, na, nb, C), passes)


def hyena_long_conv_gate(z, xg, bias, spectra, order, tabs, passes):
    B, L, C = z.shape
    na, nb = tabs["na"], tabs["nb"]
    P = B // 2
    nat = lambda a: a.reshape(P, na, nb, C)
    mid_dtype = BF16 if passes == 1 else F32
    a = fft_level1(*tabs["fwd"], nat(z), passes, out_dtype=mid_dtype)
    bv = fft_mid(*tabs["g"], *tabs["ginv"], spectra, order, a.reshape(P, 2, na, nb, C), passes)
    y = fft_level1(*tabs["inv"], bv.reshape(P, 2 * na, nb, C), passes,
                   gate_args=(nat(z), nat(xg), bias[None]), out_dtype=z.dtype)
    return y.reshape(B, L, C)


def hyena_mix(u, row0, L, spectra, conv_w, conv_b, bias, tabs, passes):
    v, x1, x2 = hyena_short_conv(u, conv_w, conv_b[None], row0, L)
    zz = hyena_long_conv_gate(v, x1, bias[0], spectra, 0, tabs, passes)
    return hyena_long_conv_gate(zz, x2, bias[1], spectra, 1, tabs, passes)


FFT_PASSES = 1
FILTER_PASSES = 1


def _rope_tables_t(n):
    rows = n // GRID_W
    r = jnp.repeat(jnp.arange(rows, dtype=F32), GRID_W)
    col = jnp.tile(jnp.arange(GRID_W, dtype=F32), rows)
    axis_dim = HEAD_DIM // 2
    inv = ROPE_THETA ** (-jnp.arange(0, axis_dim, 2, dtype=F32) / axis_dim)
    ang = jnp.concatenate([r[:, None] * inv, col[:, None] * inv], axis=-1)
    return jnp.cos(ang).T, jnp.sin(ang).T


def _layer_weights(l, w_in, gq_qn, gq_kn):
    w = w_in[l]
    d_qk = DA_HEADS * 2 * HEAD_DIM
    gq_kv = GKV_HEADS * HEAD_DIM
    gq_q = GQ_HEADS * HEAD_DIM
    o = np.cumsum([0, d_qk, d_qk, gq_kv, gq_kv, d_qk, gq_q, 3 * HY_WIDTH])
    ka, va, kb, vb, qa, qb, hy = (w[:, o[i]:o[i + 1]] for i in range(7))
    gates = w[:, o[7]:]
    qscale = (HEAD_DIM ** -0.5) * LOG2E
    w_qk = jnp.concatenate([ka, kb, qa, qb], axis=1).T.astype(BF16)
    g_qk = jnp.concatenate([jnp.ones((d_qk,), F32), jnp.tile(gq_kn[l], GKV_HEADS),
                            jnp.full((d_qk,), qscale, F32), jnp.tile(gq_qn[l], GQ_HEADS) * qscale])[:, None]
    n_ka, n_kb, n_qa = d_qk // HEAD_DIM, gq_kv // HEAD_DIM, d_qk // HEAD_DIM
    rms_heads = frozenset(range(n_ka, n_ka + n_kb)) | frozenset(range(n_ka + n_kb + n_qa,
                                                                       n_ka + n_kb + n_qa + gq_q // HEAD_DIM))
    return dict(w_qk=w_qk, g_qk=g_qk, rms_heads=rms_heads, w_va=va.T.astype(BF16), w_vb=vb.T.astype(BF16),
                w_tok=jnp.concatenate([hy, gates], axis=1).astype(BF16))


def kernel(x, c, ctx, c_ctx, w_mod, b_mod, w_in, da_lq1, da_lk1, da_lq2, da_lk2, da_subln, gq_qn, gq_kn,
           hy_conv_w, hy_conv_b, hy_w1, hy_b1, hy_w2, hy_b2, hy_w3, hy_freq, hy_bias, w_pa, w_pb, w_pc, w_o,
           ln1_g, ln1_b, w_up, ffn_conv_w, ffn_conv_b, w_down, ln2_g, ln2_b):
    B, n_lat, D = x.shape
    n_ctx = ctx.shape[1]
    depth = w_in.shape[0]
    alpha = (2 * depth) ** 0.25
    cos_l, sin_l = _rope_tables_t(n_lat)
    cos_t = jnp.concatenate([cos_l, jnp.ones((HALF, n_ctx), F32)], axis=1)
    sin_t = jnp.concatenate([sin_l, jnp.zeros((HALF, n_ctx), F32)], axis=1)
    lat, ctx_span, all_span = (0, n_lat), (n_lat, n_ctx), (0, n_lat + n_ctx)
    tabs_l = _fft_tables(n_lat)
    tabs_c = _fft_tables(n_ctx)
    xc = ctx

    cond = jnp.concatenate([c, c_ctx[None]], axis=0)
    cond = jnp.pad(jax.nn.silu(cond), ((0, 16 - (B + 1) % 16), (0, 0))).astype(BF16)

    for l in range(depth):
        last = l == depth - 1
        lam_init = 0.8 - 0.6 * math.exp(-0.3 * l)
        lam = (jnp.exp(jnp.sum(da_lq1[l] * da_lk1[l])) - jnp.exp(jnp.sum(da_lq2[l] * da_lk2[l])) + lam_init)
        lam = lam.reshape(1).astype(F32)
        mod = matmul(cond, w_mod[l].astype(BF16), F32, tn_pref=1024) + b_mod[l]
        sh1, sc1, g1, sh2, sc2, g2 = [m[:, None, :] for m in jnp.split(mod[:B], 6, axis=-1)]
        mc = [jnp.broadcast_to(m[None, None, :], (B, 1, D)) for m in jnp.split(mod[B], 6)]
        lw = _layer_weights(l, w_in, gq_qn, gq_kn)
        subln = da_subln[l][:, None]
        hy_mlp = (hy_w1[l], hy_b1[l], hy_w2[l], hy_b2[l], hy_w3[l], hy_freq[l])
        spectra_l = hyena_filter_spectra(*hy_mlp, tabs_l, FILTER_PASSES)
        wpa, wpb, wpc, wo = (w[l].astype(BF16) for w in (w_pa, w_pb, w_pc, w_o))
        wup, wdn = w_up[l].astype(BF16), w_down[l].astype(BF16)
        ln1 = (ln1_g[l][None], ln1_b[l][None])
        ln2 = (ln2_g[l][None], ln2_b[l][None])

        q_all, k_tok, k_n2, va, vb, u, gates = proj_all(x, xc, sh1, sc1, mc[0], mc[1], cos_t, sin_t, lw)
        kmax = jnp.sqrt(k_n2[:, :N_KHEADS, 0]).reshape(B * N_KHEADS)
        n_da_maps = DA_HEADS * 2

        def attend(q_span, kv_span):
            da = flash_attention(q_all, k_tok, va, kmax, lam, subln, mode="da", q_span=q_span, kv_span=kv_span,
                                 q_row_block=0, k_col_block=0, khead0=0, post_scale=1.0 - lam_init)
            gq = flash_attention(q_all, k_tok, vb, kmax, lam, subln, mode="gqa", q_span=q_span, kv_span=kv_span,
                                 q_row_block=n_da_maps * HEAD_DIM // 128, k_col_block=n_da_maps * HEAD_DIM // 128,
                                 khead0=n_da_maps)
            return da, gq

        hy_args = (hy_conv_w[l], hy_conv_b[l], hy_bias[l])
        a_l, b_l = attend(lat, all_span)
        c_l = hyena_mix(u, *lat, spectra_l, *hy_args, tabs_l, FFT_PASSES)
        x_new, h2 = merge_resid_ln(a_l, b_l, c_l, gates, lat[0], wpa, wpb, wpc, wo, x, g1, *ln1,
                                   sh2, sc2, alpha)
        x_new = conv_ffn_resid_ln(h2, wup, ffn_conv_w[l], ffn_conv_b[l][None], wdn, x_new, g2, *ln2, alpha)

        if not last:
            spectra_c = hyena_filter_spectra(*hy_mlp, tabs_c, FILTER_PASSES)
            a_c, b_c = attend(ctx_span, ctx_span)
            c_c = hyena_mix(u, *ctx_span, spectra_c, *hy_args, tabs_c, FFT_PASSES)
            xc, hc2 = merge_resid_ln(a_c, b_c, c_c, gates, ctx_span[0], wpa, wpb, wpc, wo, xc, mc[2],
                                     *ln1, mc[3], mc[4], alpha)
            xc = conv_ffn_resid_ln(hc2, wup, ffn_conv_w[l], ffn_conv_b[l][None], wdn, xc, mc[5], *ln2, alpha)
        x = x_new
    return x
```

```python
import functools
import math

import numpy as np
import jax
import jax.numpy as jnp
from jax import lax
from jax.experimental import pallas as pl
from jax.experimental.pallas import tpu as pltpu

F32 = jnp.float32
BF16 = jnp.bfloat16

HEAD_DIM = 64
HALF = HEAD_DIM // 2
GRID_W = 64
ROPE_THETA = 10000.0
DA_HEADS = 4
GQ_HEADS = 8
GKV_HEADS = 2
HY_WIDTH = 512
HY_ORDER = 2
HY_EMB = 33
HY_EMB_PAD = 128
HY_BANDS = (HY_EMB - 1) // 2
HY_HIDDEN = 64
HY_MIN_DECAY = math.log(1e-2) / 1.5
HY_MAX_DECAY = math.log(1e-2) / 0.3
LN_EPS = 1e-6
LOG2E = 1.4426950408889634
NEG_BIG = -1e30

VMEM_LIMIT = 56 * 1024 * 1024


def _cp(n_axes):
    return pltpu.CompilerParams(dimension_semantics=("arbitrary",) * n_axes,
                                vmem_limit_bytes=VMEM_LIMIT)


def _tile(n, pref, mult):
    if n <= pref:
        return n
    t = (pref // mult) * mult
    while t >= mult:
        if n % t == 0:
            return t
        t -= mult
    return n


def _ln(x):
    mu = jnp.mean(x, axis=-1, keepdims=True)
    xc = x - mu
    return xc * lax.rsqrt(jnp.mean(xc * xc, axis=-1, keepdims=True) + LN_EPS)


N_KHEADS = DA_HEADS * 2 + GKV_HEADS
KNORM_ROWS = 16


def _proj_all_kernel(x_ref, xc_ref, sh_ref, sc_ref, shc_ref, scc_ref, cos_ref, sin_ref, wqk_ref, gqk_ref,
                     wva_ref, wvb_ref, wtok_ref,
                     q_ref, k_ref, kn_ref, va_ref, vb_ref, u_ref, gt_ref, kt_sc,
                     *, n_lat_tiles, rms_heads, n_hy):
    i = pl.program_id(1)

    @pl.when(i == 0)
    def _():
        kn_ref[...] = jnp.zeros(kn_ref.shape, F32)

    is_latent = i < n_lat_tiles
    xin = jnp.where(is_latent, x_ref[0], xc_ref[0])
    shift = jnp.where(is_latent, sh_ref[0], shc_ref[0])
    scale = jnp.where(is_latent, sc_ref[0], scc_ref[0])
    h = (_ln(xin) * (1.0 + scale) + shift).astype(BF16)
    nt = (((1,), (1,)), ((), ()))

    acc = lax.dot_general(wqk_ref[...], h, nt, preferred_element_type=F32)
    va = lax.dot_general(wva_ref[...], h, nt, preferred_element_type=F32)
    vb = lax.dot_general(wvb_ref[...], h, nt, preferred_element_type=F32)
    tok = jnp.dot(h, wtok_ref[...], preferred_element_type=F32)
    va_ref[0] = va.astype(va_ref.dtype)
    vb_ref[0] = vb.astype(vb_ref.dtype)
    c, s = cos_ref[...], sin_ref[...]
    n_k_rows = kt_sc.shape[0]
    for hd in range(acc.shape[0] // HEAD_DIM):
        lo = hd * HEAD_DIM
        xh = acc[lo:lo + HEAD_DIM]
        if hd in rms_heads:
            xh = xh * lax.rsqrt(jnp.mean(xh * xh, axis=0, keepdims=True) + LN_EPS)
        xh = xh * gqk_ref[lo:lo + HEAD_DIM]
        x1, x2 = xh[:HALF], xh[HALF:]
        r1, r2 = x1 * c - x2 * s, x1 * s + x2 * c
        if lo < n_k_rows:
            kt_sc[lo:lo + HALF] = r1
            kt_sc[lo + HALF:lo + HEAD_DIM] = r2
        else:
            q_ref[0, lo - n_k_rows:lo - n_k_rows + HALF] = r1.astype(q_ref.dtype)
            q_ref[0, lo - n_k_rows + HALF:lo - n_k_rows + HEAD_DIM] = r2.astype(q_ref.dtype)

    k_bf = kt_sc[...].astype(BF16)
    k_ref[0] = k_bf.astype(F32).T.astype(BF16)
    k_sq = k_bf.astype(F32) ** 2
    for hd in range(n_k_rows // HEAD_DIM):
        n2 = jnp.sum(k_sq[hd * HEAD_DIM:(hd + 1) * HEAD_DIM], axis=0, keepdims=True)
        kn_ref[0, hd:hd + 1] = jnp.maximum(kn_ref[0, hd:hd + 1], jnp.max(n2, axis=1, keepdims=True))

    u_ref[0] = tok[:, :n_hy].astype(u_ref.dtype)
    gt_ref[0] = jax.nn.sigmoid(tok[:, n_hy:]).astype(gt_ref.dtype)


def proj_all(x, xc, shift, scale, shift_c, scale_c, cos_t, sin_t, lw):
    B, N, D = x.shape
    Nc = xc.shape[1]
    ntot = N + Nc
    tt = _tile(Nc, 256, 128)
    assert N % tt == 0
    nl = N // tt
    n_k_rows = N_KHEADS * HEAD_DIM
    n_q_rows = lw["w_qk"].shape[0] - n_k_rows
    n_va, n_vb = lw["w_va"].shape[0], lw["w_vb"].shape[0]
    n_tok = lw["w_tok"].shape[1]
    n_hy = 3 * HY_WIDTH
    per_b = lambda b, i: (b, 0, 0)
    const = lambda b, i: (0, 0)
    resident = dict(pipeline_mode=pl.Buffered(1))
    feat = lambda rows: pl.BlockSpec((1, rows, tt), lambda b, i: (b, 0, i))
    tokm = lambda cols: pl.BlockSpec((1, tt, cols), lambda b, i: (b, i, 0))
    sds = jax.ShapeDtypeStruct
    return pl.pallas_call(
        functools.partial(_proj_all_kernel, n_lat_tiles=nl, rms_heads=lw["rms_heads"], n_hy=n_hy),
        out_shape=(sds((B, n_q_rows, ntot), BF16), sds((B, ntot, n_k_rows), BF16),
                   sds((B, KNORM_ROWS, 128), F32), sds((B, n_va, ntot), BF16), sds((B, n_vb, ntot), BF16),
                   sds((B, ntot, n_hy), BF16), sds((B, ntot, n_tok - n_hy), BF16)),
        grid=(B, ntot // tt),
        in_specs=[pl.BlockSpec((1, tt, D), lambda b, i: (b, jnp.minimum(i, nl - 1), 0)),
                  pl.BlockSpec((1, tt, D), lambda b, i: (b, jnp.maximum(i - nl, 0), 0)),
                  pl.BlockSpec((1, 1, D), per_b), pl.BlockSpec((1, 1, D), per_b),
                  pl.BlockSpec((1, 1, D), per_b), pl.BlockSpec((1, 1, D), per_b),
                  pl.BlockSpec((HALF, tt), lambda b, i: (0, i)), pl.BlockSpec((HALF, tt), lambda b, i: (0, i)),
                  pl.BlockSpec(lw["w_qk"].shape, const, **resident), pl.BlockSpec((lw["w_qk"].shape[0], 1), const),
                  pl.BlockSpec(lw["w_va"].shape, const, **resident),
                  pl.BlockSpec(lw["w_vb"].shape, const, **resident),
                  pl.BlockSpec(lw["w_tok"].shape, const, **resident)],
        out_specs=(feat(n_q_rows), tokm(n_k_rows), pl.BlockSpec((1, KNORM_ROWS, 128), per_b),
                   feat(n_va), feat(n_vb), tokm(n_hy), tokm(n_tok - n_hy)),
        scratch_shapes=[pltpu.VMEM((n_k_rows, tt), F32)],
        compiler_params=_cp(2), name="proj_all",
    )(x, xc, shift, scale, shift_c, scale_c, cos_t, sin_t, lw["w_qk"], lw["g_qk"], lw["w_va"], lw["w_vb"],
      lw["w_tok"])


def _resid_ln_tail(x, y, gate, g, b, alpha):
    return _ln(alpha * x + gate * y) * g + b


def _mm_kernel(a_ref, w_ref, o_ref, *, act):
    acc = jnp.dot(a_ref[...], w_ref[...], preferred_element_type=F32)
    if act == "sigmoid":
        acc = jax.nn.sigmoid(acc)
    o_ref[...] = acc.astype(o_ref.dtype)


def matmul(a, w, out_dtype, act=None, tm_pref=1024, tn_pref=512):
    M, K = a.shape
    Nn = w.shape[1]
    tm = _tile(M, tm_pref, 16)
    tn = _tile(Nn, tn_pref, 128)
    return pl.pallas_call(
        functools.partial(_mm_kernel, act=act),
        out_shape=jax.ShapeDtypeStruct((M, Nn), out_dtype),
        grid=(M // tm, Nn // tn),
        in_specs=[pl.BlockSpec((tm, K), lambda i, j: (i, 0)),
                  pl.BlockSpec((K, tn), lambda i, j: (0, j))],
        out_specs=pl.BlockSpec((tm, tn), lambda i, j: (i, j)),
        compiler_params=_cp(2), name="matmul",
    )(a, w)


FLASH_MIN_DENOM = 2.0 ** -60
FLASH_UNROLL = 4


def _flash_kernel(lam_ref, kmax_ref, q_ref, k_ref, v_ref, g_ref, o_ref, q_sc, p_buf, sh_sc, acc_sc, l_sc,
                  *, mode, nj, tk, post_scale, n_kheads, khead0):
    b, h = pl.program_id(0), pl.program_id(1)
    qf = q_ref[0].astype(F32)
    tq = qf.shape[1]
    zero = jnp.zeros((HEAD_DIM, tq), F32)
    qa, qb = qf[:HEAD_DIM], qf[HEAD_DIM:]
    if mode == "da":
        q_sc[0] = jnp.concatenate([qa, zero], axis=0).astype(BF16)
        q_sc[1] = jnp.concatenate([zero, qb], axis=0).astype(BF16)
        k_heads = (khead0 + 2 * h, khead0 + 2 * h + 1)
    else:
        group = h // 2
        for mi, qh in enumerate((qa, qb)):
            q_sc[mi] = jnp.where(group == 0, jnp.concatenate([qh, zero], axis=0),
                                 jnp.concatenate([zero, qh], axis=0)).astype(BF16)
        k_heads = (khead0 + group, khead0 + group)
    for mi, qh in enumerate((qa, qb)):
        q_norm = jnp.sqrt(jnp.sum(qh * qh, axis=0, keepdims=True))
        sh_sc[mi] = q_norm * kmax_ref[b * n_kheads + k_heads[mi]]

    def chunk(j):
        start = j * tk
        return pl.ds(start if isinstance(start, int) else pl.multiple_of(start, tk), tk)

    def stage_exp(j, slot):
        kb = k_ref[0, chunk(j), :]
        for mi in range(2):
            s = jnp.dot(kb, q_sc[mi], preferred_element_type=F32)
            p = jnp.exp2(s - sh_sc[mi])
            p_buf[slot, mi] = p.astype(BF16)
            l_sc[mi] += jnp.sum(p.reshape(tk // 8, 8, tq), axis=0)

    def stage_values(j, slot):
        vb = v_ref[0, :, chunk(j)]
        for mi in range(2):
            acc_sc[mi] += jnp.dot(vb, p_buf[slot, mi], preferred_element_type=F32)

    def accumulate():
        acc_sc[...] = jnp.zeros(acc_sc.shape, F32)
        l_sc[...] = jnp.zeros(l_sc.shape, F32)
        stage_exp(0, 0)
        steady = list(range(1, nj))
        while len(steady) % FLASH_UNROLL:
            t = steady.pop(0)
            stage_exp(t, t % 2)
            stage_values(t - 1, (t - 1) % 2)
        if steady:
            t0 = steady[0]

            def body(i, carry):
                for d in range(FLASH_UNROLL):
                    stage_exp(t0 + FLASH_UNROLL * i + d, (t0 + d) % 2)
                    stage_values(t0 + FLASH_UNROLL * i + d - 1, (t0 + d - 1) % 2)
                return carry

            lax.fori_loop(0, len(steady) // FLASH_UNROLL, body, 0)
        stage_values(nj - 1, (nj - 1) % 2)

    denominators = lambda: [jnp.sum(l_sc[mi], axis=0, keepdims=True) for mi in range(2)]
    accumulate()
    denom_min = jnp.min(jnp.minimum(*denominators()))

    @pl.when(jnp.logical_not(denom_min >= FLASH_MIN_DENOM))
    def _():
        for mi in range(2):
            def max_body(j, m, mi=mi):
                s = jnp.dot(k_ref[0, chunk(j), :], q_sc[mi], preferred_element_type=F32)
                return jnp.maximum(m, jnp.max(s, axis=0, keepdims=True))

            sh_sc[mi] = lax.fori_loop(0, nj, max_body, jnp.full((1, tq), NEG_BIG, F32))
        accumulate()

    outs = [acc_sc[mi] / l for mi, l in enumerate(denominators())]
    if mode == "da":
        o = outs[0] - lam_ref[0] * outs[1]
        o = o * lax.rsqrt(jnp.mean(o * o, axis=0, keepdims=True) + LN_EPS)
        o = o * (g_ref[...] * post_scale)
    else:
        o = jnp.concatenate(outs, axis=0)
    o_ref[0] = o.T.astype(o_ref.dtype)


def flash_attention(q_t, k_tok, v_aug, kmax, lam, gain, *, mode, q_span, kv_span, q_row_block, k_col_block,
                    khead0, post_scale=1.0, tk_pref=768):
    B = q_t.shape[0]
    q0, Nq = q_span
    k0, Nk = kv_span
    tq = _tile(Nq, 1024, 128)
    tk = _tile(Nk, tk_pref, 128)
    assert q0 % tq == 0 and k0 % Nk == 0
    nj = Nk // tk
    dv = 2 * HEAD_DIM if mode == "da" else HEAD_DIM
    n_kheads = kmax.shape[0] // B
    if mode == "da":
        k_map = lambda b, h, i: (b, k0 // Nk, k_col_block + h)
        v_map = lambda b, h, i: (b, h, k0 // Nk)
    else:
        k_map = lambda b, h, i: (b, k0 // Nk, k_col_block)
        v_map = lambda b, h, i: (b, h // 2, k0 // Nk)
    return pl.pallas_call(
        functools.partial(_flash_kernel, mode=mode, nj=nj, tk=tk, post_scale=post_scale,
                          n_kheads=n_kheads, khead0=khead0),
        out_shape=jax.ShapeDtypeStruct((B, Nq, 4 * 128), BF16),
        grid=(B, 4, Nq // tq),
        in_specs=[pl.BlockSpec(memory_space=pltpu.SMEM),
                  pl.BlockSpec(memory_space=pltpu.SMEM),
                  pl.BlockSpec((1, 128, tq), lambda b, h, i: (b, h + q_row_block, i + q0 // tq)),
                  pl.BlockSpec((1, Nk, 128), k_map),
                  pl.BlockSpec((1, dv, Nk), v_map),
                  pl.BlockSpec((128, 1), lambda b, h, i: (0, 0))],
        out_specs=pl.BlockSpec((1, tq, 128), lambda b, h, i: (b, i, h)),
        scratch_shapes=[pltpu.VMEM((2, 128, tq), BF16),
                        pltpu.VMEM((2, 2, tk, tq), BF16),
                        pltpu.VMEM((2, 1, tq), F32),
                        pltpu.VMEM((2, dv, tq), F32),
                        pltpu.VMEM((2, 8, tq), F32)],
        compiler_params=_cp(3), name="flash_" + mode,
    )(lam, kmax, q_t, k_tok, v_aug, gain)


def _shift_rows(x, prev_row, next_row):
    T = x.shape[0]
    row = lax.broadcasted_iota(jnp.int32, x.shape, 0)
    xp = jnp.where(row == 0, prev_row, pltpu.roll(x, 1, axis=0))
    xn = jnp.where(row == T - 1, next_row, pltpu.roll(x, T - 1, axis=0))
    return xp, xn


def _conv3_block(x_ref, p_ref, n_ref, w_ref, b_ref, halo):
    i = pl.program_id(1)
    last = pl.num_programs(1) - 1
    x = x_ref[0].astype(F32)
    prev_row = jnp.where(i > 0, p_ref[0].astype(F32)[halo - 1:halo], 0.0)
    next_row = jnp.where(i < last, n_ref[0].astype(F32)[0:1], 0.0)
    xp, xn = _shift_rows(x, prev_row, next_row)
    return xp * w_ref[0:1] + x * w_ref[1:2] + xn * w_ref[2:3] + b_ref[...]


def _halo_specs(tr, tc, halo, row0, n_rows, col_of):
    per = tr // halo
    n_halo = n_rows // halo
    m0, h0 = row0 // tr, row0 // halo
    return [pl.BlockSpec((1, tr, tc), lambda b, i, j: (b, m0 + i, col_of(j))),
            pl.BlockSpec((1, halo, tc), lambda b, i, j: (b, h0 + jnp.maximum(i * per - 1, 0), col_of(j))),
            pl.BlockSpec((1, halo, tc),
                         lambda b, i, j: (b, h0 + jnp.minimum((i + 1) * per, n_halo - 1), col_of(j)))]


HYENA_HALO = 16


def _dwconv_kernel(x_ref, p_ref, n_ref, w_ref, b_ref, v_ref, x1_ref, x2_ref):
    y = _conv3_block(x_ref, p_ref, n_ref, w_ref, b_ref, HYENA_HALO)
    C = v_ref.shape[2]
    v_ref[0] = y[:, :C].astype(v_ref.dtype)
    x1_ref[0] = y[:, C:2 * C].astype(x1_ref.dtype)
    x2_ref[0] = y[:, 2 * C:].astype(x2_ref.dtype)


def hyena_short_conv(u, w, b, row0, L):
    B, _, C3 = u.shape
    C = C3 // 3
    tr = _tile(L, 512, HYENA_HALO)
    assert row0 % tr == 0
    zero = lambda j: 0
    out = jax.ShapeDtypeStruct((B, L, C), BF16)
    ospec = pl.BlockSpec((1, tr, C), lambda b, i, j: (b, i, 0))
    return pl.pallas_call(
        _dwconv_kernel,
        out_shape=(out, out, out),
        grid=(B, L // tr, 1),
        in_specs=_halo_specs(tr, C3, HYENA_HALO, row0, L, zero) + [
            pl.BlockSpec((3, C3), lambda b, i, j: (0, 0)),
            pl.BlockSpec((1, C3), lambda b, i, j: (0, 0))],
        out_specs=(ospec, ospec, ospec),
        compiler_params=_cp(3), name="hyena_short_conv",
    )(u, u, u, w, b)


FFN_HALO = 16
FFN_CHUNK = 256


def _ffn_kernel(h_ref, hp_ref, hn_ref, wup_ref, cw_ref, cb_ref, wdn_ref, x_ref, gate_ref, g_ref, b_ref, o_ref,
                act_sc, *, alpha):
    i = pl.program_id(1)
    tm = h_ref.shape[1]
    dff = wdn_ref.shape[0]
    rows = tm + 2 * FFN_HALO
    h_prev = jnp.where(i == 0, jnp.zeros_like(hp_ref[0]), hp_ref[0])
    h_next = jnp.where(i == pl.num_programs(1) - 1, jnp.zeros_like(hn_ref[0]), hn_ref[0])
    h_ext = jnp.concatenate([h_prev, h_ref[0], h_next], axis=0)

    def conv_cols(c0):
        u = jnp.dot(h_ext, wup_ref[:, c0:c0 + FFN_CHUNK], preferred_element_type=F32)
        mid = slice(FFN_HALO, FFN_HALO + tm)
        u_prev = pltpu.roll(u, 1, axis=0)[mid]
        u_next = pltpu.roll(u, rows - 1, axis=0)[mid]
        w = cw_ref[:, c0:c0 + FFN_CHUNK]
        return u_prev * w[0:1] + u[mid] * w[1:2] + u_next * w[2:3] + cb_ref[:, c0:c0 + FFN_CHUNK]

    for c0 in range(0, dff, FFN_CHUNK):
        val = conv_cols(c0)
        gate = conv_cols(dff + c0)
        act_sc[:, c0:c0 + FFN_CHUNK] = (gate * jax.nn.sigmoid(gate) * val).astype(BF16)
    y = jnp.dot(act_sc[...], wdn_ref[...], preferred_element_type=F32)
    o_ref[0] = _resid_ln_tail(x_ref[0], y, gate_ref[0], g_ref[...], b_ref[...], alpha)


def conv_ffn_resid_ln(h, w_up, conv_w, conv_b, w_down, x, gate, g, b, alpha):
    B, N, D = h.shape
    dff = w_down.shape[0]
    tm = _tile(N, 512, FFN_HALO)
    per = tm // FFN_HALO
    n_halo = N // FFN_HALO
    row = lambda b_, i: (b_, i, 0)
    per_b = lambda b_, i: (b_, 0, 0)
    const2 = lambda b_, i: (0, 0)
    resident = dict(pipeline_mode=pl.Buffered(1))
    return pl.pallas_call(
        functools.partial(_ffn_kernel, alpha=alpha),
        out_shape=jax.ShapeDtypeStruct((B, N, D), F32),
        grid=(B, N // tm),
        in_specs=[pl.BlockSpec((1, tm, D), row),
                  pl.BlockSpec((1, FFN_HALO, D), lambda b_, i: (b_, jnp.maximum(i * per - 1, 0), 0)),
                  pl.BlockSpec((1, FFN_HALO, D), lambda b_, i: (b_, jnp.minimum((i + 1) * per, n_halo - 1), 0)),
                  pl.BlockSpec((D, 2 * dff), const2, **resident),
                  pl.BlockSpec((3, 2 * dff), const2), pl.BlockSpec((1, 2 * dff), const2),
                  pl.BlockSpec((dff, D), const2, **resident),
                  pl.BlockSpec((1, tm, D), row), pl.BlockSpec((1, 1, D), per_b),
                  pl.BlockSpec((1, D), const2), pl.BlockSpec((1, D), const2)],
        out_specs=pl.BlockSpec((1, tm, D), row),
        scratch_shapes=[pltpu.VMEM((tm, dff), BF16)],
        compiler_params=_cp(2), name="conv_ffn_resid_ln",
    )(h, h, h, w_up, conv_w, conv_b, w_down, x, gate, g, b)


def _merge_kernel(a_ref, b_ref, c_ref, gt_ref, wpa_ref, wpb_ref, wpc_ref, wo_ref, x_ref, gate_ref,
                  g_ref, bb_ref, sh_ref, sc_ref, ox_ref, oh_ref, *, alpha):
    D = wo_ref.shape[0]
    tm = x_ref.shape[1]
    n_sub = 2 if tm % 32 == 0 else 1
    for r in range(n_sub):
        rows = slice(r * tm // n_sub, (r + 1) * tm // n_sub)
        gt = gt_ref[0, rows]
        m = gt[:, :D].astype(F32) * jnp.dot(a_ref[0, rows], wpa_ref[...], preferred_element_type=F32)
        m = m + gt[:, D:2 * D].astype(F32) * jnp.dot(b_ref[0, rows], wpb_ref[...], preferred_element_type=F32)
        m = m + gt[:, 2 * D:].astype(F32) * jnp.dot(c_ref[0, rows].astype(BF16), wpc_ref[...],
                                                    preferred_element_type=F32)
        y = jnp.dot(m.astype(BF16), wo_ref[...], preferred_element_type=F32)
        xn = _resid_ln_tail(x_ref[0, rows], y, gate_ref[0], g_ref[...], bb_ref[...], alpha)
        ox_ref[0, rows] = xn
        oh_ref[0, rows] = (_ln(xn) * (1.0 + sc_ref[0]) + sh_ref[0]).astype(oh_ref.dtype)


def merge_resid_ln(a, b, c, gates, tok0, wpa, wpb, wpc, wo, x, gate, g, bb, sh, sc, alpha):
    B, N, D = x.shape
    W = a.shape[2]
    tm = _tile(N, 512, 16)
    assert tok0 % tm == 0
    row = lambda b_, i: (b_, i, 0)
    per_b = lambda b_, i: (b_, 0, 0)
    const2 = lambda b_, i: (0, 0)
    return pl.pallas_call(
        functools.partial(_merge_kernel, alpha=alpha),
        out_shape=(jax.ShapeDtypeStruct((B, N, D), F32), jax.ShapeDtypeStruct((B, N, D), BF16)),
        grid=(B, N // tm),
        in_specs=[pl.BlockSpec((1, tm, W), row), pl.BlockSpec((1, tm, W), row), pl.BlockSpec((1, tm, W), row),
                  pl.BlockSpec((1, tm, 3 * D), lambda b_, i: (b_, tok0 // tm + i, 0)),
                  pl.BlockSpec((W, D), const2), pl.BlockSpec((W, D), const2), pl.BlockSpec((W, D), const2),
                  pl.BlockSpec((D, D), const2),
                  pl.BlockSpec((1, tm, D), row), pl.BlockSpec((1, 1, D), per_b),
                  pl.BlockSpec((1, D), const2), pl.BlockSpec((1, D), const2),
                  pl.BlockSpec((1, 1, D), per_b), pl.BlockSpec((1, 1, D), per_b)],
        out_specs=(pl.BlockSpec((1, tm, D), row), pl.BlockSpec((1, tm, D), row)),
        compiler_params=_cp(2), name="merge_resid_ln",
    )(a, b, c, gates, wpa, wpb, wpc, wo, x, gate, g, bb, sh, sc)


def _split_bf16(x):
    hi = x.astype(BF16)
    lo = (x - hi.astype(F32)).astype(BF16)
    return hi, lo


def _dot3(a, b):
    ah, al = _split_bf16(a)
    bh, bl = _split_bf16(b)
    d = functools.partial(jnp.dot, preferred_element_type=F32)
    return d(ah, bh) + (d(ah, bl) + d(al, bh))


def _filter_kernel(z_ref, w1_ref, b1_ref, w2_ref, b2_ref, w3_ref, fr_ref, dec_ref, h_ref, s_ref, *, zero_block):
    i = pl.program_id(0)

    @pl.when(i == 0)
    def _():
        s_ref[...] = jnp.zeros(s_ref.shape, F32)

    fr = fr_ref[...]
    hid = jnp.sin(fr * (_dot3(z_ref[...], w1_ref[...]) + b1_ref[...]))
    hid = jnp.sin(fr * (_dot3(hid, w2_ref[...]) + b2_ref[...]))
    h = _dot3(hid, w3_ref[0])
    dec = dec_ref[...]
    C = dec.shape[1]
    n_ord = h.shape[1] // C
    h = h * jnp.concatenate([dec] * n_ord, axis=1)
    s_ref[...] += jnp.sum(jnp.abs(h), axis=0, keepdims=True)
    row = lax.broadcasted_iota(jnp.int32, h.shape, 0)
    h = jnp.where((row == 0) & (i == zero_block), 0.0, h)
    for o in range(n_ord):
        h_ref[o] = h[:, o * C:(o + 1) * C]


def hyena_filter_mlp(z2, w1, b1, w2, b2, w3_dir, freq, decay2):
    n2, E = z2.shape
    Hh = w2.shape[0]
    OC = w3_dir.shape[2]
    C = decay2.shape[1]
    n_ord = OC // C
    L = n2 // 2
    tr = _tile(L, 512, 8)
    nblk = n2 // tr
    c2 = lambda i: (0, 0)
    return pl.pallas_call(
        functools.partial(_filter_kernel, zero_block=L // tr),
        out_shape=(jax.ShapeDtypeStruct((n_ord, n2, C), F32), jax.ShapeDtypeStruct((1, OC), F32)),
        grid=(nblk,),
        in_specs=[pl.BlockSpec((tr, E), lambda i: (i, 0)), pl.BlockSpec((E, Hh), c2), pl.BlockSpec((1, Hh), c2),
                  pl.BlockSpec((Hh, Hh), c2), pl.BlockSpec((1, Hh), c2),
                  pl.BlockSpec((1, Hh, OC), lambda i: (i // (nblk // 2), 0, 0)),
                  pl.BlockSpec((1, Hh), c2), pl.BlockSpec((tr, C), lambda i: (i, 0))],
        out_specs=(pl.BlockSpec((n_ord, tr, C), lambda i: (0, i, 0)), pl.BlockSpec((1, OC), c2)),
        compiler_params=_cp(1), name="hyena_filter_mlp",
    )(z2, w1, b1, w2, b2, w3_dir, freq, decay2)


def _mm_split(m_hi, m_lo, x, passes):
    d = functools.partial(jnp.dot, preferred_element_type=F32)
    if passes == 1:
        return d(m_hi, x.astype(BF16))
    xh, xl = _split_bf16(x)
    return d(m_hi, xh) + (d(m_lo, xh) + d(m_hi, xl))


FFT_ROWS = 8
FFT_BLOCK = 16


def _level1_apply(m_refs, x, r_out, passes):
    r_in, _, C = x.shape
    x = x.astype(F32)
    m_hi = m_refs[0][...]
    m_lo = m_refs[1][...] if passes > 1 else None
    groups = []
    for g in range(FFT_BLOCK // FFT_ROWS):
        xg = x[:, g * FFT_ROWS:(g + 1) * FFT_ROWS].reshape(r_in * FFT_ROWS, C)
        groups.append(_mm_split(m_hi, m_lo, xg, passes).reshape(r_out, FFT_ROWS, C))
    return jnp.concatenate(groups, axis=1)


def _level1_kernel(*refs, passes):
    n_mat = 1 if passes == 1 else 2
    x_ref, o_ref = refs[n_mat:]
    o_ref[0] = _level1_apply(refs[:n_mat], x_ref[0], o_ref.shape[1], passes).astype(o_ref.dtype)


def _level1_gate_kernel(*refs, passes):
    n_mat = 1 if passes == 1 else 2
    x_ref, z_ref, xg_ref, bias_ref, o_ref = refs[n_mat:]
    y = _level1_apply(refs[:n_mat], x_ref[0], o_ref.shape[1], passes)
    o_ref[0] = (xg_ref[0].astype(F32) * (y + bias_ref[...] * z_ref[0].astype(F32))).astype(o_ref.dtype)


def fft_level1(m_hi, m_lo, x, passes, gate_args=None, out_dtype=F32):
    P, R_in, Nb, C = x.shape
    R_out = m_hi.shape[0] // FFT_ROWS
    mats = (m_hi,) if passes == 1 else (m_hi, m_lo)
    mspec = pl.BlockSpec(m_hi.shape, lambda p, j: (0, 0), pipeline_mode=pl.Buffered(1))
    xspec = pl.BlockSpec((1, R_in, FFT_BLOCK, C), lambda p, j: (p, 0, j, 0))
    ospec = pl.BlockSpec((1, R_out, FFT_BLOCK, C), lambda p, j: (p, 0, j, 0))
    if gate_args is None:
        kern = functools.partial(_level1_kernel, passes=passes)
        ins, specs = (*mats, x), [mspec] * len(mats) + [xspec]
    else:
        z, xg, bias = gate_args
        kern = functools.partial(_level1_gate_kernel, passes=passes)
        ins = (*mats, x, z, xg, bias)
        specs = [mspec] * len(mats) + [xspec, ospec, ospec, pl.BlockSpec((1, C), lambda p, j: (0, 0))]
    return pl.pallas_call(
        kern, out_shape=jax.ShapeDtypeStruct((P, R_out, Nb, C), out_dtype),
        grid=(P, Nb // FFT_BLOCK), in_specs=specs, out_specs=ospec,
        compiler_params=_cp(2), name="fft_level1",
    )(*ins)


FFT_MID_K1 = 4


def _fft_mid_kernel(*refs, passes):
    n_mat = 2 if passes == 1 else 4
    gh_ref, gih_ref = refs[0], refs[1]
    gl_ref, gil_ref = (refs[2], refs[3]) if passes > 1 else (None, None)
    h_ref, a_ref, o_ref = refs[n_mat:]
    n_pairs, _, kb, nb, C = a_ref.shape
    for kk in range(kb):
        hr, hi = h_ref[0, 0, kk], h_ref[0, 1, kk]
        for p in range(n_pairs):
            x = a_ref[p, :, kk].reshape(2 * nb, C)
            X = _mm_split(gh_ref[kk], None if gl_ref is None else gl_ref[kk], x, passes)
            xr, xi = X[:nb], X[nb:]
            Y = jnp.concatenate([xr * hr - xi * hi, xr * hi + xi * hr], axis=0)
            Bv = _mm_split(gih_ref[kk], None if gil_ref is None else gil_ref[kk], Y, passes)
            o_ref[p, :, kk] = Bv.reshape(2, nb, C).astype(o_ref.dtype)


def fft_mid(g_hi, g_lo, gi_hi, gi_lo, spectra, order, a5, passes):
    P, _, Na, Nb, C = a5.shape
    kb = FFT_MID_K1
    gspec = pl.BlockSpec((kb, 2 * Nb, 2 * Nb), lambda k: (k, 0, 0))
    mats = (g_hi, gi_hi) if passes == 1 else (g_hi, gi_hi, g_lo, gi_lo)
    blk = pl.BlockSpec((P, 2, kb, Nb, C), lambda k: (0, 0, k, 0, 0))
    return pl.pallas_call(
        functools.partial(_fft_mid_kernel, passes=passes),
        out_shape=jax.ShapeDtypeStruct(a5.shape, a5.dtype),
        grid=(Na // kb,),
        in_specs=[gspec] * len(mats) + [pl.BlockSpec((1, 2, kb, Nb, C), lambda k: (order, 0, k, 0, 0)), blk],
        out_specs=blk,
        compiler_params=_cp(1), name="fft_mid",
    )(*mats, spectra, a5)


def _fft_spec_kernel(*refs, passes):
    n_mat = 1 if passes == 1 else 2
    gh_ref = refs[0]
    gl_ref = refs[1] if passes > 1 else None
    inv_ref, a_ref, o_ref = refs[n_mat:]
    n_filt, _, kb, nb, C = a_ref.shape
    for kk in range(kb):
        for p in range(n_filt):
            x = a_ref[p, :, kk].reshape(2 * nb, C)
            X = _mm_split(gh_ref[kk], None if gl_ref is None else gl_ref[kk], x, passes) * inv_ref[p]
            o_ref[p, :, kk] = X.reshape(2, nb, C)


def fft_filter_spectrum(g_hi, g_lo, inv_norm, a5, passes):
    P, _, Na, Nb, C = a5.shape
    kb = FFT_MID_K1
    mats = (g_hi,) if passes == 1 else (g_hi, g_lo)
    gspec = pl.BlockSpec((kb, 2 * Nb, 2 * Nb), lambda k: (k, 0, 0))
    blk = pl.BlockSpec((P, 2, kb, Nb, C), lambda k: (0, 0, k, 0, 0))
    return pl.pallas_call(
        functools.partial(_fft_spec_kernel, passes=passes),
        out_shape=jax.ShapeDtypeStruct(a5.shape, F32),
        grid=(Na // kb,),
        in_specs=[gspec] * len(mats) + [pl.BlockSpec((P, 1, C), lambda k: (0, 0, 0)), blk],
        out_specs=blk,
        compiler_params=_cp(1), name="fft_filter_spectrum",
    )(*mats, inv_norm, a5)


def _fft_factors(n_fft):
    na = 1 << (int(math.log2(n_fft)) // 2)
    return na, n_fft // na


def _fft_tables(L):
    n_fft = 2 * L
    na, nb = _fft_factors(n_fft)
    nh = na // 2
    iota = lambda shape, axis: lax.broadcasted_iota(jnp.int32, shape, axis)

    def cis(num, den):
        ang = (2.0 * math.pi / den) * (num % den).astype(F32)
        return jnp.cos(ang), -jnp.sin(ang)

    def block(diag_sign, pr, pc, cr, ci):
        return jnp.where(pr == pc, cr, jnp.where(pr == 1, diag_sign * ci, -diag_sign * ci))

    def kron_matrix(rows, cols, entry):
        shape = (rows * FFT_ROWS, cols * FFT_ROWS)
        I, J = iota(shape, 0), iota(shape, 1)
        m = entry(I // FFT_ROWS, J // FFT_ROWS)
        return _split_bf16(jnp.where(I % FFT_ROWS == J % FFT_ROWS, m, 0.0))

    def fwd_entry(i, j):
        cr, ci = cis((i % na) * (j % nh), na)
        return block(1, i // na, j // nh, cr, ci)

    def flt_entry(i, j):
        cr, ci = cis((i % na) * j, na)
        return jnp.where(i // na == 0, cr, ci)

    def inv_entry(i, j):
        cr, ci = cis((i % nh) * (j % na), na)
        return block(-1, i // nh, j // na, cr, ci) / n_fft

    gshape = (na, 2 * nb, 2 * nb)
    k1, gi_, gj_ = iota(gshape, 0), iota(gshape, 1), iota(gshape, 2)
    cr, ci = cis((gi_ % nb) * (gj_ % nb) * na + k1 * (gj_ % nb), n_fft)
    g = _split_bf16(block(1, gi_ // nb, gj_ // nb, cr, ci))
    cr, ci = cis((gj_ % nb) * (gi_ % nb) * na + k1 * (gi_ % nb), n_fft)
    ginv = _split_bf16(block(-1, gi_ // nb, gj_ // nb, cr, ci))

    C = HY_WIDTH
    n = jnp.arange(n_fft, dtype=jnp.int32)
    pos = jnp.where(n < L, n, jnp.where(n == L, 0, n_fft - n)).astype(F32)[:, None]
    t = pos * (1.0 / (L - 1))
    f = jnp.linspace(1e-4, HY_BANDS - 1, HY_BANDS, dtype=F32)
    ang = (2.0 * math.pi / L) * pos * f[None, :]
    z2 = jnp.concatenate([t, jnp.cos(ang), -jnp.sin(ang), jnp.zeros((n_fft, HY_EMB_PAD - HY_EMB), F32)], axis=-1)
    deltas = jnp.abs(jnp.linspace(HY_MIN_DECAY, HY_MAX_DECAY, C, dtype=F32))
    decay2 = jnp.exp(-t * deltas)
    return dict(na=na, nb=nb, fwd=kron_matrix(2 * na, na, fwd_entry), flt=kron_matrix(2 * na, na, flt_entry),
                inv=kron_matrix(na, 2 * na, inv_entry), g=g, ginv=ginv, z2=z2, decay2=decay2)


def hyena_filter_spectra(w1, b1, w2, b2, w3, freq, tabs, passes):
    C = HY_WIDTH
    na, nb = tabs["na"], tabs["nb"]
    w1p = jnp.pad(w1, ((0, HY_EMB_PAD - HY_EMB), (0, 0)))
    w3_dir = jnp.transpose(w3.reshape(HY_HIDDEN, HY_ORDER, 2, C), (2, 0, 1, 3)).reshape(2, HY_HIDDEN, HY_ORDER * C)
    filt, s = hyena_filter_mlp(tabs["z2"], w1p, b1[None], w2, b2[None], w3_dir, freq[None], tabs["decay2"])
    inv_norm = (1.0 / s).reshape(HY_ORDER, 1, C)
    a = fft_level1(*tabs["flt"], filt.reshape(HY_ORDER, na, nb, C), passes,
                   out_dtype=BF16 if passes == 1 else F32)
    return fft_filter_spectrum(*tabs["g"], inv_norm, a.reshape(HY_ORDER, 2, na, nb, C), passes)


def hyena_long_conv_gate(z, xg, bias, spectra, order, tabs, passes):
    B, L, C = z.shape
    na, nb = tabs["na"], tabs["nb"]
    P = B // 2
    nat = lambda a: a.reshape(P, na, nb, C)
    mid_dtype = BF16 if passes == 1 else F32
    a = fft_level1(*tabs["fwd"], nat(z), passes, out_dtype=mid_dtype)
    bv = fft_mid(*tabs["g"], *tabs["ginv"], spectra, order, a.reshape(P, 2, na, nb, C), passes)
    y = fft_level1(*tabs["inv"], bv.reshape(P, 2 * na, nb, C), passes,
                   gate_args=(nat(z), nat(xg), bias[None]), out_dtype=z.dtype)
    return y.reshape(B, L, C)


def hyena_mix(u, row0, L, spectra, conv_w, conv_b, bias, tabs, passes):
    v, x1, x2 = hyena_short_conv(u, conv_w, conv_b[None], row0, L)
    zz = hyena_long_conv_gate(v, x1, bias[0], spectra, 0, tabs, passes)
    return hyena_long_conv_gate(zz, x2, bias[1], spectra, 1, tabs, passes)


FFT_PASSES = 1
FILTER_PASSES = 1


def _rope_tables_t(n):
    rows = n // GRID_W
    r = jnp.repeat(jnp.arange(rows, dtype=F32), GRID_W)
    col = jnp.tile(jnp.arange(GRID_W, dtype=F32), rows)
    axis_dim = HEAD_DIM // 2
    inv = ROPE_THETA ** (-jnp.arange(0, axis_dim, 2, dtype=F32) / axis_dim)
    ang = jnp.concatenate([r[:, None] * inv, col[:, None] * inv], axis=-1)
    return jnp.cos(ang).T, jnp.sin(ang).T


def _layer_weights(l, w_in, gq_qn, gq_kn):
    w = w_in[l]
    d_qk = DA_HEADS * 2 * HEAD_DIM
    gq_kv = GKV_HEADS * HEAD_DIM
    gq_q = GQ_HEADS * HEAD_DIM
    o = np.cumsum([0, d_qk, d_qk, gq_kv, gq_kv, d_qk, gq_q, 3 * HY_WIDTH])
    ka, va, kb, vb, qa, qb, hy = (w[:, o[i]:o[i + 1]] for i in range(7))
    gates = w[:, o[7]:]
    qscale = (HEAD_DIM ** -0.5) * LOG2E
    w_qk = jnp.concatenate([ka, kb, qa, qb], axis=1).T.astype(BF16)
    g_qk = jnp.concatenate([jnp.ones((d_qk,), F32), jnp.tile(gq_kn[l], GKV_HEADS),
                            jnp.full((d_qk,), qscale, F32), jnp.tile(gq_qn[l], GQ_HEADS) * qscale])[:, None]
    n_ka, n_kb, n_qa = d_qk // HEAD_DIM, gq_kv // HEAD_DIM, d_qk // HEAD_DIM
    rms_heads = frozenset(range(n_ka, n_ka + n_kb)) | frozenset(range(n_ka + n_kb + n_qa,
                                                                       n_ka + n_kb + n_qa + gq_q // HEAD_DIM))
    return dict(w_qk=w_qk, g_qk=g_qk, rms_heads=rms_heads, w_va=va.T.astype(BF16), w_vb=vb.T.astype(BF16),
                w_tok=jnp.concatenate([hy, gates], axis=1).astype(BF16))


def kernel(x, c, ctx, c_ctx, w_mod, b_mod, w_in, da_lq1, da_lk1, da_lq2, da_lk2, da_subln, gq_qn, gq_kn,
           hy_conv_w, hy_conv_b, hy_w1, hy_b1, hy_w2, hy_b2, hy_w3, hy_freq, hy_bias, w_pa, w_pb, w_pc, w_o,
           ln1_g, ln1_b, w_up, ffn_conv_w, ffn_conv_b, w_down, ln2_g, ln2_b):
    B, n_lat, D = x.shape
    n_ctx = ctx.shape[1]
    depth = w_in.shape[0]
    alpha = (2 * depth) ** 0.25
    cos_l, sin_l = _rope_tables_t(n_lat)
    cos_t = jnp.concatenate([cos_l, jnp.ones((HALF, n_ctx), F32)], axis=1)
    sin_t = jnp.concatenate([sin_l, jnp.zeros((HALF, n_ctx), F32)], axis=1)
    lat, ctx_span, all_span = (0, n_lat), (n_lat, n_ctx), (0, n_lat + n_ctx)
    tabs_l = _fft_tables(n_lat)
    tabs_c = _fft_tables(n_ctx)
    xc = ctx

    cond = jnp.concatenate([c, c_ctx[None]], axis=0)
    cond = jnp.pad(jax.nn.silu(cond), ((0, 16 - (B + 1) % 16), (0, 0))).astype(BF16)

    for l in range(depth):
        last = l == depth - 1
        lam_init = 0.8 - 0.6 * math.exp(-0.3 * l)
        lam = (jnp.exp(jnp.sum(da_lq1[l] * da_lk1[l])) - jnp.exp(jnp.sum(da_lq2[l] * da_lk2[l])) + lam_init)
        lam = lam.reshape(1).astype(F32)
        mod = matmul(cond, w_mod[l].astype(BF16), F32, tn_pref=1024) + b_mod[l]
        sh1, sc1, g1, sh2, sc2, g2 = [m[:, None, :] for m in jnp.split(mod[:B], 6, axis=-1)]
        mc = [jnp.broadcast_to(m[None, None, :], (B, 1, D)) for m in jnp.split(mod[B], 6)]
        lw = _layer_weights(l, w_in, gq_qn, gq_kn)
        subln = da_subln[l][:, None]
        hy_mlp = (hy_w1[l], hy_b1[l], hy_w2[l], hy_b2[l], hy_w3[l], hy_freq[l])
        spectra_l = hyena_filter_spectra(*hy_mlp, tabs_l, FILTER_PASSES)
        wpa, wpb, wpc, wo = (w[l].astype(BF16) for w in (w_pa, w_pb, w_pc, w_o))
        wup, wdn = w_up[l].astype(BF16), w_down[l].astype(BF16)
        ln1 = (ln1_g[l][None], ln1_b[l][None])
        ln2 = (ln2_g[l][None], ln2_b[l][None])

        q_all, k_tok, k_n2, va, vb, u, gates = proj_all(x, xc, sh1, sc1, mc[0], mc[1], cos_t, sin_t, lw)
        kmax = jnp.sqrt(k_n2[:, :N_KHEADS, 0]).reshape(B * N_KHEADS)
        n_da_maps = DA_HEADS * 2

        def attend(q_span, kv_span):
            da = flash_attention(q_all, k_tok, va, kmax, lam, subln, mode="da", q_span=q_span, kv_span=kv_span,
                                 q_row_block=0, k_col_block=0, khead0=0, post_scale=1.0 - lam_init)
            gq = flash_attention(q_all, k_tok, vb, kmax, lam, subln, mode="gqa", q_span=q_span, kv_span=kv_span,
                                 q_row_block=n_da_maps * HEAD_DIM // 128, k_col_block=n_da_maps * HEAD_DIM // 128,
                                 khead0=n_da_maps)
            return da, gq

        hy_args = (hy_conv_w[l], hy_conv_b[l], hy_bias[l])
        a_l, b_l = attend(lat, all_span)
        c_l = hyena_mix(u, *lat, spectra_l, *hy_args, tabs_l, FFT_PASSES)
        x_new, h2 = merge_resid_ln(a_l, b_l, c_l, gates, lat[0], wpa, wpb, wpc, wo, x, g1, *ln1,
                                   sh2, sc2, alpha)
        x_new = conv_ffn_resid_ln(h2, wup, ffn_conv_w[l], ffn_conv_b[l][None], wdn, x_new, g2, *ln2, alpha)

        if not last:
            spectra_c = hyena_filter_spectra(*hy_mlp, tabs_c, FILTER_PASSES)
            a_c, b_c = attend(ctx_span, ctx_span)
            c_c = hyena_mix(u, *ctx_span, spectra_c, *hy_args, tabs_c, FFT_PASSES)
            xc, hc2 = merge_resid_ln(a_c, b_c, c_c, gates, ctx_span[0], wpa, wpb, wpc, wo, xc, mc[2],
                                     *ln1, mc[3], mc[4], alpha)
            xc = conv_ffn_resid_ln(hc2, wup, ffn_conv_w[l], ffn_conv_b[l][None], wdn, xc, mc[5], *ln2, alpha)
        x = x_new
    return x
```

```python
import functools
import math

import numpy as np
import jax
import jax.numpy as jnp
from jax import lax
from jax.experimental import pallas as pl
from jax.experimental.pallas import tpu as pltpu

F32 = jnp.float32
BF16 = jnp.bfloat16

HEAD_DIM = 64
HALF = HEAD_DIM // 2
GRID_W = 64
ROPE_THETA = 10000.0
DA_HEADS = 4
GQ_HEADS = 8
GKV_HEADS = 2
HY_WIDTH = 512
HY_ORDER = 2
HY_EMB = 33
HY_EMB_PAD = 128
HY_BANDS = (HY_EMB - 1) // 2
HY_HIDDEN = 64
HY_MIN_DECAY = math.log(1e-2) / 1.5
HY_MAX_DECAY = math.log(1e-2) / 0.3
LN_EPS = 1e-6
LOG2E = 1.4426950408889634
NEG_BIG = -1e30

VMEM_LIMIT = 56 * 1024 * 1024


def _cp(n_axes):
    return pltpu.CompilerParams(dimension_semantics=("arbitrary",) * n_axes,
                                vmem_limit_bytes=VMEM_LIMIT)


def _tile(n, pref, mult):
    if n <= pref:
        return n
    t = (pref // mult) * mult
    while t >= mult:
        if n % t == 0:
            return t
        t -= mult
    return n


def _ln(x):
    mu = jnp.mean(x, axis=-1, keepdims=True)
    xc = x - mu
    return xc * lax.rsqrt(jnp.mean(xc * xc, axis=-1, keepdims=True) + LN_EPS)


N_KHEADS = DA_HEADS * 2 + GKV_HEADS
KNORM_ROWS = 16


def _proj_all_kernel(x_ref, xc_ref, sh_ref, sc_ref, shc_ref, scc_ref, cos_ref, sin_ref, wqk_ref, gqk_ref,
                     wva_ref, wvb_ref, wtok_ref,
                     q_ref, k_ref, kn_ref, va_ref, vb_ref, u_ref, gt_ref, kt_sc,
                     *, n_lat_tiles, rms_heads, n_hy):
    i = pl.program_id(1)

    @pl.when(i == 0)
    def _():
        kn_ref[...] = jnp.zeros(kn_ref.shape, F32)

    is_latent = i < n_lat_tiles
    xin = jnp.where(is_latent, x_ref[0], xc_ref[0])
    shift = jnp.where(is_latent, sh_ref[0], shc_ref[0])
    scale = jnp.where(is_latent, sc_ref[0], scc_ref[0])
    h = (_ln(xin) * (1.0 + scale) + shift).astype(BF16)
    nt = (((1,), (1,)), ((), ()))

    acc = lax.dot_general(wqk_ref[...], h, nt, preferred_element_type=F32)
    va = lax.dot_general(wva_ref[...], h, nt, preferred_element_type=F32)
    vb = lax.dot_general(wvb_ref[...], h, nt, preferred_element_type=F32)
    tok = jnp.dot(h, wtok_ref[...], preferred_element_type=F32)
    va_ref[0] = va.astype(va_ref.dtype)
    vb_ref[0] = vb.astype(vb_ref.dtype)
    c, s = cos_ref[...], sin_ref[...]
    n_k_rows = kt_sc.shape[0]
    for hd in range(acc.shape[0] // HEAD_DIM):
        lo = hd * HEAD_DIM
        xh = acc[lo:lo + HEAD_DIM]
        if hd in rms_heads:
            xh = xh * lax.rsqrt(jnp.mean(xh * xh, axis=0, keepdims=True) + LN_EPS)
        xh = xh * gqk_ref[lo:lo + HEAD_DIM]
        x1, x2 = xh[:HALF], xh[HALF:]
        r1, r2 = x1 * c - x2 * s, x1 * s + x2 * c
        if lo < n_k_rows:
            kt_sc[lo:lo + HALF] = r1
            kt_sc[lo + HALF:lo + HEAD_DIM] = r2
        else:
            q_ref[0, lo - n_k_rows:lo - n_k_rows + HALF] = r1.astype(q_ref.dtype)
            q_ref[0, lo - n_k_rows + HALF:lo - n_k_rows + HEAD_DIM] = r2.astype(q_ref.dtype)

    k_bf = kt_sc[...].astype(BF16)
    k_ref[0] = k_bf.astype(F32).T.astype(BF16)
    k_sq = k_bf.astype(F32) ** 2
    for hd in range(n_k_rows // HEAD_DIM):
        n2 = jnp.sum(k_sq[hd * HEAD_DIM:(hd + 1) * HEAD_DIM], axis=0, keepdims=True)
        kn_ref[0, hd:hd + 1] = jnp.maximum(kn_ref[0, hd:hd + 1], jnp.max(n2, axis=1, keepdims=True))

    u_ref[0] = tok[:, :n_hy].astype(u_ref.dtype)
    gt_ref[0] = jax.nn.sigmoid(tok[:, n_hy:]).astype(gt_ref.dtype)


def proj_all(x, xc, shift, scale, shift_c, scale_c, cos_t, sin_t, lw):
    B, N, D = x.shape
    Nc = xc.shape[1]
    ntot = N + Nc
    tt = _tile(Nc, 256, 128)
    assert N % tt == 0
    nl = N // tt
    n_k_rows = N_KHEADS * HEAD_DIM
    n_q_rows = lw["w_qk"].shape[0] - n_k_rows
    n_va, n_vb = lw["w_va"].shape[0], lw["w_vb"].shape[0]
    n_tok = lw["w_tok"].shape[1]
    n_hy = 3 * HY_WIDTH
    per_b = lambda b, i: (b, 0, 0)
    const = lambda b, i: (0, 0)
    resident = dict(pipeline_mode=pl.Buffered(1))
    feat = lambda rows: pl.BlockSpec((1, rows, tt), lambda b, i: (b, 0, i))
    tokm = lambda cols: pl.BlockSpec((1, tt, cols), lambda b, i: (b, i, 0))
    sds = jax.ShapeDtypeStruct
    return pl.pallas_call(
        functools.partial(_proj_all_kernel, n_lat_tiles=nl, rms_heads=lw["rms_heads"], n_hy=n_hy),
        out_shape=(sds((B, n_q_rows, ntot), BF16), sds((B, ntot, n_k_rows), BF16),
                   sds((B, KNORM_ROWS, 128), F32), sds((B, n_va, ntot), BF16), sds((B, n_vb, ntot), BF16),
                   sds((B, ntot, n_hy), BF16), sds((B, ntot, n_tok - n_hy), BF16)),
        grid=(B, ntot // tt),
        in_specs=[pl.BlockSpec((1, tt, D), lambda b, i: (b, jnp.minimum(i, nl - 1), 0)),
                  pl.BlockSpec((1, tt, D), lambda b, i: (b, jnp.maximum(i - nl, 0), 0)),
                  pl.BlockSpec((1, 1, D), per_b), pl.BlockSpec((1, 1, D), per_b),
                  pl.BlockSpec((1, 1, D), per_b), pl.BlockSpec((1, 1, D), per_b),
                  pl.BlockSpec((HALF, tt), lambda b, i: (0, i)), pl.BlockSpec((HALF, tt), lambda b, i: (0, i)),
                  pl.BlockSpec(lw["w_qk"].shape, const, **resident), pl.BlockSpec((lw["w_qk"].shape[0], 1), const),
                  pl.BlockSpec(lw["w_va"].shape, const, **resident),
                  pl.BlockSpec(lw["w_vb"].shape, const, **resident),
                  pl.BlockSpec(lw["w_tok"].shape, const, **resident)],
        out_specs=(feat(n_q_rows), tokm(n_k_rows), pl.BlockSpec((1, KNORM_ROWS, 128), per_b),
                   feat(n_va), feat(n_vb), tokm(n_hy), tokm(n_tok - n_hy)),
        scratch_shapes=[pltpu.VMEM((n_k_rows, tt), F32)],
        compiler_params=_cp(2), name="proj_all",
    )(x, xc, shift, scale, shift_c, scale_c, cos_t, sin_t, lw["w_qk"], lw["g_qk"], lw["w_va"], lw["w_vb"],
      lw["w_tok"])


def _resid_ln_tail(x, y, gate, g, b, alpha):
    return _ln(alpha * x + gate * y) * g + b


def _mm_kernel(a_ref, w_ref, o_ref, *, act):
    acc = jnp.dot(a_ref[...], w_ref[...], preferred_element_type=F32)
    if act == "sigmoid":
        acc = jax.nn.sigmoid(acc)
    o_ref[...] = acc.astype(o_ref.dtype)


def matmul(a, w, out_dtype, act=None, tm_pref=1024, tn_pref=512):
    M, K = a.shape
    Nn = w.shape[1]
    tm = _tile(M, tm_pref, 16)
    tn = _tile(Nn, tn_pref, 128)
    return pl.pallas_call(
        functools.partial(_mm_kernel, act=act),
        out_shape=jax.ShapeDtypeStruct((M, Nn), out_dtype),
        grid=(M // tm, Nn // tn),
        in_specs=[pl.BlockSpec((tm, K), lambda i, j: (i, 0)),
                  pl.BlockSpec((K, tn), lambda i, j: (0, j))],
        out_specs=pl.BlockSpec((tm, tn), lambda i, j: (i, j)),
        compiler_params=_cp(2), name="matmul",
    )(a, w)


FLASH_MIN_DENOM = 2.0 ** -60
FLASH_UNROLL = 4


def _flash_kernel(lam_ref, kmax_ref, q_ref, k_ref, v_ref, g_ref, o_ref, q_sc, p_buf, sh_sc, acc_sc, l_sc,
                  *, mode, nj, tk, post_scale, n_kheads, khead0):
    b, h = pl.program_id(0), pl.program_id(1)
    qf = q_ref[0].astype(F32)
    tq = qf.shape[1]
    zero = jnp.zeros((HEAD_DIM, tq), F32)
    qa, qb = qf[:HEAD_DIM], qf[HEAD_DIM:]
    if mode == "da":
        q_sc[0] = jnp.concatenate([qa, zero], axis=0).astype(BF16)
        q_sc[1] = jnp.concatenate([zero, qb], axis=0).astype(BF16)
        k_heads = (khead0 + 2 * h, khead0 + 2 * h + 1)
    else:
        group = h // 2
        for mi, qh in enumerate((qa, qb)):
            q_sc[mi] = jnp.where(group == 0, jnp.concatenate([qh, zero], axis=0),
                                 jnp.concatenate([zero, qh], axis=0)).astype(BF16)
        k_heads = (khead0 + group, khead0 + group)
    for mi, qh in enumerate((qa, qb)):
        q_norm = jnp.sqrt(jnp.sum(qh * qh, axis=0, keepdims=True))
        sh_sc[mi] = q_norm * kmax_ref[b * n_kheads + k_heads[mi]]

    def chunk(j):
        start = j * tk
        return pl.ds(start if isinstance(start, int) else pl.multiple_of(start, tk), tk)

    def stage_exp(j, slot):
        kb = k_ref[0, chunk(j), :]
        for mi in range(2):
            s = jnp.dot(kb, q_sc[mi], preferred_element_type=F32)
            p = jnp.exp2(s - sh_sc[mi])
            p_buf[slot, mi] = p.astype(BF16)
            l_sc[mi] += jnp.sum(p.reshape(tk // 8, 8, tq), axis=0)

    def stage_values(j, slot):
        vb = v_ref[0, :, chunk(j)]
        for mi in range(2):
            acc_sc[mi] += jnp.dot(vb, p_buf[slot, mi], preferred_element_type=F32)

    def accumulate():
        acc_sc[...] = jnp.zeros(acc_sc.shape, F32)
        l_sc[...] = jnp.zeros(l_sc.shape, F32)
        stage_exp(0, 0)
        steady = list(range(1, nj))
        while len(steady) % FLASH_UNROLL:
            t = steady.pop(0)
            stage_exp(t, t % 2)
            stage_values(t - 1, (t - 1) % 2)
        if steady:
            t0 = steady[0]

            def body(i, carry):
                for d in range(FLASH_UNROLL):
                    stage_exp(t0 + FLASH_UNROLL * i + d, (t0 + d) % 2)
                    stage_values(t0 + FLASH_UNROLL * i + d - 1, (t0 + d - 1) % 2)
                return carry

            lax.fori_loop(0, len(steady) // FLASH_UNROLL, body, 0)
        stage_values(nj - 1, (nj - 1) % 2)

    denominators = lambda: [jnp.sum(l_sc[mi], axis=0, keepdims=True) for mi in range(2)]
    accumulate()
    denom_min = jnp.min(jnp.minimum(*denominators()))

    @pl.when(jnp.logical_not(denom_min >= FLASH_MIN_DENOM))
    def _():
        for mi in range(2):
            def max_body(j, m, mi=mi):
                s = jnp.dot(k_ref[0, chunk(j), :], q_sc[mi], preferred_element_type=F32)
                return jnp.maximum(m, jnp.max(s, axis=0, keepdims=True))

            sh_sc[mi] = lax.fori_loop(0, nj, max_body, jnp.full((1, tq), NEG_BIG, F32))
        accumulate()

    outs = [acc_sc[mi] / l for mi, l in enumerate(denominators())]
    if mode == "da":
        o = outs[0] - lam_ref[0] * outs[1]
        o = o * lax.rsqrt(jnp.mean(o * o, axis=0, keepdims=True) + LN_EPS)
        o = o * (g_ref[...] * post_scale)
    else:
        o = jnp.concatenate(outs, axis=0)
    o_ref[0] = o.T.astype(o_ref.dtype)


def flash_attention(q_t, k_tok, v_aug, kmax, lam, gain, *, mode, q_span, kv_span, q_row_block, k_col_block,
                    khead0, post_scale=1.0, tk_pref=768):
    B = q_t.shape[0]
    q0, Nq = q_span
    k0, Nk = kv_span
    tq = _tile(Nq, 1024, 128)
    tk = _tile(Nk, tk_pref, 128)
    assert q0 % tq == 0 and k0 % Nk == 0
    nj = Nk // tk
    dv = 2 * HEAD_DIM if mode == "da" else HEAD_DIM
    n_kheads = kmax.shape[0] // B
    if mode == "da":
        k_map = lambda b, h, i: (b, k0 // Nk, k_col_block + h)
        v_map = lambda b, h, i: (b, h, k0 // Nk)
    else:
        k_map = lambda b, h, i: (b, k0 // Nk, k_col_block)
        v_map = lambda b, h, i: (b, h // 2, k0 // Nk)
    return pl.pallas_call(
        functools.partial(_flash_kernel, mode=mode, nj=nj, tk=tk, post_scale=post_scale,
                          n_kheads=n_kheads, khead0=khead0),
        out_shape=jax.ShapeDtypeStruct((B, Nq, 4 * 128), BF16),
        grid=(B, 4, Nq // tq),
        in_specs=[pl.BlockSpec(memory_space=pltpu.SMEM),
                  pl.BlockSpec(memory_space=pltpu.SMEM),
                  pl.BlockSpec((1, 128, tq), lambda b, h, i: (b, h + q_row_block, i + q0 // tq)),
                  pl.BlockSpec((1, Nk, 128), k_map),
                  pl.BlockSpec((1, dv, Nk), v_map),
                  pl.BlockSpec((128, 1), lambda b, h, i: (0, 0))],
        out_specs=pl.BlockSpec((1, tq, 128), lambda b, h, i: (b, i, h)),
        scratch_shapes=[pltpu.VMEM((2, 128, tq), BF16),
                        pltpu.VMEM((2, 2, tk, tq), BF16),
                        pltpu.VMEM((2, 1, tq), F32),
                        pltpu.VMEM((2, dv, tq), F32),
                        pltpu.VMEM((2, 8, tq), F32)],
        compiler_params=_cp(3), name="flash_" + mode,
    )(lam, kmax, q_t, k_tok, v_aug, gain)


def _shift_rows(x, prev_row, next_row):
    T = x.shape[0]
    row = lax.broadcasted_iota(jnp.int32, x.shape, 0)
    xp = jnp.where(row == 0, prev_row, pltpu.roll(x, 1, axis=0))
    xn = jnp.where(row == T - 1, next_row, pltpu.roll(x, T - 1, axis=0))
    return xp, xn


def _conv3_block(x_ref, p_ref, n_ref, w_ref, b_ref, halo):
    i = pl.program_id(1)
    last = pl.num_programs(1) - 1
    x = x_ref[0].astype(F32)
    prev_row = jnp.where(i > 0, p_ref[0].astype(F32)[halo - 1:halo], 0.0)
    next_row = jnp.where(i < last, n_ref[0].astype(F32)[0:1], 0.0)
    xp, xn = _shift_rows(x, prev_row, next_row)
    return xp * w_ref[0:1] + x * w_ref[1:2] + xn * w_ref[2:3] + b_ref[...]


def _halo_specs(tr, tc, halo, row0, n_rows, col_of):
    per = tr // halo
    n_halo = n_rows // halo
    m0, h0 = row0 // tr, row0 // halo
    return [pl.BlockSpec((1, tr, tc), lambda b, i, j: (b, m0 + i, col_of(j))),
            pl.BlockSpec((1, halo, tc), lambda b, i, j: (b, h0 + jnp.maximum(i * per - 1, 0), col_of(j))),
            pl.BlockSpec((1, halo, tc),
                         lambda b, i, j: (b, h0 + jnp.minimum((i + 1) * per, n_halo - 1), col_of(j)))]


HYENA_HALO = 16


def _dwconv_kernel(x_ref, p_ref, n_ref, w_ref, b_ref, v_ref, x1_ref, x2_ref):
    y = _conv3_block(x_ref, p_ref, n_ref, w_ref, b_ref, HYENA_HALO)
    C = v_ref.shape[2]
    v_ref[0] = y[:, :C].astype(v_ref.dtype)
    x1_ref[0] = y[:, C:2 * C].astype(x1_ref.dtype)
    x2_ref[0] = y[:, 2 * C:].astype(x2_ref.dtype)


def hyena_short_conv(u, w, b, row0, L):
    B, _, C3 = u.shape
    C = C3 // 3
    tr = _tile(L, 512, HYENA_HALO)
    assert row0 % tr == 0
    zero = lambda j: 0
    out = jax.ShapeDtypeStruct((B, L, C), BF16)
    ospec = pl.BlockSpec((1, tr, C), lambda b, i, j: (b, i, 0))
    return pl.pallas_call(
        _dwconv_kernel,
        out_shape=(out, out, out),
        grid=(B, L // tr, 1),
        in_specs=_halo_specs(tr, C3, HYENA_HALO, row0, L, zero) + [
            pl.BlockSpec((3, C3), lambda b, i, j: (0, 0)),
            pl.BlockSpec((1, C3), lambda b, i, j: (0, 0))],
        out_specs=(ospec, ospec, ospec),
        compiler_params=_cp(3), name="hyena_short_conv",
    )(u, u, u, w, b)


FFN_HALO = 16
FFN_CHUNK = 256


def _ffn_kernel(h_ref, hp_ref, hn_ref, wup_ref, cw_ref, cb_ref, wdn_ref, x_ref, gate_ref, g_ref, b_ref, o_ref,
                act_sc, *, alpha):
    i = pl.program_id(1)
    tm = h_ref.shape[1]
    dff = wdn_ref.shape[0]
    rows = tm + 2 * FFN_HALO
    h_prev = jnp.where(i == 0, jnp.zeros_like(hp_ref[0]), hp_ref[0])
    h_next = jnp.where(i == pl.num_programs(1) - 1, jnp.zeros_like(hn_ref[0]), hn_ref[0])
    h_ext = jnp.concatenate([h_prev, h_ref[0], h_next], axis=0)

    def conv_cols(c0):
        u = jnp.dot(h_ext, wup_ref[:, c0:c0 + FFN_CHUNK], preferred_element_type=F32)
        mid = slice(FFN_HALO, FFN_HALO + tm)
        u_prev = pltpu.roll(u, 1, axis=0)[mid]
        u_next = pltpu.roll(u, rows - 1, axis=0)[mid]
        w = cw_ref[:, c0:c0 + FFN_CHUNK]
        return u_prev * w[0:1] + u[mid] * w[1:2] + u_next * w[2:3] + cb_ref[:, c0:c0 + FFN_CHUNK]

    for c0 in range(0, dff, FFN_CHUNK):
        val = conv_cols(c0)
        gate = conv_cols(dff + c0)
        act_sc[:, c0:c0 + FFN_CHUNK] = (gate * jax.nn.sigmoid(gate) * val).astype(BF16)
    y = jnp.dot(act_sc[...], wdn_ref[...], preferred_element_type=F32)
    o_ref[0] = _resid_ln_tail(x_ref[0], y, gate_ref[0], g_ref[...], b_ref[...], alpha)


def conv_ffn_resid_ln(h, w_up, conv_w, conv_b, w_down, x, gate, g, b, alpha):
    B, N, D = h.shape
    dff = w_down.shape[0]
    tm = _tile(N, 512, FFN_HALO)
    per = tm // FFN_HALO
    n_halo = N // FFN_HALO
    row = lambda b_, i: (b_, i, 0)
    per_b = lambda b_, i: (b_, 0, 0)
    const2 = lambda b_, i: (0, 0)
    resident = dict(pipeline_mode=pl.Buffered(1))
    return pl.pallas_call(
        functools.partial(_ffn_kernel, alpha=alpha),
        out_shape=jax.ShapeDtypeStruct((B, N, D), F32),
        grid=(B, N // tm),
        in_specs=[pl.BlockSpec((1, tm, D), row),
                  pl.BlockSpec((1, FFN_HALO, D), lambda b_, i: (b_, jnp.maximum(i * per - 1, 0), 0)),
                  pl.BlockSpec((1, FFN_HALO, D), lambda b_, i: (b_, jnp.minimum((i + 1) * per, n_halo - 1), 0)),
                  pl.BlockSpec((D, 2 * dff), const2, **resident),
                  pl.BlockSpec((3, 2 * dff), const2), pl.BlockSpec((1, 2 * dff), const2),
                  pl.BlockSpec((dff, D), const2, **resident),
                  pl.BlockSpec((1, tm, D), row), pl.BlockSpec((1, 1, D), per_b),
                  pl.BlockSpec((1, D), const2), pl.BlockSpec((1, D), const2)],
        out_specs=pl.BlockSpec((1, tm, D), row),
        scratch_shapes=[pltpu.VMEM((tm, dff), BF16)],
        compiler_params=_cp(2), name="conv_ffn_resid_ln",
    )(h, h, h, w_up, conv_w, conv_b, w_down, x, gate, g, b)


def _merge_kernel(a_ref, b_ref, c_ref, gt_ref, wpa_ref, wpb_ref, wpc_ref, wo_ref, x_ref, gate_ref,
                  g_ref, bb_ref, sh_ref, sc_ref, ox_ref, oh_ref, *, alpha):
    D = wo_ref.shape[0]
    tm = x_ref.shape[1]
    n_sub = 2 if tm % 32 == 0 else 1
    for r in range(n_sub):
        rows = slice(r * tm // n_sub, (r + 1) * tm // n_sub)
        gt = gt_ref[0, rows]
        m = gt[:, :D].astype(F32) * jnp.dot(a_ref[0, rows], wpa_ref[...], preferred_element_type=F32)
        m = m + gt[:, D:2 * D].astype(F32) * jnp.dot(b_ref[0, rows], wpb_ref[...], preferred_element_type=F32)
        m = m + gt[:, 2 * D:].astype(F32) * jnp.dot(c_ref[0, rows].astype(BF16), wpc_ref[...],
                                                    preferred_element_type=F32)
        y = jnp.dot(m.astype(BF16), wo_ref[...], preferred_element_type=F32)
        xn = _resid_ln_tail(x_ref[0, rows], y, gate_ref[0], g_ref[...], bb_ref[...], alpha)
        ox_ref[0, rows] = xn
        oh_ref[0, rows] = (_ln(xn) * (1.0 + sc_ref[0]) + sh_ref[0]).astype(oh_ref.dtype)


def merge_resid_ln(a, b, c, gates, tok0, wpa, wpb, wpc, wo, x, gate, g, bb, sh, sc, alpha):
    B, N, D = x.shape
    W = a.shape[2]
    tm = _tile(N, 512, 16)
    assert tok0 % tm == 0
    row = lambda b_, i: (b_, i, 0)
    per_b = lambda b_, i: (b_, 0, 0)
    const2 = lambda b_, i: (0, 0)
    return pl.pallas_call(
        functools.partial(_merge_kernel, alpha=alpha),
        out_shape=(jax.ShapeDtypeStruct((B, N, D), F32), jax.ShapeDtypeStruct((B, N, D), BF16)),
        grid=(B, N // tm),
        in_specs=[pl.BlockSpec((1, tm, W), row), pl.BlockSpec((1, tm, W), row), pl.BlockSpec((1, tm, W), row),
                  pl.BlockSpec((1, tm, 3 * D), lambda b_, i: (b_, tok0 // tm + i, 0)),
                  pl.BlockSpec((W, D), const2), pl.BlockSpec((W, D), const2), pl.BlockSpec((W, D), const2),
                  pl.BlockSpec((D, D), const2),
                  pl.BlockSpec((1, tm, D), row), pl.BlockSpec((1, 1, D), per_b),
                  pl.BlockSpec((1, D), const2), pl.BlockSpec((1, D), const2),
                  pl.BlockSpec((1, 1, D), per_b), pl.BlockSpec((1, 1, D), per_b)],
        out_specs=(pl.BlockSpec((1, tm, D), row), pl.BlockSpec((1, tm, D), row)),
        compiler_params=_cp(2), name="merge_resid_ln",
    )(a, b, c, gates, wpa, wpb, wpc, wo, x, gate, g, bb, sh, sc)


def _split_bf16(x):
    hi = x.astype(BF16)
    lo = (x - hi.astype(F32)).astype(BF16)
    return hi, lo


def _dot3(a, b):
    ah, al = _split_bf16(a)
    bh, bl = _split_bf16(b)
    d = functools.partial(jnp.dot, preferred_element_type=F32)
    return d(ah, bh) + (d(ah, bl) + d(al, bh))


def _filter_kernel(z_ref, w1_ref, b1_ref, w2_ref, b2_ref, w3_ref, fr_ref, dec_ref, h_ref, s_ref, *, zero_block):
    i = pl.program_id(0)

    @pl.when(i == 0)
    def _():
        s_ref[...] = jnp.zeros(s_ref.shape, F32)

    fr = fr_ref[...]
    hid = jnp.sin(fr * (_dot3(z_ref[...], w1_ref[...]) + b1_ref[...]))
    hid = jnp.sin(fr * (_dot3(hid, w2_ref[...]) + b2_ref[...]))
    h = _dot3(hid, w3_ref[0])
    dec = dec_ref[...]
    C = dec.shape[1]
    n_ord = h.shape[1] // C
    h = h * jnp.concatenate([dec] * n_ord, axis=1)
    s_ref[...] += jnp.sum(jnp.abs(h), axis=0, keepdims=True)
    row = lax.broadcasted_iota(jnp.int32, h.shape, 0)
    h = jnp.where((row == 0) & (i == zero_block), 0.0, h)
    for o in range(n_ord):
        h_ref[o] = h[:, o * C:(o + 1) * C]


def hyena_filter_mlp(z2, w1, b1, w2, b2, w3_dir, freq, decay2):
    n2, E = z2.shape
    Hh = w2.shape[0]
    OC = w3_dir.shape[2]
    C = decay2.shape[1]
    n_ord = OC // C
    L = n2 // 2
    tr = _tile(L, 512, 8)
    nblk = n2 // tr
    c2 = lambda i: (0, 0)
    return pl.pallas_call(
        functools.partial(_filter_kernel, zero_block=L // tr),
        out_shape=(jax.ShapeDtypeStruct((n_ord, n2, C), F32), jax.ShapeDtypeStruct((1, OC), F32)),
        grid=(nblk,),
        in_specs=[pl.BlockSpec((tr, E), lambda i: (i, 0)), pl.BlockSpec((E, Hh), c2), pl.BlockSpec((1, Hh), c2),
                  pl.BlockSpec((Hh, Hh), c2), pl.BlockSpec((1, Hh), c2),
                  pl.BlockSpec((1, Hh, OC), lambda i: (i // (nblk // 2), 0, 0)),
                  pl.BlockSpec((1, Hh), c2), pl.BlockSpec((tr, C), lambda i: (i, 0))],
        out_specs=(pl.BlockSpec((n_ord, tr, C), lambda i: (0, i, 0)), pl.BlockSpec((1, OC), c2)),
        compiler_params=_cp(1), name="hyena_filter_mlp",
    )(z2, w1, b1, w2, b2, w3_dir, freq, decay2)


def _mm_split(m_hi, m_lo, x, passes):
    d = functools.partial(jnp.dot, preferred_element_type=F32)
    if passes == 1:
        return d(m_hi, x.astype(BF16))
    xh, xl = _split_bf16(x)
    return d(m_hi, xh) + (d(m_lo, xh) + d(m_hi, xl))


FFT_ROWS = 8
FFT_BLOCK = 16


def _level1_apply(m_refs, x, r_out, passes):
    r_in, _, C = x.shape
    x = x.astype(F32)
    m_hi = m_refs[0][...]
    m_lo = m_refs[1][...] if passes > 1 else None
    groups = []
    for g in range(FFT_BLOCK // FFT_ROWS):
        xg = x[:, g * FFT_ROWS:(g + 1) * FFT_ROWS].reshape(r_in * FFT_ROWS, C)
        groups.append(_mm_split(m_hi, m_lo, xg, passes).reshape(r_out, FFT_ROWS, C))
    return jnp.concatenate(groups, axis=1)


def _level1_kernel(*refs, passes):
    n_mat = 1 if passes == 1 else 2
    x_ref, o_ref = refs[n_mat:]
    o_ref[0] = _level1_apply(refs[:n_mat], x_ref[0], o_ref.shape[1], passes).astype(o_ref.dtype)


def _level1_gate_kernel(*refs, passes):
    n_mat = 1 if passes == 1 else 2
    x_ref, z_ref, xg_ref, bias_ref, o_ref = refs[n_mat:]
    y = _level1_apply(refs[:n_mat], x_ref[0], o_ref.shape[1], passes)
    o_ref[0] = (xg_ref[0].astype(F32) * (y + bias_ref[...] * z_ref[0].astype(F32))).astype(o_ref.dtype)


def fft_level1(m_hi, m_lo, x, passes, gate_args=None, out_dtype=F32):
    P, R_in, Nb, C = x.shape
    R_out = m_hi.shape[0] // FFT_ROWS
    mats = (m_hi,) if passes == 1 else (m_hi, m_lo)
    mspec = pl.BlockSpec(m_hi.shape, lambda p, j: (0, 0), pipeline_mode=pl.Buffered(1))
    xspec = pl.BlockSpec((1, R_in, FFT_BLOCK, C), lambda p, j: (p, 0, j, 0))
    ospec = pl.BlockSpec((1, R_out, FFT_BLOCK, C), lambda p, j: (p, 0, j, 0))
    if gate_args is None:
        kern = functools.partial(_level1_kernel, passes=passes)
        ins, specs = (*mats, x), [mspec] * len(mats) + [xspec]
    else:
        z, xg, bias = gate_args
        kern = functools.partial(_level1_gate_kernel, passes=passes)
        ins = (*mats, x, z, xg, bias)
        specs = [mspec] * len(mats) + [xspec, ospec, ospec, pl.BlockSpec((1, C), lambda p, j: (0, 0))]
    return pl.pallas_call(
        kern, out_shape=jax.ShapeDtypeStruct((P, R_out, Nb, C), out_dtype),
        grid=(P, Nb // FFT_BLOCK), in_specs=specs, out_specs=ospec,
        compiler_params=_cp(2), name="fft_level1",
    )(*ins)


FFT_MID_K1 = 8


def _fft_mid_kernel(*refs, passes):
    n_mat = 2 if passes == 1 else 4
    gh_ref, gih_ref = refs[0], refs[1]
    gl_ref, gil_ref = (refs[2], refs[3]) if passes > 1 else (None, None)
    h_ref, a_ref, o_ref = refs[n_mat:]
    n_pairs, _, kb, nb, C = a_ref.shape
    for kk in range(kb):
        hr, hi = h_ref[0, 0, kk], h_ref[0, 1, kk]
        for p in range(n_pairs):
            x = a_ref[p, :, kk].reshape(2 * nb, C)
            X = _mm_split(gh_ref[kk], None if gl_ref is None else gl_ref[kk], x, passes)
            xr, xi = X[:nb], X[nb:]
            Y = jnp.concatenate([xr * hr - xi * hi, xr * hi + xi * hr], axis=0)
            Bv = _mm_split(gih_ref[kk], None if gil_ref is None else gil_ref[kk], Y, passes)
            o_ref[p, :, kk] = Bv.reshape(2, nb, C).astype(o_ref.dtype)


def fft_mid(g_hi, g_lo, gi_hi, gi_lo, spectra, order, a5, passes):
    P, _, Na, Nb, C = a5.shape
    kb = FFT_MID_K1
    gspec = pl.BlockSpec((kb, 2 * Nb, 2 * Nb), lambda k: (k, 0, 0))
    mats = (g_hi, gi_hi) if passes == 1 else (g_hi, gi_hi, g_lo, gi_lo)
    blk = pl.BlockSpec((P, 2, kb, Nb, C), lambda k: (0, 0, k, 0, 0))
    return pl.pallas_call(
        functools.partial(_fft_mid_kernel, passes=passes),
        out_shape=jax.ShapeDtypeStruct(a5.shape, a5.dtype),
        grid=(Na // kb,),
        in_specs=[gspec] * len(mats) + [pl.BlockSpec((1, 2, kb, Nb, C), lambda k: (order, 0, k, 0, 0)), blk],
        out_specs=blk,
        compiler_params=_cp(1), name="fft_mid",
    )(*mats, spectra, a5)


def _fft_spec_kernel(*refs, passes):
    n_mat = 1 if passes == 1 else 2
    gh_ref = refs[0]
    gl_ref = refs[1] if passes > 1 else None
    inv_ref, a_ref, o_ref = refs[n_mat:]
    n_filt, _, kb, nb, C = a_ref.shape
    for kk in range(kb):
        for p in range(n_filt):
            x = a_ref[p, :, kk].reshape(2 * nb, C)
            X = _mm_split(gh_ref[kk], None if gl_ref is None else gl_ref[kk], x, passes) * inv_ref[p]
            o_ref[p, :, kk] = X.reshape(2, nb, C)


def fft_filter_spectrum(g_hi, g_lo, inv_norm, a5, passes):
    P, _, Na, Nb, C = a5.shape
    kb = FFT_MID_K1
    mats = (g_hi,) if passes == 1 else (g_hi, g_lo)
    gspec = pl.BlockSpec((kb, 2 * Nb, 2 * Nb), lambda k: (k, 0, 0))
    blk = pl.BlockSpec((P, 2, kb, Nb, C), lambda k: (0, 0, k, 0, 0))
    return pl.pallas_call(
        functools.partial(_fft_spec_kernel, passes=passes),
        out_shape=jax.ShapeDtypeStruct(a5.shape, F32),
        grid=(Na // kb,),
        in_specs=[gspec] * len(mats) + [pl.BlockSpec((P, 1, C), lambda k: (0, 0, 0)), blk],
        out_specs=blk,
        compiler_params=_cp(1), name="fft_filter_spectrum",
    )(*mats, inv_norm, a5)


def _fft_factors(n_fft):
    na = 1 << (int(math.log2(n_fft)) // 2)
    return na, n_fft // na


def _fft_tables(L):
    n_fft = 2 * L
    na, nb = _fft_factors(n_fft)
    nh = na // 2
    iota = lambda shape, axis: lax.broadcasted_iota(jnp.int32, shape, axis)

    def cis(num, den):
        ang = (2.0 * math.pi / den) * (num % den).astype(F32)
        return jnp.cos(ang), -jnp.sin(ang)

    def block(diag_sign, pr, pc, cr, ci):
        return jnp.where(pr == pc, cr, jnp.where(pr == 1, diag_sign * ci, -diag_sign * ci))

    def kron_matrix(rows, cols, entry):
        shape = (rows * FFT_ROWS, cols * FFT_ROWS)
        I, J = iota(shape, 0), iota(shape, 1)
        m = entry(I // FFT_ROWS, J // FFT_ROWS)
        return _split_bf16(jnp.where(I % FFT_ROWS == J % FFT_ROWS, m, 0.0))

    def fwd_entry(i, j):
        cr, ci = cis((i % na) * (j % nh), na)
        return block(1, i // na, j // nh, cr, ci)

    def flt_entry(i, j):
        cr, ci = cis((i % na) * j, na)
        return jnp.where(i // na == 0, cr, ci)

    def inv_entry(i, j):
        cr, ci = cis((i % nh) * (j % na), na)
        return block(-1, i // nh, j // na, cr, ci) / n_fft

    gshape = (na, 2 * nb, 2 * nb)
    k1, gi_, gj_ = iota(gshape, 0), iota(gshape, 1), iota(gshape, 2)
    cr, ci = cis((gi_ % nb) * (gj_ % nb) * na + k1 * (gj_ % nb), n_fft)
    g = _split_bf16(block(1, gi_ // nb, gj_ // nb, cr, ci))
    cr, ci = cis((gj_ % nb) * (gi_ % nb) * na + k1 * (gi_ % nb), n_fft)
    ginv = _split_bf16(block(-1, gi_ // nb, gj_ // nb, cr, ci))

    C = HY_WIDTH
    n = jnp.arange(n_fft, dtype=jnp.int32)
    pos = jnp.where(n < L, n, jnp.where(n == L, 0, n_fft - n)).astype(F32)[:, None]
    t = pos * (1.0 / (L - 1))
    f = jnp.linspace(1e-4, HY_BANDS - 1, HY_BANDS, dtype=F32)
    ang = (2.0 * math.pi / L) * pos * f[None, :]
    z2 = jnp.concatenate([t, jnp.cos(ang), -jnp.sin(ang), jnp.zeros((n_fft, HY_EMB_PAD - HY_EMB), F32)], axis=-1)
    deltas = jnp.abs(jnp.linspace(HY_MIN_DECAY, HY_MAX_DECAY, C, dtype=F32))
    decay2 = jnp.exp(-t * deltas)
    return dict(na=na, nb=nb, fwd=kron_matrix(2 * na, na, fwd_entry), flt=kron_matrix(2 * na, na, flt_entry),
                inv=kron_matrix(na, 2 * na, inv_entry), g=g, ginv=ginv, z2=z2, decay2=decay2)


def hyena_filter_spectra(w1, b1, w2, b2, w3, freq, tabs, passes):
    C = HY_WIDTH
    na, nb = tabs["na"], tabs["nb"]
    w1p = jnp.pad(w1, ((0, HY_EMB_PAD - HY_EMB), (0, 0)))
    w3_dir = jnp.transpose(w3.reshape(HY_HIDDEN, HY_ORDER, 2, C), (2, 0, 1, 3)).reshape(2, HY_HIDDEN, HY_ORDER * C)
    filt, s = hyena_filter_mlp(tabs["z2"], w1p, b1[None], w2, b2[None], w3_dir, freq[None], tabs["decay2"])
    inv_norm = (1.0 / s).reshape(HY_ORDER, 1, C)
    a = fft_level1(*tabs["flt"], filt.reshape(HY_ORDER, na, nb, C), passes,
                   out_dtype=BF16 if passes == 1 else F32)
    return fft_filter_spectrum(*tabs["g"], inv_norm, a.reshape(HY_ORDER, 2, na, nb, C), passes)


def hyena_long_conv_gate(z, xg, bias, spectra, order, tabs, passes):
    B, L, C = z.shape
    na, nb = tabs["na"], tabs["nb"]
    P = B // 2
    nat = lambda a: a.reshape(P, na, nb, C)
    mid_dtype = BF16 if passes == 1 else F32
    a = fft_level1(*tabs["fwd"], nat(z), passes, out_dtype=mid_dtype)
    bv = fft_mid(*tabs["g"], *tabs["ginv"], spectra, order, a.reshape(P, 2, na, nb, C), passes)
    y = fft_level1(*tabs["inv"], bv.reshape(P, 2 * na, nb, C), passes,
                   gate_args=(nat(z), nat(xg), bias[None]), out_dtype=z.dtype)
    return y.reshape(B, L, C)


def hyena_mix(u, row0, L, spectra, conv_w, conv_b, bias, tabs, passes):
    v, x1, x2 = hyena_short_conv(u, conv_w, conv_b[None], row0, L)
    zz = hyena_long_conv_gate(v, x1, bias[0], spectra, 0, tabs, passes)
    return hyena_long_conv_gate(zz, x2, bias[1], spectra, 1, tabs, passes)


FFT_PASSES = 1
FILTER_PASSES = 1


def _rope_tables_t(n):
    rows = n // GRID_W
    r = jnp.repeat(jnp.arange(rows, dtype=F32), GRID_W)
    col = jnp.tile(jnp.arange(GRID_W, dtype=F32), rows)
    axis_dim = HEAD_DIM // 2
    inv = ROPE_THETA ** (-jnp.arange(0, axis_dim, 2, dtype=F32) / axis_dim)
    ang = jnp.concatenate([r[:, None] * inv, col[:, None] * inv], axis=-1)
    return jnp.cos(ang).T, jnp.sin(ang).T


def _layer_weights(l, w_in, gq_qn, gq_kn):
    w = w_in[l]
    d_qk = DA_HEADS * 2 * HEAD_DIM
    gq_kv = GKV_HEADS * HEAD_DIM
    gq_q = GQ_HEADS * HEAD_DIM
    o = np.cumsum([0, d_qk, d_qk, gq_kv, gq_kv, d_qk, gq_q, 3 * HY_WIDTH])
    ka, va, kb, vb, qa, qb, hy = (w[:, o[i]:o[i + 1]] for i in range(7))
    gates = w[:, o[7]:]
    qscale = (HEAD_DIM ** -0.5) * LOG2E
    w_qk = jnp.concatenate([ka, kb, qa, qb], axis=1).T.astype(BF16)
    g_qk = jnp.concatenate([jnp.ones((d_qk,), F32), jnp.tile(gq_kn[l], GKV_HEADS),
                            jnp.full((d_qk,), qscale, F32), jnp.tile(gq_qn[l], GQ_HEADS) * qscale])[:, None]
    n_ka, n_kb, n_qa = d_qk // HEAD_DIM, gq_kv // HEAD_DIM, d_qk // HEAD_DIM
    rms_heads = frozenset(range(n_ka, n_ka + n_kb)) | frozenset(range(n_ka + n_kb + n_qa,
                                                                       n_ka + n_kb + n_qa + gq_q // HEAD_DIM))
    return dict(w_qk=w_qk, g_qk=g_qk, rms_heads=rms_heads, w_va=va.T.astype(BF16), w_vb=vb.T.astype(BF16),
                w_tok=jnp.concatenate([hy, gates], axis=1).astype(BF16))


def kernel(x, c, ctx, c_ctx, w_mod, b_mod, w_in, da_lq1, da_lk1, da_lq2, da_lk2, da_subln, gq_qn, gq_kn,
           hy_conv_w, hy_conv_b, hy_w1, hy_b1, hy_w2, hy_b2, hy_w3, hy_freq, hy_bias, w_pa, w_pb, w_pc, w_o,
           ln1_g, ln1_b, w_up, ffn_conv_w, ffn_conv_b, w_down, ln2_g, ln2_b):
    B, n_lat, D = x.shape
    n_ctx = ctx.shape[1]
    depth = w_in.shape[0]
    alpha = (2 * depth) ** 0.25
    cos_l, sin_l = _rope_tables_t(n_lat)
    cos_t = jnp.concatenate([cos_l, jnp.ones((HALF, n_ctx), F32)], axis=1)
    sin_t = jnp.concatenate([sin_l, jnp.zeros((HALF, n_ctx), F32)], axis=1)
    lat, ctx_span, all_span = (0, n_lat), (n_lat, n_ctx), (0, n_lat + n_ctx)
    tabs_l = _fft_tables(n_lat)
    tabs_c = _fft_tables(n_ctx)
    xc = ctx

    cond = jnp.concatenate([c, c_ctx[None]], axis=0)
    cond = jnp.pad(jax.nn.silu(cond), ((0, 16 - (B + 1) % 16), (0, 0))).astype(BF16)

    for l in range(depth):
        last = l == depth - 1
        lam_init = 0.8 - 0.6 * math.exp(-0.3 * l)
        lam = (jnp.exp(jnp.sum(da_lq1[l] * da_lk1[l])) - jnp.exp(jnp.sum(da_lq2[l] * da_lk2[l])) + lam_init)
        lam = lam.reshape(1).astype(F32)
        mod = matmul(cond, w_mod[l].astype(BF16), F32, tn_pref=1024) + b_mod[l]
        sh1, sc1, g1, sh2, sc2, g2 = [m[:, None, :] for m in jnp.split(mod[:B], 6, axis=-1)]
        mc = [jnp.broadcast_to(m[None, None, :], (B, 1, D)) for m in jnp.split(mod[B], 6)]
        lw = _layer_weights(l, w_in, gq_qn, gq_kn)
        subln = da_subln[l][:, None]
        hy_mlp = (hy_w1[l], hy_b1[l], hy_w2[l], hy_b2[l], hy_w3[l], hy_freq[l])
        spectra_l = hyena_filter_spectra(*hy_mlp, tabs_l, FILTER_PASSES)
        wpa, wpb, wpc, wo = (w[l].astype(BF16) for w in (w_pa, w_pb, w_pc, w_o))
        wup, wdn = w_up[l].astype(BF16), w_down[l].astype(BF16)
        ln1 = (ln1_g[l][None], ln1_b[l][None])
        ln2 = (ln2_g[l][None], ln2_b[l][None])

        q_all, k_tok, k_n2, va, vb, u, gates = proj_all(x, xc, sh1, sc1, mc[0], mc[1], cos_t, sin_t, lw)
        kmax = jnp.sqrt(k_n2[:, :N_KHEADS, 0]).reshape(B * N_KHEADS)
        n_da_maps = DA_HEADS * 2

        def attend(q_span, kv_span):
            da = flash_attention(q_all, k_tok, va, kmax, lam, subln, mode="da", q_span=q_span, kv_span=kv_span,
                                 q_row_block=0, k_col_block=0, khead0=0, post_scale=1.0 - lam_init)
            gq = flash_attention(q_all, k_tok, vb, kmax, lam, subln, mode="gqa", q_span=q_span, kv_span=kv_span,
                                 q_row_block=n_da_maps * HEAD_DIM // 128, k_col_block=n_da_maps * HEAD_DIM // 128,
                                 khead0=n_da_maps)
            return da, gq

        hy_args = (hy_conv_w[l], hy_conv_b[l], hy_bias[l])
        a_l, b_l = attend(lat, all_span)
        c_l = hyena_mix(u, *lat, spectra_l, *hy_args, tabs_l, FFT_PASSES)
        x_new, h2 = merge_resid_ln(a_l, b_l, c_l, gates, lat[0], wpa, wpb, wpc, wo, x, g1, *ln1,
                                   sh2, sc2, alpha)
        x_new = conv_ffn_resid_ln(h2, wup, ffn_conv_w[l], ffn_conv_b[l][None], wdn, x_new, g2, *ln2, alpha)

        if not last:
            spectra_c = hyena_filter_spectra(*hy_mlp, tabs_c, FILTER_PASSES)
            a_c, b_c = attend(ctx_span, ctx_span)
            c_c = hyena_mix(u, *ctx_span, spectra_c, *hy_args, tabs_c, FFT_PASSES)
            xc, hc2 = merge_resid_ln(a_c, b_c, c_c, gates, ctx_span[0], wpa, wpb, wpc, wo, xc, mc[2],
                                     *ln1, mc[3], mc[4], alpha)
            xc = conv_ffn_resid_ln(hc2, wup, ffn_conv_w[l], ffn_conv_b[l][None], wdn, xc, mc[5], *ln2, alpha)
        x = x_new
    return x
```

```python
import functools
import math

import numpy as np
import jax
import jax.numpy as jnp
from jax import lax
from jax.experimental import pallas as pl
from jax.experimental.pallas import tpu as pltpu

F32 = jnp.float32
BF16 = jnp.bfloat16

HEAD_DIM = 64
HALF = HEAD_DIM // 2
GRID_W = 64
ROPE_THETA = 10000.0
DA_HEADS = 4
GQ_HEADS = 8
GKV_HEADS = 2
HY_WIDTH = 512
HY_ORDER = 2
HY_EMB = 33
HY_EMB_PAD = 128
HY_BANDS = (HY_EMB - 1) // 2
HY_HIDDEN = 64
HY_MIN_DECAY = math.log(1e-2) / 1.5
HY_MAX_DECAY = math.log(1e-2) / 0.3
LN_EPS = 1e-6
LOG2E = 1.4426950408889634
NEG_BIG = -1e30

VMEM_LIMIT = 56 * 1024 * 1024


def _cp(n_axes):
    return pltpu.CompilerParams(dimension_semantics=("arbitrary",) * n_axes,
                                vmem_limit_bytes=VMEM_LIMIT)


def _tile(n, pref, mult):
    if n <= pref:
        return n
    t = (pref // mult) * mult
    while t >= mult:
        if n % t == 0:
            return t
        t -= mult
    return n


def _ln(x):
    mu = jnp.mean(x, axis=-1, keepdims=True)
    xc = x - mu
    return xc * lax.rsqrt(jnp.mean(xc * xc, axis=-1, keepdims=True) + LN_EPS)


N_KHEADS = DA_HEADS * 2 + GKV_HEADS
KNORM_ROWS = 16


def _proj_all_kernel(x_ref, xc_ref, sh_ref, sc_ref, shc_ref, scc_ref, cos_ref, sin_ref, wqk_ref, gqk_ref,
                     wva_ref, wvb_ref, wtok_ref,
                     q_ref, k_ref, kn_ref, va_ref, vb_ref, u_ref, gt_ref, kt_sc,
                     *, n_lat_tiles, rms_heads, n_hy):
    i = pl.program_id(1)

    @pl.when(i == 0)
    def _():
        kn_ref[...] = jnp.zeros(kn_ref.shape, F32)

    is_latent = i < n_lat_tiles
    xin = jnp.where(is_latent, x_ref[0], xc_ref[0])
    shift = jnp.where(is_latent, sh_ref[0], shc_ref[0])
    scale = jnp.where(is_latent, sc_ref[0], scc_ref[0])
    h = (_ln(xin) * (1.0 + scale) + shift).astype(BF16)
    nt = (((1,), (1,)), ((), ()))

    acc = lax.dot_general(wqk_ref[...], h, nt, preferred_element_type=F32)
    va = lax.dot_general(wva_ref[...], h, nt, preferred_element_type=F32)
    vb = lax.dot_general(wvb_ref[...], h, nt, preferred_element_type=F32)
    tok = jnp.dot(h, wtok_ref[...], preferred_element_type=F32)
    va_ref[0] = va.astype(va_ref.dtype)
    vb_ref[0] = vb.astype(vb_ref.dtype)
    c, s = cos_ref[...], sin_ref[...]
    n_k_rows = kt_sc.shape[0]
    for hd in range(acc.shape[0] // HEAD_DIM):
        lo = hd * HEAD_DIM
        xh = acc[lo:lo + HEAD_DIM]
        if hd in rms_heads:
            xh = xh * lax.rsqrt(jnp.mean(xh * xh, axis=0, keepdims=True) + LN_EPS)
        xh = xh * gqk_ref[lo:lo + HEAD_DIM]
        x1, x2 = xh[:HALF], xh[HALF:]
        r1, r2 = x1 * c - x2 * s, x1 * s + x2 * c
        if lo < n_k_rows:
            kt_sc[lo:lo + HALF] = r1
            kt_sc[lo + HALF:lo + HEAD_DIM] = r2
        else:
            q_ref[0, lo - n_k_rows:lo - n_k_rows + HALF] = r1.astype(q_ref.dtype)
            q_ref[0, lo - n_k_rows + HALF:lo - n_k_rows + HEAD_DIM] = r2.astype(q_ref.dtype)

    k_bf = kt_sc[...].astype(BF16)
    k_ref[0] = k_bf.astype(F32).T.astype(BF16)
    k_sq = k_bf.astype(F32) ** 2
    for hd in range(n_k_rows // HEAD_DIM):
        n2 = jnp.sum(k_sq[hd * HEAD_DIM:(hd + 1) * HEAD_DIM], axis=0, keepdims=True)
        kn_ref[0, hd:hd + 1] = jnp.maximum(kn_ref[0, hd:hd + 1], jnp.max(n2, axis=1, keepdims=True))

    u_ref[0] = tok[:, :n_hy].astype(u_ref.dtype)
    gt_ref[0] = jax.nn.sigmoid(tok[:, n_hy:]).astype(gt_ref.dtype)


def proj_all(x, xc, shift, scale, shift_c, scale_c, cos_t, sin_t, lw):
    B, N, D = x.shape
    Nc = xc.shape[1]
    ntot = N + Nc
    tt = _tile(Nc, 256, 128)
    assert N % tt == 0
    nl = N // tt
    n_k_rows = N_KHEADS * HEAD_DIM
    n_q_rows = lw["w_qk"].shape[0] - n_k_rows
    n_va, n_vb = lw["w_va"].shape[0], lw["w_vb"].shape[0]
    n_tok = lw["w_tok"].shape[1]
    n_hy = 3 * HY_WIDTH
    per_b = lambda b, i: (b, 0, 0)
    const = lambda b, i: (0, 0)
    resident = dict(pipeline_mode=pl.Buffered(1))
    feat = lambda rows: pl.BlockSpec((1, rows, tt), lambda b, i: (b, 0, i))
    tokm = lambda cols: pl.BlockSpec((1, tt, cols), lambda b, i: (b, i, 0))
    sds = jax.ShapeDtypeStruct
    return pl.pallas_call(
        functools.partial(_proj_all_kernel, n_lat_tiles=nl, rms_heads=lw["rms_heads"], n_hy=n_hy),
        out_shape=(sds((B, n_q_rows, ntot), BF16), sds((B, ntot, n_k_rows), BF16),
                   sds((B, KNORM_ROWS, 128), F32), sds((B, n_va, ntot), BF16), sds((B, n_vb, ntot), BF16),
                   sds((B, ntot, n_hy), BF16), sds((B, ntot, n_tok - n_hy), BF16)),
        grid=(B, ntot // tt),
        in_specs=[pl.BlockSpec((1, tt, D), lambda b, i: (b, jnp.minimum(i, nl - 1), 0)),
                  pl.BlockSpec((1, tt, D), lambda b, i: (b, jnp.maximum(i - nl, 0), 0)),
                  pl.BlockSpec((1, 1, D), per_b), pl.BlockSpec((1, 1, D), per_b),
                  pl.BlockSpec((1, 1, D), per_b), pl.BlockSpec((1, 1, D), per_b),
                  pl.BlockSpec((HALF, tt), lambda b, i: (0, i)), pl.BlockSpec((HALF, tt), lambda b, i: (0, i)),
                  pl.BlockSpec(lw["w_qk"].shape, const, **resident), pl.BlockSpec((lw["w_qk"].shape[0], 1), const),
                  pl.BlockSpec(lw["w_va"].shape, const, **resident),
                  pl.BlockSpec(lw["w_vb"].shape, const, **resident),
                  pl.BlockSpec(lw["w_tok"].shape, const, **resident)],
        out_specs=(feat(n_q_rows), tokm(n_k_rows), pl.BlockSpec((1, KNORM_ROWS, 128), per_b),
                   feat(n_va), feat(n_vb), tokm(n_hy), tokm(n_tok - n_hy)),
        scratch_shapes=[pltpu.VMEM((n_k_rows, tt), F32)],
        compiler_params=_cp(2), name="proj_all",
    )(x, xc, shift, scale, shift_c, scale_c, cos_t, sin_t, lw["w_qk"], lw["g_qk"], lw["w_va"], lw["w_vb"],
      lw["w_tok"])


def _resid_ln_tail(x, y, gate, g, b, alpha):
    return _ln(alpha * x + gate * y) * g + b


def _mm_kernel(a_ref, w_ref, o_ref, *, act):
    acc = jnp.dot(a_ref[...], w_ref[...], preferred_element_type=F32)
    if act == "sigmoid":
        acc = jax.nn.sigmoid(acc)
    o_ref[...] = acc.astype(o_ref.dtype)


def matmul(a, w, out_dtype, act=None, tm_pref=1024, tn_pref=512):
    M, K = a.shape
    Nn = w.shape[1]
    tm = _tile(M, tm_pref, 16)
    tn = _tile(Nn, tn_pref, 128)
    return pl.pallas_call(
        functools.partial(_mm_kernel, act=act),
        out_shape=jax.ShapeDtypeStruct((M, Nn), out_dtype),
        grid=(M // tm, Nn // tn),
        in_specs=[pl.BlockSpec((tm, K), lambda i, j: (i, 0)),
                  pl.BlockSpec((K, tn), lambda i, j: (0, j))],
        out_specs=pl.BlockSpec((tm, tn), lambda i, j: (i, j)),
        compiler_params=_cp(2), name="matmul",
    )(a, w)


FLASH_MIN_DENOM = 2.0 ** -60
FLASH_UNROLL = 4


def _flash_kernel(lam_ref, kmax_ref, q_ref, k_ref, v_ref, g_ref, o_ref, q_sc, p_buf, sh_sc, acc_sc, l_sc,
                  *, mode, nj, tk, post_scale, n_kheads, khead0):
    b, h = pl.program_id(0), pl.program_id(1)
    qf = q_ref[0].astype(F32)
    tq = qf.shape[1]
    zero = jnp.zeros((HEAD_DIM, tq), F32)
    qa, qb = qf[:HEAD_DIM], qf[HEAD_DIM:]
    if mode == "da":
        q_sc[0] = jnp.concatenate([qa, zero], axis=0).astype(BF16)
        q_sc[1] = jnp.concatenate([zero, qb], axis=0).astype(BF16)
        k_heads = (khead0 + 2 * h, khead0 + 2 * h + 1)
    else:
        group = h // 2
        for mi, qh in enumerate((qa, qb)):
            q_sc[mi] = jnp.where(group == 0, jnp.concatenate([qh, zero], axis=0),
                                 jnp.concatenate([zero, qh], axis=0)).astype(BF16)
        k_heads = (khead0 + group, khead0 + group)
    for mi, qh in enumerate((qa, qb)):
        q_norm = jnp.sqrt(jnp.sum(qh * qh, axis=0, keepdims=True))
        sh_sc[mi] = q_norm * kmax_ref[b * n_kheads + k_heads[mi]]

    def chunk(j):
        start = j * tk
        return pl.ds(start if isinstance(start, int) else pl.multiple_of(start, tk), tk)

    def stage_exp(j, slot):
        kb = k_ref[0, chunk(j), :]
        for mi in range(2):
            s = jnp.dot(kb, q_sc[mi], preferred_element_type=F32)
            p = jnp.exp2(s - sh_sc[mi])
            p_buf[slot, mi] = p.astype(BF16)
            l_sc[mi] += jnp.sum(p.reshape(tk // 8, 8, tq), axis=0)

    def stage_values(j, slot):
        vb = v_ref[0, :, chunk(j)]
        for mi in range(2):
            acc_sc[mi] += jnp.dot(vb, p_buf[slot, mi], preferred_element_type=F32)

    def accumulate():
        acc_sc[...] = jnp.zeros(acc_sc.shape, F32)
        l_sc[...] = jnp.zeros(l_sc.shape, F32)
        stage_exp(0, 0)
        steady = list(range(1, nj))
        while len(steady) % FLASH_UNROLL:
            t = steady.pop(0)
            stage_exp(t, t % 2)
            stage_values(t - 1, (t - 1) % 2)
        if steady:
            t0 = steady[0]

            def body(i, carry):
                for d in range(FLASH_UNROLL):
                    stage_exp(t0 + FLASH_UNROLL * i + d, (t0 + d) % 2)
                    stage_values(t0 + FLASH_UNROLL * i + d - 1, (t0 + d - 1) % 2)
                return carry

            lax.fori_loop(0, len(steady) // FLASH_UNROLL, body, 0)
        stage_values(nj - 1, (nj - 1) % 2)

    denominators = lambda: [jnp.sum(l_sc[mi], axis=0, keepdims=True) for mi in range(2)]
    accumulate()
    denom_min = jnp.min(jnp.minimum(*denominators()))

    @pl.when(jnp.logical_not(denom_min >= FLASH_MIN_DENOM))
    def _():
        for mi in range(2):
            def max_body(j, m, mi=mi):
                s = jnp.dot(k_ref[0, chunk(j), :], q_sc[mi], preferred_element_type=F32)
                return jnp.maximum(m, jnp.max(s, axis=0, keepdims=True))

            sh_sc[mi] = lax.fori_loop(0, nj, max_body, jnp.full((1, tq), NEG_BIG, F32))
        accumulate()

    outs = [acc_sc[mi] / l for mi, l in enumerate(denominators())]
    if mode == "da":
        o = outs[0] - lam_ref[0] * outs[1]
        o = o * lax.rsqrt(jnp.mean(o * o, axis=0, keepdims=True) + LN_EPS)
        o = o * (g_ref[...] * post_scale)
    else:
        o = jnp.concatenate(outs, axis=0)
    o_ref[0] = o.T.astype(o_ref.dtype)


def flash_attention(q_t, k_tok, v_aug, kmax, lam, gain, *, mode, q_span, kv_span, q_row_block, k_col_block,
                    khead0, post_scale=1.0, tk_pref=1408):
    B = q_t.shape[0]
    q0, Nq = q_span
    k0, Nk = kv_span
    tq = _tile(Nq, 1024, 128)
    tk = _tile(Nk, tk_pref, 128)
    assert q0 % tq == 0 and k0 % Nk == 0
    nj = Nk // tk
    dv = 2 * HEAD_DIM if mode == "da" else HEAD_DIM
    n_kheads = kmax.shape[0] // B
    if mode == "da":
        k_map = lambda b, h, i: (b, k0 // Nk, k_col_block + h)
        v_map = lambda b, h, i: (b, h, k0 // Nk)
    else:
        k_map = lambda b, h, i: (b, k0 // Nk, k_col_block)
        v_map = lambda b, h, i: (b, h // 2, k0 // Nk)
    return pl.pallas_call(
        functools.partial(_flash_kernel, mode=mode, nj=nj, tk=tk, post_scale=post_scale,
                          n_kheads=n_kheads, khead0=khead0),
        out_shape=jax.ShapeDtypeStruct((B, Nq, 4 * 128), BF16),
        grid=(B, 4, Nq // tq),
        in_specs=[pl.BlockSpec(memory_space=pltpu.SMEM),
                  pl.BlockSpec(memory_space=pltpu.SMEM),
                  pl.BlockSpec((1, 128, tq), lambda b, h, i: (b, h + q_row_block, i + q0 // tq)),
                  pl.BlockSpec((1, Nk, 128), k_map),
                  pl.BlockSpec((1, dv, Nk), v_map),
                  pl.BlockSpec((128, 1), lambda b, h, i: (0, 0))],
        out_specs=pl.BlockSpec((1, tq, 128), lambda b, h, i: (b, i, h)),
        scratch_shapes=[pltpu.VMEM((2, 128, tq), BF16),
                        pltpu.VMEM((2, 2, tk, tq), BF16),
                        pltpu.VMEM((2, 1, tq), F32),
                        pltpu.VMEM((2, dv, tq), F32),
                        pltpu.VMEM((2, 8, tq), F32)],
        compiler_params=_cp(3), name="flash_" + mode,
    )(lam, kmax, q_t, k_tok, v_aug, gain)


def _shift_rows(x, prev_row, next_row):
    T = x.shape[0]
    row = lax.broadcasted_iota(jnp.int32, x.shape, 0)
    xp = jnp.where(row == 0, prev_row, pltpu.roll(x, 1, axis=0))
    xn = jnp.where(row == T - 1, next_row, pltpu.roll(x, T - 1, axis=0))
    return xp, xn


def _conv3_block(x_ref, p_ref, n_ref, w_ref, b_ref, halo):
    i = pl.program_id(1)
    last = pl.num_programs(1) - 1
    x = x_ref[0].astype(F32)
    prev_row = jnp.where(i > 0, p_ref[0].astype(F32)[halo - 1:halo], 0.0)
    next_row = jnp.where(i < last, n_ref[0].astype(F32)[0:1], 0.0)
    xp, xn = _shift_rows(x, prev_row, next_row)
    return xp * w_ref[0:1] + x * w_ref[1:2] + xn * w_ref[2:3] + b_ref[...]


def _halo_specs(tr, tc, halo, row0, n_rows, col_of):
    per = tr // halo
    n_halo = n_rows // halo
    m0, h0 = row0 // tr, row0 // halo
    return [pl.BlockSpec((1, tr, tc), lambda b, i, j: (b, m0 + i, col_of(j))),
            pl.BlockSpec((1, halo, tc), lambda b, i, j: (b, h0 + jnp.maximum(i * per - 1, 0), col_of(j))),
            pl.BlockSpec((1, halo, tc),
                         lambda b, i, j: (b, h0 + jnp.minimum((i + 1) * per, n_halo - 1), col_of(j)))]


HYENA_HALO = 16


def _dwconv_kernel(x_ref, p_ref, n_ref, w_ref, b_ref, v_ref, x1_ref, x2_ref):
    y = _conv3_block(x_ref, p_ref, n_ref, w_ref, b_ref, HYENA_HALO)
    C = v_ref.shape[2]
    v_ref[0] = y[:, :C].astype(v_ref.dtype)
    x1_ref[0] = y[:, C:2 * C].astype(x1_ref.dtype)
    x2_ref[0] = y[:, 2 * C:].astype(x2_ref.dtype)


def hyena_short_conv(u, w, b, row0, L):
    B, _, C3 = u.shape
    C = C3 // 3
    tr = _tile(L, 512, HYENA_HALO)
    assert row0 % tr == 0
    zero = lambda j: 0
    out = jax.ShapeDtypeStruct((B, L, C), BF16)
    ospec = pl.BlockSpec((1, tr, C), lambda b, i, j: (b, i, 0))
    return pl.pallas_call(
        _dwconv_kernel,
        out_shape=(out, out, out),
        grid=(B, L // tr, 1),
        in_specs=_halo_specs(tr, C3, HYENA_HALO, row0, L, zero) + [
            pl.BlockSpec((3, C3), lambda b, i, j: (0, 0)),
            pl.BlockSpec((1, C3), lambda b, i, j: (0, 0))],
        out_specs=(ospec, ospec, ospec),
        compiler_params=_cp(3), name="hyena_short_conv",
    )(u, u, u, w, b)


FFN_HALO = 16
FFN_CHUNK = 256


def _ffn_kernel(h_ref, hp_ref, hn_ref, wup_ref, cw_ref, cb_ref, wdn_ref, x_ref, gate_ref, g_ref, b_ref, o_ref,
                act_sc, *, alpha):
    i = pl.program_id(1)
    tm = h_ref.shape[1]
    dff = wdn_ref.shape[0]
    rows = tm + 2 * FFN_HALO
    h_prev = jnp.where(i == 0, jnp.zeros_like(hp_ref[0]), hp_ref[0])
    h_next = jnp.where(i == pl.num_programs(1) - 1, jnp.zeros_like(hn_ref[0]), hn_ref[0])
    h_ext = jnp.concatenate([h_prev, h_ref[0], h_next], axis=0)

    def conv_cols(c0):
        u = jnp.dot(h_ext, wup_ref[:, c0:c0 + FFN_CHUNK], preferred_element_type=F32)
        mid = slice(FFN_HALO, FFN_HALO + tm)
        u_prev = pltpu.roll(u, 1, axis=0)[mid]
        u_next = pltpu.roll(u, rows - 1, axis=0)[mid]
        w = cw_ref[:, c0:c0 + FFN_CHUNK]
        return u_prev * w[0:1] + u[mid] * w[1:2] + u_next * w[2:3] + cb_ref[:, c0:c0 + FFN_CHUNK]

    for c0 in range(0, dff, FFN_CHUNK):
        val = conv_cols(c0)
        gate = conv_cols(dff + c0)
        act_sc[:, c0:c0 + FFN_CHUNK] = (gate * jax.nn.sigmoid(gate) * val).astype(BF16)
    y = jnp.dot(act_sc[...], wdn_ref[...], preferred_element_type=F32)
    o_ref[0] = _resid_ln_tail(x_ref[0], y, gate_ref[0], g_ref[...], b_ref[...], alpha)


def conv_ffn_resid_ln(h, w_up, conv_w, conv_b, w_down, x, gate, g, b, alpha):
    B, N, D = h.shape
    dff = w_down.shape[0]
    tm = _tile(N, 512, FFN_HALO)
    per = tm // FFN_HALO
    n_halo = N // FFN_HALO
    row = lambda b_, i: (b_, i, 0)
    per_b = lambda b_, i: (b_, 0, 0)
    const2 = lambda b_, i: (0, 0)
    resident = dict(pipeline_mode=pl.Buffered(1))
    return pl.pallas_call(
        functools.partial(_ffn_kernel, alpha=alpha),
        out_shape=jax.ShapeDtypeStruct((B, N, D), F32),
        grid=(B, N // tm),
        in_specs=[pl.BlockSpec((1, tm, D), row),
                  pl.BlockSpec((1, FFN_HALO, D), lambda b_, i: (b_, jnp.maximum(i * per - 1, 0), 0)),
                  pl.BlockSpec((1, FFN_HALO, D), lambda b_, i: (b_, jnp.minimum((i + 1) * per, n_halo - 1), 0)),
                  pl.BlockSpec((D, 2 * dff), const2, **resident),
                  pl.BlockSpec((3, 2 * dff), const2), pl.BlockSpec((1, 2 * dff), const2),
                  pl.BlockSpec((dff, D), const2, **resident),
                  pl.BlockSpec((1, tm, D), row), pl.BlockSpec((1, 1, D), per_b),
                  pl.BlockSpec((1, D), const2), pl.BlockSpec((1, D), const2)],
        out_specs=pl.BlockSpec((1, tm, D), row),
        scratch_shapes=[pltpu.VMEM((tm, dff), BF16)],
        compiler_params=_cp(2), name="conv_ffn_resid_ln",
    )(h, h, h, w_up, conv_w, conv_b, w_down, x, gate, g, b)


def _merge_kernel(a_ref, b_ref, c_ref, gt_ref, wpa_ref, wpb_ref, wpc_ref, wo_ref, x_ref, gate_ref,
                  g_ref, bb_ref, sh_ref, sc_ref, ox_ref, oh_ref, *, alpha):
    D = wo_ref.shape[0]
    tm = x_ref.shape[1]
    n_sub = 2 if tm % 32 == 0 else 1
    for r in range(n_sub):
        rows = slice(r * tm // n_sub, (r + 1) * tm // n_sub)
        gt = gt_ref[0, rows]
        m = gt[:, :D].astype(F32) * jnp.dot(a_ref[0, rows], wpa_ref[...], preferred_element_type=F32)
        m = m + gt[:, D:2 * D].astype(F32) * jnp.dot(b_ref[0, rows], wpb_ref[...], preferred_element_type=F32)
        m = m + gt[:, 2 * D:].astype(F32) * jnp.dot(c_ref[0, rows].astype(BF16), wpc_ref[...],
                                                    preferred_element_type=F32)
        y = jnp.dot(m.astype(BF16), wo_ref[...], preferred_element_type=F32)
        xn = _resid_ln_tail(x_ref[0, rows], y, gate_ref[0], g_ref[...], bb_ref[...], alpha)
        ox_ref[0, rows] = xn
        oh_ref[0, rows] = (_ln(xn) * (1.0 + sc_ref[0]) + sh_ref[0]).astype(oh_ref.dtype)


def merge_resid_ln(a, b, c, gates, tok0, wpa, wpb, wpc, wo, x, gate, g, bb, sh, sc, alpha):
    B, N, D = x.shape
    W = a.shape[2]
    tm = _tile(N, 512, 16)
    assert tok0 % tm == 0
    row = lambda b_, i: (b_, i, 0)
    per_b = lambda b_, i: (b_, 0, 0)
    const2 = lambda b_, i: (0, 0)
    return pl.pallas_call(
        functools.partial(_merge_kernel, alpha=alpha),
        out_shape=(jax.ShapeDtypeStruct((B, N, D), F32), jax.ShapeDtypeStruct((B, N, D), BF16)),
        grid=(B, N // tm),
        in_specs=[pl.BlockSpec((1, tm, W), row), pl.BlockSpec((1, tm, W), row), pl.BlockSpec((1, tm, W), row),
                  pl.BlockSpec((1, tm, 3 * D), lambda b_, i: (b_, tok0 // tm + i, 0)),
                  pl.BlockSpec((W, D), const2), pl.BlockSpec((W, D), const2), pl.BlockSpec((W, D), const2),
                  pl.BlockSpec((D, D), const2),
                  pl.BlockSpec((1, tm, D), row), pl.BlockSpec((1, 1, D), per_b),
                  pl.BlockSpec((1, D), const2), pl.BlockSpec((1, D), const2),
                  pl.BlockSpec((1, 1, D), per_b), pl.BlockSpec((1, 1, D), per_b)],
        out_specs=(pl.BlockSpec((1, tm, D), row), pl.BlockSpec((1, tm, D), row)),
        compiler_params=_cp(2), name="merge_resid_ln",
    )(a, b, c, gates, wpa, wpb, wpc, wo, x, gate, g, bb, sh, sc)


def _split_bf16(x):
    hi = x.astype(BF16)
    lo = (x - hi.astype(F32)).astype(BF16)
    return hi, lo


def _dot3(a, b):
    ah, al = _split_bf16(a)
    bh, bl = _split_bf16(b)
    d = functools.partial(jnp.dot, preferred_element_type=F32)
    return d(ah, bh) + (d(ah, bl) + d(al, bh))


def _filter_kernel(z_ref, w1_ref, b1_ref, w2_ref, b2_ref, w3_ref, fr_ref, dec_ref, h_ref, s_ref, *, zero_block):
    i = pl.program_id(0)

    @pl.when(i == 0)
    def _():
        s_ref[...] = jnp.zeros(s_ref.shape, F32)

    fr = fr_ref[...]
    hid = jnp.sin(fr * (_dot3(z_ref[...], w1_ref[...]) + b1_ref[...]))
    hid = jnp.sin(fr * (_dot3(hid, w2_ref[...]) + b2_ref[...]))
    h = _dot3(hid, w3_ref[0])
    dec = dec_ref[...]
    C = dec.shape[1]
    n_ord = h.shape[1] // C
    h = h * jnp.concatenate([dec] * n_ord, axis=1)
    s_ref[...] += jnp.sum(jnp.abs(h), axis=0, keepdims=True)
    row = lax.broadcasted_iota(jnp.int32, h.shape, 0)
    h = jnp.where((row == 0) & (i == zero_block), 0.0, h)
    for o in range(n_ord):
        h_ref[o] = h[:, o * C:(o + 1) * C]


def hyena_filter_mlp(z2, w1, b1, w2, b2, w3_dir, freq, decay2):
    n2, E = z2.shape
    Hh = w2.shape[0]
    OC = w3_dir.shape[2]
    C = decay2.shape[1]
    n_ord = OC // C
    L = n2 // 2
    tr = _tile(L, 512, 8)
    nblk = n2 // tr
    c2 = lambda i: (0, 0)
    return pl.pallas_call(
        functools.partial(_filter_kernel, zero_block=L // tr),
        out_shape=(jax.ShapeDtypeStruct((n_ord, n2, C), F32), jax.ShapeDtypeStruct((1, OC), F32)),
        grid=(nblk,),
        in_specs=[pl.BlockSpec((tr, E), lambda i: (i, 0)), pl.BlockSpec((E, Hh), c2), pl.BlockSpec((1, Hh), c2),
                  pl.BlockSpec((Hh, Hh), c2), pl.BlockSpec((1, Hh), c2),
                  pl.BlockSpec((1, Hh, OC), lambda i: (i // (nblk // 2), 0, 0)),
                  pl.BlockSpec((1, Hh), c2), pl.BlockSpec((tr, C), lambda i: (i, 0))],
        out_specs=(pl.BlockSpec((n_ord, tr, C), lambda i: (0, i, 0)), pl.BlockSpec((1, OC), c2)),
        compiler_params=_cp(1), name="hyena_filter_mlp",
    )(z2, w1, b1, w2, b2, w3_dir, freq, decay2)


def _mm_split(m_hi, m_lo, x, passes):
    d = functools.partial(jnp.dot, preferred_element_type=F32)
    if passes == 1:
        return d(m_hi, x.astype(BF16))
    xh, xl = _split_bf16(x)
    return d(m_hi, xh) + (d(m_lo, xh) + d(m_hi, xl))


FFT_ROWS = 8
FFT_BLOCK = 16


def _level1_apply(m_refs, x, r_out, passes):
    r_in, _, C = x.shape
    x = x.astype(F32)
    m_hi = m_refs[0][...]
    m_lo = m_refs[1][...] if passes > 1 else None
    groups = []
    for g in range(FFT_BLOCK // FFT_ROWS):
        xg = x[:, g * FFT_ROWS:(g + 1) * FFT_ROWS].reshape(r_in * FFT_ROWS, C)
        groups.append(_mm_split(m_hi, m_lo, xg, passes).reshape(r_out, FFT_ROWS, C))
    return jnp.concatenate(groups, axis=1)


def _level1_kernel(*refs, passes):
    n_mat = 1 if passes == 1 else 2
    x_ref, o_ref = refs[n_mat:]
    o_ref[0] = _level1_apply(refs[:n_mat], x_ref[0], o_ref.shape[1], passes).astype(o_ref.dtype)


def _level1_gate_kernel(*refs, passes):
    n_mat = 1 if passes == 1 else 2
    x_ref, z_ref, xg_ref, bias_ref, o_ref = refs[n_mat:]
    y = _level1_apply(refs[:n_mat], x_ref[0], o_ref.shape[1], passes)
    o_ref[0] = (xg_ref[0].astype(F32) * (y + bias_ref[...] * z_ref[0].astype(F32))).astype(o_ref.dtype)


def fft_level1(m_hi, m_lo, x, passes, gate_args=None, out_dtype=F32):
    P, R_in, Nb, C = x.shape
    R_out = m_hi.shape[0] // FFT_ROWS
    mats = (m_hi,) if passes == 1 else (m_hi, m_lo)
    mspec = pl.BlockSpec(m_hi.shape, lambda p, j: (0, 0), pipeline_mode=pl.Buffered(1))
    xspec = pl.BlockSpec((1, R_in, FFT_BLOCK, C), lambda p, j: (p, 0, j, 0))
    ospec = pl.BlockSpec((1, R_out, FFT_BLOCK, C), lambda p, j: (p, 0, j, 0))
    if gate_args is None:
        kern = functools.partial(_level1_kernel, passes=passes)
        ins, specs = (*mats, x), [mspec] * len(mats) + [xspec]
    else:
        z, xg, bias = gate_args
        kern = functools.partial(_level1_gate_kernel, passes=passes)
        ins = (*mats, x, z, xg, bias)
        specs = [mspec] * len(mats) + [xspec, ospec, ospec, pl.BlockSpec((1, C), lambda p, j: (0, 0))]
    return pl.pallas_call(
        kern, out_shape=jax.ShapeDtypeStruct((P, R_out, Nb, C), out_dtype),
        grid=(P, Nb // FFT_BLOCK), in_specs=specs, out_specs=ospec,
        compiler_params=_cp(2), name="fft_level1",
    )(*ins)


FFT_MID_K1 = 8


def _fft_mid_kernel(*refs, passes):
    n_mat = 2 if passes == 1 else 4
    gh_ref, gih_ref = refs[0], refs[1]
    gl_ref, gil_ref = (refs[2], refs[3]) if passes > 1 else (None, None)
    h_ref, a_ref, o_ref = refs[n_mat:]
    n_pairs, _, kb, nb, C = a_ref.shape
    for kk in range(kb):
        hr, hi = h_ref[0, 0, kk], h_ref[0, 1, kk]
        for p in range(n_pairs):
            x = a_ref[p, :, kk].reshape(2 * nb, C)
            X = _mm_split(gh_ref[kk], None if gl_ref is None else gl_ref[kk], x, passes)
            xr, xi = X[:nb], X[nb:]
            Y = jnp.concatenate([xr * hr - xi * hi, xr * hi + xi * hr], axis=0)
            Bv = _mm_split(gih_ref[kk], None if gil_ref is None else gil_ref[kk], Y, passes)
            o_ref[p, :, kk] = Bv.reshape(2, nb, C).astype(o_ref.dtype)


def fft_mid(g_hi, g_lo, gi_hi, gi_lo, spectra, order, a5, passes):
    P, _, Na, Nb, C = a5.shape
    kb = FFT_MID_K1
    gspec = pl.BlockSpec((kb, 2 * Nb, 2 * Nb), lambda k: (k, 0, 0))
    mats = (g_hi, gi_hi) if passes == 1 else (g_hi, gi_hi, g_lo, gi_lo)
    blk = pl.BlockSpec((P, 2, kb, Nb, C), lambda k: (0, 0, k, 0, 0))
    return pl.pallas_call(
        functools.partial(_fft_mid_kernel, passes=passes),
        out_shape=jax.ShapeDtypeStruct(a5.shape, a5.dtype),
        grid=(Na // kb,),
        in_specs=[gspec] * len(mats) + [pl.BlockSpec((1, 2, kb, Nb, C), lambda k: (order, 0, k, 0, 0)), blk],
        out_specs=blk,
        compiler_params=_cp(1), name="fft_mid",
    )(*mats, spectra, a5)


def _fft_spec_kernel(*refs, passes):
    n_mat = 1 if passes == 1 else 2
    gh_ref = refs[0]
    gl_ref = refs[1] if passes > 1 else None
    inv_ref, a_ref, o_ref = refs[n_mat:]
    n_filt, _, kb, nb, C = a_ref.shape
    for kk in range(kb):
        for p in range(n_filt):
            x = a_ref[p, :, kk].reshape(2 * nb, C)
            X = _mm_split(gh_ref[kk], None if gl_ref is None else gl_ref[kk], x, passes) * inv_ref[p]
            o_ref[p, :, kk] = X.reshape(2, nb, C)


def fft_filter_spectrum(g_hi, g_lo, inv_norm, a5, passes):
    P, _, Na, Nb, C = a5.shape
    kb = FFT_MID_K1
    mats = (g_hi,) if passes == 1 else (g_hi, g_lo)
    gspec = pl.BlockSpec((kb, 2 * Nb, 2 * Nb), lambda k: (k, 0, 0))
    blk = pl.BlockSpec((P, 2, kb, Nb, C), lambda k: (0, 0, k, 0, 0))
    return pl.pallas_call(
        functools.partial(_fft_spec_kernel, passes=passes),
        out_shape=jax.ShapeDtypeStruct(a5.shape, F32),
        grid=(Na // kb,),
        in_specs=[gspec] * len(mats) + [pl.BlockSpec((P, 1, C), lambda k: (0, 0, 0)), blk],
        out_specs=blk,
        compiler_params=_cp(1), name="fft_filter_spectrum",
    )(*mats, inv_norm, a5)


def _fft_factors(n_fft):
    na = 1 << (int(math.log2(n_fft)) // 2)
    return na, n_fft // na


def _fft_tables(L):
    n_fft = 2 * L
    na, nb = _fft_factors(n_fft)
    nh = na // 2
    iota = lambda shape, axis: lax.broadcasted_iota(jnp.int32, shape, axis)

    def cis(num, den):
        ang = (2.0 * math.pi / den) * (num % den).astype(F32)
        return jnp.cos(ang), -jnp.sin(ang)

    def block(diag_sign, pr, pc, cr, ci):
        return jnp.where(pr == pc, cr, jnp.where(pr == 1, diag_sign * ci, -diag_sign * ci))

    def kron_matrix(rows, cols, entry):
        shape = (rows * FFT_ROWS, cols * FFT_ROWS)
        I, J = iota(shape, 0), iota(shape, 1)
        m = entry(I // FFT_ROWS, J // FFT_ROWS)
        return _split_bf16(jnp.where(I % FFT_ROWS == J % FFT_ROWS, m, 0.0))

    def fwd_entry(i, j):
        cr, ci = cis((i % na) * (j % nh), na)
        return block(1, i // na, j // nh, cr, ci)

    def flt_entry(i, j):
        cr, ci = cis((i % na) * j, na)
        return jnp.where(i // na == 0, cr, ci)

    def inv_entry(i, j):
        cr, ci = cis((i % nh) * (j % na), na)
        return block(-1, i // nh, j // na, cr, ci) / n_fft

    gshape = (na, 2 * nb, 2 * nb)
    k1, gi_, gj_ = iota(gshape, 0), iota(gshape, 1), iota(gshape, 2)
    cr, ci = cis((gi_ % nb) * (gj_ % nb) * na + k1 * (gj_ % nb), n_fft)
    g = _split_bf16(block(1, gi_ // nb, gj_ // nb, cr, ci))
    cr, ci = cis((gj_ % nb) * (gi_ % nb) * na + k1 * (gi_ % nb), n_fft)
    ginv = _split_bf16(block(-1, gi_ // nb, gj_ // nb, cr, ci))

    C = HY_WIDTH
    n = jnp.arange(n_fft, dtype=jnp.int32)
    pos = jnp.where(n < L, n, jnp.where(n == L, 0, n_fft - n)).astype(F32)[:, None]
    t = pos * (1.0 / (L - 1))
    f = jnp.linspace(1e-4, HY_BANDS - 1, HY_BANDS, dtype=F32)
    ang = (2.0 * math.pi / L) * pos * f[None, :]
    z2 = jnp.concatenate([t, jnp.cos(ang), -jnp.sin(ang), jnp.zeros((n_fft, HY_EMB_PAD - HY_EMB), F32)], axis=-1)
    deltas = jnp.abs(jnp.linspace(HY_MIN_DECAY, HY_MAX_DECAY, C, dtype=F32))
    decay2 = jnp.exp(-t * deltas)
    return dict(na=na, nb=nb, fwd=kron_matrix(2 * na, na, fwd_entry), flt=kron_matrix(2 * na, na, flt_entry),
                inv=kron_matrix(na, 2 * na, inv_entry), g=g, ginv=ginv, z2=z2, decay2=decay2)


def hyena_filter_spectra(w1, b1, w2, b2, w3, freq, tabs, passes):
    C = HY_WIDTH
    na, nb = tabs["na"], tabs["nb"]
    w1p = jnp.pad(w1, ((0, HY_EMB_PAD - HY_EMB), (0, 0)))
    w3_dir = jnp.transpose(w3.reshape(HY_HIDDEN, HY_ORDER, 2, C), (2, 0, 1, 3)).reshape(2, HY_HIDDEN, HY_ORDER * C)
    filt, s = hyena_filter_mlp(tabs["z2"], w1p, b1[None], w2, b2[None], w3_dir, freq[None], tabs["decay2"])
    inv_norm = (1.0 / s).reshape(HY_ORDER, 1, C)
    a = fft_level1(*tabs["flt"], filt.reshape(HY_ORDER, na, nb, C), passes,
                   out_dtype=BF16 if passes == 1 else F32)
    return fft_filter_spectrum(*tabs["g"], inv_norm, a.reshape(HY_ORDER, 2, na, nb, C), passes)


def hyena_long_conv_gate(z, xg, bias, spectra, order, tabs, passes):
    B, L, C = z.shape
    na, nb = tabs["na"], tabs["nb"]
    P = B // 2
    nat = lambda a: a.reshape(P, na, nb, C)
    mid_dtype = BF16 if passes == 1 else F32
    a = fft_level1(*tabs["fwd"], nat(z), passes, out_dtype=mid_dtype)
    bv = fft_mid(*tabs["g"], *tabs["ginv"], spectra, order, a.reshape(P, 2, na, nb, C), passes)
    y = fft_level1(*tabs["inv"], bv.reshape(P, 2 * na, nb, C), passes,
                   gate_args=(nat(z), nat(xg), bias[None]), out_dtype=z.dtype)
    return y.reshape(B, L, C)


def hyena_mix(u, row0, L, spectra, conv_w, conv_b, bias, tabs, passes):
    v, x1, x2 = hyena_short_conv(u, conv_w, conv_b[None], row0, L)
    zz = hyena_long_conv_gate(v, x1, bias[0], spectra, 0, tabs, passes)
    return hyena_long_conv_gate(zz, x2, bias[1], spectra, 1, tabs, passes)


FFT_PASSES = 1
FILTER_PASSES = 1


def _rope_tables_t(n):
    rows = n // GRID_W
    r = jnp.repeat(jnp.arange(rows, dtype=F32), GRID_W)
    col = jnp.tile(jnp.arange(GRID_W, dtype=F32), rows)
    axis_dim = HEAD_DIM // 2
    inv = ROPE_THETA ** (-jnp.arange(0, axis_dim, 2, dtype=F32) / axis_dim)
    ang = jnp.concatenate([r[:, None] * inv, col[:, None] * inv], axis=-1)
    return jnp.cos(ang).T, jnp.sin(ang).T


def _layer_weights(l, w_in, gq_qn, gq_kn):
    w = w_in[l]
    d_qk = DA_HEADS * 2 * HEAD_DIM
    gq_kv = GKV_HEADS * HEAD_DIM
    gq_q = GQ_HEADS * HEAD_DIM
    o = np.cumsum([0, d_qk, d_qk, gq_kv, gq_kv, d_qk, gq_q, 3 * HY_WIDTH])
    ka, va, kb, vb, qa, qb, hy = (w[:, o[i]:o[i + 1]] for i in range(7))
    gates = w[:, o[7]:]
    qscale = (HEAD_DIM ** -0.5) * LOG2E
    w_qk = jnp.concatenate([ka, kb, qa, qb], axis=1).T.astype(BF16)
    g_qk = jnp.concatenate([jnp.ones((d_qk,), F32), jnp.tile(gq_kn[l], GKV_HEADS),
                            jnp.full((d_qk,), qscale, F32), jnp.tile(gq_qn[l], GQ_HEADS) * qscale])[:, None]
    n_ka, n_kb, n_qa = d_qk // HEAD_DIM, gq_kv // HEAD_DIM, d_qk // HEAD_DIM
    rms_heads = frozenset(range(n_ka, n_ka + n_kb)) | frozenset(range(n_ka + n_kb + n_qa,
                                                                       n_ka + n_kb + n_qa + gq_q // HEAD_DIM))
    return dict(w_qk=w_qk, g_qk=g_qk, rms_heads=rms_heads, w_va=va.T.astype(BF16), w_vb=vb.T.astype(BF16),
                w_tok=jnp.concatenate([hy, gates], axis=1).astype(BF16))


def kernel(x, c, ctx, c_ctx, w_mod, b_mod, w_in, da_lq1, da_lk1, da_lq2, da_lk2, da_subln, gq_qn, gq_kn,
           hy_conv_w, hy_conv_b, hy_w1, hy_b1, hy_w2, hy_b2, hy_w3, hy_freq, hy_bias, w_pa, w_pb, w_pc, w_o,
           ln1_g, ln1_b, w_up, ffn_conv_w, ffn_conv_b, w_down, ln2_g, ln2_b):
    B, n_lat, D = x.shape
    n_ctx = ctx.shape[1]
    depth = w_in.shape[0]
    alpha = (2 * depth) ** 0.25
    cos_l, sin_l = _rope_tables_t(n_lat)
    cos_t = jnp.concatenate([cos_l, jnp.ones((HALF, n_ctx), F32)], axis=1)
    sin_t = jnp.concatenate([sin_l, jnp.zeros((HALF, n_ctx), F32)], axis=1)
    lat, ctx_span, all_span = (0, n_lat), (n_lat, n_ctx), (0, n_lat + n_ctx)
    tabs_l = _fft_tables(n_lat)
    tabs_c = _fft_tables(n_ctx)
    xc = ctx

    cond = jnp.concatenate([c, c_ctx[None]], axis=0)
    cond = jnp.pad(jax.nn.silu(cond), ((0, 16 - (B + 1) % 16), (0, 0))).astype(BF16)

    for l in range(depth):
        last = l == depth - 1
        lam_init = 0.8 - 0.6 * math.exp(-0.3 * l)
        lam = (jnp.exp(jnp.sum(da_lq1[l] * da_lk1[l])) - jnp.exp(jnp.sum(da_lq2[l] * da_lk2[l])) + lam_init)
        lam = lam.reshape(1).astype(F32)
        mod = matmul(cond, w_mod[l].astype(BF16), F32, tn_pref=1024) + b_mod[l]
        sh1, sc1, g1, sh2, sc2, g2 = [m[:, None, :] for m in jnp.split(mod[:B], 6, axis=-1)]
        mc = [jnp.broadcast_to(m[None, None, :], (B, 1, D)) for m in jnp.split(mod[B], 6)]
        lw = _layer_weights(l, w_in, gq_qn, gq_kn)
        subln = da_subln[l][:, None]
        hy_mlp = (hy_w1[l], hy_b1[l], hy_w2[l], hy_b2[l], hy_w3[l], hy_freq[l])
        spectra_l = hyena_filter_spectra(*hy_mlp, tabs_l, FILTER_PASSES)
        wpa, wpb, wpc, wo = (w[l].astype(BF16) for w in (w_pa, w_pb, w_pc, w_o))
        wup, wdn = w_up[l].astype(BF16), w_down[l].astype(BF16)
        ln1 = (ln1_g[l][None], ln1_b[l][None])
        ln2 = (ln2_g[l][None], ln2_b[l][None])

        q_all, k_tok, k_n2, va, vb, u, gates = proj_all(x, xc, sh1, sc1, mc[0], mc[1], cos_t, sin_t, lw)
        kmax = jnp.sqrt(k_n2[:, :N_KHEADS, 0]).reshape(B * N_KHEADS)
        n_da_maps = DA_HEADS * 2

        def attend(q_span, kv_span):
            da = flash_attention(q_all, k_tok, va, kmax, lam, subln, mode="da", q_span=q_span, kv_span=kv_span,
                                 q_row_block=0, k_col_block=0, khead0=0, post_scale=1.0 - lam_init)
            gq = flash_attention(q_all, k_tok, vb, kmax, lam, subln, mode="gqa", q_span=q_span, kv_span=kv_span,
                                 q_row_block=n_da_maps * HEAD_DIM // 128, k_col_block=n_da_maps * HEAD_DIM // 128,
                                 khead0=n_da_maps)
            return da, gq

        hy_args = (hy_conv_w[l], hy_conv_b[l], hy_bias[l])
        a_l, b_l = attend(lat, all_span)
        c_l = hyena_mix(u, *lat, spectra_l, *hy_args, tabs_l, FFT_PASSES)
        x_new, h2 = merge_resid_ln(a_l, b_l, c_l, gates, lat[0], wpa, wpb, wpc, wo, x, g1, *ln1,
                                   sh2, sc2, alpha)
        x_new = conv_ffn_resid_ln(h2, wup, ffn_conv_w[l], ffn_conv_b[l][None], wdn, x_new, g2, *ln2, alpha)

        if not last:
            spectra_c = hyena_filter_spectra(*hy_mlp, tabs_c, FILTER_PASSES)
            a_c, b_c = attend(ctx_span, ctx_span)
            c_c = hyena_mix(u, *ctx_span, spectra_c, *hy_args, tabs_c, FFT_PASSES)
            xc, hc2 = merge_resid_ln(a_c, b_c, c_c, gates, ctx_span[0], wpa, wpb, wpc, wo, xc, mc[2],
                                     *ln1, mc[3], mc[4], alpha)
            xc = conv_ffn_resid_ln(hc2, wup, ffn_conv_w[l], ffn_conv_b[l][None], wdn, xc, mc[5], *ln2, alpha)
        x = x_new
    return x
```

```python
import functools
import math

import numpy as np
import jax
import jax.numpy as jnp
from jax import lax
from jax.experimental import pallas as pl
from jax.experimental.pallas import tpu as pltpu

F32 = jnp.float32
BF16 = jnp.bfloat16

HEAD_DIM = 64
HALF = HEAD_DIM // 2
GRID_W = 64
ROPE_THETA = 10000.0
DA_HEADS = 4
GQ_HEADS = 8
GKV_HEADS = 2
HY_WIDTH = 512
HY_ORDER = 2
HY_EMB = 33
HY_EMB_PAD = 128
HY_BANDS = (HY_EMB - 1) // 2
HY_HIDDEN = 64
HY_MIN_DECAY = math.log(1e-2) / 1.5
HY_MAX_DECAY = math.log(1e-2) / 0.3
LN_EPS = 1e-6
LOG2E = 1.4426950408889634
NEG_BIG = -1e30

VMEM_LIMIT = 56 * 1024 * 1024


def _cp(n_axes):
    return pltpu.CompilerParams(dimension_semantics=("arbitrary",) * n_axes,
                                vmem_limit_bytes=VMEM_LIMIT)


def _tile(n, pref, mult):
    if n <= pref:
        return n
    t = (pref // mult) * mult
    while t >= mult:
        if n % t == 0:
            return t
        t -= mult
    return n


def _ln(x):
    mu = jnp.mean(x, axis=-1, keepdims=True)
    xc = x - mu
    return xc * lax.rsqrt(jnp.mean(xc * xc, axis=-1, keepdims=True) + LN_EPS)


N_KHEADS = DA_HEADS * 2 + GKV_HEADS
KNORM_ROWS = 16


def _proj_all_kernel(x_ref, xc_ref, sh_ref, sc_ref, shc_ref, scc_ref, cos_ref, sin_ref, wqk_ref, gqk_ref,
                     wva_ref, wvb_ref, wtok_ref,
                     q_ref, k_ref, kn_ref, va_ref, vb_ref, u_ref, gt_ref, kt_sc,
                     *, n_lat_tiles, rms_heads, n_hy):
    i = pl.program_id(1)

    @pl.when(i == 0)
    def _():
        kn_ref[...] = jnp.zeros(kn_ref.shape, F32)

    is_latent = i < n_lat_tiles
    xin = jnp.where(is_latent, x_ref[0], xc_ref[0])
    shift = jnp.where(is_latent, sh_ref[0], shc_ref[0])
    scale = jnp.where(is_latent, sc_ref[0], scc_ref[0])
    h = (_ln(xin) * (1.0 + scale) + shift).astype(BF16)
    nt = (((1,), (1,)), ((), ()))

    acc = lax.dot_general(wqk_ref[...], h, nt, preferred_element_type=F32)
    va = lax.dot_general(wva_ref[...], h, nt, preferred_element_type=F32)
    vb = lax.dot_general(wvb_ref[...], h, nt, preferred_element_type=F32)
    tok = jnp.dot(h, wtok_ref[...], preferred_element_type=F32)
    va_ref[0] = va.astype(va_ref.dtype)
    vb_ref[0] = vb.astype(vb_ref.dtype)
    c, s = cos_ref[...], sin_ref[...]
    n_k_rows = kt_sc.shape[0]
    for hd in range(acc.shape[0] // HEAD_DIM):
        lo = hd * HEAD_DIM
        xh = acc[lo:lo + HEAD_DIM]
        if hd in rms_heads:
            xh = xh * lax.rsqrt(jnp.mean(xh * xh, axis=0, keepdims=True) + LN_EPS)
        xh = xh * gqk_ref[lo:lo + HEAD_DIM]
        x1, x2 = xh[:HALF], xh[HALF:]
        r1, r2 = x1 * c - x2 * s, x1 * s + x2 * c
        if lo < n_k_rows:
            kt_sc[lo:lo + HALF] = r1
            kt_sc[lo + HALF:lo + HEAD_DIM] = r2
        else:
            q_ref[0, lo - n_k_rows:lo - n_k_rows + HALF] = r1.astype(q_ref.dtype)
            q_ref[0, lo - n_k_rows + HALF:lo - n_k_rows + HEAD_DIM] = r2.astype(q_ref.dtype)

    k_bf = kt_sc[...].astype(BF16)
    k_ref[0] = k_bf.astype(F32).T.astype(BF16)
    k_sq = k_bf.astype(F32) ** 2
    for hd in range(n_k_rows // HEAD_DIM):
        n2 = jnp.sum(k_sq[hd * HEAD_DIM:(hd + 1) * HEAD_DIM], axis=0, keepdims=True)
        kn_ref[0, hd:hd + 1] = jnp.maximum(kn_ref[0, hd:hd + 1], jnp.max(n2, axis=1, keepdims=True))

    u_ref[0] = tok[:, :n_hy].astype(u_ref.dtype)
    gt_ref[0] = jax.nn.sigmoid(tok[:, n_hy:]).astype(gt_ref.dtype)


def proj_all(x, xc, shift, scale, shift_c, scale_c, cos_t, sin_t, lw):
    B, N, D = x.shape
    Nc = xc.shape[1]
    ntot = N + Nc
    tt = _tile(Nc, 256, 128)
    assert N % tt == 0
    nl = N // tt
    n_k_rows = N_KHEADS * HEAD_DIM
    n_q_rows = lw["w_qk"].shape[0] - n_k_rows
    n_va, n_vb = lw["w_va"].shape[0], lw["w_vb"].shape[0]
    n_tok = lw["w_tok"].shape[1]
    n_hy = 3 * HY_WIDTH
    per_b = lambda b, i: (b, 0, 0)
    const = lambda b, i: (0, 0)
    resident = dict(pipeline_mode=pl.Buffered(1))
    feat = lambda rows: pl.BlockSpec((1, rows, tt), lambda b, i: (b, 0, i))
    tokm = lambda cols: pl.BlockSpec((1, tt, cols), lambda b, i: (b, i, 0))
    sds = jax.ShapeDtypeStruct
    return pl.pallas_call(
        functools.partial(_proj_all_kernel, n_lat_tiles=nl, rms_heads=lw["rms_heads"], n_hy=n_hy),
        out_shape=(sds((B, n_q_rows, ntot), BF16), sds((B, ntot, n_k_rows), BF16),
                   sds((B, KNORM_ROWS, 128), F32), sds((B, n_va, ntot), BF16), sds((B, n_vb, ntot), BF16),
                   sds((B, ntot, n_hy), BF16), sds((B, ntot, n_tok - n_hy), BF16)),
        grid=(B, ntot // tt),
        in_specs=[pl.BlockSpec((1, tt, D), lambda b, i: (b, jnp.minimum(i, nl - 1), 0)),
                  pl.BlockSpec((1, tt, D), lambda b, i: (b, jnp.maximum(i - nl, 0), 0)),
                  pl.BlockSpec((1, 1, D), per_b), pl.BlockSpec((1, 1, D), per_b),
                  pl.BlockSpec((1, 1, D), per_b), pl.BlockSpec((1, 1, D), per_b),
                  pl.BlockSpec((HALF, tt), lambda b, i: (0, i)), pl.BlockSpec((HALF, tt), lambda b, i: (0, i)),
                  pl.BlockSpec(lw["w_qk"].shape, const, **resident), pl.BlockSpec((lw["w_qk"].shape[0], 1), const),
                  pl.BlockSpec(lw["w_va"].shape, const, **resident),
                  pl.BlockSpec(lw["w_vb"].shape, const, **resident),
                  pl.BlockSpec(lw["w_tok"].shape, const, **resident)],
        out_specs=(feat(n_q_rows), tokm(n_k_rows), pl.BlockSpec((1, KNORM_ROWS, 128), per_b),
                   feat(n_va), feat(n_vb), tokm(n_hy), tokm(n_tok - n_hy)),
        scratch_shapes=[pltpu.VMEM((n_k_rows, tt), F32)],
        compiler_params=_cp(2), name="proj_all",
    )(x, xc, shift, scale, shift_c, scale_c, cos_t, sin_t, lw["w_qk"], lw["g_qk"], lw["w_va"], lw["w_vb"],
      lw["w_tok"])


def _resid_ln_tail(x, y, gate, g, b, alpha):
    return _ln(alpha * x + gate * y) * g + b


def _mm_kernel(a_ref, w_ref, o_ref, *, act):
    acc = jnp.dot(a_ref[...], w_ref[...], preferred_element_type=F32)
    if act == "sigmoid":
        acc = jax.nn.sigmoid(acc)
    o_ref[...] = acc.astype(o_ref.dtype)


def matmul(a, w, out_dtype, act=None, tm_pref=1024, tn_pref=512):
    M, K = a.shape
    Nn = w.shape[1]
    tm = _tile(M, tm_pref, 16)
    tn = _tile(Nn, tn_pref, 128)
    return pl.pallas_call(
        functools.partial(_mm_kernel, act=act),
        out_shape=jax.ShapeDtypeStruct((M, Nn), out_dtype),
        grid=(M // tm, Nn // tn),
        in_specs=[pl.BlockSpec((tm, K), lambda i, j: (i, 0)),
                  pl.BlockSpec((K, tn), lambda i, j: (0, j))],
        out_specs=pl.BlockSpec((tm, tn), lambda i, j: (i, j)),
        compiler_params=_cp(2), name="matmul",
    )(a, w)


FLASH_MIN_DENOM = 2.0 ** -60
FLASH_UNROLL = 4


def _flash_kernel(lam_ref, kmax_ref, q_ref, k_ref, v_ref, g_ref, o_ref, q_sc, p_buf, sh_sc, acc_sc, l_sc,
                  *, mode, nj, tk, post_scale, n_kheads, khead0):
    b, h = pl.program_id(0), pl.program_id(1)
    qf = q_ref[0].astype(F32)
    tq = qf.shape[1]
    zero = jnp.zeros((HEAD_DIM, tq), F32)
    qa, qb = qf[:HEAD_DIM], qf[HEAD_DIM:]
    if mode == "da":
        q_sc[0] = jnp.concatenate([qa, zero], axis=0).astype(BF16)
        q_sc[1] = jnp.concatenate([zero, qb], axis=0).astype(BF16)
        k_heads = (khead0 + 2 * h, khead0 + 2 * h + 1)
    else:
        group = h // 2
        for mi, qh in enumerate((qa, qb)):
            q_sc[mi] = jnp.where(group == 0, jnp.concatenate([qh, zero], axis=0),
                                 jnp.concatenate([zero, qh], axis=0)).astype(BF16)
        k_heads = (khead0 + group, khead0 + group)
    for mi, qh in enumerate((qa, qb)):
        q_norm = jnp.sqrt(jnp.sum(qh * qh, axis=0, keepdims=True))
        sh_sc[mi] = q_norm * kmax_ref[b * n_kheads + k_heads[mi]]

    def chunk(j):
        start = j * tk
        return pl.ds(start if isinstance(start, int) else pl.multiple_of(start, tk), tk)

    def stage_exp(j, slot):
        kb = k_ref[0, chunk(j), :]
        for mi in range(2):
            s = jnp.dot(kb, q_sc[mi], preferred_element_type=F32)
            p = jnp.exp2(s - sh_sc[mi])
            p_buf[slot, mi] = p.astype(BF16)
            l_sc[mi] += jnp.sum(p.reshape(tk // 8, 8, tq), axis=0)

    def stage_values(j, slot):
        vb = v_ref[0, :, chunk(j)]
        for mi in range(2):
            acc_sc[mi] += jnp.dot(vb, p_buf[slot, mi], preferred_element_type=F32)

    def accumulate():
        acc_sc[...] = jnp.zeros(acc_sc.shape, F32)
        l_sc[...] = jnp.zeros(l_sc.shape, F32)
        stage_exp(0, 0)
        steady = list(range(1, nj))
        while len(steady) % FLASH_UNROLL:
            t = steady.pop(0)
            stage_exp(t, t % 2)
            stage_values(t - 1, (t - 1) % 2)
        if steady:
            t0 = steady[0]

            def body(i, carry):
                for d in range(FLASH_UNROLL):
                    stage_exp(t0 + FLASH_UNROLL * i + d, (t0 + d) % 2)
                    stage_values(t0 + FLASH_UNROLL * i + d - 1, (t0 + d - 1) % 2)
                return carry

            lax.fori_loop(0, len(steady) // FLASH_UNROLL, body, 0)
        stage_values(nj - 1, (nj - 1) % 2)

    denominators = lambda: [jnp.sum(l_sc[mi], axis=0, keepdims=True) for mi in range(2)]
    accumulate()
    denom_min = jnp.min(jnp.minimum(*denominators()))

    @pl.when(jnp.logical_not(denom_min >= FLASH_MIN_DENOM))
    def _():
        for mi in range(2):
            def max_body(j, m, mi=mi):
                s = jnp.dot(k_ref[0, chunk(j), :], q_sc[mi], preferred_element_type=F32)
                return jnp.maximum(m, jnp.max(s, axis=0, keepdims=True))

            sh_sc[mi] = lax.fori_loop(0, nj, max_body, jnp.full((1, tq), NEG_BIG, F32))
        accumulate()

    outs = [acc_sc[mi] / l for mi, l in enumerate(denominators())]
    if mode == "da":
        o = outs[0] - lam_ref[0] * outs[1]
        o = o * lax.rsqrt(jnp.mean(o * o, axis=0, keepdims=True) + LN_EPS)
        o = o * (g_ref[...] * post_scale)
    else:
        o = jnp.concatenate(outs, axis=0)
    o_ref[0] = o.T.astype(o_ref.dtype)


def flash_attention(q_t, k_tok, v_aug, kmax, lam, gain, *, mode, q_span, kv_span, q_row_block, k_col_block,
                    khead0, post_scale=1.0, tk_pref=2816):
    B = q_t.shape[0]
    q0, Nq = q_span
    k0, Nk = kv_span
    tq = _tile(Nq, 1024, 128)
    tk = _tile(Nk, tk_pref, 128)
    assert q0 % tq == 0 and k0 % Nk == 0
    nj = Nk // tk
    dv = 2 * HEAD_DIM if mode == "da" else HEAD_DIM
    n_kheads = kmax.shape[0] // B
    if mode == "da":
        k_map = lambda b, h, i: (b, k0 // Nk, k_col_block + h)
        v_map = lambda b, h, i: (b, h, k0 // Nk)
    else:
        k_map = lambda b, h, i: (b, k0 // Nk, k_col_block)
        v_map = lambda b, h, i: (b, h // 2, k0 // Nk)
    return pl.pallas_call(
        functools.partial(_flash_kernel, mode=mode, nj=nj, tk=tk, post_scale=post_scale,
                          n_kheads=n_kheads, khead0=khead0),
        out_shape=jax.ShapeDtypeStruct((B, Nq, 4 * 128), BF16),
        grid=(B, 4, Nq // tq),
        in_specs=[pl.BlockSpec(memory_space=pltpu.SMEM),
                  pl.BlockSpec(memory_space=pltpu.SMEM),
                  pl.BlockSpec((1, 128, tq), lambda b, h, i: (b, h + q_row_block, i + q0 // tq)),
                  pl.BlockSpec((1, Nk, 128), k_map),
                  pl.BlockSpec((1, dv, Nk), v_map),
                  pl.BlockSpec((128, 1), lambda b, h, i: (0, 0))],
        out_specs=pl.BlockSpec((1, tq, 128), lambda b, h, i: (b, i, h)),
        scratch_shapes=[pltpu.VMEM((2, 128, tq), BF16),
                        pltpu.VMEM((2, 2, tk, tq), BF16),
                        pltpu.VMEM((2, 1, tq), F32),
                        pltpu.VMEM((2, dv, tq), F32),
                        pltpu.VMEM((2, 8, tq), F32)],
        compiler_params=_cp(3), name="flash_" + mode,
    )(lam, kmax, q_t, k_tok, v_aug, gain)


def _shift_rows(x, prev_row, next_row):
    T = x.shape[0]
    row = lax.broadcasted_iota(jnp.int32, x.shape, 0)
    xp = jnp.where(row == 0, prev_row, pltpu.roll(x, 1, axis=0))
    xn = jnp.where(row == T - 1, next_row, pltpu.roll(x, T - 1, axis=0))
    return xp, xn


def _conv3_block(x_ref, p_ref, n_ref, w_ref, b_ref, halo):
    i = pl.program_id(1)
    last = pl.num_programs(1) - 1
    x = x_ref[0].astype(F32)
    prev_row = jnp.where(i > 0, p_ref[0].astype(F32)[halo - 1:halo], 0.0)
    next_row = jnp.where(i < last, n_ref[0].astype(F32)[0:1], 0.0)
    xp, xn = _shift_rows(x, prev_row, next_row)
    return xp * w_ref[0:1] + x * w_ref[1:2] + xn * w_ref[2:3] + b_ref[...]


def _halo_specs(tr, tc, halo, row0, n_rows, col_of):
    per = tr // halo
    n_halo = n_rows // halo
    m0, h0 = row0 // tr, row0 // halo
    return [pl.BlockSpec((1, tr, tc), lambda b, i, j: (b, m0 + i, col_of(j))),
            pl.BlockSpec((1, halo, tc), lambda b, i, j: (b, h0 + jnp.maximum(i * per - 1, 0), col_of(j))),
            pl.BlockSpec((1, halo, tc),
                         lambda b, i, j: (b, h0 + jnp.minimum((i + 1) * per, n_halo - 1), col_of(j)))]


HYENA_HALO = 16


def _dwconv_kernel(x_ref, p_ref, n_ref, w_ref, b_ref, v_ref, x1_ref, x2_ref):
    y = _conv3_block(x_ref, p_ref, n_ref, w_ref, b_ref, HYENA_HALO)
    C = v_ref.shape[2]
    v_ref[0] = y[:, :C].astype(v_ref.dtype)
    x1_ref[0] = y[:, C:2 * C].astype(x1_ref.dtype)
    x2_ref[0] = y[:, 2 * C:].astype(x2_ref.dtype)


def hyena_short_conv(u, w, b, row0, L):
    B, _, C3 = u.shape
    C = C3 // 3
    tr = _tile(L, 512, HYENA_HALO)
    assert row0 % tr == 0
    zero = lambda j: 0
    out = jax.ShapeDtypeStruct((B, L, C), BF16)
    ospec = pl.BlockSpec((1, tr, C), lambda b, i, j: (b, i, 0))
    return pl.pallas_call(
        _dwconv_kernel,
        out_shape=(out, out, out),
        grid=(B, L // tr, 1),
        in_specs=_halo_specs(tr, C3, HYENA_HALO, row0, L, zero) + [
            pl.BlockSpec((3, C3), lambda b, i, j: (0, 0)),
            pl.BlockSpec((1, C3), lambda b, i, j: (0, 0))],
        out_specs=(ospec, ospec, ospec),
        compiler_params=_cp(3), name="hyena_short_conv",
    )(u, u, u, w, b)


FFN_HALO = 16
FFN_CHUNK = 256


def _ffn_kernel(h_ref, hp_ref, hn_ref, wup_ref, cw_ref, cb_ref, wdn_ref, x_ref, gate_ref, g_ref, b_ref, o_ref,
                act_sc, *, alpha):
    i = pl.program_id(1)
    tm = h_ref.shape[1]
    dff = wdn_ref.shape[0]
    rows = tm + 2 * FFN_HALO
    h_prev = jnp.where(i == 0, jnp.zeros_like(hp_ref[0]), hp_ref[0])
    h_next = jnp.where(i == pl.num_programs(1) - 1, jnp.zeros_like(hn_ref[0]), hn_ref[0])
    h_ext = jnp.concatenate([h_prev, h_ref[0], h_next], axis=0)

    def conv_cols(c0):
        u = jnp.dot(h_ext, wup_ref[:, c0:c0 + FFN_CHUNK], preferred_element_type=F32)
        mid = slice(FFN_HALO, FFN_HALO + tm)
        u_prev = pltpu.roll(u, 1, axis=0)[mid]
        u_next = pltpu.roll(u, rows - 1, axis=0)[mid]
        w = cw_ref[:, c0:c0 + FFN_CHUNK]
        return u_prev * w[0:1] + u[mid] * w[1:2] + u_next * w[2:3] + cb_ref[:, c0:c0 + FFN_CHUNK]

    for c0 in range(0, dff, FFN_CHUNK):
        val = conv_cols(c0)
        gate = conv_cols(dff + c0)
        act_sc[:, c0:c0 + FFN_CHUNK] = (gate * jax.nn.sigmoid(gate) * val).astype(BF16)
    y = jnp.dot(act_sc[...], wdn_ref[...], preferred_element_type=F32)
    o_ref[0] = _resid_ln_tail(x_ref[0], y, gate_ref[0], g_ref[...], b_ref[...], alpha)


def conv_ffn_resid_ln(h, w_up, conv_w, conv_b, w_down, x, gate, g, b, alpha):
    B, N, D = h.shape
    dff = w_down.shape[0]
    tm = _tile(N, 512, FFN_HALO)
    per = tm // FFN_HALO
    n_halo = N // FFN_HALO
    row = lambda b_, i: (b_, i, 0)
    per_b = lambda b_, i: (b_, 0, 0)
    const2 = lambda b_, i: (0, 0)
    resident = dict(pipeline_mode=pl.Buffered(1))
    return pl.pallas_call(
        functools.partial(_ffn_kernel, alpha=alpha),
        out_shape=jax.ShapeDtypeStruct((B, N, D), F32),
        grid=(B, N // tm),
        in_specs=[pl.BlockSpec((1, tm, D), row),
                  pl.BlockSpec((1, FFN_HALO, D), lambda b_, i: (b_, jnp.maximum(i * per - 1, 0), 0)),
                  pl.BlockSpec((1, FFN_HALO, D), lambda b_, i: (b_, jnp.minimum((i + 1) * per, n_halo - 1), 0)),
                  pl.BlockSpec((D, 2 * dff), const2, **resident),
                  pl.BlockSpec((3, 2 * dff), const2), pl.BlockSpec((1, 2 * dff), const2),
                  pl.BlockSpec((dff, D), const2, **resident),
                  pl.BlockSpec((1, tm, D), row), pl.BlockSpec((1, 1, D), per_b),
                  pl.BlockSpec((1, D), const2), pl.BlockSpec((1, D), const2)],
        out_specs=pl.BlockSpec((1, tm, D), row),
        scratch_shapes=[pltpu.VMEM((tm, dff), BF16)],
        compiler_params=_cp(2), name="conv_ffn_resid_ln",
    )(h, h, h, w_up, conv_w, conv_b, w_down, x, gate, g, b)


def _merge_kernel(a_ref, b_ref, c_ref, gt_ref, wpa_ref, wpb_ref, wpc_ref, wo_ref, x_ref, gate_ref,
                  g_ref, bb_ref, sh_ref, sc_ref, ox_ref, oh_ref, *, alpha):
    D = wo_ref.shape[0]
    tm = x_ref.shape[1]
    n_sub = 2 if tm % 32 == 0 else 1
    for r in range(n_sub):
        rows = slice(r * tm // n_sub, (r + 1) * tm // n_sub)
        gt = gt_ref[0, rows]
        m = gt[:, :D].astype(F32) * jnp.dot(a_ref[0, rows], wpa_ref[...], preferred_element_type=F32)
        m = m + gt[:, D:2 * D].astype(F32) * jnp.dot(b_ref[0, rows], wpb_ref[...], preferred_element_type=F32)
        m = m + gt[:, 2 * D:].astype(F32) * jnp.dot(c_ref[0, rows].astype(BF16), wpc_ref[...],
                                                    preferred_element_type=F32)
        y = jnp.dot(m.astype(BF16), wo_ref[...], preferred_element_type=F32)
        xn = _resid_ln_tail(x_ref[0, rows], y, gate_ref[0], g_ref[...], bb_ref[...], alpha)
        ox_ref[0, rows] = xn
        oh_ref[0, rows] = (_ln(xn) * (1.0 + sc_ref[0]) + sh_ref[0]).astype(oh_ref.dtype)


def merge_resid_ln(a, b, c, gates, tok0, wpa, wpb, wpc, wo, x, gate, g, bb, sh, sc, alpha):
    B, N, D = x.shape
    W = a.shape[2]
    tm = _tile(N, 512, 16)
    assert tok0 % tm == 0
    row = lambda b_, i: (b_, i, 0)
    per_b = lambda b_, i: (b_, 0, 0)
    const2 = lambda b_, i: (0, 0)
    return pl.pallas_call(
        functools.partial(_merge_kernel, alpha=alpha),
        out_shape=(jax.ShapeDtypeStruct((B, N, D), F32), jax.ShapeDtypeStruct((B, N, D), BF16)),
        grid=(B, N // tm),
        in_specs=[pl.BlockSpec((1, tm, W), row), pl.BlockSpec((1, tm, W), row), pl.BlockSpec((1, tm, W), row),
                  pl.BlockSpec((1, tm, 3 * D), lambda b_, i: (b_, tok0 // tm + i, 0)),
                  pl.BlockSpec((W, D), const2), pl.BlockSpec((W, D), const2), pl.BlockSpec((W, D), const2),
                  pl.BlockSpec((D, D), const2),
                  pl.BlockSpec((1, tm, D), row), pl.BlockSpec((1, 1, D), per_b),
                  pl.BlockSpec((1, D), const2), pl.BlockSpec((1, D), const2),
                  pl.BlockSpec((1, 1, D), per_b), pl.BlockSpec((1, 1, D), per_b)],
        out_specs=(pl.BlockSpec((1, tm, D), row), pl.BlockSpec((1, tm, D), row)),
        compiler_params=_cp(2), name="merge_resid_ln",
    )(a, b, c, gates, wpa, wpb, wpc, wo, x, gate, g, bb, sh, sc)


def _split_bf16(x):
    hi = x.astype(BF16)
    lo = (x - hi.astype(F32)).astype(BF16)
    return hi, lo


def _dot3(a, b):
    ah, al = _split_bf16(a)
    bh, bl = _split_bf16(b)
    d = functools.partial(jnp.dot, preferred_element_type=F32)
    return d(ah, bh) + (d(ah, bl) + d(al, bh))


def _filter_kernel(z_ref, w1_ref, b1_ref, w2_ref, b2_ref, w3_ref, fr_ref, dec_ref, h_ref, s_ref, *, zero_block):
    i = pl.program_id(0)

    @pl.when(i == 0)
    def _():
        s_ref[...] = jnp.zeros(s_ref.shape, F32)

    fr = fr_ref[...]
    hid = jnp.sin(fr * (_dot3(z_ref[...], w1_ref[...]) + b1_ref[...]))
    hid = jnp.sin(fr * (_dot3(hid, w2_ref[...]) + b2_ref[...]))
    h = _dot3(hid, w3_ref[0])
    dec = dec_ref[...]
    C = dec.shape[1]
    n_ord = h.shape[1] // C
    h = h * jnp.concatenate([dec] * n_ord, axis=1)
    s_ref[...] += jnp.sum(jnp.abs(h), axis=0, keepdims=True)
    row = lax.broadcasted_iota(jnp.int32, h.shape, 0)
    h = jnp.where((row == 0) & (i == zero_block), 0.0, h)
    for o in range(n_ord):
        h_ref[o] = h[:, o * C:(o + 1) * C]


def hyena_filter_mlp(z2, w1, b1, w2, b2, w3_dir, freq, decay2):
    n2, E = z2.shape
    Hh = w2.shape[0]
    OC = w3_dir.shape[2]
    C = decay2.shape[1]
    n_ord = OC // C
    L = n2 // 2
    tr = _tile(L, 512, 8)
    nblk = n2 // tr
    c2 = lambda i: (0, 0)
    return pl.pallas_call(
        functools.partial(_filter_kernel, zero_block=L // tr),
        out_shape=(jax.ShapeDtypeStruct((n_ord, n2, C), F32), jax.ShapeDtypeStruct((1, OC), F32)),
        grid=(nblk,),
        in_specs=[pl.BlockSpec((tr, E), lambda i: (i, 0)), pl.BlockSpec((E, Hh), c2), pl.BlockSpec((1, Hh), c2),
                  pl.BlockSpec((Hh, Hh), c2), pl.BlockSpec((1, Hh), c2),
                  pl.BlockSpec((1, Hh, OC), lambda i: (i // (nblk // 2), 0, 0)),
                  pl.BlockSpec((1, Hh), c2), pl.BlockSpec((tr, C), lambda i: (i, 0))],
        out_specs=(pl.BlockSpec((n_ord, tr, C), lambda i: (0, i, 0)), pl.BlockSpec((1, OC), c2)),
        compiler_params=_cp(1), name="hyena_filter_mlp",
    )(z2, w1, b1, w2, b2, w3_dir, freq, decay2)


def _mm_split(m_hi, m_lo, x, passes):
    d = functools.partial(jnp.dot, preferred_element_type=F32)
    if passes == 1:
        return d(m_hi, x.astype(BF16))
    xh, xl = _split_bf16(x)
    return d(m_hi, xh) + (d(m_lo, xh) + d(m_hi, xl))


FFT_ROWS = 8
FFT_BLOCK = 16


def _level1_apply(m_refs, x, r_out, passes):
    r_in, _, C = x.shape
    x = x.astype(F32)
    m_hi = m_refs[0][...]
    m_lo = m_refs[1][...] if passes > 1 else None
    groups = []
    for g in range(FFT_BLOCK // FFT_ROWS):
        xg = x[:, g * FFT_ROWS:(g + 1) * FFT_ROWS].reshape(r_in * FFT_ROWS, C)
        groups.append(_mm_split(m_hi, m_lo, xg, passes).reshape(r_out, FFT_ROWS, C))
    return jnp.concatenate(groups, axis=1)


def _level1_kernel(*refs, passes):
    n_mat = 1 if passes == 1 else 2
    x_ref, o_ref = refs[n_mat:]
    o_ref[0] = _level1_apply(refs[:n_mat], x_ref[0], o_ref.shape[1], passes).astype(o_ref.dtype)


def _level1_gate_kernel(*refs, passes):
    n_mat = 1 if passes == 1 else 2
    x_ref, z_ref, xg_ref, bias_ref, o_ref = refs[n_mat:]
    y = _level1_apply(refs[:n_mat], x_ref[0], o_ref.shape[1], passes)
    o_ref[0] = (xg_ref[0].astype(F32) * (y + bias_ref[...] * z_ref[0].astype(F32))).astype(o_ref.dtype)


def fft_level1(m_hi, m_lo, x, passes, gate_args=None, out_dtype=F32):
    P, R_in, Nb, C = x.shape
    R_out = m_hi.shape[0] // FFT_ROWS
    mats = (m_hi,) if passes == 1 else (m_hi, m_lo)
    mspec = pl.BlockSpec(m_hi.shape, lambda p, j: (0, 0), pipeline_mode=pl.Buffered(1))
    xspec = pl.BlockSpec((1, R_in, FFT_BLOCK, C), lambda p, j: (p, 0, j, 0))
    ospec = pl.BlockSpec((1, R_out, FFT_BLOCK, C), lambda p, j: (p, 0, j, 0))
    if gate_args is None:
        kern = functools.partial(_level1_kernel, passes=passes)
        ins, specs = (*mats, x), [mspec] * len(mats) + [xspec]
    else:
        z, xg, bias = gate_args
        kern = functools.partial(_level1_gate_kernel, passes=passes)
        ins = (*mats, x, z, xg, bias)
        specs = [mspec] * len(mats) + [xspec, ospec, ospec, pl.BlockSpec((1, C), lambda p, j: (0, 0))]
    return pl.pallas_call(
        kern, out_shape=jax.ShapeDtypeStruct((P, R_out, Nb, C), out_dtype),
        grid=(P, Nb // FFT_BLOCK), in_specs=specs, out_specs=ospec,
        compiler_params=_cp(2), name="fft_level1",
    )(*ins)


FFT_MID_K1 = 8


def _fft_mid_kernel(*refs, passes):
    n_mat = 2 if passes == 1 else 4
    gh_ref, gih_ref = refs[0], refs[1]
    gl_ref, gil_ref = (refs[2], refs[3]) if passes > 1 else (None, None)
    h_ref, a_ref, o_ref = refs[n_mat:]
    n_pairs, _, kb, nb, C = a_ref.shape
    for kk in range(kb):
        hr, hi = h_ref[0, 0, kk], h_ref[0, 1, kk]
        for p in range(n_pairs):
            x = a_ref[p, :, kk].reshape(2 * nb, C)
            X = _mm_split(gh_ref[kk], None if gl_ref is None else gl_ref[kk], x, passes)
            xr, xi = X[:nb], X[nb:]
            Y = jnp.concatenate([xr * hr - xi * hi, xr * hi + xi * hr], axis=0)
            Bv = _mm_split(gih_ref[kk], None if gil_ref is None else gil_ref[kk], Y, passes)
            o_ref[p, :, kk] = Bv.reshape(2, nb, C).astype(o_ref.dtype)


def fft_mid(g_hi, g_lo, gi_hi, gi_lo, spectra, order, a5, passes):
    P, _, Na, Nb, C = a5.shape
    kb = FFT_MID_K1
    gspec = pl.BlockSpec((kb, 2 * Nb, 2 * Nb), lambda k: (k, 0, 0))
    mats = (g_hi, gi_hi) if passes == 1 else (g_hi, gi_hi, g_lo, gi_lo)
    blk = pl.BlockSpec((P, 2, kb, Nb, C), lambda k: (0, 0, k, 0, 0))
    return pl.pallas_call(
        functools.partial(_fft_mid_kernel, passes=passes),
        out_shape=jax.ShapeDtypeStruct(a5.shape, a5.dtype),
        grid=(Na // kb,),
        in_specs=[gspec] * len(mats) + [pl.BlockSpec((1, 2, kb, Nb, C), lambda k: (order, 0, k, 0, 0)), blk],
        out_specs=blk,
        compiler_params=_cp(1), name="fft_mid",
    )(*mats, spectra, a5)


def _fft_spec_kernel(*refs, passes):
    n_mat = 1 if passes == 1 else 2
    gh_ref = refs[0]
    gl_ref = refs[1] if passes > 1 else None
    inv_ref, a_ref, o_ref = refs[n_mat:]
    n_filt, _, kb, nb, C = a_ref.shape
    for kk in range(kb):
        for p in range(n_filt):
            x = a_ref[p, :, kk].reshape(2 * nb, C)
            X = _mm_split(gh_ref[kk], None if gl_ref is None else gl_ref[kk], x, passes) * inv_ref[p]
            o_ref[p, :, kk] = X.reshape(2, nb, C)


def fft_filter_spectrum(g_hi, g_lo, inv_norm, a5, passes):
    P, _, Na, Nb, C = a5.shape
    kb = FFT_MID_K1
    mats = (g_hi,) if passes == 1 else (g_hi, g_lo)
    gspec = pl.BlockSpec((kb, 2 * Nb, 2 * Nb), lambda k: (k, 0, 0))
    blk = pl.BlockSpec((P, 2, kb, Nb, C), lambda k: (0, 0, k, 0, 0))
    return pl.pallas_call(
        functools.partial(_fft_spec_kernel, passes=passes),
        out_shape=jax.ShapeDtypeStruct(a5.shape, F32),
        grid=(Na // kb,),
        in_specs=[gspec] * len(mats) + [pl.BlockSpec((P, 1, C), lambda k: (0, 0, 0)), blk],
        out_specs=blk,
        compiler_params=_cp(1), name="fft_filter_spectrum",
    )(*mats, inv_norm, a5)


def _fft_factors(n_fft):
    na = 1 << (int(math.log2(n_fft)) // 2)
    return na, n_fft // na


def _fft_tables(L):
    n_fft = 2 * L
    na, nb = _fft_factors(n_fft)
    nh = na // 2
    iota = lambda shape, axis: lax.broadcasted_iota(jnp.int32, shape, axis)

    def cis(num, den):
        ang = (2.0 * math.pi / den) * (num % den).astype(F32)
        return jnp.cos(ang), -jnp.sin(ang)

    def block(diag_sign, pr, pc, cr, ci):
        return jnp.where(pr == pc, cr, jnp.where(pr == 1, diag_sign * ci, -diag_sign * ci))

    def kron_matrix(rows, cols, entry):
        shape = (rows * FFT_ROWS, cols * FFT_ROWS)
        I, J = iota(shape, 0), iota(shape, 1)
        m = entry(I // FFT_ROWS, J // FFT_ROWS)
        return _split_bf16(jnp.where(I % FFT_ROWS == J % FFT_ROWS, m, 0.0))

    def fwd_entry(i, j):
        cr, ci = cis((i % na) * (j % nh), na)
        return block(1, i // na, j // nh, cr, ci)

    def flt_entry(i, j):
        cr, ci = cis((i % na) * j, na)
        return jnp.where(i // na == 0, cr, ci)

    def inv_entry(i, j):
        cr, ci = cis((i % nh) * (j % na), na)
        return block(-1, i // nh, j // na, cr, ci) / n_fft

    gshape = (na, 2 * nb, 2 * nb)
    k1, gi_, gj_ = iota(gshape, 0), iota(gshape, 1), iota(gshape, 2)
    cr, ci = cis((gi_ % nb) * (gj_ % nb) * na + k1 * (gj_ % nb), n_fft)
    g = _split_bf16(block(1, gi_ // nb, gj_ // nb, cr, ci))
    cr, ci = cis((gj_ % nb) * (gi_ % nb) * na + k1 * (gi_ % nb), n_fft)
    ginv = _split_bf16(block(-1, gi_ // nb, gj_ // nb, cr, ci))

    C = HY_WIDTH
    n = jnp.arange(n_fft, dtype=jnp.int32)
    pos = jnp.where(n < L, n, jnp.where(n == L, 0, n_fft - n)).astype(F32)[:, None]
    t = pos * (1.0 / (L - 1))
    f = jnp.linspace(1e-4, HY_BANDS - 1, HY_BANDS, dtype=F32)
    ang = (2.0 * math.pi / L) * pos * f[None, :]
    z2 = jnp.concatenate([t, jnp.cos(ang), -jnp.sin(ang), jnp.zeros((n_fft, HY_EMB_PAD - HY_EMB), F32)], axis=-1)
    deltas = jnp.abs(jnp.linspace(HY_MIN_DECAY, HY_MAX_DECAY, C, dtype=F32))
    decay2 = jnp.exp(-t * deltas)
    return dict(na=na, nb=nb, fwd=kron_matrix(2 * na, na, fwd_entry), flt=kron_matrix(2 * na, na, flt_entry),
                inv=kron_matrix(na, 2 * na, inv_entry), g=g, ginv=ginv, z2=z2, decay2=decay2)


def hyena_filter_spectra(w1, b1, w2, b2, w3, freq, tabs, passes):
    C = HY_WIDTH
    na, nb = tabs["na"], tabs["nb"]
    w1p = jnp.pad(w1, ((0, HY_EMB_PAD - HY_EMB), (0, 0)))
    w3_dir = jnp.transpose(w3.reshape(HY_HIDDEN, HY_ORDER, 2, C), (2, 0, 1, 3)).reshape(2, HY_HIDDEN, HY_ORDER * C)
    filt, s = hyena_filter_mlp(tabs["z2"], w1p, b1[None], w2, b2[None], w3_dir, freq[None], tabs["decay2"])
    inv_norm = (1.0 / s).reshape(HY_ORDER, 1, C)
    a = fft_level1(*tabs["flt"], filt.reshape(HY_ORDER, na, nb, C), passes,
                   out_dtype=BF16 if passes == 1 else F32)
    return fft_filter_spectrum(*tabs["g"], inv_norm, a.reshape(HY_ORDER, 2, na, nb, C), passes)


def hyena_long_conv_gate(z, xg, bias, spectra, order, tabs, passes):
    B, L, C = z.shape
    na, nb = tabs["na"], tabs["nb"]
    P = B // 2
    nat = lambda a: a.reshape(P, na, nb, C)
    mid_dtype = BF16 if passes == 1 else F32
    a = fft_level1(*tabs["fwd"], nat(z), passes, out_dtype=mid_dtype)
    bv = fft_mid(*tabs["g"], *tabs["ginv"], spectra, order, a.reshape(P, 2, na, nb, C), passes)
    y = fft_level1(*tabs["inv"], bv.reshape(P, 2 * na, nb, C), passes,
                   gate_args=(nat(z), nat(xg), bias[None]), out_dtype=z.dtype)
    return y.reshape(B, L, C)


def hyena_mix(u, row0, L, spectra, conv_w, conv_b, bias, tabs, passes):
    v, x1, x2 = hyena_short_conv(u, conv_w, conv_b[None], row0, L)
    zz = hyena_long_conv_gate(v, x1, bias[0], spectra, 0, tabs, passes)
    return hyena_long_conv_gate(zz, x2, bias[1], spectra, 1, tabs, passes)


FFT_PASSES = 1
FILTER_PASSES = 1


def _rope_tables_t(n):
    rows = n // GRID_W
    r = jnp.repeat(jnp.arange(rows, dtype=F32), GRID_W)
    col = jnp.tile(jnp.arange(GRID_W, dtype=F32), rows)
    axis_dim = HEAD_DIM // 2
    inv = ROPE_THETA ** (-jnp.arange(0, axis_dim, 2, dtype=F32) / axis_dim)
    ang = jnp.concatenate([r[:, None] * inv, col[:, None] * inv], axis=-1)
    return jnp.cos(ang).T, jnp.sin(ang).T


def _layer_weights(l, w_in, gq_qn, gq_kn):
    w = w_in[l]
    d_qk = DA_HEADS * 2 * HEAD_DIM
    gq_kv = GKV_HEADS * HEAD_DIM
    gq_q = GQ_HEADS * HEAD_DIM
    o = np.cumsum([0, d_qk, d_qk, gq_kv, gq_kv, d_qk, gq_q, 3 * HY_WIDTH])
    ka, va, kb, vb, qa, qb, hy = (w[:, o[i]:o[i + 1]] for i in range(7))
    gates = w[:, o[7]:]
    qscale = (HEAD_DIM ** -0.5) * LOG2E
    w_qk = jnp.concatenate([ka, kb, qa, qb], axis=1).T.astype(BF16)
    g_qk = jnp.concatenate([jnp.ones((d_qk,), F32), jnp.tile(gq_kn[l], GKV_HEADS),
                            jnp.full((d_qk,), qscale, F32), jnp.tile(gq_qn[l], GQ_HEADS) * qscale])[:, None]
    n_ka, n_kb, n_qa = d_qk // HEAD_DIM, gq_kv // HEAD_DIM, d_qk // HEAD_DIM
    rms_heads = frozenset(range(n_ka, n_ka + n_kb)) | frozenset(range(n_ka + n_kb + n_qa,
                                                                       n_ka + n_kb + n_qa + gq_q // HEAD_DIM))
    return dict(w_qk=w_qk, g_qk=g_qk, rms_heads=rms_heads, w_va=va.T.astype(BF16), w_vb=vb.T.astype(BF16),
                w_tok=jnp.concatenate([hy, gates], axis=1).astype(BF16))


def kernel(x, c, ctx, c_ctx, w_mod, b_mod, w_in, da_lq1, da_lk1, da_lq2, da_lk2, da_subln, gq_qn, gq_kn,
           hy_conv_w, hy_conv_b, hy_w1, hy_b1, hy_w2, hy_b2, hy_w3, hy_freq, hy_bias, w_pa, w_pb, w_pc, w_o,
           ln1_g, ln1_b, w_up, ffn_conv_w, ffn_conv_b, w_down, ln2_g, ln2_b):
    B, n_lat, D = x.shape
    n_ctx = ctx.shape[1]
    depth = w_in.shape[0]
    alpha = (2 * depth) ** 0.25
    cos_l, sin_l = _rope_tables_t(n_lat)
    cos_t = jnp.concatenate([cos_l, jnp.ones((HALF, n_ctx), F32)], axis=1)
    sin_t = jnp.concatenate([sin_l, jnp.zeros((HALF, n_ctx), F32)], axis=1)
    lat, ctx_span, all_span = (0, n_lat), (n_lat, n_ctx), (0, n_lat + n_ctx)
    tabs_l = _fft_tables(n_lat)
    tabs_c = _fft_tables(n_ctx)
    xc = ctx

    cond = jnp.concatenate([c, c_ctx[None]], axis=0)
    cond = jnp.pad(jax.nn.silu(cond), ((0, 16 - (B + 1) % 16), (0, 0))).astype(BF16)

    for l in range(depth):
        last = l == depth - 1
        lam_init = 0.8 - 0.6 * math.exp(-0.3 * l)
        lam = (jnp.exp(jnp.sum(da_lq1[l] * da_lk1[l])) - jnp.exp(jnp.sum(da_lq2[l] * da_lk2[l])) + lam_init)
        lam = lam.reshape(1).astype(F32)
        mod = matmul(cond, w_mod[l].astype(BF16), F32, tn_pref=1024) + b_mod[l]
        sh1, sc1, g1, sh2, sc2, g2 = [m[:, None, :] for m in jnp.split(mod[:B], 6, axis=-1)]
        mc = [jnp.broadcast_to(m[None, None, :], (B, 1, D)) for m in jnp.split(mod[B], 6)]
        lw = _layer_weights(l, w_in, gq_qn, gq_kn)
        subln = da_subln[l][:, None]
        hy_mlp = (hy_w1[l], hy_b1[l], hy_w2[l], hy_b2[l], hy_w3[l], hy_freq[l])
        spectra_l = hyena_filter_spectra(*hy_mlp, tabs_l, FILTER_PASSES)
        wpa, wpb, wpc, wo = (w[l].astype(BF16) for w in (w_pa, w_pb, w_pc, w_o))
        wup, wdn = w_up[l].astype(BF16), w_down[l].astype(BF16)
        ln1 = (ln1_g[l][None], ln1_b[l][None])
        ln2 = (ln2_g[l][None], ln2_b[l][None])

        q_all, k_tok, k_n2, va, vb, u, gates = proj_all(x, xc, sh1, sc1, mc[0], mc[1], cos_t, sin_t, lw)
        kmax = jnp.sqrt(k_n2[:, :N_KHEADS, 0]).reshape(B * N_KHEADS)
        n_da_maps = DA_HEADS * 2

        def attend(q_span, kv_span):
            da = flash_attention(q_all, k_tok, va, kmax, lam, subln, mode="da", q_span=q_span, kv_span=kv_span,
                                 q_row_block=0, k_col_block=0, khead0=0, post_scale=1.0 - lam_init)
            gq = flash_attention(q_all, k_tok, vb, kmax, lam, subln, mode="gqa", q_span=q_span, kv_span=kv_span,
                                 q_row_block=n_da_maps * HEAD_DIM // 128, k_col_block=n_da_maps * HEAD_DIM // 128,
                                 khead0=n_da_maps)
            return da, gq

        hy_args = (hy_conv_w[l], hy_conv_b[l], hy_bias[l])
        a_l, b_l = attend(lat, all_span)
        c_l = hyena_mix(u, *lat, spectra_l, *hy_args, tabs_l, FFT_PASSES)
        x_new, h2 = merge_resid_ln(a_l, b_l, c_l, gates, lat[0], wpa, wpb, wpc, wo, x, g1, *ln1,
                                   sh2, sc2, alpha)
        x_new = conv_ffn_resid_ln(h2, wup, ffn_conv_w[l], ffn_conv_b[l][None], wdn, x_new, g2, *ln2, alpha)

        if not last:
            spectra_c = hyena_filter_spectra(*hy_mlp, tabs_c, FILTER_PASSES)
            a_c, b_c = attend(ctx_span, ctx_span)
            c_c = hyena_mix(u, *ctx_span, spectra_c, *hy_args, tabs_c, FFT_PASSES)
            xc, hc2 = merge_resid_ln(a_c, b_c, c_c, gates, ctx_span[0], wpa, wpb, wpc, wo, xc, mc[2],
                                     *ln1, mc[3], mc[4], alpha)
            xc = conv_ffn_resid_ln(hc2, wup, ffn_conv_w[l], ffn_conv_b[l][None], wdn, xc, mc[5], *ln2, alpha)
        x = x_new
    return x
```

```python
import functools
import math

import numpy as np
import jax
import jax.numpy as jnp
from jax import lax
from jax.experimental import pallas as pl
from jax.experimental.pallas import tpu as pltpu

F32 = jnp.float32
BF16 = jnp.bfloat16

HEAD_DIM = 64
HALF = HEAD_DIM // 2
GRID_W = 64
ROPE_THETA = 10000.0
DA_HEADS = 4
GQ_HEADS = 8
GKV_HEADS = 2
HY_WIDTH = 512
HY_ORDER = 2
HY_EMB = 33
HY_EMB_PAD = 128
HY_BANDS = (HY_EMB - 1) // 2
HY_HIDDEN = 64
HY_MIN_DECAY = math.log(1e-2) / 1.5
HY_MAX_DECAY = math.log(1e-2) / 0.3
LN_EPS = 1e-6
LOG2E = 1.4426950408889634
NEG_BIG = -1e30

VMEM_LIMIT = 56 * 1024 * 1024


def _cp(n_axes):
    return pltpu.CompilerParams(dimension_semantics=("arbitrary",) * n_axes,
                                vmem_limit_bytes=VMEM_LIMIT)


def _tile(n, pref, mult):
    if n <= pref:
        return n
    t = (pref // mult) * mult
    while t >= mult:
        if n % t == 0:
            return t
        t -= mult
    return n


def _ln(x):
    mu = jnp.mean(x, axis=-1, keepdims=True)
    xc = x - mu
    return xc * lax.rsqrt(jnp.mean(xc * xc, axis=-1, keepdims=True) + LN_EPS)


N_KHEADS = DA_HEADS * 2 + GKV_HEADS
KNORM_ROWS = 16


def _proj_all_kernel(x_ref, xc_ref, sh_ref, sc_ref, shc_ref, scc_ref, cos_ref, sin_ref, wqk_ref, gqk_ref,
                     wva_ref, wvb_ref, wtok_ref,
                     q_ref, k_ref, kn_ref, va_ref, vb_ref, u_ref, gt_ref, kt_sc,
                     *, n_lat_tiles, rms_heads, n_hy):
    i = pl.program_id(1)

    @pl.when(i == 0)
    def _():
        kn_ref[...] = jnp.zeros(kn_ref.shape, F32)

    is_latent = i < n_lat_tiles
    xin = jnp.where(is_latent, x_ref[0], xc_ref[0])
    shift = jnp.where(is_latent, sh_ref[0], shc_ref[0])
    scale = jnp.where(is_latent, sc_ref[0], scc_ref[0])
    h = (_ln(xin) * (1.0 + scale) + shift).astype(BF16)
    nt = (((1,), (1,)), ((), ()))

    acc = lax.dot_general(wqk_ref[...], h, nt, preferred_element_type=F32)
    va = lax.dot_general(wva_ref[...], h, nt, preferred_element_type=F32)
    vb = lax.dot_general(wvb_ref[...], h, nt, preferred_element_type=F32)
    tok = jnp.dot(h, wtok_ref[...], preferred_element_type=F32)
    va_ref[0] = va.astype(va_ref.dtype)
    vb_ref[0] = vb.astype(vb_ref.dtype)
    c, s = cos_ref[...], sin_ref[...]
    n_k_rows = kt_sc.shape[0]
    for hd in range(acc.shape[0] // HEAD_DIM):
        lo = hd * HEAD_DIM
        xh = acc[lo:lo + HEAD_DIM]
        if hd in rms_heads:
            xh = xh * lax.rsqrt(jnp.mean(xh * xh, axis=0, keepdims=True) + LN_EPS)
        xh = xh * gqk_ref[lo:lo + HEAD_DIM]
        x1, x2 = xh[:HALF], xh[HALF:]
        r1, r2 = x1 * c - x2 * s, x1 * s + x2 * c
        if lo < n_k_rows:
            kt_sc[lo:lo + HALF] = r1
            kt_sc[lo + HALF:lo + HEAD_DIM] = r2
        else:
            q_ref[0, lo - n_k_rows:lo - n_k_rows + HALF] = r1.astype(q_ref.dtype)
            q_ref[0, lo - n_k_rows + HALF:lo - n_k_rows + HEAD_DIM] = r2.astype(q_ref.dtype)

    k_bf = kt_sc[...].astype(BF16)
    k_ref[0] = k_bf.astype(F32).T.astype(BF16)
    k_sq = k_bf.astype(F32) ** 2
    for hd in range(n_k_rows // HEAD_DIM):
        n2 = jnp.sum(k_sq[hd * HEAD_DIM:(hd + 1) * HEAD_DIM], axis=0, keepdims=True)
        kn_ref[0, hd:hd + 1] = jnp.maximum(kn_ref[0, hd:hd + 1], jnp.max(n2, axis=1, keepdims=True))

    u_ref[0] = tok[:, :n_hy].astype(u_ref.dtype)
    gt_ref[0] = jax.nn.sigmoid(tok[:, n_hy:]).astype(gt_ref.dtype)


def proj_all(x, xc, shift, scale, shift_c, scale_c, cos_t, sin_t, lw):
    B, N, D = x.shape
    Nc = xc.shape[1]
    ntot = N + Nc
    tt = _tile(Nc, 256, 128)
    assert N % tt == 0
    nl = N // tt
    n_k_rows = N_KHEADS * HEAD_DIM
    n_q_rows = lw["w_qk"].shape[0] - n_k_rows
    n_va, n_vb = lw["w_va"].shape[0], lw["w_vb"].shape[0]
    n_tok = lw["w_tok"].shape[1]
    n_hy = 3 * HY_WIDTH
    per_b = lambda b, i: (b, 0, 0)
    const = lambda b, i: (0, 0)
    resident = dict(pipeline_mode=pl.Buffered(1))
    feat = lambda rows: pl.BlockSpec((1, rows, tt), lambda b, i: (b, 0, i))
    tokm = lambda cols: pl.BlockSpec((1, tt, cols), lambda b, i: (b, i, 0))
    sds = jax.ShapeDtypeStruct
    return pl.pallas_call(
        functools.partial(_proj_all_kernel, n_lat_tiles=nl, rms_heads=lw["rms_heads"], n_hy=n_hy),
        out_shape=(sds((B, n_q_rows, ntot), BF16), sds((B, ntot, n_k_rows), BF16),
                   sds((B, KNORM_ROWS, 128), F32), sds((B, n_va, ntot), BF16), sds((B, n_vb, ntot), BF16),
                   sds((B, ntot, n_hy), BF16), sds((B, ntot, n_tok - n_hy), BF16)),
        grid=(B, ntot // tt),
        in_specs=[pl.BlockSpec((1, tt, D), lambda b, i: (b, jnp.minimum(i, nl - 1), 0)),
                  pl.BlockSpec((1, tt, D), lambda b, i: (b, jnp.maximum(i - nl, 0), 0)),
                  pl.BlockSpec((1, 1, D), per_b), pl.BlockSpec((1, 1, D), per_b),
                  pl.BlockSpec((1, 1, D), per_b), pl.BlockSpec((1, 1, D), per_b),
                  pl.BlockSpec((HALF, tt), lambda b, i: (0, i)), pl.BlockSpec((HALF, tt), lambda b, i: (0, i)),
                  pl.BlockSpec(lw["w_qk"].shape, const, **resident), pl.BlockSpec((lw["w_qk"].shape[0], 1), const),
                  pl.BlockSpec(lw["w_va"].shape, const, **resident),
                  pl.BlockSpec(lw["w_vb"].shape, const, **resident),
                  pl.BlockSpec(lw["w_tok"].shape, const, **resident)],
        out_specs=(feat(n_q_rows), tokm(n_k_rows), pl.BlockSpec((1, KNORM_ROWS, 128), per_b),
                   feat(n_va), feat(n_vb), tokm(n_hy), tokm(n_tok - n_hy)),
        scratch_shapes=[pltpu.VMEM((n_k_rows, tt), F32)],
        compiler_params=_cp(2), name="proj_all",
    )(x, xc, shift, scale, shift_c, scale_c, cos_t, sin_t, lw["w_qk"], lw["g_qk"], lw["w_va"], lw["w_vb"],
      lw["w_tok"])


def _resid_ln_tail(x, y, gate, g, b, alpha):
    return _ln(alpha * x + gate * y) * g + b


def _mm_kernel(a_ref, w_ref, o_ref, *, act):
    acc = jnp.dot(a_ref[...], w_ref[...], preferred_element_type=F32)
    if act == "sigmoid":
        acc = jax.nn.sigmoid(acc)
    o_ref[...] = acc.astype(o_ref.dtype)


def matmul(a, w, out_dtype, act=None, tm_pref=1024, tn_pref=512):
    M, K = a.shape
    Nn = w.shape[1]
    tm = _tile(M, tm_pref, 16)
    tn = _tile(Nn, tn_pref, 128)
    return pl.pallas_call(
        functools.partial(_mm_kernel, act=act),
        out_shape=jax.ShapeDtypeStruct((M, Nn), out_dtype),
        grid=(M // tm, Nn // tn),
        in_specs=[pl.BlockSpec((tm, K), lambda i, j: (i, 0)),
                  pl.BlockSpec((K, tn), lambda i, j: (0, j))],
        out_specs=pl.BlockSpec((tm, tn), lambda i, j: (i, j)),
        compiler_params=_cp(2), name="matmul",
    )(a, w)


FLASH_MIN_DENOM = 2.0 ** -60
FLASH_UNROLL = 4


def _flash_kernel(lam_ref, kmax_ref, q_ref, k_ref, v_ref, g_ref, o_ref, q_sc, p_buf, sh_sc, acc_sc, l_sc,
                  *, mode, nj, tk, post_scale, n_kheads, khead0):
    b, h = pl.program_id(0), pl.program_id(1)
    qf = q_ref[0].astype(F32)
    tq = qf.shape[1]
    zero = jnp.zeros((HEAD_DIM, tq), F32)
    qa, qb = qf[:HEAD_DIM], qf[HEAD_DIM:]
    if mode == "da":
        q_sc[0] = jnp.concatenate([qa, zero], axis=0).astype(BF16)
        q_sc[1] = jnp.concatenate([zero, qb], axis=0).astype(BF16)
        k_heads = (khead0 + 2 * h, khead0 + 2 * h + 1)
    else:
        group = h // 2
        for mi, qh in enumerate((qa, qb)):
            q_sc[mi] = jnp.where(group == 0, jnp.concatenate([qh, zero], axis=0),
                                 jnp.concatenate([zero, qh], axis=0)).astype(BF16)
        k_heads = (khead0 + group, khead0 + group)
    for mi, qh in enumerate((qa, qb)):
        q_norm = jnp.sqrt(jnp.sum(qh * qh, axis=0, keepdims=True))
        sh_sc[mi] = q_norm * kmax_ref[b * n_kheads + k_heads[mi]]

    def chunk(j):
        start = j * tk
        return pl.ds(start if isinstance(start, int) else pl.multiple_of(start, tk), tk)

    def stage_exp(j, slot):
        kb = k_ref[0, chunk(j), :]
        for mi in range(2):
            s = jnp.dot(kb, q_sc[mi], preferred_element_type=F32)
            p = jnp.exp2(s - sh_sc[mi])
            p_buf[slot, mi] = p.astype(BF16)
            l_sc[mi] += jnp.sum(p.reshape(tk // 8, 8, tq), axis=0)

    def stage_values(j, slot):
        vb = v_ref[0, :, chunk(j)]
        for mi in range(2):
            acc_sc[mi] += jnp.dot(vb, p_buf[slot, mi], preferred_element_type=F32)

    def accumulate():
        acc_sc[...] = jnp.zeros(acc_sc.shape, F32)
        l_sc[...] = jnp.zeros(l_sc.shape, F32)
        stage_exp(0, 0)
        steady = list(range(1, nj))
        while len(steady) % FLASH_UNROLL:
            t = steady.pop(0)
            stage_exp(t, t % 2)
            stage_values(t - 1, (t - 1) % 2)
        if steady:
            t0 = steady[0]

            def body(i, carry):
                for d in range(FLASH_UNROLL):
                    stage_exp(t0 + FLASH_UNROLL * i + d, (t0 + d) % 2)
                    stage_values(t0 + FLASH_UNROLL * i + d - 1, (t0 + d - 1) % 2)
                return carry

            lax.fori_loop(0, len(steady) // FLASH_UNROLL, body, 0)
        stage_values(nj - 1, (nj - 1) % 2)

    denominators = lambda: [jnp.sum(l_sc[mi], axis=0, keepdims=True) for mi in range(2)]
    accumulate()
    denom_min = jnp.min(jnp.minimum(*denominators()))

    @pl.when(jnp.logical_not(denom_min >= FLASH_MIN_DENOM))
    def _():
        for mi in range(2):
            def max_body(j, m, mi=mi):
                s = jnp.dot(k_ref[0, chunk(j), :], q_sc[mi], preferred_element_type=F32)
                return jnp.maximum(m, jnp.max(s, axis=0, keepdims=True))

            sh_sc[mi] = lax.fori_loop(0, nj, max_body, jnp.full((1, tq), NEG_BIG, F32))
        accumulate()

    outs = [acc_sc[mi] / l for mi, l in enumerate(denominators())]
    if mode == "da":
        o = outs[0] - lam_ref[0] * outs[1]
        o = o * lax.rsqrt(jnp.mean(o * o, axis=0, keepdims=True) + LN_EPS)
        o = o * (g_ref[...] * post_scale)
    else:
        o = jnp.concatenate(outs, axis=0)
    o_ref[0] = o.T.astype(o_ref.dtype)


def flash_attention(q_t, k_tok, v_aug, kmax, lam, gain, *, mode, q_span, kv_span, q_row_block, k_col_block,
                    khead0, post_scale=1.0, tk_pref=2816):
    B = q_t.shape[0]
    q0, Nq = q_span
    k0, Nk = kv_span
    tq = _tile(Nq, 1024, 128)
    tk = _tile(Nk, tk_pref, 128)
    assert q0 % tq == 0 and k0 % Nk == 0
    nj = Nk // tk
    dv = 2 * HEAD_DIM if mode == "da" else HEAD_DIM
    n_kheads = kmax.shape[0] // B
    if mode == "da":
        k_map = lambda b, h, i: (b, k0 // Nk, k_col_block + h)
        v_map = lambda b, h, i: (b, h, k0 // Nk)
    else:
        k_map = lambda b, h, i: (b, k0 // Nk, k_col_block)
        v_map = lambda b, h, i: (b, h // 2, k0 // Nk)
    return pl.pallas_call(
        functools.partial(_flash_kernel, mode=mode, nj=nj, tk=tk, post_scale=post_scale,
                          n_kheads=n_kheads, khead0=khead0),
        out_shape=jax.ShapeDtypeStruct((B, Nq, 4 * 128), BF16),
        grid=(B, 4, Nq // tq),
        in_specs=[pl.BlockSpec(memory_space=pltpu.SMEM),
                  pl.BlockSpec(memory_space=pltpu.SMEM),
                  pl.BlockSpec((1, 128, tq), lambda b, h, i: (b, h + q_row_block, i + q0 // tq)),
                  pl.BlockSpec((1, Nk, 128), k_map),
                  pl.BlockSpec((1, dv, Nk), v_map),
                  pl.BlockSpec((128, 1), lambda b, h, i: (0, 0))],
        out_specs=pl.BlockSpec((1, tq, 128), lambda b, h, i: (b, i, h)),
        scratch_shapes=[pltpu.VMEM((2, 128, tq), BF16),
                        pltpu.VMEM((2, 2, tk, tq), BF16),
                        pltpu.VMEM((2, 1, tq), F32),
                        pltpu.VMEM((2, dv, tq), F32),
                        pltpu.VMEM((2, 8, tq), F32)],
        compiler_params=_cp(3), name="flash_" + mode,
    )(lam, kmax, q_t, k_tok, v_aug, gain)


def _shift_rows(x, prev_row, next_row):
    T = x.shape[0]
    row = lax.broadcasted_iota(jnp.int32, x.shape, 0)
    xp = jnp.where(row == 0, prev_row, pltpu.roll(x, 1, axis=0))
    xn = jnp.where(row == T - 1, next_row, pltpu.roll(x, T - 1, axis=0))
    return xp, xn


def _conv3_block(x_ref, p_ref, n_ref, w_ref, b_ref, halo):
    i = pl.program_id(1)
    last = pl.num_programs(1) - 1
    x = x_ref[0].astype(F32)
    prev_row = jnp.where(i > 0, p_ref[0].astype(F32)[halo - 1:halo], 0.0)
    next_row = jnp.where(i < last, n_ref[0].astype(F32)[0:1], 0.0)
    xp, xn = _shift_rows(x, prev_row, next_row)
    return xp * w_ref[0:1] + x * w_ref[1:2] + xn * w_ref[2:3] + b_ref[...]


def _halo_specs(tr, tc, halo, row0, n_rows, col_of):
    per = tr // halo
    n_halo = n_rows // halo
    m0, h0 = row0 // tr, row0 // halo
    return [pl.BlockSpec((1, tr, tc), lambda b, i, j: (b, m0 + i, col_of(j))),
            pl.BlockSpec((1, halo, tc), lambda b, i, j: (b, h0 + jnp.maximum(i * per - 1, 0), col_of(j))),
            pl.BlockSpec((1, halo, tc),
                         lambda b, i, j: (b, h0 + jnp.minimum((i + 1) * per, n_halo - 1), col_of(j)))]


HYENA_HALO = 16


def _dwconv_kernel(x_ref, p_ref, n_ref, w_ref, b_ref, v_ref, x1_ref, x2_ref):
    y = _conv3_block(x_ref, p_ref, n_ref, w_ref, b_ref, HYENA_HALO)
    C = v_ref.shape[2]
    v_ref[0] = y[:, :C].astype(v_ref.dtype)
    x1_ref[0] = y[:, C:2 * C].astype(x1_ref.dtype)
    x2_ref[0] = y[:, 2 * C:].astype(x2_ref.dtype)


def hyena_short_conv(u, w, b, row0, L):
    B, _, C3 = u.shape
    C = C3 // 3
    tr = _tile(L, 512, HYENA_HALO)
    assert row0 % tr == 0
    zero = lambda j: 0
    out = jax.ShapeDtypeStruct((B, L, C), BF16)
    ospec = pl.BlockSpec((1, tr, C), lambda b, i, j: (b, i, 0))
    return pl.pallas_call(
        _dwconv_kernel,
        out_shape=(out, out, out),
        grid=(B, L // tr, 1),
        in_specs=_halo_specs(tr, C3, HYENA_HALO, row0, L, zero) + [
            pl.BlockSpec((3, C3), lambda b, i, j: (0, 0)),
            pl.BlockSpec((1, C3), lambda b, i, j: (0, 0))],
        out_specs=(ospec, ospec, ospec),
        compiler_params=_cp(3), name="hyena_short_conv",
    )(u, u, u, w, b)


FFN_HALO = 16
FFN_CHUNK = 256


def _ffn_kernel(h_ref, hp_ref, hn_ref, wup_ref, cw_ref, cb_ref, wdn_ref, x_ref, gate_ref, g_ref, b_ref, o_ref,
                act_sc, *, alpha):
    i = pl.program_id(1)
    tm = h_ref.shape[1]
    dff = wdn_ref.shape[0]
    rows = tm + 2 * FFN_HALO
    h_prev = jnp.where(i == 0, jnp.zeros_like(hp_ref[0]), hp_ref[0])
    h_next = jnp.where(i == pl.num_programs(1) - 1, jnp.zeros_like(hn_ref[0]), hn_ref[0])
    h_ext = jnp.concatenate([h_prev, h_ref[0], h_next], axis=0)

    def conv_cols(c0):
        u = jnp.dot(h_ext, wup_ref[:, c0:c0 + FFN_CHUNK], preferred_element_type=F32)
        mid = slice(FFN_HALO, FFN_HALO + tm)
        u_prev = pltpu.roll(u, 1, axis=0)[mid]
        u_next = pltpu.roll(u, rows - 1, axis=0)[mid]
        w = cw_ref[:, c0:c0 + FFN_CHUNK]
        return u_prev * w[0:1] + u[mid] * w[1:2] + u_next * w[2:3] + cb_ref[:, c0:c0 + FFN_CHUNK]

    for c0 in range(0, dff, FFN_CHUNK):
        val = conv_cols(c0)
        gate = conv_cols(dff + c0)
        act_sc[:, c0:c0 + FFN_CHUNK] = (gate * jax.nn.sigmoid(gate) * val).astype(BF16)
    y = jnp.dot(act_sc[...], wdn_ref[...], preferred_element_type=F32)
    o_ref[0] = _resid_ln_tail(x_ref[0], y, gate_ref[0], g_ref[...], b_ref[...], alpha)


def conv_ffn_resid_ln(h, w_up, conv_w, conv_b, w_down, x, gate, g, b, alpha):
    B, N, D = h.shape
    dff = w_down.shape[0]
    tm = _tile(N, 512, FFN_HALO)
    per = tm // FFN_HALO
    n_halo = N // FFN_HALO
    row = lambda b_, i: (b_, i, 0)
    per_b = lambda b_, i: (b_, 0, 0)
    const2 = lambda b_, i: (0, 0)
    resident = dict(pipeline_mode=pl.Buffered(1))
    return pl.pallas_call(
        functools.partial(_ffn_kernel, alpha=alpha),
        out_shape=jax.ShapeDtypeStruct((B, N, D), F32),
        grid=(B, N // tm),
        in_specs=[pl.BlockSpec((1, tm, D), row),
                  pl.BlockSpec((1, FFN_HALO, D), lambda b_, i: (b_, jnp.maximum(i * per - 1, 0), 0)),
                  pl.BlockSpec((1, FFN_HALO, D), lambda b_, i: (b_, jnp.minimum((i + 1) * per, n_halo - 1), 0)),
                  pl.BlockSpec((D, 2 * dff), const2, **resident),
                  pl.BlockSpec((3, 2 * dff), const2), pl.BlockSpec((1, 2 * dff), const2),
                  pl.BlockSpec((dff, D), const2, **resident),
                  pl.BlockSpec((1, tm, D), row), pl.BlockSpec((1, 1, D), per_b),
                  pl.BlockSpec((1, D), const2), pl.BlockSpec((1, D), const2)],
        out_specs=pl.BlockSpec((1, tm, D), row),
        scratch_shapes=[pltpu.VMEM((tm, dff), BF16)],
        compiler_params=_cp(2), name="conv_ffn_resid_ln",
    )(h, h, h, w_up, conv_w, conv_b, w_down, x, gate, g, b)


def _merge_kernel(a_ref, b_ref, c_ref, gt_ref, wpa_ref, wpb_ref, wpc_ref, wo_ref, x_ref, gate_ref,
                  g_ref, bb_ref, sh_ref, sc_ref, ox_ref, oh_ref, *, alpha):
    D = wo_ref.shape[0]
    tm = x_ref.shape[1]
    n_sub = 2 if tm % 32 == 0 else 1
    for r in range(n_sub):
        rows = slice(r * tm // n_sub, (r + 1) * tm // n_sub)
        gt = gt_ref[0, rows]
        m = gt[:, :D].astype(F32) * jnp.dot(a_ref[0, rows], wpa_ref[...], preferred_element_type=F32)
        m = m + gt[:, D:2 * D].astype(F32) * jnp.dot(b_ref[0, rows], wpb_ref[...], preferred_element_type=F32)
        m = m + gt[:, 2 * D:].astype(F32) * jnp.dot(c_ref[0, rows].astype(BF16), wpc_ref[...],
                                                    preferred_element_type=F32)
        y = jnp.dot(m.astype(BF16), wo_ref[...], preferred_element_type=F32)
        xn = _resid_ln_tail(x_ref[0, rows], y, gate_ref[0], g_ref[...], bb_ref[...], alpha)
        ox_ref[0, rows] = xn
        oh_ref[0, rows] = (_ln(xn) * (1.0 + sc_ref[0]) + sh_ref[0]).astype(oh_ref.dtype)


def merge_resid_ln(a, b, c, gates, tok0, wpa, wpb, wpc, wo, x, gate, g, bb, sh, sc, alpha):
    B, N, D = x.shape
    W = a.shape[2]
    tm = _tile(N, 512, 16)
    assert tok0 % tm == 0
    row = lambda b_, i: (b_, i, 0)
    per_b = lambda b_, i: (b_, 0, 0)
    const2 = lambda b_, i: (0, 0)
    return pl.pallas_call(
        functools.partial(_merge_kernel, alpha=alpha),
        out_shape=(jax.ShapeDtypeStruct((B, N, D), F32), jax.ShapeDtypeStruct((B, N, D), BF16)),
        grid=(B, N // tm),
        in_specs=[pl.BlockSpec((1, tm, W), row), pl.BlockSpec((1, tm, W), row), pl.BlockSpec((1, tm, W), row),
                  pl.BlockSpec((1, tm, 3 * D), lambda b_, i: (b_, tok0 // tm + i, 0)),
                  pl.BlockSpec((W, D), const2), pl.BlockSpec((W, D), const2), pl.BlockSpec((W, D), const2),
                  pl.BlockSpec((D, D), const2),
                  pl.BlockSpec((1, tm, D), row), pl.BlockSpec((1, 1, D), per_b),
                  pl.BlockSpec((1, D), const2), pl.BlockSpec((1, D), const2),
                  pl.BlockSpec((1, 1, D), per_b), pl.BlockSpec((1, 1, D), per_b)],
        out_specs=(pl.BlockSpec((1, tm, D), row), pl.BlockSpec((1, tm, D), row)),
        compiler_params=_cp(2), name="merge_resid_ln",
    )(a, b, c, gates, wpa, wpb, wpc, wo, x, gate, g, bb, sh, sc)


def _split_bf16(x):
    hi = x.astype(BF16)
    lo = (x - hi.astype(F32)).astype(BF16)
    return hi, lo


def _dot3(a, b):
    ah, al = _split_bf16(a)
    bh, bl = _split_bf16(b)
    d = functools.partial(jnp.dot, preferred_element_type=F32)
    return d(ah, bh) + (d(ah, bl) + d(al, bh))


def _filter_kernel(z_ref, w1_ref, b1_ref, w2_ref, b2_ref, w3_ref, fr_ref, dec_ref, h_ref, s_ref, *, zero_block):
    i = pl.program_id(0)

    @pl.when(i == 0)
    def _():
        s_ref[...] = jnp.zeros(s_ref.shape, F32)

    fr = fr_ref[...]
    hid = jnp.sin(fr * (_dot3(z_ref[...], w1_ref[...]) + b1_ref[...]))
    hid = jnp.sin(fr * (_dot3(hid, w2_ref[...]) + b2_ref[...]))
    h = _dot3(hid, w3_ref[0])
    dec = dec_ref[...]
    C = dec.shape[1]
    n_ord = h.shape[1] // C
    h = h * jnp.concatenate([dec] * n_ord, axis=1)
    s_ref[...] += jnp.sum(jnp.abs(h), axis=0, keepdims=True)
    row = lax.broadcasted_iota(jnp.int32, h.shape, 0)
    h = jnp.where((row == 0) & (i == zero_block), 0.0, h)
    for o in range(n_ord):
        h_ref[o] = h[:, o * C:(o + 1) * C]


def hyena_filter_mlp(z2, w1, b1, w2, b2, w3_dir, freq, decay2):
    n2, E = z2.shape
    Hh = w2.shape[0]
    OC = w3_dir.shape[2]
    C = decay2.shape[1]
    n_ord = OC // C
    L = n2 // 2
    tr = _tile(L, 512, 8)
    nblk = n2 // tr
    c2 = lambda i: (0, 0)
    return pl.pallas_call(
        functools.partial(_filter_kernel, zero_block=L // tr),
        out_shape=(jax.ShapeDtypeStruct((n_ord, n2, C), F32), jax.ShapeDtypeStruct((1, OC), F32)),
        grid=(nblk,),
        in_specs=[pl.BlockSpec((tr, E), lambda i: (i, 0)), pl.BlockSpec((E, Hh), c2), pl.BlockSpec((1, Hh), c2),
                  pl.BlockSpec((Hh, Hh), c2), pl.BlockSpec((1, Hh), c2),
                  pl.BlockSpec((1, Hh, OC), lambda i: (i // (nblk // 2), 0, 0)),
                  pl.BlockSpec((1, Hh), c2), pl.BlockSpec((tr, C), lambda i: (i, 0))],
        out_specs=(pl.BlockSpec((n_ord, tr, C), lambda i: (0, i, 0)), pl.BlockSpec((1, OC), c2)),
        compiler_params=_cp(1), name="hyena_filter_mlp",
    )(z2, w1, b1, w2, b2, w3_dir, freq, decay2)


def _mm_split(m_hi, m_lo, x, passes):
    d = functools.partial(jnp.dot, preferred_element_type=F32)
    if passes == 1:
        return d(m_hi, x.astype(BF16))
    xh, xl = _split_bf16(x)
    return d(m_hi, xh) + (d(m_lo, xh) + d(m_hi, xl))


FFT_ROWS = 8
FFT_BLOCK = 16


def _level1_apply(m_refs, x, r_out, passes):
    r_in, _, C = x.shape
    x = x.astype(F32)
    m_hi = m_refs[0][...]
    m_lo = m_refs[1][...] if passes > 1 else None
    groups = []
    for g in range(FFT_BLOCK // FFT_ROWS):
        xg = x[:, g * FFT_ROWS:(g + 1) * FFT_ROWS].reshape(r_in * FFT_ROWS, C)
        groups.append(_mm_split(m_hi, m_lo, xg, passes).reshape(r_out, FFT_ROWS, C))
    return jnp.concatenate(groups, axis=1)


def _level1_kernel(*refs, passes):
    n_mat = 1 if passes == 1 else 2
    x_ref, o_ref = refs[n_mat:]
    o_ref[0] = _level1_apply(refs[:n_mat], x_ref[0], o_ref.shape[1], passes).astype(o_ref.dtype)


def _level1_gate_kernel(*refs, passes):
    n_mat = 1 if passes == 1 else 2
    x_ref, z_ref, xg_ref, bias_ref, o_ref = refs[n_mat:]
    y = _level1_apply(refs[:n_mat], x_ref[0], o_ref.shape[1], passes)
    o_ref[0] = (xg_ref[0].astype(F32) * (y + bias_ref[...] * z_ref[0].astype(F32))).astype(o_ref.dtype)


def fft_level1(m_hi, m_lo, x, passes, gate_args=None, out_dtype=F32):
    P, R_in, Nb, C = x.shape
    R_out = m_hi.shape[0] // FFT_ROWS
    mats = (m_hi,) if passes == 1 else (m_hi, m_lo)
    mspec = pl.BlockSpec(m_hi.shape, lambda p, j: (0, 0), pipeline_mode=pl.Buffered(1))
    xspec = pl.BlockSpec((1, R_in, FFT_BLOCK, C), lambda p, j: (p, 0, j, 0))
    ospec = pl.BlockSpec((1, R_out, FFT_BLOCK, C), lambda p, j: (p, 0, j, 0))
    if gate_args is None:
        kern = functools.partial(_level1_kernel, passes=passes)
        ins, specs = (*mats, x), [mspec] * len(mats) + [xspec]
    else:
        z, xg, bias = gate_args
        kern = functools.partial(_level1_gate_kernel, passes=passes)
        ins = (*mats, x, z, xg, bias)
        specs = [mspec] * len(mats) + [xspec, ospec, ospec, pl.BlockSpec((1, C), lambda p, j: (0, 0))]
    return pl.pallas_call(
        kern, out_shape=jax.ShapeDtypeStruct((P, R_out, Nb, C), out_dtype),
        grid=(P, Nb // FFT_BLOCK), in_specs=specs, out_specs=ospec,
        compiler_params=_cp(2), name="fft_level1",
    )(*ins)


FFT_MID_K1 = 8


def _fft_mid_kernel(*refs, passes):
    n_mat = 2 if passes == 1 else 4
    gh_ref, gih_ref = refs[0], refs[1]
    gl_ref, gil_ref = (refs[2], refs[3]) if passes > 1 else (None, None)
    h_ref, a_ref, o_ref = refs[n_mat:]
    n_pairs, _, kb, nb, C = a_ref.shape
    for kk in range(kb):
        hr, hi = h_ref[0, 0, kk], h_ref[0, 1, kk]
        for p in range(n_pairs):
            x = a_ref[p, :, kk].reshape(2 * nb, C)
            X = _mm_split(gh_ref[kk], None if gl_ref is None else gl_ref[kk], x, passes)
            xr, xi = X[:nb], X[nb:]
            Y = jnp.concatenate([xr * hr - xi * hi, xr * hi + xi * hr], axis=0)
            Bv = _mm_split(gih_ref[kk], None if gil_ref is None else gil_ref[kk], Y, passes)
            o_ref[p, :, kk] = Bv.reshape(2, nb, C).astype(o_ref.dtype)


def fft_mid(g_hi, g_lo, gi_hi, gi_lo, spectra, order, a5, passes):
    P, _, Na, Nb, C = a5.shape
    kb = FFT_MID_K1
    gspec = pl.BlockSpec((kb, 2 * Nb, 2 * Nb), lambda k: (k, 0, 0))
    mats = (g_hi, gi_hi) if passes == 1 else (g_hi, gi_hi, g_lo, gi_lo)
    blk = pl.BlockSpec((P, 2, kb, Nb, C), lambda k: (0, 0, k, 0, 0))
    return pl.pallas_call(
        functools.partial(_fft_mid_kernel, passes=passes),
        out_shape=jax.ShapeDtypeStruct(a5.shape, a5.dtype),
        grid=(Na // kb,),
        in_specs=[gspec] * len(mats) + [pl.BlockSpec((1, 2, kb, Nb, C), lambda k: (order, 0, k, 0, 0)), blk],
        out_specs=blk,
        compiler_params=_cp(1), name="fft_mid",
    )(*mats, spectra, a5)


def _fft_spec_kernel(*refs, passes):
    n_mat = 1 if passes == 1 else 2
    gh_ref = refs[0]
    gl_ref = refs[1] if passes > 1 else None
    inv_ref, a_ref, o_ref = refs[n_mat:]
    n_filt, _, kb, nb, C = a_ref.shape
    for kk in range(kb):
        for p in range(n_filt):
            x = a_ref[p, :, kk].reshape(2 * nb, C)
            X = _mm_split(gh_ref[kk], None if gl_ref is None else gl_ref[kk], x, passes) * inv_ref[p]
            o_ref[p, :, kk] = X.reshape(2, nb, C)


def fft_filter_spectrum(g_hi, g_lo, inv_norm, a5, passes):
    P, _, Na, Nb, C = a5.shape
    kb = FFT_MID_K1
    mats = (g_hi,) if passes == 1 else (g_hi, g_lo)
    gspec = pl.BlockSpec((kb, 2 * Nb, 2 * Nb), lambda k: (k, 0, 0))
    blk = pl.BlockSpec((P, 2, kb, Nb, C), lambda k: (0, 0, k, 0, 0))
    return pl.pallas_call(
        functools.partial(_fft_spec_kernel, passes=passes),
        out_shape=jax.ShapeDtypeStruct(a5.shape, F32),
        grid=(Na // kb,),
        in_specs=[gspec] * len(mats) + [pl.BlockSpec((P, 1, C), lambda k: (0, 0, 0)), blk],
        out_specs=blk,
        compiler_params=_cp(1), name="fft_filter_spectrum",
    )(*mats, inv_norm, a5)


def _fft_factors(n_fft):
    na = 1 << (int(math.log2(n_fft)) // 2)
    return na, n_fft // na


def _fft_tables(L):
    n_fft = 2 * L
    na, nb = _fft_factors(n_fft)
    nh = na // 2
    iota = lambda shape, axis: lax.broadcasted_iota(jnp.int32, shape, axis)

    def cis(num, den):
        ang = (2.0 * math.pi / den) * (num % den).astype(F32)
        return jnp.cos(ang), -jnp.sin(ang)

    def block(diag_sign, pr, pc, cr, ci):
        return jnp.where(pr == pc, cr, jnp.where(pr == 1, diag_sign * ci, -diag_sign * ci))

    def kron_matrix(rows, cols, entry):
        shape = (rows * FFT_ROWS, cols * FFT_ROWS)
        I, J = iota(shape, 0), iota(shape, 1)
        m = entry(I // FFT_ROWS, J // FFT_ROWS)
        return _split_bf16(jnp.where(I % FFT_ROWS == J % FFT_ROWS, m, 0.0))

    def fwd_entry(i, j):
        cr, ci = cis((i % na) * (j % nh), na)
        return block(1, i // na, j // nh, cr, ci)

    def flt_entry(i, j):
        cr, ci = cis((i % na) * j, na)
        return jnp.where(i // na == 0, cr, ci)

    def inv_entry(i, j):
        cr, ci = cis((i % nh) * (j % na), na)
        return block(-1, i // nh, j // na, cr, ci) / n_fft

    gshape = (na, 2 * nb, 2 * nb)
    k1, gi_, gj_ = iota(gshape, 0), iota(gshape, 1), iota(gshape, 2)
    cr, ci = cis((gi_ % nb) * (gj_ % nb) * na + k1 * (gj_ % nb), n_fft)
    g = _split_bf16(block(1, gi_ // nb, gj_ // nb, cr, ci))
    cr, ci = cis((gj_ % nb) * (gi_ % nb) * na + k1 * (gi_ % nb), n_fft)
    ginv = _split_bf16(block(-1, gi_ // nb, gj_ // nb, cr, ci))

    C = HY_WIDTH
    n = jnp.arange(n_fft, dtype=jnp.int32)
    pos = jnp.where(n < L, n, jnp.where(n == L, 0, n_fft - n)).astype(F32)[:, None]
    t = pos * (1.0 / (L - 1))
    f = jnp.linspace(1e-4, HY_BANDS - 1, HY_BANDS, dtype=F32)
    ang = (2.0 * math.pi / L) * pos * f[None, :]
    z2 = jnp.concatenate([t, jnp.cos(ang), -jnp.sin(ang), jnp.zeros((n_fft, HY_EMB_PAD - HY_EMB), F32)], axis=-1)
    deltas = jnp.abs(jnp.linspace(HY_MIN_DECAY, HY_MAX_DECAY, C, dtype=F32))
    decay2 = jnp.exp(-t * deltas)
    return dict(na=na, nb=nb, fwd=kron_matrix(2 * na, na, fwd_entry), flt=kron_matrix(2 * na, na, flt_entry),
                inv=kron_matrix(na, 2 * na, inv_entry), g=g, ginv=ginv, z2=z2, decay2=decay2)


def hyena_filter_spectra(w1, b1, w2, b2, w3, freq, tabs, passes):
    C = HY_WIDTH
    na, nb = tabs["na"], tabs["nb"]
    w1p = jnp.pad(w1, ((0, HY_EMB_PAD - HY_EMB), (0, 0)))
    w3_dir = jnp.transpose(w3.reshape(HY_HIDDEN, HY_ORDER, 2, C), (2, 0, 1, 3)).reshape(2, HY_HIDDEN, HY_ORDER * C)
    filt, s = hyena_filter_mlp(tabs["z2"], w1p, b1[None], w2, b2[None], w3_dir, freq[None], tabs["decay2"])
    inv_norm = (1.0 / s).reshape(HY_ORDER, 1, C)
    a = fft_level1(*tabs["flt"], filt.reshape(HY_ORDER, na, nb, C), passes,
                   out_dtype=BF16 if passes == 1 else F32)
    return fft_filter_spectrum(*tabs["g"], inv_norm, a.reshape(HY_ORDER, 2, na, nb, C), passes)


def hyena_long_conv_gate(z, xg, bias, spectra, order, tabs, passes):
    B, L, C = z.shape
    na, nb = tabs["na"], tabs["nb"]
    P = B // 2
    nat = lambda a: a.reshape(P, na, nb, C)
    mid_dtype = BF16 if passes == 1 else F32
    a = fft_level1(*tabs["fwd"], nat(z), passes, out_dtype=mid_dtype)
    bv = fft_mid(*tabs["g"], *tabs["ginv"], spectra, order, a.reshape(P, 2, na, nb, C), passes)
    y = fft_level1(*tabs["inv"], bv.reshape(P, 2 * na, nb, C), passes,
                   gate_args=(nat(z), nat(xg), bias[None]), out_dtype=z.dtype)
    return y.reshape(B, L, C)


def hyena_mix(u, row0, L, spectra, conv_w, conv_b, bias, tabs, passes):
    v, x1, x2 = hyena_short_conv(u, conv_w, conv_b[None], row0, L)
    zz = hyena_long_conv_gate(v, x1, bias[0], spectra, 0, tabs, passes)
    return hyena_long_conv_gate(zz, x2, bias[1], spectra, 1, tabs, passes)


FFT_PASSES = 1
FILTER_PASSES = 1


def _rope_tables_t(n):
    rows = n // GRID_W
    r = jnp.repeat(jnp.arange(rows, dtype=F32), GRID_W)
    col = jnp.tile(jnp.arange(GRID_W, dtype=F32), rows)
    axis_dim = HEAD_DIM // 2
    inv = ROPE_THETA ** (-jnp.arange(0, axis_dim, 2, dtype=F32) / axis_dim)
    ang = jnp.concatenate([r[:, None] * inv, col[:, None] * inv], axis=-1)
    return jnp.cos(ang).T, jnp.sin(ang).T


def _layer_weights(l, w_in, gq_qn, gq_kn):
    w = w_in[l]
    d_qk = DA_HEADS * 2 * HEAD_DIM
    gq_kv = GKV_HEADS * HEAD_DIM
    gq_q = GQ_HEADS * HEAD_DIM
    o = np.cumsum([0, d_qk, d_qk, gq_kv, gq_kv, d_qk, gq_q, 3 * HY_WIDTH])
    ka, va, kb, vb, qa, qb, hy = (w[:, o[i]:o[i + 1]] for i in range(7))
    gates = w[:, o[7]:]
    qscale = (HEAD_DIM ** -0.5) * LOG2E
    w_qk = jnp.concatenate([ka, kb, qa, qb], axis=1).T.astype(BF16)
    g_qk = jnp.concatenate([jnp.ones((d_qk,), F32), jnp.tile(gq_kn[l], GKV_HEADS),
                            jnp.full((d_qk,), qscale, F32), jnp.tile(gq_qn[l], GQ_HEADS) * qscale])[:, None]
    n_ka, n_kb, n_qa = d_qk // HEAD_DIM, gq_kv // HEAD_DIM, d_qk // HEAD_DIM
    rms_heads = frozenset(range(n_ka, n_ka + n_kb)) | frozenset(range(n_ka + n_kb + n_qa,
                                                                       n_ka + n_kb + n_qa + gq_q // HEAD_DIM))
    return dict(w_qk=w_qk, g_qk=g_qk, rms_heads=rms_heads, w_va=va.T.astype(BF16), w_vb=vb.T.astype(BF16),
                w_tok=jnp.concatenate([hy, gates], axis=1).astype(BF16))


def kernel(x, c, ctx, c_ctx, w_mod, b_mod, w_in, da_lq1, da_lk1, da_lq2, da_lk2, da_subln, gq_qn, gq_kn,
           hy_conv_w, hy_conv_b, hy_w1, hy_b1, hy_w2, hy_b2, hy_w3, hy_freq, hy_bias, w_pa, w_pb, w_pc, w_o,
           ln1_g, ln1_b, w_up, ffn_conv_w, ffn_conv_b, w_down, ln2_g, ln2_b):
    B, n_lat, D = x.shape
    n_ctx = ctx.shape[1]
    depth = w_in.shape[0]
    alpha = (2 * depth) ** 0.25
    cos_l, sin_l = _rope_tables_t(n_lat)
    cos_t = jnp.concatenate([cos_l, jnp.ones((HALF, n_ctx), F32)], axis=1)
    sin_t = jnp.concatenate([sin_l, jnp.zeros((HALF, n_ctx), F32)], axis=1)
    lat, ctx_span, all_span = (0, n_lat), (n_lat, n_ctx), (0, n_lat + n_ctx)
    tabs_l = _fft_tables(n_lat)
    tabs_c = _fft_tables(n_ctx)
    xc = ctx

    cond = jnp.concatenate([c, c_ctx[None]], axis=0)
    cond = jnp.pad(jax.nn.silu(cond), ((0, 16 - (B + 1) % 16), (0, 0))).astype(BF16)

    stacked_bf16 = tuple(w.astype(BF16) for w in (w_pa, w_pb, w_pc, w_o, w_up, w_down))

    for l in range(depth):
        last = l == depth - 1
        lam_init = 0.8 - 0.6 * math.exp(-0.3 * l)
        lam = (jnp.exp(jnp.sum(da_lq1[l] * da_lk1[l])) - jnp.exp(jnp.sum(da_lq2[l] * da_lk2[l])) + lam_init)
        lam = lam.reshape(1).astype(F32)
        mod = matmul(cond, w_mod[l].astype(BF16), F32, tn_pref=1024) + b_mod[l]
        sh1, sc1, g1, sh2, sc2, g2 = [m[:, None, :] for m in jnp.split(mod[:B], 6, axis=-1)]
        mc = [jnp.broadcast_to(m[None, None, :], (B, 1, D)) for m in jnp.split(mod[B], 6)]
        lw = _layer_weights(l, w_in, gq_qn, gq_kn)
        subln = da_subln[l][:, None]
        hy_mlp = (hy_w1[l], hy_b1[l], hy_w2[l], hy_b2[l], hy_w3[l], hy_freq[l])
        spectra_l = hyena_filter_spectra(*hy_mlp, tabs_l, FILTER_PASSES)
        wpa, wpb, wpc, wo, wup, wdn = (w[l] for w in stacked_bf16)
        ln1 = (ln1_g[l][None], ln1_b[l][None])
        ln2 = (ln2_g[l][None], ln2_b[l][None])

        q_all, k_tok, k_n2, va, vb, u, gates = proj_all(x, xc, sh1, sc1, mc[0], mc[1], cos_t, sin_t, lw)
        kmax = jnp.sqrt(k_n2[:, :N_KHEADS, 0]).reshape(B * N_KHEADS)
        n_da_maps = DA_HEADS * 2

        def attend(q_span, kv_span):
            da = flash_attention(q_all, k_tok, va, kmax, lam, subln, mode="da", q_span=q_span, kv_span=kv_span,
                                 q_row_block=0, k_col_block=0, khead0=0, post_scale=1.0 - lam_init)
            gq = flash_attention(q_all, k_tok, vb, kmax, lam, subln, mode="gqa", q_span=q_span, kv_span=kv_span,
                                 q_row_block=n_da_maps * HEAD_DIM // 128, k_col_block=n_da_maps * HEAD_DIM // 128,
                                 khead0=n_da_maps)
            return da, gq

        hy_args = (hy_conv_w[l], hy_conv_b[l], hy_bias[l])
        a_l, b_l = attend(lat, all_span)
        c_l = hyena_mix(u, *lat, spectra_l, *hy_args, tabs_l, FFT_PASSES)
        x_new, h2 = merge_resid_ln(a_l, b_l, c_l, gates, lat[0], wpa, wpb, wpc, wo, x, g1, *ln1,
                                   sh2, sc2, alpha)
        x_new = conv_ffn_resid_ln(h2, wup, ffn_conv_w[l], ffn_conv_b[l][None], wdn, x_new, g2, *ln2, alpha)

        if not last:
            spectra_c = hyena_filter_spectra(*hy_mlp, tabs_c, FILTER_PASSES)
            a_c, b_c = attend(ctx_span, ctx_span)
            c_c = hyena_mix(u, *ctx_span, spectra_c, *hy_args, tabs_c, FFT_PASSES)
            xc, hc2 = merge_resid_ln(a_c, b_c, c_c, gates, ctx_span[0], wpa, wpb, wpc, wo, xc, mc[2],
                                     *ln1, mc[3], mc[4], alpha)
            xc = conv_ffn_resid_ln(hc2, wup, ffn_conv_w[l], ffn_conv_b[l][None], wdn, xc, mc[5], *ln2, alpha)
        x = x_new
    return x
```
